```python
import jax, jax.numpy as jnp
from jax import lax
import numpy as np

D_MODEL = 1024
BATCH = 4
SEQ = 8192
DEPTH = 2

CTX_LEN = 256
GRID_W = 64
N_EVEN = (DEPTH + 1) // 2
N_ODD = DEPTH // 2
EPS = 1e-6

LRU_WIDTH = D_MODEL
LRU_BLOCKS = 16
LRU_BLOCK = LRU_WIDTH // LRU_BLOCKS
LRU_C = 8.0
CONV_WIDTH = 4
CONV_LEFT = 2
DN_HEADS = 8
DN_DK = 128
DN_DV = 128
DN_CHUNK = 64
DN_QKV = DN_HEADS * (2 * DN_DK + DN_DV)
GATE_COL_SCALE = 0.1
EVEN_SPLITS = [LRU_WIDTH, 2 * LRU_WIDTH, 2 * LRU_WIDTH + DN_QKV, 2 * LRU_WIDTH + DN_QKV + DN_HEADS * DN_DV,
               2 * LRU_WIDTH + DN_QKV + DN_HEADS * DN_DV + 2 * DN_HEADS]
EVEN_IN_COLS = 2 * LRU_WIDTH + DN_QKV + DN_HEADS * DN_DV + 4 * DN_HEADS
EVEN_MIX = LRU_WIDTH + DN_HEADS * DN_DV
N_Q_HEADS = 8
N_KV_HEADS = 4
Q_PER_KV = N_Q_HEADS // N_KV_HEADS
HEAD_DIM = 128
ROPE_AXIS_DIM = HEAD_DIM // 2
ROPE_THETA = 10000.0
Q_BLOCK = 128
ATT_COLS = (N_Q_HEADS + 2 * N_KV_HEADS) * HEAD_DIM
D_FF = 4 * D_MODEL

kernel_name = 'hybrid_rglru_gdn_gqa_prefix_dit'

F32 = jnp.float32


def rms_norm(x, g):
    xf = x.astype(F32)
    y = xf * lax.rsqrt(jnp.mean(xf * xf, axis=-1, keepdims=True) + EPS)
    return (y * g.astype(F32)).astype(x.dtype)


def l2_normalize(x):
    xf = x.astype(F32)
    return xf * lax.rsqrt(jnp.sum(xf * xf, axis=-1, keepdims=True) + EPS)


def dw_conv(x, w, b=None):
    L = x.shape[1]
    xp = jnp.pad(x, ((0, 0), (CONV_LEFT, CONV_WIDTH - 1 - CONV_LEFT), (0, 0)))
    y = xp[:, 0:L] * w[0]
    for j in range(1, CONV_WIDTH):
        y = y + xp[:, j:j + L] * w[j]
    return y if b is None else y + b


def sq_relu_mlp(h, w1, w2):
    return jnp.square(jax.nn.relu(h @ w1)) @ w2


def lru_gates(xc, w_a, b_a, w_i, b_i, lam):
    xf = xc.astype(F32)
    xb = xf.reshape(*xf.shape[:-1], LRU_BLOCKS, LRU_BLOCK)
    r = jax.nn.sigmoid(jnp.einsum('blnj,njk->blnk', xb, w_a.astype(F32)).reshape(xf.shape) + b_a.astype(F32))
    i = jax.nn.sigmoid(jnp.einsum('blnj,njk->blnk', xb, w_i.astype(F32)).reshape(xf.shape) + b_i.astype(F32))
    log_a = -LRU_C * r * jax.nn.softplus(-lam.astype(F32))
    a = jnp.exp(log_a)
    b = jnp.sqrt(-jnp.expm1(2.0 * log_a)) * (i * xf)
    return a, b


def linear_scan(a, b, h0, reverse):
    def combine(e1, e2):
        a1, b1 = e1
        a2, b2 = e2
        return a1 * a2, a2 * b1 + b2
    a_cum, b_cum = lax.associative_scan(combine, (a, b), axis=1, reverse=reverse)
    return b_cum if h0 is None else a_cum * h0[:, None] + b_cum


def delta_chunked(q, k, v, beta, g, s0):
    bsz, L, H, dk = q.shape
    dv = v.shape[-1]
    n = L // DN_CHUNK

    def to_chunks(t):
        return jnp.transpose(t.reshape(bsz, n, DN_CHUNK, H, -1), (1, 0, 3, 2, 4))

    qc, kc, vc = to_chunks(q), to_chunks(k), to_chunks(v)
    bc = to_chunks(beta[..., None])[..., 0]
    gcum = jnp.cumsum(to_chunks(g[..., None])[..., 0], axis=-1)
    idx = jnp.arange(DN_CHUNK)
    causal = idx[:, None] >= idx[None, :]
    strict = idx[:, None] > idx[None, :]
    decay = jnp.exp(jnp.where(causal, gcum[..., :, None] - gcum[..., None, :], -jnp.inf))
    k_beta = kc * bc[..., None]
    m = jnp.where(strict, jnp.einsum('nbhid,nbhjd->nbhij', k_beta, kc) * decay, 0.0)
    eye = jnp.eye(DN_CHUNK, dtype=F32)
    rhs = jnp.concatenate([vc * bc[..., None], k_beta * jnp.exp(gcum)[..., None]], axis=-1)
    sol = lax.linalg.triangular_solve(eye + m, rhs, left_side=True, lower=True, unit_diagonal=True)
    u, w = sol[..., :dv], sol[..., dv:]
    attn = jnp.einsum('nbhid,nbhjd->nbhij', qc, kc) * decay
    q_dec = qc * jnp.exp(gcum)[..., None]
    g_last = gcum[..., -1]
    k_dec = kc * jnp.exp(g_last[..., None] - gcum)[..., None]

    def step(S, xs):
        u_i, w_i, q_i, k_i, attn_i, gl = xs
        v_new = u_i - jnp.einsum('bhck,bhkv->bhcv', w_i, S)
        o = jnp.einsum('bhck,bhkv->bhcv', q_i, S) + jnp.einsum('bhij,bhjv->bhiv', attn_i, v_new)
        S = S * jnp.exp(gl)[..., None, None] + jnp.einsum('bhck,bhcv->bhkv', k_i, v_new)
        return S, o

    S, o = lax.scan(step, s0, (u, w, q_dec, k_dec, attn, g_last))
    o = jnp.transpose(o, (1, 0, 3, 2, 4)).reshape(bsz, L, H, dv)
    return o, S


def even_mixer(hc, hx, w_in, lru_cw, lru_cb, lru_wa, lru_ba, lru_wi, lru_bi, lru_lam,
               dn_cw, dn_alog, dn_dtb, dn_g, w_out, with_ctx_out):
    def prep(h):
        bsz, L, _ = h.shape
        xr, yr, qkv, z, beta_raw, alpha_raw = jnp.split(h @ w_in, EVEN_SPLITS, axis=-1)
        xr = dw_conv(xr, lru_cw, lru_cb)
        q, k, v = jnp.split(jax.nn.silu(dw_conv(qkv, dn_cw)), [DN_HEADS * DN_DK, 2 * DN_HEADS * DN_DK], axis=-1)
        q = l2_normalize(q.reshape(bsz, L, DN_HEADS, DN_DK)) * (DN_DK ** -0.5)
        k = l2_normalize(k.reshape(bsz, L, DN_HEADS, DN_DK))
        v = v.reshape(bsz, L, DN_HEADS, DN_DV).astype(F32)
        beta = jax.nn.sigmoid(beta_raw.astype(F32)).reshape(bsz, L, 2, DN_HEADS)
        g = -jnp.exp(dn_alog.astype(F32)) * jax.nn.softplus(
            alpha_raw.astype(F32).reshape(bsz, L, 2, DN_HEADS) + dn_dtb.astype(F32))
        return xr, yr, q, k, v, z, beta, g

    xr_c, yr_c, q_c, k_c, v_c, z_c, beta_c, g_c = prep(hc)
    xr_x, yr_x, q_x, k_x, v_x, z_x, beta_x, g_x = prep(hx)
    s0 = jnp.zeros((hx.shape[0], DN_HEADS, DN_DK, DN_DV), F32)
    lru_c, lru_x, dn_c, dn_x = [], [], [], []
    for d, rev in enumerate((False, True)):
        flip = (lambda t: jnp.flip(t, axis=1)) if rev else (lambda t: t)
        a_c, b_c = lru_gates(xr_c, lru_wa[d], lru_ba[d], lru_wi[d], lru_bi[d], lru_lam[d])
        h_c = linear_scan(a_c, b_c, None, rev)
        a_x, b_x = lru_gates(xr_x, lru_wa[d], lru_ba[d], lru_wi[d], lru_bi[d], lru_lam[d])
        h_x = linear_scan(a_x, b_x, h_c[:, 0] if rev else h_c[:, -1], rev)
        lru_c.append(h_c)
        lru_x.append(h_x)
        o_c, s_c = delta_chunked(flip(q_c), flip(k_c), flip(v_c), flip(beta_c[:, :, d]), flip(g_c[:, :, d]), s0)
        o_x, _ = delta_chunked(flip(q_x), flip(k_x), flip(v_x), flip(beta_x[:, :, d]), flip(g_x[:, :, d]), s_c)
        dn_c.append(flip(o_c))
        dn_x.append(flip(o_x))

    def merge(h_lru, o_dn, yr, z):
        bsz, L, _ = yr.shape
        lru = h_lru.astype(yr.dtype) * jax.nn.gelu(yr)
        dn = rms_norm(o_dn, dn_g).astype(z.dtype) * jax.nn.silu(z.reshape(bsz, L, DN_HEADS, DN_DV))
        return jnp.concatenate([lru, dn.reshape(bsz, L, DN_HEADS * DN_DV)], axis=-1) @ w_out

    ox = merge(lru_x[0] + lru_x[1], dn_x[0] + dn_x[1], yr_x, z_x)
    oc = merge(lru_c[0] + lru_c[1], dn_c[0] + dn_c[1], yr_c, z_c) if with_ctx_out else None
    return oc, ox


def rope_tables(n_tokens):
    rows = n_tokens // GRID_W
    row_id = jnp.repeat(jnp.arange(rows), GRID_W).astype(F32)
    col_id = jnp.tile(jnp.arange(GRID_W), rows).astype(F32)
    inv_freq = ROPE_THETA ** (-jnp.arange(0, ROPE_AXIS_DIM, 2, dtype=F32) / ROPE_AXIS_DIM)
    ang_r = row_id[:, None] * inv_freq[None]
    ang_c = col_id[:, None] * inv_freq[None]
    return jnp.cos(ang_r), jnp.sin(ang_r), jnp.cos(ang_c), jnp.sin(ang_c)


def rope_1d(x, cos, sin):
    half = x.shape[-1] // 2
    x1, x2 = x[..., :half], x[..., half:]
    c, s = cos[None, :, None], sin[None, :, None]
    return jnp.concatenate([x1 * c - x2 * s, x2 * c + x1 * s], axis=-1)


def rope_2d(x, rope):
    cr, sr, cc, sc = rope
    xf = x.astype(F32)
    out = jnp.concatenate([rope_1d(xf[..., :ROPE_AXIS_DIM], cr, sr), rope_1d(xf[..., ROPE_AXIS_DIM:], cc, sc)], axis=-1)
    return out.astype(x.dtype)


def gqa_attend(q, k, v):
    bsz, lq = q.shape[:2]
    qg = q.reshape(bsz, lq, N_KV_HEADS, Q_PER_KV, HEAD_DIM)
    s = jnp.einsum('bqgrd,bkgd->bgrqk', qg, k).astype(F32) * (HEAD_DIM ** -0.5)
    p = jax.nn.softmax(s, axis=-1).astype(v.dtype)
    o = jnp.einsum('bgrqk,bkgd->bqgrd', p, v)
    return o.reshape(bsz, lq, N_Q_HEADS * HEAD_DIM)


def odd_mixer(hc, hx, w_qkv, q_g, k_g, w_o, rope, with_ctx_out):
    def proj(h):
        bsz, L, _ = h.shape
        q, k, v = jnp.split(h @ w_qkv, [N_Q_HEADS * HEAD_DIM, (N_Q_HEADS + N_KV_HEADS) * HEAD_DIM], axis=-1)
        q = rms_norm(q.reshape(bsz, L, N_Q_HEADS, HEAD_DIM), q_g)
        k = rms_norm(k.reshape(bsz, L, N_KV_HEADS, HEAD_DIM), k_g)
        return q, k, v.reshape(bsz, L, N_KV_HEADS, HEAD_DIM)

    q_c, k_c, v_c = proj(hc)
    q_x, k_x, v_x = proj(hx)
    q_x = rope_2d(q_x, rope)
    k_x = rope_2d(k_x, rope)
    k_all = jnp.concatenate([k_c, k_x], axis=1)
    v_all = jnp.concatenate([v_c, v_x], axis=1)
    bsz, L = hx.shape[:2]
    nb = L // Q_BLOCK
    qb = jnp.transpose(q_x.reshape(bsz, nb, Q_BLOCK, N_Q_HEADS, HEAD_DIM), (1, 0, 2, 3, 4))
    ob = lax.map(lambda qi: gqa_attend(qi, k_all, v_all), qb)
    ox = jnp.transpose(ob, (1, 0, 2, 3)).reshape(bsz, L, N_Q_HEADS * HEAD_DIM) @ w_o
    oc = gqa_attend(q_c, k_c, v_c) @ w_o if with_ctx_out else None
    return oc, ox


def setup_inputs(seed: int = 0) -> dict:
    key = jax.random.key(seed)
    ks = iter(jax.random.split(key, 32))

    def nrm(shape, scale):
        return scale * jax.random.normal(next(ks), shape, F32)

    def gain(shape):
        return 1.0 + nrm(shape, 0.02)

    D = D_MODEL
    x = nrm((BATCH, SEQ, D), 1.0)
    c = nrm((BATCH, D), 1.0)
    ctx = nrm((BATCH, CTX_LEN, D), 1.0)
    c_ctx = nrm((D,), 1.0)
    norm_mix_g = gain((DEPTH, D))
    norm_mlp_g = gain((DEPTH, D))
    w_mod = nrm((DEPTH, D, 6 * D), 0.5 * D ** -0.5)
    b_mod = nrm((DEPTH, 6 * D), 0.02)
    mlp_w1 = nrm((DEPTH, D, D_FF), D ** -0.5)
    mlp_w2 = nrm((DEPTH, D_FF, D), D_FF ** -0.5)
    col_scale = jnp.concatenate([jnp.ones((EVEN_IN_COLS - 4 * DN_HEADS,), F32),
                                 jnp.full((4 * DN_HEADS,), GATE_COL_SCALE, F32)])
    even_w_in = nrm((N_EVEN, D, EVEN_IN_COLS), D ** -0.5) * col_scale
    lru_conv_w = nrm((N_EVEN, CONV_WIDTH, LRU_WIDTH), CONV_WIDTH ** -0.5)
    lru_conv_b = nrm((N_EVEN, LRU_WIDTH), 0.02)
    lru_w_a = nrm((N_EVEN, 2, LRU_BLOCKS, LRU_BLOCK, LRU_BLOCK), LRU_BLOCK ** -0.5)
    lru_b_a = nrm((N_EVEN, 2, LRU_WIDTH), 0.02)
    lru_w_i = nrm((N_EVEN, 2, LRU_BLOCKS, LRU_BLOCK, LRU_BLOCK), LRU_BLOCK ** -0.5)
    lru_b_i = nrm((N_EVEN, 2, LRU_WIDTH), 0.02)
    a_pow = jax.random.uniform(next(ks), (N_EVEN, 2, LRU_WIDTH), F32, minval=0.9, maxval=0.999)
    a0 = a_pow ** (1.0 / LRU_C)
    lru_lambda = jnp.log(a0) - jnp.log1p(-a0)
    dn_conv_w = nrm((N_EVEN, CONV_WIDTH, DN_QKV), CONV_WIDTH ** -0.5)
    dn_a_log = jnp.log(jax.random.uniform(next(ks), (N_EVEN, 2, DN_HEADS), F32, minval=1.0, maxval=16.0))
    dt = jnp.exp(jax.random.uniform(next(ks), (N_EVEN, 2, DN_HEADS), F32,
                                    minval=float(np.log(0.001)), maxval=float(np.log(0.1))))
    dn_dt_bias = dt + jnp.log(-jnp.expm1(-dt))
    dn_norm_g = gain((N_EVEN, DN_DV))
    even_w_out = nrm((N_EVEN, EVEN_MIX, D), EVEN_MIX ** -0.5)
    attn_w_qkv = nrm((N_ODD, D, ATT_COLS), D ** -0.5)
    attn_q_norm_g = gain((N_ODD, HEAD_DIM))
    attn_k_norm_g = gain((N_ODD, HEAD_DIM))
    attn_w_o = nrm((N_ODD, N_Q_HEADS * HEAD_DIM, D), (N_Q_HEADS * HEAD_DIM) ** -0.5)
    final_norm_g = gain((D,))
    return {'x': x, 'c': c, 'ctx': ctx, 'c_ctx': c_ctx, 'norm_mix_g': norm_mix_g, 'norm_mlp_g': norm_mlp_g,
            'w_mod': w_mod, 'b_mod': b_mod, 'mlp_w1': mlp_w1, 'mlp_w2': mlp_w2, 'even_w_in': even_w_in,
            'lru_conv_w': lru_conv_w, 'lru_conv_b': lru_conv_b, 'lru_w_a': lru_w_a, 'lru_b_a': lru_b_a,
            'lru_w_i': lru_w_i, 'lru_b_i': lru_b_i, 'lru_lambda': lru_lambda, 'dn_conv_w': dn_conv_w,
            'dn_a_log': dn_a_log, 'dn_dt_bias': dn_dt_bias, 'dn_norm_g': dn_norm_g, 'even_w_out': even_w_out,
            'attn_w_qkv': attn_w_qkv, 'attn_q_norm_g': attn_q_norm_g, 'attn_k_norm_g': attn_k_norm_g,
            'attn_w_o': attn_w_o, 'final_norm_g': final_norm_g}


def reference(x, c, ctx, c_ctx, norm_mix_g, norm_mlp_g, w_mod, b_mod, mlp_w1, mlp_w2, even_w_in,
              lru_conv_w, lru_conv_b, lru_w_a, lru_b_a, lru_w_i, lru_b_i, lru_lambda, dn_conv_w,
              dn_a_log, dn_dt_bias, dn_norm_g, even_w_out, attn_w_qkv, attn_q_norm_g, attn_k_norm_g,
              attn_w_o, final_norm_g):
    rope = rope_tables(x.shape[1])
    xs, cs = x, ctx
    for l in range(DEPTH):
        last = l == DEPTH - 1
        j = l // 2
        mod_x = (jax.nn.silu(c) @ w_mod[l] + b_mod[l])[:, None, :]
        mod_c = (jax.nn.silu(c_ctx) @ w_mod[l] + b_mod[l])[None, None, :]
        sh1, sc1, g1, sh2, sc2, g2 = jnp.split(mod_x, 6, axis=-1)
        csh1, csc1, cg1, csh2, csc2, cg2 = jnp.split(mod_c, 6, axis=-1)
        hx = rms_norm(xs, norm_mix_g[l]) * (1 + sc1) + sh1
        hc = rms_norm(cs, norm_mix_g[l]) * (1 + csc1) + csh1
        if l % 2 == 0:
            oc, ox = even_mixer(hc, hx, even_w_in[j], lru_conv_w[j], lru_conv_b[j], lru_w_a[j], lru_b_a[j],
                                lru_w_i[j], lru_b_i[j], lru_lambda[j], dn_conv_w[j], dn_a_log[j],
                                dn_dt_bias[j], dn_norm_g[j], even_w_out[j], not last)
        else:
            oc, ox = odd_mixer(hc, hx, attn_w_qkv[j], attn_q_norm_g[j], attn_k_norm_g[j], attn_w_o[j],
                               rope, not last)
        xs = xs + g1 * ox
        xs = xs + g2 * sq_relu_mlp(rms_norm(xs, norm_mlp_g[l]) * (1 + sc2) + sh2, mlp_w1[l], mlp_w2[l])
        if not last:
            cs = cs + cg1 * oc
            cs = cs + cg2 * sq_relu_mlp(rms_norm(cs, norm_mlp_g[l]) * (1 + csc2) + csh2, mlp_w1[l], mlp_w2[l])
    return rms_norm(xs, final_norm_g)
```

```python
import functools

import jax
import jax.numpy as jnp
from jax import lax
from jax.experimental import pallas as pl
from jax.experimental.pallas import tpu as pltpu

F32 = jnp.float32
BF16 = jnp.bfloat16

EPS = 1e-6
LANES = 128
SUBLANES = 8
VMEM_LIMIT = 56 * 1024 * 1024

LRU_BLOCK = 64
LRU_C = 8.0
CONV_WIDTH = 4
CONV_LEFT = 2
DN_HEADS = 8
DN_DK = 128
DN_CHUNK = 64
N_Q_HEADS = 8
N_KV_HEADS = 4
Q_PER_KV = N_Q_HEADS // N_KV_HEADS
HEAD_DIM = 128
GRID_W = 64
ROPE_THETA = 10000.0
NEG_BIG = -1e30


def _cparams(sem):
    return pltpu.CompilerParams(dimension_semantics=sem, vmem_limit_bytes=VMEM_LIMIT)


def _bdot(a, b):
    return jnp.dot(a.astype(BF16), b.astype(BF16), preferred_element_type=F32)


def _bdot_nt(a, b):
    return lax.dot_general(a.astype(BF16), b.astype(BF16), (((1,), (1,)), ((), ())),
                           preferred_element_type=F32)


def _bdot_tn(a, b):
    return lax.dot_general(a.astype(BF16), b.astype(BF16), (((0,), (0,)), ((), ())),
                           preferred_element_type=F32)


def _sigmoid(x):
    return jax.nn.sigmoid(x)


def _softplus(x):
    return jnp.maximum(x, 0.0) + jnp.log1p(jnp.exp(-jnp.abs(x)))


def _rms(x):
    return x * lax.rsqrt(jnp.mean(x * x, axis=-1, keepdims=True) + EPS)


def _col_tile(n, cap):
    best = LANES
    t = LANES
    while t <= min(n, cap):
        if n % t == 0:
            best = t
        t += LANES
    return best


def _mod_kernel(c_ref, w_ref, b_ref, o_ref):
    s = c_ref[...]
    s = s * _sigmoid(s)
    o_ref[...] = _bdot(s, w_ref[...]) + b_ref[...]


def _mod_call(cc, w_mod, b_mod):
    depth, d, n = w_mod.shape
    tn = _col_tile(n, 1536)
    return pl.pallas_call(
        _mod_kernel,
        grid=(depth, n // tn),
        in_specs=[pl.BlockSpec((SUBLANES, d), lambda l, j: (0, 0)),
                  pl.BlockSpec((None, d, tn), lambda l, j: (l, 0, j)),
                  pl.BlockSpec((None, 1, tn), lambda l, j: (l, 0, j))],
        out_specs=pl.BlockSpec((None, SUBLANES, tn), lambda l, j: (l, 0, j)),
        out_shape=jax.ShapeDtypeStruct((depth, SUBLANES, n), F32),
        compiler_params=_cparams(("parallel", "parallel")),
        name="mod",
    )(cc, w_mod, b_mod.reshape(depth, 1, n))


def _normlin_kernel(x_ref, g_ref, sc_ref, sh_ref, w_ref, o_ref, h_ref):
    @pl.when(pl.program_id(2) == 0)
    def _():
        y = _rms(x_ref[...]) * g_ref[...]
        h_ref[...] = (y * (1.0 + sc_ref[...]) + sh_ref[...]).astype(BF16)

    o_ref[...] = jnp.dot(h_ref[...], w_ref[...], preferred_element_type=F32)


def _normlin_call(x, g, sc, sh, w, tm):
    bsz, seq, d = x.shape
    n = w.shape[1]
    tn = _col_tile(n, 1024)
    return pl.pallas_call(
        _normlin_kernel,
        grid=(bsz, seq // tm, n // tn),
        in_specs=[pl.BlockSpec((None, tm, d), lambda b, i, j: (b, i, 0)),
                  pl.BlockSpec((1, d), lambda b, i, j: (0, 0)),
                  pl.BlockSpec((None, 1, d), lambda b, i, j: (b, 0, 0)),
                  pl.BlockSpec((None, 1, d), lambda b, i, j: (b, 0, 0)),
                  pl.BlockSpec((d, tn), lambda b, i, j: (0, j))],
        out_specs=pl.BlockSpec((None, tm, tn), lambda b, i, j: (b, i, j)),
        out_shape=jax.ShapeDtypeStruct((bsz, seq, n), F32),
        scratch_shapes=[pltpu.VMEM((tm, d), BF16)],
        compiler_params=_cparams(("parallel", "parallel", "arbitrary")),
        name="normlin",
    )(x, g.reshape(1, d), sc, sh, w)


def _mlp_kernel(x_ref, g_ref, sc_ref, sh_ref, gate_ref, w1_ref, w2_ref, fg_ref, o_ref,
                h_ref, acc_ref, *, final_norm):
    k = pl.program_id(2)

    @pl.when(k == 0)
    def _():
        y = _rms(x_ref[...]) * g_ref[...]
        h_ref[...] = (y * (1.0 + sc_ref[...]) + sh_ref[...]).astype(BF16)
        acc_ref[...] = jnp.zeros_like(acc_ref)

    a = jnp.dot(h_ref[...], w1_ref[...], preferred_element_type=F32)
    a = jnp.square(jnp.maximum(a, 0.0)).astype(BF16)
    acc_ref[...] += jnp.dot(a, w2_ref[...], preferred_element_type=F32)

    @pl.when(k == pl.num_programs(2) - 1)
    def _():
        y = x_ref[...] + gate_ref[...] * acc_ref[...]
        if final_norm:
            y = _rms(y) * fg_ref[...]
        o_ref[...] = y


def _mlp_call(x, g, sc, sh, gate, w1, w2, fg, tm, final_norm):
    bsz, seq, d = x.shape
    f = w1.shape[1]
    tf = _col_tile(f, 1024)
    vec = pl.BlockSpec((None, 1, d), lambda b, i, k: (b, 0, 0))
    return pl.pallas_call(
        functools.partial(_mlp_kernel, final_norm=final_norm),
        grid=(bsz, seq // tm, f // tf),
        in_specs=[pl.BlockSpec((None, tm, d), lambda b, i, k: (b, i, 0)),
                  pl.BlockSpec((1, d), lambda b, i, k: (0, 0)),
                  vec, vec, vec,
                  pl.BlockSpec((d, tf), lambda b, i, k: (0, k)),
                  pl.BlockSpec((tf, d), lambda b, i, k: (k, 0)),
                  pl.BlockSpec((1, d), lambda b, i, k: (0, 0))],
        out_specs=pl.BlockSpec((None, tm, d), lambda b, i, k: (b, i, 0)),
        out_shape=jax.ShapeDtypeStruct((bsz, seq, d), F32),
        scratch_shapes=[pltpu.VMEM((tm, d), BF16), pltpu.VMEM((tm, d), F32)],
        compiler_params=_cparams(("parallel", "parallel", "arbitrary")),
        name="mlp",
    )(x, g.reshape(1, d), sc, sh, gate, w1, w2, fg.reshape(1, d))


CONV_PAD = SUBLANES


def _fill_padded(xpad_ref, x_ref, seq, rows):
    zeros = jnp.zeros((CONV_PAD, xpad_ref.shape[1]), F32)
    xpad_ref[0:CONV_PAD, :] = zeros
    xpad_ref[CONV_PAD + seq:CONV_PAD + seq + CONV_PAD, :] = zeros

    def body(t, carry):
        r0 = pl.multiple_of(t * rows, rows)
        xpad_ref[pl.ds(r0 + CONV_PAD, rows), :] = x_ref[pl.ds(r0, rows), :]
        return carry

    lax.fori_loop(0, seq // rows, body, 0)


def _conv_tile(xpad_ref, r0, rows, w):
    n = rows + 2 * CONV_PAD
    win = xpad_ref[pl.ds(r0, n), :]
    acc = None
    for j in range(CONV_WIDTH):
        off = j - CONV_LEFT
        shifted = win if off == 0 else pltpu.roll(win, (-off) % n, 0)
        term = shifted[CONV_PAD:CONV_PAD + rows] * w[j:j + 1]
        acc = term if acc is None else acc + term
    return acc


def _dnprep_kernel(x_ref, w_ref, o_ref, xpad_ref, *, seq, rows):
    j = pl.program_id(1)
    is_qk = j < 2 * DN_HEADS
    scale = jnp.where(j < DN_HEADS, DN_DK ** -0.5, 1.0).astype(F32)
    _fill_padded(xpad_ref, x_ref, seq, rows)
    w = w_ref[...]

    def body(t, carry):
        r0 = pl.multiple_of(t * rows, rows)
        y = _conv_tile(xpad_ref, r0, rows, w)
        y = y * _sigmoid(y)
        nrm = (y * lax.rsqrt(jnp.sum(y * y, axis=-1, keepdims=True) + EPS)) * scale
        o_ref[pl.ds(r0, rows), :] = jnp.where(is_qk, nrm, y)
        return carry

    lax.fori_loop(0, seq // rows, body, 0)


def _dnprep_call(proj, conv_w, col0_blocks, rows):
    bsz, seq, _ = proj.shape
    ncol = conv_w.shape[1] // LANES
    return pl.pallas_call(
        functools.partial(_dnprep_kernel, seq=seq, rows=rows),
        grid=(bsz, ncol),
        in_specs=[pl.BlockSpec((None, seq, LANES), lambda b, j: (b, 0, col0_blocks + j)),
                  pl.BlockSpec((CONV_WIDTH, LANES), lambda b, j: (0, j))],
        out_specs=pl.BlockSpec((None, seq, LANES), lambda b, j: (b, 0, j)),
        out_shape=jax.ShapeDtypeStruct((bsz, seq, ncol * LANES), F32),
        scratch_shapes=[pltpu.VMEM((seq + 2 * CONV_PAD, LANES), F32)],
        compiler_params=_cparams(("parallel", "parallel")),
        name="dnprep",
    )(proj, conv_w)


def _scan_tile(a, b, carry, rev, rows):
    row = lax.broadcasted_iota(jnp.int32, a.shape, 0) % SUBLANES
    s = 1
    while s < SUBLANES:
        if rev:
            a_s = pltpu.roll(a, rows - s, 0)
            b_s = pltpu.roll(b, rows - s, 0)
            valid = row < SUBLANES - s
        else:
            a_s = pltpu.roll(a, s, 0)
            b_s = pltpu.roll(b, s, 0)
            valid = row >= s
        b = b + a * jnp.where(valid, b_s, 0.0)
        a = a * jnp.where(valid, a_s, 1.0)
        s *= 2
    nv = rows // SUBLANES
    hs = [None] * nv
    for v in (range(nv - 1, -1, -1) if rev else range(nv)):
        sl = slice(v * SUBLANES, (v + 1) * SUBLANES)
        h = a[sl] * carry + b[sl]
        carry = h[0:1] if rev else h[SUBLANES - 1:SUBLANES]
        hs[v] = h
    return jnp.concatenate(hs, axis=0), carry


def _gelu_tanh(x):
    return 0.5 * x * (1.0 + jnp.tanh(0.7978845608028654 * (x + 0.044715 * (x * x * x))))


def _lru_kernel(x_ref, y_ref, cw_ref, cb_ref, wa_ref, wi_ref, ba_ref, bi_ref, lam_ref, h0_ref,
                o_ref, ht_ref, xpad_ref, xc_ref, *, seq, rows):
    _fill_padded(xpad_ref, x_ref, seq, rows)
    cw = cw_ref[...]
    cb = cb_ref[...]
    nt = seq // rows

    def gates(xc, d):
        r = _sigmoid(_bdot(xc, wa_ref[d]) + ba_ref[d:d + 1, :])
        i = _sigmoid(_bdot(xc, wi_ref[d]) + bi_ref[d:d + 1, :])
        log_a = (-LRU_C * r) * _softplus(-lam_ref[d:d + 1, :])
        a = jnp.exp(log_a)
        b = jnp.sqrt(-jnp.tanh(log_a) * (a * a + 1.0)) * (i * xc)
        return a, b

    def fwd_body(t, carry):
        r0 = pl.multiple_of(t * rows, rows)
        xc = _conv_tile(xpad_ref, r0, rows, cw) + cb
        xc_ref[pl.ds(r0, rows), :] = xc
        a, b = gates(xc, 0)
        h, carry = _scan_tile(a, b, carry, False, rows)
        o_ref[pl.ds(r0, rows), :] = h
        return carry

    carry_f = lax.fori_loop(0, nt, fwd_body, h0_ref[0:1, :])

    def bwd_body(t, carry):
        r0 = pl.multiple_of((nt - 1 - t) * rows, rows)
        xc = xc_ref[pl.ds(r0, rows), :]
        a, b = gates(xc, 1)
        h, carry = _scan_tile(a, b, carry, True, rows)
        o_ref[pl.ds(r0, rows), :] = (o_ref[pl.ds(r0, rows), :] + h) * _gelu_tanh(y_ref[pl.ds(r0, rows), :])
        return carry

    carry_b = lax.fori_loop(0, nt, bwd_body, h0_ref[1:2, :])
    ht_ref[0:1, :] = carry_f
    ht_ref[1:2, :] = carry_b


def _lru_call(proj, cw, cb, wa, wi, ba, bi, lam, h0, rows):
    bsz, seq, _ = proj.shape
    width = cw.shape[1]
    ncol = width // LANES
    vec2 = pl.BlockSpec((2, LANES), lambda b, j: (0, j))
    wspec = pl.BlockSpec((2, None, LANES, LANES), lambda b, j: (0, j, 0, 0))
    return pl.pallas_call(
        functools.partial(_lru_kernel, seq=seq, rows=rows),
        grid=(bsz, ncol),
        in_specs=[pl.BlockSpec((None, seq, LANES), lambda b, j: (b, 0, j)),
                  pl.BlockSpec((None, seq, LANES), lambda b, j: (b, 0, ncol + j)),
                  pl.BlockSpec((CONV_WIDTH, LANES), lambda b, j: (0, j)),
                  pl.BlockSpec((1, LANES), lambda b, j: (0, j)),
                  wspec, wspec, vec2, vec2, vec2,
                  pl.BlockSpec((None, 2, LANES), lambda b, j: (b, 0, j))],
        out_specs=[pl.BlockSpec((None, seq, LANES), lambda b, j: (b, 0, j)),
                   pl.BlockSpec((None, 2, LANES), lambda b, j: (b, 0, j))],
        out_shape=[jax.ShapeDtypeStruct((bsz, seq, width), F32),
                   jax.ShapeDtypeStruct((bsz, 2, width), F32)],
        scratch_shapes=[pltpu.VMEM((seq + 2 * CONV_PAD, LANES), F32),
                        pltpu.VMEM((seq, LANES), F32)],
        compiler_params=_cparams(("parallel", "parallel")),
        name="lru",
    )(proj, proj, cw, cb.reshape(1, width), wa, wi, ba, bi, lam, h0)


DN_GROUP = 2 * DN_CHUNK


def _split3(x):
    x1 = x.astype(BF16)
    r1 = x - x1.astype(F32)
    x2 = r1.astype(BF16)
    x3 = (r1 - x2.astype(F32)).astype(BF16)
    return x1, x2, x3


def _delta_kernel(q_ref, k_ref, v_ref, gr_ref, grt_ref, cvec_ref, ccol_ref, cum_ref, cumt_ref,
                  ones_ref, s0_ref, o_ref, st_ref, s_ref, *, rev, rows):
    step = pl.program_id(1)
    d = 1 if rev else 0

    @pl.when(step == 0)
    def _():
        s_ref[...] = s0_ref[...]

    raw = gr_ref[...]
    sig = _sigmoid(raw)
    g = -jnp.exp(cvec_ref[0:1, :]) * _softplus(raw + cvec_ref[1:2, :])
    g1, g2, g3 = _split3(g)
    cum = cum_ref[...]
    ones = ones_ref[...]

    def mm(m, parts):
        return sum(jnp.dot(m, p, preferred_element_type=F32) for p in parts)

    gc = mm(cum, (g1, g2, g3))
    gl = mm(ones, (g1, g2, g3))
    e_gc = jnp.exp(gc)
    e_rest = jnp.exp(gl - gc)
    e_gl = jnp.exp(gl)
    gt = -jnp.exp(ccol_ref[0]) * _softplus(grt_ref[...] + ccol_ref[1])
    t1, t2, t3 = _split3(gt)
    cumt = cumt_ref[...]
    gct = sum(jnp.dot(p, cumt, preferred_element_type=F32) for p in (t1, t2, t3))

    ri = lax.broadcasted_iota(jnp.int32, (DN_GROUP, DN_GROUP), 0)
    ci = lax.broadcasted_iota(jnp.int32, (DN_GROUP, DN_GROUP), 1)
    same = (ri // DN_CHUNK) == (ci // DN_CHUNK)
    if rev:
        incl = same & (ri <= ci)
        strict = same & (ri < ci)
    else:
        incl = same & (ri >= ci)
        strict = same & (ri > ci)

    ngroups = rows // DN_GROUP
    zeros_c = jnp.zeros((DN_CHUNK, DN_DK), F32)

    def bcast(x, rsl, lane):
        return jnp.broadcast_to(x[rsl, lane:lane + 1], (DN_GROUP, LANES))

    for gi in (range(ngroups - 1, -1, -1) if rev else range(ngroups)):
        rsl = slice(gi * DN_GROUP, (gi + 1) * DN_GROUP)
        for h in range(DN_HEADS):
            lb = d * DN_HEADS + h
            lg = 2 * DN_HEADS + d * DN_HEADS + h
            csl = slice(h * DN_DK, (h + 1) * DN_DK)
            q_h = q_ref[rsl, csl]
            k_h = k_ref[rsl, csl]
            v_h = v_ref[rsl, csl]
            beta_b = bcast(sig, rsl, lb)
            egc_b = bcast(e_gc, rsl, lg)
            kb = k_h * beta_b
            vb = v_h * beta_b
            kbg = kb * egc_b
            qd = q_h * egc_b
            kd = k_h * bcast(e_rest, rsl, lg)
            egl_b = bcast(e_gl, rsl, lg)
            k16 = k_h.astype(BF16)
            kk = _bdot_nt(kb, k16)
            qk = _bdot_nt(q_h, k16)
            diff = bcast(gc, rsl, lg) - gct[lg:lg + 1, rsl]
            decay = jnp.exp(jnp.where(incl, diff, NEG_BIG))
            m = jnp.where(strict, kk * decay, 0.0)
            attn = qk * decay
            p = -m
            a = m
            npow = 1
            while npow < DN_CHUNK // 2:
                a = _bdot(a, a)
                p = p + a + _bdot(p, a)
                npow *= 2
            rhs = jnp.concatenate([vb, kbg], axis=1)
            uw = rhs + _bdot(p, rhs)
            u = uw[:, :DN_DK]
            w = uw[:, DN_DK:]
            s = s_ref[h]
            for cc in ((1, 0) if rev else (0, 1)):
                c0 = cc * DN_CHUNK
                cs = slice(c0, c0 + DN_CHUNK)
                r2 = _bdot(jnp.concatenate([w[cs], qd[cs]], axis=0), s)
                v_new = u[cs] - r2[:DN_CHUNK]
                v_pad = (jnp.concatenate([v_new, zeros_c], axis=0) if cc == 0
                         else jnp.concatenate([zeros_c, v_new], axis=0))
                o_c = r2[DN_CHUNK:] + _bdot(attn[cs], v_pad)
                egl = jnp.concatenate([egl_b[cs], egl_b[cs]], axis=0)
                s = s * egl + _bdot_tn(kd[cs], v_new)
                o_ref[gi * DN_GROUP + c0:gi * DN_GROUP + c0 + DN_CHUNK, csl] = o_c
            s_ref[h] = s

    @pl.when(step == pl.num_programs(1) - 1)
    def _():
        st_ref[...] = s_ref[...]


def _delta_consts(rows, rev):
    t = jnp.arange(rows)
    same = (t[:, None] // DN_CHUNK) == (t[None, :] // DN_CHUNK)
    cum = same & ((t[:, None] <= t[None, :]) if rev else (t[:, None] >= t[None, :]))
    return cum.astype(BF16), cum.T.astype(BF16), same.astype(BF16)


def _delta_call(qkv, proj, gate_block, gates_t, cvec, ccol, s0, rev, rows):
    bsz, seq, _ = qkv.shape
    width = DN_HEADS * DN_DK
    nsteps = seq // rows
    cum, cumt, ones = _delta_consts(rows, rev)

    def blk(i):
        return nsteps - 1 - i if rev else i

    full = lambda shape: pl.BlockSpec(shape, lambda b, i: (0,) * len(shape))
    state = pl.BlockSpec((None, DN_HEADS, DN_DK, DN_DK), lambda b, i: (b, 0, 0, 0))
    return pl.pallas_call(
        functools.partial(_delta_kernel, rev=rev, rows=rows),
        grid=(bsz, nsteps),
        in_specs=[pl.BlockSpec((None, rows, width), lambda b, i: (b, blk(i), 0)),
                  pl.BlockSpec((None, rows, width), lambda b, i: (b, blk(i), 1)),
                  pl.BlockSpec((None, rows, width), lambda b, i: (b, blk(i), 2)),
                  pl.BlockSpec((None, rows, LANES), lambda b, i: (b, blk(i), gate_block)),
                  pl.BlockSpec((None, 4 * DN_HEADS, rows), lambda b, i: (b, 0, blk(i))),
                  full((SUBLANES, LANES)),
                  full((2, 4 * DN_HEADS, rows)),
                  full((rows, rows)), full((rows, rows)), full((rows, rows)),
                  state],
        out_specs=[pl.BlockSpec((None, rows, width), lambda b, i: (b, blk(i), 0)), state],
        out_shape=[jax.ShapeDtypeStruct((bsz, seq, width), F32),
                   jax.ShapeDtypeStruct((bsz, DN_HEADS, DN_DK, DN_DK), F32)],
        scratch_shapes=[pltpu.VMEM((DN_HEADS, DN_DK, DN_DK), F32)],
        compiler_params=_cparams(("parallel", "arbitrary")),
        name="delta_bwd" if rev else "delta_fwd",
    )(qkv, qkv, qkv, proj, gates_t, cvec, ccol, cum, cumt, ones, s0)


def _evenout_kernel(lru_ref, of_ref, ob_ref, z_ref, g_ref, w_ref, res_ref, gate_ref, o_ref, a_ref):
    width = lru_ref.shape[1]
    a_ref[:, 0:width] = lru_ref[...].astype(BF16)
    for h in range(DN_HEADS):
        csl = slice(h * DN_DK, (h + 1) * DN_DK)
        o = of_ref[:, csl] + ob_ref[:, csl]
        z = z_ref[:, csl]
        y = (_rms(o) * g_ref[:, csl]) * (z * _sigmoid(z))
        a_ref[:, width + h * DN_DK:width + (h + 1) * DN_DK] = y.astype(BF16)
    o_ref[...] = res_ref[...] + gate_ref[...] * jnp.dot(a_ref[...], w_ref[...], preferred_element_type=F32)


def _evenout_call(lru, o_f, o_b, proj, z_block, dn_g, w_out, res, gate, tm):
    bsz, seq, width = lru.shape
    dnw = o_f.shape[2]
    d = res.shape[2]
    row = lambda c: pl.BlockSpec((None, tm, c), lambda b, i: (b, i, 0))
    return pl.pallas_call(
        _evenout_kernel,
        grid=(bsz, seq // tm),
        in_specs=[row(width), row(dnw), row(dnw),
                  pl.BlockSpec((None, tm, dnw), lambda b, i: (b, i, z_block)),
                  pl.BlockSpec((1, dnw), lambda b, i: (0, 0)),
                  pl.BlockSpec((width + dnw, d), lambda b, i: (0, 0)),
                  row(d),
                  pl.BlockSpec((None, 1, d), lambda b, i: (b, 0, 0))],
        out_specs=row(d),
        out_shape=jax.ShapeDtypeStruct((bsz, seq, d), F32),
        scratch_shapes=[pltpu.VMEM((tm, width + dnw), BF16)],
        compiler_params=_cparams(("parallel", "parallel")),
        name="evenout",
    )(lru, o_f, o_b, proj, dn_g, w_out, res, gate)


def _linres_kernel(a_ref, w_ref, res_ref, gate_ref, o_ref):
    o_ref[...] = res_ref[...] + gate_ref[...] * jnp.dot(a_ref[...].astype(BF16), w_ref[...],
                                                        preferred_element_type=F32)


def _linres_call(a, w, res, gate, tm):
    bsz, seq, kdim = a.shape
    d = res.shape[2]
    return pl.pallas_call(
        _linres_kernel,
        grid=(bsz, seq // tm),
        in_specs=[pl.BlockSpec((None, tm, kdim), lambda b, i: (b, i, 0)),
                  pl.BlockSpec((kdim, d), lambda b, i: (0, 0)),
                  pl.BlockSpec((None, tm, d), lambda b, i: (b, i, 0)),
                  pl.BlockSpec((None, 1, d), lambda b, i: (b, 0, 0))],
        out_specs=pl.BlockSpec((None, tm, d), lambda b, i: (b, i, 0)),
        out_shape=jax.ShapeDtypeStruct((bsz, seq, d), F32),
        compiler_params=_cparams(("parallel", "parallel")),
        name="linres",
    )(a, w, res, gate)


def _aprep_kernel(*refs, rope, want_q):
    if rope:
        x_ref, qg_ref, kg_ref, cos_ref, sin_ref = refs[:5]
        outs = refs[5:]
    else:
        x_ref, qg_ref, kg_ref = refs[:3]
        outs = refs[3:]
    if want_q:
        q_ref, k_ref, v_ref = outs
    else:
        k_ref, v_ref = outs

    if rope:
        cos = cos_ref[...]
        sin = sin_ref[...]
        lane = lax.broadcasted_iota(jnp.int32, cos.shape, 1)
        first = (lane % (HEAD_DIM // 2)) < (HEAD_DIM // 4)

    def norm_rope(x, g):
        y = _rms(x) * g
        if rope:
            partner = jnp.where(first, pltpu.roll(y, HEAD_DIM - HEAD_DIM // 4, 1),
                                pltpu.roll(y, HEAD_DIM // 4, 1))
            y = y * cos + partner * sin
        return y

    if want_q:
        for h in range(N_Q_HEADS):
            x = x_ref[:, h * HEAD_DIM:(h + 1) * HEAD_DIM]
            q_ref[h] = (norm_rope(x, qg_ref[...]) * (HEAD_DIM ** -0.5)).astype(BF16)
    for h in range(N_KV_HEADS):
        c0 = (N_Q_HEADS + h) * HEAD_DIM
        k_ref[h] = norm_rope(x_ref[:, c0:c0 + HEAD_DIM], kg_ref[...]).astype(BF16)
        c1 = (N_Q_HEADS + N_KV_HEADS + h) * HEAD_DIM
        v_ref[h] = x_ref[:, c1:c1 + HEAD_DIM].astype(BF16)


def _aprep_call(qkv, qg, kg, rope_tabs, want_q, tm):
    bsz, seq, cols = qkv.shape
    rope = rope_tabs is not None
    in_specs = [pl.BlockSpec((None, tm, cols), lambda b, i: (b, i, 0)),
                pl.BlockSpec((1, HEAD_DIM), lambda b, i: (0, 0)),
                pl.BlockSpec((1, HEAD_DIM), lambda b, i: (0, 0))]
    args = [qkv, qg.reshape(1, HEAD_DIM), kg.reshape(1, HEAD_DIM)]
    if rope:
        in_specs += [pl.BlockSpec((tm, HEAD_DIM), lambda b, i: (i, 0))] * 2
        args += list(rope_tabs)
    hspec = lambda n: pl.BlockSpec((None, n, tm, HEAD_DIM), lambda b, i: (b, 0, i, 0))
    hshape = lambda n: jax.ShapeDtypeStruct((bsz, n, seq, HEAD_DIM), BF16)
    out_specs = [hspec(N_KV_HEADS), hspec(N_KV_HEADS)]
    out_shape = [hshape(N_KV_HEADS), hshape(N_KV_HEADS)]
    if want_q:
        out_specs = [hspec(N_Q_HEADS)] + out_specs
        out_shape = [hshape(N_Q_HEADS)] + out_shape
    return pl.pallas_call(
        functools.partial(_aprep_kernel, rope=rope, want_q=want_q),
        grid=(bsz, seq // tm),
        in_specs=in_specs,
        out_specs=out_specs,
        out_shape=out_shape,
        compiler_params=_cparams(("parallel", "parallel")),
        name="aprep_x" if rope else "aprep_ctx",
    )(*args)


def _rope_tables(n_tokens):
    rows = n_tokens // GRID_W
    axis = HEAD_DIM // 2
    row_id = jnp.repeat(jnp.arange(rows), GRID_W).astype(F32)
    col_id = jnp.tile(jnp.arange(GRID_W), rows).astype(F32)
    inv_freq = ROPE_THETA ** (-jnp.arange(0, axis, 2, dtype=F32) / axis)
    ang_r = row_id[:, None] * inv_freq[None]
    ang_c = col_id[:, None] * inv_freq[None]
    cr, sr, cc, sc = jnp.cos(ang_r), jnp.sin(ang_r), jnp.cos(ang_c), jnp.sin(ang_c)
    cos = jnp.concatenate([cr, cr, cc, cc], axis=-1)
    sin = jnp.concatenate([-sr, sr, -sc, sc], axis=-1)
    return cos, sin


def _flash_kernel(q_ref, kc_ref, vc_ref, k_ref, v_ref, o_ref, m_ref, l_ref, acc_ref, *, tq):
    j = pl.program_id(3)
    q = q_ref[...].reshape(Q_PER_KV * tq, HEAD_DIM)

    def update(k, v):
        s = lax.dot_general(q, k, (((1,), (1,)), ((), ())), preferred_element_type=F32)
        m_prev = m_ref[...]
        m_new = jnp.maximum(m_prev, jnp.max(s, axis=-1, keepdims=True))
        alpha = jnp.exp(m_prev - m_new)
        p = jnp.exp(s - m_new)
        l_ref[...] = alpha * l_ref[...] + jnp.sum(p, axis=-1, keepdims=True)
        acc_ref[...] = alpha * acc_ref[...] + jnp.dot(p.astype(BF16), v, preferred_element_type=F32)
        m_ref[...] = m_new

    @pl.when(j == 0)
    def _():
        m_ref[...] = jnp.full_like(m_ref, NEG_BIG)
        l_ref[...] = jnp.zeros_like(l_ref)
        acc_ref[...] = jnp.zeros_like(acc_ref)
        update(kc_ref[...], vc_ref[...])

    update(k_ref[...], v_ref[...])

    @pl.when(j == pl.num_programs(3) - 1)
    def _():
        o = acc_ref[...] / l_ref[...]
        for r in range(Q_PER_KV):
            o_ref[:, r * HEAD_DIM:(r + 1) * HEAD_DIM] = o[r * tq:(r + 1) * tq]


def _flash_call(q, k_c, v_c, k_x, v_x, tq, tk):
    bsz, _, seq, _ = q.shape
    lc = k_c.shape[2]
    q5 = q.reshape(bsz, N_KV_HEADS, Q_PER_KV, seq, HEAD_DIM)
    kv_x = pl.BlockSpec((None, None, tk, HEAD_DIM), lambda b, g, i, j: (b, g, j, 0))
    kv_c = pl.BlockSpec((None, None, lc, HEAD_DIM), lambda b, g, i, j: (b, g, 0, 0))
    return pl.pallas_call(
        functools.partial(_flash_kernel, tq=tq),
        grid=(bsz, N_KV_HEADS, seq // tq, seq // tk),
        in_specs=[pl.BlockSpec((None, None, Q_PER_KV, tq, HEAD_DIM), lambda b, g, i, j: (b, g, 0, i, 0)),
                  kv_c, kv_c, kv_x, kv_x],
        out_specs=pl.BlockSpec((None, tq, Q_PER_KV * HEAD_DIM), lambda b, g, i, j: (b, i, g)),
        out_shape=jax.ShapeDtypeStruct((bsz, seq, N_Q_HEADS * HEAD_DIM), F32),
        scratch_shapes=[pltpu.VMEM((Q_PER_KV * tq, 1), F32),
                        pltpu.VMEM((Q_PER_KV * tq, 1), F32),
                        pltpu.VMEM((Q_PER_KV * tq, HEAD_DIM), F32)],
        compiler_params=_cparams(("parallel", "parallel", "parallel", "arbitrary")),
        name="flash",
    )(q5, k_c, v_c, k_x, v_x)


def _row_tile(seq, cap):
    t = min(seq, cap)
    while seq % t:
        t //= 2
    return t


def _even_weights(w_in, lru_wa, lru_wi, dn_alog, dn_dtb, dn_g, rows_x, rows_c):
    d, cols = w_in.shape
    ngate = 4 * DN_HEADS
    main = cols - ngate
    w_pad = jnp.concatenate([w_in[:, :main], w_in[:, main:], jnp.zeros((d, LANES - ngate), F32)], axis=1)

    def blockdiag(w):
        z = jnp.zeros_like(w[:, 0::2])
        top = jnp.concatenate([w[:, 0::2], z], axis=-1)
        bot = jnp.concatenate([z, w[:, 1::2]], axis=-1)
        return jnp.concatenate([top, bot], axis=-2).astype(BF16)

    alog = dn_alog.reshape(-1)
    dtb = dn_dtb.reshape(-1)
    cvec = jnp.zeros((SUBLANES, LANES), F32)
    cvec = cvec.at[0, 2 * DN_HEADS:ngate].set(alog).at[1, 2 * DN_HEADS:ngate].set(dtb)

    def ccol(rows):
        c = jnp.zeros((2, ngate), F32).at[0, 2 * DN_HEADS:].set(alog).at[1, 2 * DN_HEADS:].set(dtb)
        return jnp.broadcast_to(c[:, :, None], (2, ngate, rows))

    return dict(w_in=w_pad.astype(BF16), main=main, wa=blockdiag(lru_wa), wi=blockdiag(lru_wi),
                cvec=cvec, ccol_x=ccol(rows_x), ccol_c=ccol(rows_c),
                dn_g=jnp.tile(dn_g, DN_HEADS).reshape(1, -1))


def _even_stream(h_in, res, gate, norm_g, sc, sh, ew, lru_cw, lru_cb, lru_ba, lru_bi, lru_lam, dn_cw,
                 w_out, h0, s0, tm, rows, dn_rows):
    width = lru_cw.shape[1]
    proj = _normlin_call(h_in, norm_g, sc, sh, ew["w_in"], tm)
    lru, h_t = _lru_call(proj, lru_cw, lru_cb, ew["wa"], ew["wi"], lru_ba, lru_bi, lru_lam, h0, rows)
    qkv = _dnprep_call(proj, dn_cw, 2 * width // LANES, rows)
    main = ew["main"]
    gates_t = jnp.transpose(proj[:, :, main:main + 4 * DN_HEADS], (0, 2, 1))
    ccol = ew["ccol_x"] if dn_rows == ew["ccol_x"].shape[2] else ew["ccol_c"]
    outs, states = [], []
    for rev in (False, True):
        o_d, s_d = _delta_call(qkv, proj, main // LANES, gates_t, ew["cvec"], ccol,
                               s0[:, :, 1 if rev else 0], rev, dn_rows)
        outs.append(o_d)
        states.append(s_d)
    z_block = (2 * width + dn_cw.shape[1]) // (DN_HEADS * DN_DK)
    out = _evenout_call(lru, outs[0], outs[1], proj, z_block, ew["dn_g"], w_out, res, gate, tm)
    return out, h_t, jnp.stack(states, axis=2)


def kernel(x, c, ctx, c_ctx, norm_mix_g, norm_mlp_g, w_mod, b_mod, mlp_w1, mlp_w2, even_w_in, lru_conv_w,
           lru_conv_b, lru_w_a, lru_b_a, lru_w_i, lru_b_i, lru_lambda, dn_conv_w, dn_a_log, dn_dt_bias,
           dn_norm_g, even_w_out, attn_w_qkv, attn_q_norm_g, attn_k_norm_g, attn_w_o, final_norm_g):
    bsz, seq, d = x.shape
    lc = ctx.shape[1]
    depth = w_mod.shape[0]
    tm_x = _row_tile(seq, 512)
    tm_c = _row_tile(lc, 512)
    tm_mlp = _row_tile(seq, 1024)
    rows_x = _row_tile(seq, 256)
    rows_c = _row_tile(lc, 256)

    cc = jnp.zeros((SUBLANES, d), F32).at[:bsz].set(c).at[bsz].set(c_ctx)
    mod = _mod_call(cc, w_mod, b_mod).reshape(depth, SUBLANES, 6, d)

    rope = _rope_tables(seq)
    xs, cs = x, ctx
    for l in range(depth):
        last = l == depth - 1
        j = l // 2
        mx = [mod[l, :bsz, k][:, None, :] for k in range(6)]
        mc = [jnp.broadcast_to(mod[l, bsz, k][None, None, :], (bsz, 1, d)) for k in range(6)]
        sh1, sc1, g1, sh2, sc2, g2 = mx
        csh1, csc1, cg1, csh2, csc2, cg2 = mc
        w1 = mlp_w1[l].astype(BF16)
        w2 = mlp_w2[l].astype(BF16)
        if l % 2 == 0:
            ew = _even_weights(even_w_in[j], lru_w_a[j], lru_w_i[j], dn_a_log[j], dn_dt_bias[j], dn_norm_g[j],
                               rows_x, rows_c)
            w_out = even_w_out[j].astype(BF16)
            h0 = jnp.zeros((bsz, 2, lru_conv_w.shape[2]), F32)
            s0 = jnp.zeros((bsz, DN_HEADS, 2, DN_DK, DN_DK), F32)
            common = (ew, lru_conv_w[j], lru_conv_b[j], lru_b_a[j], lru_b_i[j], lru_lambda[j], dn_conv_w[j], w_out)
            cs_mix, h_t, s_t = _even_stream(cs, cs, cg1, norm_mix_g[l], csc1, csh1, *common, h0, s0,
                                            tm_c, rows_c, rows_c)
            xs, _, _ = _even_stream(xs, xs, g1, norm_mix_g[l], sc1, sh1, *common, h_t, s_t,
                                    tm_x, rows_x, rows_x)
        else:
            w_qkv = attn_w_qkv[j].astype(BF16)
            w_o = attn_w_o[j].astype(BF16)
            qkv_c = _normlin_call(cs, norm_mix_g[l], csc1, csh1, w_qkv, tm_c)
            qkv_x = _normlin_call(xs, norm_mix_g[l], sc1, sh1, w_qkv, tm_x)
            if last:
                k_c, v_c = _aprep_call(qkv_c, attn_q_norm_g[j], attn_k_norm_g[j], None, False, rows_c)
            else:
                q_c, k_c, v_c = _aprep_call(qkv_c, attn_q_norm_g[j], attn_k_norm_g[j], None, True, rows_c)
            q_x, k_x, v_x = _aprep_call(qkv_x, attn_q_norm_g[j], attn_k_norm_g[j], rope, True, rows_x)
            o_x = _flash_call(q_x, k_c, v_c, k_x, v_x, _row_tile(seq, 512), _row_tile(seq, 512))
            xs = _linres_call(o_x, w_o, xs, g1, tm_x)
            if not last:
                o_c = _flash_ctx_call(q_c, k_c, v_c)
                cs_mix = _linres_call(o_c, w_o, cs, cg1, tm_c)
        xs = _mlp_call(xs, norm_mlp_g[l], sc2, sh2, g2, w1, w2, final_norm_g, tm_mlp, last)
        if not last:
            cs = _mlp_call(cs_mix, norm_mlp_g[l], csc2, csh2, cg2, w1, w2, final_norm_g, tm_c, False)
    return xs


def _flash_ctx_kernel(q_ref, k_ref, v_ref, o_ref, *, lc):
    q = q_ref[...].reshape(Q_PER_KV * lc, HEAD_DIM)
    s = lax.dot_general(q, k_ref[...], (((1,), (1,)), ((), ())), preferred_element_type=F32)
    p = jnp.exp(s - jnp.max(s, axis=-1, keepdims=True))
    o = jnp.dot(p.astype(BF16), v_ref[...], preferred_element_type=F32) / jnp.sum(p, axis=-1, keepdims=True)
    for r in range(Q_PER_KV):
        o_ref[:, r * HEAD_DIM:(r + 1) * HEAD_DIM] = o[r * lc:(r + 1) * lc]


def _flash_ctx_call(q, k, v):
    bsz, _, lc, _ = q.shape
    q5 = q.reshape(bsz, N_KV_HEADS, Q_PER_KV, lc, HEAD_DIM)
    kv = pl.BlockSpec((None, None, lc, HEAD_DIM), lambda b, g: (b, g, 0, 0))
    return pl.pallas_call(
        functools.partial(_flash_ctx_kernel, lc=lc),
        grid=(bsz, N_KV_HEADS),
        in_specs=[pl.BlockSpec((None, None, Q_PER_KV, lc, HEAD_DIM), lambda b, g: (b, g, 0, 0, 0)), kv, kv],
        out_specs=pl.BlockSpec((None, lc, Q_PER_KV * HEAD_DIM), lambda b, g: (b, 0, g)),
        out_shape=jax.ShapeDtypeStruct((bsz, lc, N_Q_HEADS * HEAD_DIM), F32),
        compiler_params=_cparams(("parallel", "parallel")),
        name="flash_ctx",
    )(q5, k, v)
```

```python
import functools

import jax
import jax.numpy as jnp
from jax import lax
from jax.experimental import pallas as pl
from jax.experimental.pallas import tpu as pltpu

F32 = jnp.float32
BF16 = jnp.bfloat16

EPS = 1e-6
LANES = 128
SUBLANES = 8
VMEM_LIMIT = 56 * 1024 * 1024

LRU_BLOCK = 64
LRU_C = 8.0
CONV_WIDTH = 4
CONV_LEFT = 2
DN_HEADS = 8
DN_DK = 128
DN_CHUNK = 64
N_Q_HEADS = 8
N_KV_HEADS = 4
Q_PER_KV = N_Q_HEADS // N_KV_HEADS
HEAD_DIM = 128
GRID_W = 64
ROPE_THETA = 10000.0
NEG_BIG = -1e30
LOG2E = 1.4426950408889634
FLASH_TK_CAP = 768


def _cparams(sem):
    return pltpu.CompilerParams(dimension_semantics=sem, vmem_limit_bytes=VMEM_LIMIT)


def _bdot(a, b):
    return jnp.dot(a.astype(BF16), b.astype(BF16), preferred_element_type=F32)


def _bdot_nt(a, b):
    return lax.dot_general(a.astype(BF16), b.astype(BF16), (((1,), (1,)), ((), ())),
                           preferred_element_type=F32)


def _bdot_tn(a, b):
    return lax.dot_general(a.astype(BF16), b.astype(BF16), (((0,), (0,)), ((), ())),
                           preferred_element_type=F32)


def _sigmoid(x):
    return jax.nn.sigmoid(x)


def _softplus(x):
    return jnp.maximum(x, 0.0) + jnp.log1p(jnp.exp(-jnp.abs(x)))


def _rms(x):
    return x * lax.rsqrt(jnp.mean(x * x, axis=-1, keepdims=True) + EPS)


def _col_tile(n, cap):
    best = LANES
    t = LANES
    while t <= min(n, cap):
        if n % t == 0:
            best = t
        t += LANES
    return best


def _mod_kernel(c_ref, w_ref, b_ref, o_ref):
    s = c_ref[...]
    s = s * _sigmoid(s)
    o_ref[...] = _bdot(s, w_ref[...]) + b_ref[...]


def _mod_call(cc, w_mod, b_mod):
    depth, d, n = w_mod.shape
    tn = _col_tile(n, 1536)
    return pl.pallas_call(
        _mod_kernel,
        grid=(depth, n // tn),
        in_specs=[pl.BlockSpec((SUBLANES, d), lambda l, j: (0, 0)),
                  pl.BlockSpec((None, d, tn), lambda l, j: (l, 0, j)),
                  pl.BlockSpec((None, 1, tn), lambda l, j: (l, 0, j))],
        out_specs=pl.BlockSpec((None, SUBLANES, tn), lambda l, j: (l, 0, j)),
        out_shape=jax.ShapeDtypeStruct((depth, SUBLANES, n), F32),
        compiler_params=_cparams(("parallel", "parallel")),
        name="mod",
    )(cc, w_mod, b_mod.reshape(depth, 1, n))


def _normlin_kernel(x_ref, g_ref, sc_ref, sh_ref, w_ref, o_ref, h_ref):
    @pl.when(pl.program_id(2) == 0)
    def _():
        y = _rms(x_ref[...]) * g_ref[...]
        h_ref[...] = (y * (1.0 + sc_ref[...]) + sh_ref[...]).astype(BF16)

    o_ref[...] = jnp.dot(h_ref[...], w_ref[...], preferred_element_type=F32)


def _normlin_call(x, g, sc, sh, w, tm):
    bsz, seq, d = x.shape
    n = w.shape[1]
    tn = _col_tile(n, 1024)
    return pl.pallas_call(
        _normlin_kernel,
        grid=(bsz, seq // tm, n // tn),
        in_specs=[pl.BlockSpec((None, tm, d), lambda b, i, j: (b, i, 0)),
                  pl.BlockSpec((1, d), lambda b, i, j: (0, 0)),
                  pl.BlockSpec((None, 1, d), lambda b, i, j: (b, 0, 0)),
                  pl.BlockSpec((None, 1, d), lambda b, i, j: (b, 0, 0)),
                  pl.BlockSpec((d, tn), lambda b, i, j: (0, j))],
        out_specs=pl.BlockSpec((None, tm, tn), lambda b, i, j: (b, i, j)),
        out_shape=jax.ShapeDtypeStruct((bsz, seq, n), F32),
        scratch_shapes=[pltpu.VMEM((tm, d), BF16)],
        compiler_params=_cparams(("parallel", "parallel", "arbitrary")),
        name="normlin",
    )(x, g.reshape(1, d), sc, sh, w)


def _mlp_kernel(x_ref, g_ref, sc_ref, sh_ref, gate_ref, w1_ref, w2_ref, fg_ref, o_ref,
                h_ref, acc_ref, *, final_norm):
    k = pl.program_id(2)

    @pl.when(k == 0)
    def _():
        y = _rms(x_ref[...]) * g_ref[...]
        h_ref[...] = (y * (1.0 + sc_ref[...]) + sh_ref[...]).astype(BF16)
        acc_ref[...] = jnp.zeros_like(acc_ref)

    a = jnp.dot(h_ref[...], w1_ref[...], preferred_element_type=F32)
    a = jnp.square(jnp.maximum(a, 0.0)).astype(BF16)
    acc_ref[...] += jnp.dot(a, w2_ref[...], preferred_element_type=F32)

    @pl.when(k == pl.num_programs(2) - 1)
    def _():
        y = x_ref[...] + gate_ref[...] * acc_ref[...]
        if final_norm:
            y = _rms(y) * fg_ref[...]
        o_ref[...] = y


def _mlp_call(x, g, sc, sh, gate, w1, w2, fg, tm, final_norm):
    bsz, seq, d = x.shape
    f = w1.shape[1]
    tf = _col_tile(f, 1024)
    vec = pl.BlockSpec((None, 1, d), lambda b, i, k: (b, 0, 0))
    return pl.pallas_call(
        functools.partial(_mlp_kernel, final_norm=final_norm),
        grid=(bsz, seq // tm, f // tf),
        in_specs=[pl.BlockSpec((None, tm, d), lambda b, i, k: (b, i, 0)),
                  pl.BlockSpec((1, d), lambda b, i, k: (0, 0)),
                  vec, vec, vec,
                  pl.BlockSpec((d, tf), lambda b, i, k: (0, k)),
                  pl.BlockSpec((tf, d), lambda b, i, k: (k, 0)),
                  pl.BlockSpec((1, d), lambda b, i, k: (0, 0))],
        out_specs=pl.BlockSpec((None, tm, d), lambda b, i, k: (b, i, 0)),
        out_shape=jax.ShapeDtypeStruct((bsz, seq, d), F32),
        scratch_shapes=[pltpu.VMEM((tm, d), BF16), pltpu.VMEM((tm, d), F32)],
        compiler_params=_cparams(("parallel", "parallel", "arbitrary")),
        name="mlp",
    )(x, g.reshape(1, d), sc, sh, gate, w1, w2, fg.reshape(1, d))


CONV_PAD = SUBLANES


def _fill_padded(xpad_ref, x_ref, seq, rows):
    zeros = jnp.zeros((CONV_PAD, xpad_ref.shape[1]), F32)
    xpad_ref[0:CONV_PAD, :] = zeros
    xpad_ref[CONV_PAD + seq:CONV_PAD + seq + CONV_PAD, :] = zeros

    def body(t, carry):
        r0 = pl.multiple_of(t * rows, rows)
        xpad_ref[pl.ds(r0 + CONV_PAD, rows), :] = x_ref[pl.ds(r0, rows), :]
        return carry

    lax.fori_loop(0, seq // rows, body, 0)


def _conv_tile(xpad_ref, r0, rows, w):
    n = rows + 2 * CONV_PAD
    win = xpad_ref[pl.ds(r0, n), :]
    acc = None
    for j in range(CONV_WIDTH):
        off = j - CONV_LEFT
        shifted = win if off == 0 else pltpu.roll(win, (-off) % n, 0)
        term = shifted[CONV_PAD:CONV_PAD + rows] * w[j:j + 1]
        acc = term if acc is None else acc + term
    return acc


def _dnprep_kernel(x_ref, w_ref, o_ref, xpad_ref, *, seq, rows):
    j = pl.program_id(1)
    is_qk = j < 2 * DN_HEADS
    scale = jnp.where(j < DN_HEADS, DN_DK ** -0.5, 1.0).astype(F32)
    _fill_padded(xpad_ref, x_ref, seq, rows)
    w = w_ref[...]

    def body(t, carry):
        r0 = pl.multiple_of(t * rows, rows)
        y = _conv_tile(xpad_ref, r0, rows, w)
        y = y * _sigmoid(y)
        nrm = (y * lax.rsqrt(jnp.sum(y * y, axis=-1, keepdims=True) + EPS)) * scale
        o_ref[pl.ds(r0, rows), :] = jnp.where(is_qk, nrm, y)
        return carry

    lax.fori_loop(0, seq // rows, body, 0)


def _dnprep_call(proj, conv_w, col0_blocks, rows):
    bsz, seq, _ = proj.shape
    ncol = conv_w.shape[1] // LANES
    return pl.pallas_call(
        functools.partial(_dnprep_kernel, seq=seq, rows=rows),
        grid=(bsz, ncol),
        in_specs=[pl.BlockSpec((None, seq, LANES), lambda b, j: (b, 0, col0_blocks + j)),
                  pl.BlockSpec((CONV_WIDTH, LANES), lambda b, j: (0, j))],
        out_specs=pl.BlockSpec((None, seq, LANES), lambda b, j: (b, 0, j)),
        out_shape=jax.ShapeDtypeStruct((bsz, seq, ncol * LANES), F32),
        scratch_shapes=[pltpu.VMEM((seq + 2 * CONV_PAD, LANES), F32)],
        compiler_params=_cparams(("parallel", "parallel")),
        name="dnprep",
    )(proj, conv_w)


def _scan_tile(a, b, carry, rev, rows):
    row = lax.broadcasted_iota(jnp.int32, a.shape, 0) % SUBLANES
    s = 1
    while s < SUBLANES:
        if rev:
            a_s = pltpu.roll(a, rows - s, 0)
            b_s = pltpu.roll(b, rows - s, 0)
            valid = row < SUBLANES - s
        else:
            a_s = pltpu.roll(a, s, 0)
            b_s = pltpu.roll(b, s, 0)
            valid = row >= s
        b = b + a * jnp.where(valid, b_s, 0.0)
        a = a * jnp.where(valid, a_s, 1.0)
        s *= 2
    nv = rows // SUBLANES
    hs = [None] * nv
    for v in (range(nv - 1, -1, -1) if rev else range(nv)):
        sl = slice(v * SUBLANES, (v + 1) * SUBLANES)
        h = a[sl] * carry + b[sl]
        carry = h[0:1] if rev else h[SUBLANES - 1:SUBLANES]
        hs[v] = h
    return jnp.concatenate(hs, axis=0), carry


def _gelu_tanh(x):
    return 0.5 * x * (1.0 + jnp.tanh(0.7978845608028654 * (x + 0.044715 * (x * x * x))))


def _lru_kernel(x_ref, y_ref, cw_ref, cb_ref, wa_ref, wi_ref, ba_ref, bi_ref, lam_ref, h0_ref,
                o_ref, ht_ref, xpad_ref, xc_ref, *, seq, rows):
    _fill_padded(xpad_ref, x_ref, seq, rows)
    cw = cw_ref[...]
    cb = cb_ref[...]
    nt = seq // rows

    def gates(xc, d):
        r = _sigmoid(_bdot(xc, wa_ref[d]) + ba_ref[d:d + 1, :])
        i = _sigmoid(_bdot(xc, wi_ref[d]) + bi_ref[d:d + 1, :])
        log_a = (-LRU_C * r) * _softplus(-lam_ref[d:d + 1, :])
        a = jnp.exp(log_a)
        b = jnp.sqrt(-jnp.tanh(log_a) * (a * a + 1.0)) * (i * xc)
        return a, b

    def fwd_body(t, carry):
        r0 = pl.multiple_of(t * rows, rows)
        xc = _conv_tile(xpad_ref, r0, rows, cw) + cb
        xc_ref[pl.ds(r0, rows), :] = xc
        a, b = gates(xc, 0)
        h, carry = _scan_tile(a, b, carry, False, rows)
        o_ref[pl.ds(r0, rows), :] = h
        return carry

    carry_f = lax.fori_loop(0, nt, fwd_body, h0_ref[0:1, :])

    def bwd_body(t, carry):
        r0 = pl.multiple_of((nt - 1 - t) * rows, rows)
        xc = xc_ref[pl.ds(r0, rows), :]
        a, b = gates(xc, 1)
        h, carry = _scan_tile(a, b, carry, True, rows)
        o_ref[pl.ds(r0, rows), :] = (o_ref[pl.ds(r0, rows), :] + h) * _gelu_tanh(y_ref[pl.ds(r0, rows), :])
        return carry

    carry_b = lax.fori_loop(0, nt, bwd_body, h0_ref[1:2, :])
    ht_ref[0:1, :] = carry_f
    ht_ref[1:2, :] = carry_b


def _lru_call(proj, cw, cb, wa, wi, ba, bi, lam, h0, rows):
    bsz, seq, _ = proj.shape
    width = cw.shape[1]
    ncol = width // LANES
    vec2 = pl.BlockSpec((2, LANES), lambda b, j: (0, j))
    wspec = pl.BlockSpec((2, None, LANES, LANES), lambda b, j: (0, j, 0, 0))
    return pl.pallas_call(
        functools.partial(_lru_kernel, seq=seq, rows=rows),
        grid=(bsz, ncol),
        in_specs=[pl.BlockSpec((None, seq, LANES), lambda b, j: (b, 0, j)),
                  pl.BlockSpec((None, seq, LANES), lambda b, j: (b, 0, ncol + j)),
                  pl.BlockSpec((CONV_WIDTH, LANES), lambda b, j: (0, j)),
                  pl.BlockSpec((1, LANES), lambda b, j: (0, j)),
                  wspec, wspec, vec2, vec2, vec2,
                  pl.BlockSpec((None, 2, LANES), lambda b, j: (b, 0, j))],
        out_specs=[pl.BlockSpec((None, seq, LANES), lambda b, j: (b, 0, j)),
                   pl.BlockSpec((None, 2, LANES), lambda b, j: (b, 0, j))],
        out_shape=[jax.ShapeDtypeStruct((bsz, seq, width), F32),
                   jax.ShapeDtypeStruct((bsz, 2, width), F32)],
        scratch_shapes=[pltpu.VMEM((seq + 2 * CONV_PAD, LANES), F32),
                        pltpu.VMEM((seq, LANES), F32)],
        compiler_params=_cparams(("parallel", "parallel")),
        name="lru",
    )(proj, proj, cw, cb.reshape(1, width), wa, wi, ba, bi, lam, h0)


DN_GROUP = 2 * DN_CHUNK


def _split3(x):
    x1 = x.astype(BF16)
    r1 = x - x1.astype(F32)
    x2 = r1.astype(BF16)
    x3 = (r1 - x2.astype(F32)).astype(BF16)
    return x1, x2, x3


def _delta_kernel(q_ref, k_ref, v_ref, gr_ref, grt_ref, cvec_ref, ccol_ref, cum_ref, cumt_ref,
                  ones_ref, s0_ref, o_ref, st_ref, s_ref, *, rev, rows):
    step = pl.program_id(1)
    d = 1 if rev else 0

    @pl.when(step == 0)
    def _():
        s_ref[...] = s0_ref[...]

    raw = gr_ref[...]
    sig = _sigmoid(raw)
    g = -jnp.exp(cvec_ref[0:1, :]) * _softplus(raw + cvec_ref[1:2, :])
    g1, g2, g3 = _split3(g)
    cum = cum_ref[...]
    ones = ones_ref[...]

    def mm(m, parts):
        return sum(jnp.dot(m, p, preferred_element_type=F32) for p in parts)

    gc = mm(cum, (g1, g2, g3))
    gl = mm(ones, (g1, g2, g3))
    e_gc = jnp.exp(gc)
    e_rest = jnp.exp(gl - gc)
    e_gl = jnp.exp(gl)
    gt = -jnp.exp(ccol_ref[0]) * _softplus(grt_ref[...] + ccol_ref[1])
    t1, t2, t3 = _split3(gt)
    cumt = cumt_ref[...]
    gct = sum(jnp.dot(p, cumt, preferred_element_type=F32) for p in (t1, t2, t3))

    ri = lax.broadcasted_iota(jnp.int32, (DN_GROUP, DN_GROUP), 0)
    ci = lax.broadcasted_iota(jnp.int32, (DN_GROUP, DN_GROUP), 1)
    same = (ri // DN_CHUNK) == (ci // DN_CHUNK)
    if rev:
        incl = same & (ri <= ci)
        strict = same & (ri < ci)
    else:
        incl = same & (ri >= ci)
        strict = same & (ri > ci)

    ngroups = rows // DN_GROUP
    zeros_c = jnp.zeros((DN_CHUNK, DN_DK), F32)

    def bcast(x, rsl, lane):
        return jnp.broadcast_to(x[rsl, lane:lane + 1], (DN_GROUP, LANES))

    heads = range(DN_HEADS)
    for gi in (range(ngroups - 1, -1, -1) if rev else range(ngroups)):
        rsl = slice(gi * DN_GROUP, (gi + 1) * DN_GROUP)
        qd, kd, egl_b, rhs, kk, qk = [], [], [], [], [], []
        for h in heads:
            lb = d * DN_HEADS + h
            lg = 2 * DN_HEADS + d * DN_HEADS + h
            csl = slice(h * DN_DK, (h + 1) * DN_DK)
            q_h = q_ref[rsl, csl]
            k_h = k_ref[rsl, csl]
            beta_b = bcast(sig, rsl, lb)
            egc_b = bcast(e_gc, rsl, lg)
            kb_h = k_h * beta_b
            k16 = k_h.astype(BF16)
            kk.append(_bdot_nt(kb_h, k16))
            qk.append(_bdot_nt(q_h, k16))
            rhs.append(jnp.concatenate([v_ref[rsl, csl] * beta_b, kb_h * egc_b], axis=1))
            qd.append(q_h * egc_b)
            kd.append(k_h * bcast(e_rest, rsl, lg))
            egl_b.append(bcast(e_gl, rsl, lg))
        p, a16, attn = [], [], []
        for h in heads:
            lg = 2 * DN_HEADS + d * DN_HEADS + h
            diff = bcast(gc, rsl, lg) - gct[lg:lg + 1, rsl]
            decay = jnp.exp(jnp.where(incl, diff, NEG_BIG))
            m = jnp.where(strict, kk[h] * decay, 0.0)
            attn.append(qk[h] * decay)
            p.append(-m)
            a16.append(m.astype(BF16))
        npow = 1
        while npow < DN_CHUNK // 2:
            a = [jnp.dot(a16[h], a16[h], preferred_element_type=F32) for h in heads]
            a16 = [a[h].astype(BF16) for h in heads]
            p = [p[h] + a[h] + jnp.dot(p[h].astype(BF16), a16[h], preferred_element_type=F32) for h in heads]
            npow *= 2
        uw = [rhs[h] + _bdot(p[h], rhs[h]) for h in heads]
        s = [s_ref[h] for h in heads]
        for cc in ((1, 0) if rev else (0, 1)):
            c0 = cc * DN_CHUNK
            cs = slice(c0, c0 + DN_CHUNK)
            r2 = [_bdot(jnp.concatenate([uw[h][cs, DN_DK:], qd[h][cs]], axis=0), s[h]) for h in heads]
            v_new = [uw[h][cs, :DN_DK] - r2[h][:DN_CHUNK] for h in heads]
            for h in heads:
                v_pad = (jnp.concatenate([v_new[h], zeros_c], axis=0) if cc == 0
                         else jnp.concatenate([zeros_c, v_new[h]], axis=0))
                o_c = r2[h][DN_CHUNK:] + _bdot(attn[h][cs], v_pad)
                o_ref[gi * DN_GROUP + c0:gi * DN_GROUP + c0 + DN_CHUNK, h * DN_DK:(h + 1) * DN_DK] = o_c
            for h in heads:
                egl = jnp.concatenate([egl_b[h][cs], egl_b[h][cs]], axis=0)
                s[h] = s[h] * egl + _bdot_tn(kd[h][cs], v_new[h])
        for h in heads:
            s_ref[h] = s[h]

    @pl.when(step == pl.num_programs(1) - 1)
    def _():
        st_ref[...] = s_ref[...]


def _delta_consts(rows, rev):
    t = jnp.arange(rows)
    same = (t[:, None] // DN_CHUNK) == (t[None, :] // DN_CHUNK)
    cum = same & ((t[:, None] <= t[None, :]) if rev else (t[:, None] >= t[None, :]))
    return cum.astype(BF16), cum.T.astype(BF16), same.astype(BF16)


def _delta_call(qkv, proj, gate_block, gates_t, cvec, ccol, s0, rev, rows):
    bsz, seq, _ = qkv.shape
    width = DN_HEADS * DN_DK
    nsteps = seq // rows
    cum, cumt, ones = _delta_consts(rows, rev)

    def blk(i):
        return nsteps - 1 - i if rev else i

    full = lambda shape: pl.BlockSpec(shape, lambda b, i: (0,) * len(shape))
    state = pl.BlockSpec((None, DN_HEADS, DN_DK, DN_DK), lambda b, i: (b, 0, 0, 0))
    return pl.pallas_call(
        functools.partial(_delta_kernel, rev=rev, rows=rows),
        grid=(bsz, nsteps),
        in_specs=[pl.BlockSpec((None, rows, width), lambda b, i: (b, blk(i), 0)),
                  pl.BlockSpec((None, rows, width), lambda b, i: (b, blk(i), 1)),
                  pl.BlockSpec((None, rows, width), lambda b, i: (b, blk(i), 2)),
                  pl.BlockSpec((None, rows, LANES), lambda b, i: (b, blk(i), gate_block)),
                  pl.BlockSpec((None, 4 * DN_HEADS, rows), lambda b, i: (b, 0, blk(i))),
                  full((SUBLANES, LANES)),
                  full((2, 4 * DN_HEADS, rows)),
                  full((rows, rows)), full((rows, rows)), full((rows, rows)),
                  state],
        out_specs=[pl.BlockSpec((None, rows, width), lambda b, i: (b, blk(i), 0)), state],
        out_shape=[jax.ShapeDtypeStruct((bsz, seq, width), F32),
                   jax.ShapeDtypeStruct((bsz, DN_HEADS, DN_DK, DN_DK), F32)],
        scratch_shapes=[pltpu.VMEM((DN_HEADS, DN_DK, DN_DK), F32)],
        compiler_params=_cparams(("parallel", "arbitrary")),
        name="delta_bwd" if rev else "delta_fwd",
    )(qkv, qkv, qkv, proj, gates_t, cvec, ccol, cum, cumt, ones, s0)


def _evenout_kernel(lru_ref, of_ref, ob_ref, z_ref, g_ref, w_ref, res_ref, gate_ref, o_ref, a_ref):
    width = lru_ref.shape[1]
    a_ref[:, 0:width] = lru_ref[...].astype(BF16)
    for h in range(DN_HEADS):
        csl = slice(h * DN_DK, (h + 1) * DN_DK)
        o = of_ref[:, csl] + ob_ref[:, csl]
        z = z_ref[:, csl]
        y = (_rms(o) * g_ref[:, csl]) * (z * _sigmoid(z))
        a_ref[:, width + h * DN_DK:width + (h + 1) * DN_DK] = y.astype(BF16)
    o_ref[...] = res_ref[...] + gate_ref[...] * jnp.dot(a_ref[...], w_ref[...], preferred_element_type=F32)


def _evenout_call(lru, o_f, o_b, proj, z_block, dn_g, w_out, res, gate, tm):
    bsz, seq, width = lru.shape
    dnw = o_f.shape[2]
    d = res.shape[2]
    row = lambda c: pl.BlockSpec((None, tm, c), lambda b, i: (b, i, 0))
    return pl.pallas_call(
        _evenout_kernel,
        grid=(bsz, seq // tm),
        in_specs=[row(width), row(dnw), row(dnw),
                  pl.BlockSpec((None, tm, dnw), lambda b, i: (b, i, z_block)),
                  pl.BlockSpec((1, dnw), lambda b, i: (0, 0)),
                  pl.BlockSpec((width + dnw, d), lambda b, i: (0, 0)),
                  row(d),
                  pl.BlockSpec((None, 1, d), lambda b, i: (b, 0, 0))],
        out_specs=row(d),
        out_shape=jax.ShapeDtypeStruct((bsz, seq, d), F32),
        scratch_shapes=[pltpu.VMEM((tm, width + dnw), BF16)],
        compiler_params=_cparams(("parallel", "parallel")),
        name="evenout",
    )(lru, o_f, o_b, proj, dn_g, w_out, res, gate)


def _linres_kernel(a_ref, w_ref, res_ref, gate_ref, o_ref):
    o_ref[...] = res_ref[...] + gate_ref[...] * jnp.dot(a_ref[...].astype(BF16), w_ref[...],
                                                        preferred_element_type=F32)


def _linres_call(a, w, res, gate, tm):
    bsz, seq, kdim = a.shape
    d = res.shape[2]
    return pl.pallas_call(
        _linres_kernel,
        grid=(bsz, seq // tm),
        in_specs=[pl.BlockSpec((None, tm, kdim), lambda b, i: (b, i, 0)),
                  pl.BlockSpec((kdim, d), lambda b, i: (0, 0)),
                  pl.BlockSpec((None, tm, d), lambda b, i: (b, i, 0)),
                  pl.BlockSpec((None, 1, d), lambda b, i: (b, 0, 0))],
        out_specs=pl.BlockSpec((None, tm, d), lambda b, i: (b, i, 0)),
        out_shape=jax.ShapeDtypeStruct((bsz, seq, d), F32),
        compiler_params=_cparams(("parallel", "parallel")),
        name="linres",
    )(a, w, res, gate)


def _aprep_kernel(*refs, rope, want_q):
    if rope:
        x_ref, qg_ref, kg_ref, cos_ref, sin_ref = refs[:5]
        outs = refs[5:]
    else:
        x_ref, qg_ref, kg_ref = refs[:3]
        outs = refs[3:]
    if want_q:
        q_ref, k_ref, v_ref = outs
    else:
        k_ref, v_ref = outs

    if rope:
        cos = cos_ref[...]
        sin = sin_ref[...]
        lane = lax.broadcasted_iota(jnp.int32, cos.shape, 1)
        first = (lane % (HEAD_DIM // 2)) < (HEAD_DIM // 4)

    def norm_rope(x, g):
        y = _rms(x) * g
        if rope:
            partner = jnp.where(first, pltpu.roll(y, HEAD_DIM - HEAD_DIM // 4, 1),
                                pltpu.roll(y, HEAD_DIM // 4, 1))
            y = y * cos + partner * sin
        return y

    if want_q:
        for h in range(N_Q_HEADS):
            x = x_ref[:, h * HEAD_DIM:(h + 1) * HEAD_DIM]
            q_ref[h] = (norm_rope(x, qg_ref[...]) * (HEAD_DIM ** -0.5 * LOG2E)).T.astype(BF16)
    for h in range(N_KV_HEADS):
        c0 = (N_Q_HEADS + h) * HEAD_DIM
        k_ref[h] = norm_rope(x_ref[:, c0:c0 + HEAD_DIM], kg_ref[...]).astype(BF16)
        c1 = (N_Q_HEADS + N_KV_HEADS + h) * HEAD_DIM
        v_ref[h] = x_ref[:, c1:c1 + HEAD_DIM].T.astype(BF16)


def _aprep_call(qkv, qg, kg, rope_tabs, want_q, tm):
    bsz, seq, cols = qkv.shape
    rope = rope_tabs is not None
    in_specs = [pl.BlockSpec((None, tm, cols), lambda b, i: (b, i, 0)),
                pl.BlockSpec((1, HEAD_DIM), lambda b, i: (0, 0)),
                pl.BlockSpec((1, HEAD_DIM), lambda b, i: (0, 0))]
    args = [qkv, qg.reshape(1, HEAD_DIM), kg.reshape(1, HEAD_DIM)]
    if rope:
        in_specs += [pl.BlockSpec((tm, HEAD_DIM), lambda b, i: (i, 0))] * 2
        args += list(rope_tabs)
    hspec = lambda n: pl.BlockSpec((None, n, tm, HEAD_DIM), lambda b, i: (b, 0, i, 0))
    hshape = lambda n: jax.ShapeDtypeStruct((bsz, n, seq, HEAD_DIM), BF16)
    tspec = lambda n: pl.BlockSpec((None, n, HEAD_DIM, tm), lambda b, i: (b, 0, 0, i))
    tshape = lambda n: jax.ShapeDtypeStruct((bsz, n, HEAD_DIM, seq), BF16)
    out_specs = [hspec(N_KV_HEADS), tspec(N_KV_HEADS)]
    out_shape = [hshape(N_KV_HEADS), tshape(N_KV_HEADS)]
    if want_q:
        out_specs = [tspec(N_Q_HEADS)] + out_specs
        out_shape = [tshape(N_Q_HEADS)] + out_shape
    return pl.pallas_call(
        functools.partial(_aprep_kernel, rope=rope, want_q=want_q),
        grid=(bsz, seq // tm),
        in_specs=in_specs,
        out_specs=out_specs,
        out_shape=out_shape,
        compiler_params=_cparams(("parallel", "parallel")),
        name="aprep_x" if rope else "aprep_ctx",
    )(*args)


def _rope_tables(n_tokens):
    rows = n_tokens // GRID_W
    axis = HEAD_DIM // 2
    row_id = jnp.repeat(jnp.arange(rows), GRID_W).astype(F32)
    col_id = jnp.tile(jnp.arange(GRID_W), rows).astype(F32)
    inv_freq = ROPE_THETA ** (-jnp.arange(0, axis, 2, dtype=F32) / axis)
    ang_r = row_id[:, None] * inv_freq[None]
    ang_c = col_id[:, None] * inv_freq[None]
    cr, sr, cc, sc = jnp.cos(ang_r), jnp.sin(ang_r), jnp.cos(ang_c), jnp.sin(ang_c)
    cos = jnp.concatenate([cr, cr, cc, cc], axis=-1)
    sin = jnp.concatenate([-sr, sr, -sc, sc], axis=-1)
    return cos, sin


def _flash_kernel(qt_ref, k_ref, vt_ref, o_ref, sa_ref, sb_ref, *, tq, tk, nk):
    m_rows = Q_PER_KV * tq
    qt = jnp.concatenate([qt_ref[r] for r in range(Q_PER_KV)], axis=1)

    def scores(j):
        c0 = pl.multiple_of(j * tk, tk)
        return jnp.dot(k_ref[pl.ds(c0, tk), :], qt, preferred_element_type=F32)

    def update(j, st, carry):
        m, l, acc = carry
        m_new = jnp.maximum(m, jnp.max(st, axis=0, keepdims=True))
        alpha = jnp.exp2(m - m_new)
        p = jnp.exp2(st - m_new)
        l = alpha * l + jnp.sum(p, axis=0, keepdims=True)
        acc = alpha * acc + jnp.dot(vt_ref[j], p.astype(BF16), preferred_element_type=F32)
        return m_new, l, acc

    sa_ref[...] = scores(0)

    def body(jj, carry):
        j = 2 * jj
        sb_ref[...] = scores(j + 1)
        carry = update(j, sa_ref[...], carry)
        sa_ref[...] = scores(j + 2)
        return update(j + 1, sb_ref[...], carry)

    init = (jnp.full((1, m_rows), NEG_BIG, F32), jnp.zeros((1, m_rows), F32),
            jnp.zeros((HEAD_DIM, m_rows), F32))
    npairs = (nk - 1) // 2
    carry = lax.fori_loop(0, npairs, body, init)
    if nk - 2 * npairs == 2:
        sb_ref[...] = scores(nk - 1)
        carry = update(nk - 2, sa_ref[...], carry)
        _, l, acc = update(nk - 1, sb_ref[...], carry)
    else:
        _, l, acc = update(nk - 1, sa_ref[...], carry)
    o = (acc / l).T
    for r in range(Q_PER_KV):
        o_ref[:, r * HEAD_DIM:(r + 1) * HEAD_DIM] = o[r * tq:(r + 1) * tq]


def _flash_call(qt, k, vt, tq, tk):
    bsz, _, _, seq = qt.shape
    lk = k.shape[2]
    nk = lk // tk
    qt5 = qt.reshape(bsz, N_KV_HEADS, Q_PER_KV, HEAD_DIM, seq)
    vt5 = jnp.transpose(vt.reshape(bsz, N_KV_HEADS, HEAD_DIM, nk, tk), (0, 1, 3, 2, 4))
    return pl.pallas_call(
        functools.partial(_flash_kernel, tq=tq, tk=tk, nk=nk),
        grid=(bsz, N_KV_HEADS, seq // tq),
        in_specs=[pl.BlockSpec((None, None, Q_PER_KV, HEAD_DIM, tq), lambda b, g, i: (b, g, 0, 0, i)),
                  pl.BlockSpec((None, None, lk, HEAD_DIM), lambda b, g, i: (b, g, 0, 0)),
                  pl.BlockSpec((None, None, nk, HEAD_DIM, tk), lambda b, g, i: (b, g, 0, 0, 0))],
        out_specs=pl.BlockSpec((None, tq, Q_PER_KV * HEAD_DIM), lambda b, g, i: (b, i, g)),
        out_shape=jax.ShapeDtypeStruct((bsz, seq, N_Q_HEADS * HEAD_DIM), F32),
        scratch_shapes=[pltpu.VMEM((tk, Q_PER_KV * tq), F32)] * 2,
        compiler_params=_cparams(("parallel", "parallel", "parallel")),
        name="flash",
    )(qt5, k, vt5)


def _row_tile(seq, cap):
    t = min(seq, cap)
    while seq % t:
        t //= 2
    return t


def _even_weights(w_in, lru_wa, lru_wi, dn_alog, dn_dtb, dn_g, rows_x, rows_c):
    d, cols = w_in.shape
    ngate = 4 * DN_HEADS
    main = cols - ngate
    w_pad = jnp.concatenate([w_in[:, :main], w_in[:, main:], jnp.zeros((d, LANES - ngate), F32)], axis=1)

    def blockdiag(w):
        z = jnp.zeros_like(w[:, 0::2])
        top = jnp.concatenate([w[:, 0::2], z], axis=-1)
        bot = jnp.concatenate([z, w[:, 1::2]], axis=-1)
        return jnp.concatenate([top, bot], axis=-2).astype(BF16)

    alog = dn_alog.reshape(-1)
    dtb = dn_dtb.reshape(-1)
    cvec = jnp.zeros((SUBLANES, LANES), F32)
    cvec = cvec.at[0, 2 * DN_HEADS:ngate].set(alog).at[1, 2 * DN_HEADS:ngate].set(dtb)

    def ccol(rows):
        c = jnp.zeros((2, ngate), F32).at[0, 2 * DN_HEADS:].set(alog).at[1, 2 * DN_HEADS:].set(dtb)
        return jnp.broadcast_to(c[:, :, None], (2, ngate, rows))

    return dict(w_in=w_pad.astype(BF16), main=main, wa=blockdiag(lru_wa), wi=blockdiag(lru_wi),
                cvec=cvec, ccol_x=ccol(rows_x), ccol_c=ccol(rows_c),
                dn_g=jnp.tile(dn_g, DN_HEADS).reshape(1, -1))


def _even_stream(h_in, res, gate, norm_g, sc, sh, ew, lru_cw, lru_cb, lru_ba, lru_bi, lru_lam, dn_cw,
                 w_out, h0, s0, tm, rows, dn_rows):
    width = lru_cw.shape[1]
    proj = _normlin_call(h_in, norm_g, sc, sh, ew["w_in"], tm)
    lru, h_t = _lru_call(proj, lru_cw, lru_cb, ew["wa"], ew["wi"], lru_ba, lru_bi, lru_lam, h0, rows)
    qkv = _dnprep_call(proj, dn_cw, 2 * width // LANES, rows)
    main = ew["main"]
    gates_t = jnp.transpose(proj[:, :, main:main + 4 * DN_HEADS], (0, 2, 1))
    ccol = ew["ccol_x"] if dn_rows == ew["ccol_x"].shape[2] else ew["ccol_c"]
    outs, states = [], []
    for rev in (False, True):
        o_d, s_d = _delta_call(qkv, proj, main // LANES, gates_t, ew["cvec"], ccol,
                               s0[:, :, 1 if rev else 0], rev, dn_rows)
        outs.append(o_d)
        states.append(s_d)
    z_block = (2 * width + dn_cw.shape[1]) // (DN_HEADS * DN_DK)
    out = _evenout_call(lru, outs[0], outs[1], proj, z_block, ew["dn_g"], w_out, res, gate, tm)
    return out, h_t, jnp.stack(states, axis=2)


def kernel(x, c, ctx, c_ctx, norm_mix_g, norm_mlp_g, w_mod, b_mod, mlp_w1, mlp_w2, even_w_in, lru_conv_w,
           lru_conv_b, lru_w_a, lru_b_a, lru_w_i, lru_b_i, lru_lambda, dn_conv_w, dn_a_log, dn_dt_bias,
           dn_norm_g, even_w_out, attn_w_qkv, attn_q_norm_g, attn_k_norm_g, attn_w_o, final_norm_g):
    bsz, seq, d = x.shape
    lc = ctx.shape[1]
    depth = w_mod.shape[0]
    tm_x = _row_tile(seq, 512)
    tm_c = _row_tile(lc, 512)
    tm_mlp = _row_tile(seq, 1024)
    rows_x = _row_tile(seq, 256)
    rows_c = _row_tile(lc, 256)

    cc = jnp.zeros((SUBLANES, d), F32).at[:bsz].set(c).at[bsz].set(c_ctx)
    mod = _mod_call(cc, w_mod, b_mod).reshape(depth, SUBLANES, 6, d)

    rope = _rope_tables(seq)
    xs, cs = x, ctx
    for l in range(depth):
        last = l == depth - 1
        j = l // 2
        mx = [mod[l, :bsz, k][:, None, :] for k in range(6)]
        mc = [jnp.broadcast_to(mod[l, bsz, k][None, None, :], (bsz, 1, d)) for k in range(6)]
        sh1, sc1, g1, sh2, sc2, g2 = mx
        csh1, csc1, cg1, csh2, csc2, cg2 = mc
        w1 = mlp_w1[l].astype(BF16)
        w2 = mlp_w2[l].astype(BF16)
        if l % 2 == 0:
            ew = _even_weights(even_w_in[j], lru_w_a[j], lru_w_i[j], dn_a_log[j], dn_dt_bias[j], dn_norm_g[j],
                               rows_x, rows_c)
            w_out = even_w_out[j].astype(BF16)
            h0 = jnp.zeros((bsz, 2, lru_conv_w.shape[2]), F32)
            s0 = jnp.zeros((bsz, DN_HEADS, 2, DN_DK, DN_DK), F32)
            common = (ew, lru_conv_w[j], lru_conv_b[j], lru_b_a[j], lru_b_i[j], lru_lambda[j], dn_conv_w[j], w_out)
            cs_mix, h_t, s_t = _even_stream(cs, cs, cg1, norm_mix_g[l], csc1, csh1, *common, h0, s0,
                                            tm_c, rows_c, rows_c)
            xs, _, _ = _even_stream(xs, xs, g1, norm_mix_g[l], sc1, sh1, *common, h_t, s_t,
                                    tm_x, rows_x, rows_x)
        else:
            w_qkv = attn_w_qkv[j].astype(BF16)
            w_o = attn_w_o[j].astype(BF16)
            qkv_c = _normlin_call(cs, norm_mix_g[l], csc1, csh1, w_qkv, tm_c)
            qkv_x = _normlin_call(xs, norm_mix_g[l], sc1, sh1, w_qkv, tm_x)
            if last:
                k_c, vt_c = _aprep_call(qkv_c, attn_q_norm_g[j], attn_k_norm_g[j], None, False, rows_c)
            else:
                qt_c, k_c, vt_c = _aprep_call(qkv_c, attn_q_norm_g[j], attn_k_norm_g[j], None, True, rows_c)
            qt_x, k_x, vt_x = _aprep_call(qkv_x, attn_q_norm_g[j], attn_k_norm_g[j], rope, True, rows_x)
            k_all = jnp.concatenate([k_c, k_x], axis=2)
            vt_all = jnp.concatenate([vt_c, vt_x], axis=3)
            o_x = _flash_call(qt_x, k_all, vt_all, rows_x, _col_tile(lc + seq, FLASH_TK_CAP))
            xs = _linres_call(o_x, w_o, xs, g1, tm_x)
            if not last:
                o_c = _flash_call(qt_c, k_c, vt_c, rows_c, _col_tile(lc, FLASH_TK_CAP))
                cs_mix = _linres_call(o_c, w_o, cs, cg1, tm_c)
        xs = _mlp_call(xs, norm_mlp_g[l], sc2, sh2, g2, w1, w2, final_norm_g, tm_mlp, last)
        if not last:
            cs = _mlp_call(cs_mix, norm_mlp_g[l], csc2, csh2, cg2, w1, w2, final_norm_g, tm_c, False)
    return xs
```

```python
import functools

import jax
import jax.numpy as jnp
from jax import lax
from jax.experimental import pallas as pl
from jax.experimental.pallas import tpu as pltpu

F32 = jnp.float32
BF16 = jnp.bfloat16

EPS = 1e-6
LANES = 128
SUBLANES = 8
VMEM_LIMIT = 56 * 1024 * 1024

LRU_BLOCK = 64
LRU_C = 8.0
CONV_WIDTH = 4
CONV_LEFT = 2
DN_HEADS = 8
DN_DK = 128
DN_CHUNK = 64
N_Q_HEADS = 8
N_KV_HEADS = 4
Q_PER_KV = N_Q_HEADS // N_KV_HEADS
HEAD_DIM = 128
GRID_W = 64
ROPE_THETA = 10000.0
NEG_BIG = -1e30
LOG2E = 1.4426950408889634
FLASH_TK_CAP = 768
FLASH_UNROLL = 4


def _cparams(sem):
    return pltpu.CompilerParams(dimension_semantics=sem, vmem_limit_bytes=VMEM_LIMIT)


def _bdot(a, b):
    return jnp.dot(a.astype(BF16), b.astype(BF16), preferred_element_type=F32)


def _bdot_nt(a, b):
    return lax.dot_general(a.astype(BF16), b.astype(BF16), (((1,), (1,)), ((), ())),
                           preferred_element_type=F32)


def _bdot_tn(a, b):
    return lax.dot_general(a.astype(BF16), b.astype(BF16), (((0,), (0,)), ((), ())),
                           preferred_element_type=F32)


def _sigmoid(x):
    return jax.nn.sigmoid(x)


def _softplus(x):
    return jnp.maximum(x, 0.0) + jnp.log1p(jnp.exp(-jnp.abs(x)))


def _rms(x):
    return x * lax.rsqrt(jnp.mean(x * x, axis=-1, keepdims=True) + EPS)


def _col_tile(n, cap):
    best = LANES
    t = LANES
    while t <= min(n, cap):
        if n % t == 0:
            best = t
        t += LANES
    return best


def _mod_kernel(c_ref, w_ref, b_ref, o_ref):
    s = c_ref[...]
    s = s * _sigmoid(s)
    o_ref[...] = _bdot(s, w_ref[...]) + b_ref[...]


def _mod_call(cc, w_mod, b_mod):
    depth, d, n = w_mod.shape
    tn = _col_tile(n, 1536)
    return pl.pallas_call(
        _mod_kernel,
        grid=(depth, n // tn),
        in_specs=[pl.BlockSpec((SUBLANES, d), lambda l, j: (0, 0)),
                  pl.BlockSpec((None, d, tn), lambda l, j: (l, 0, j)),
                  pl.BlockSpec((None, 1, tn), lambda l, j: (l, 0, j))],
        out_specs=pl.BlockSpec((None, SUBLANES, tn), lambda l, j: (l, 0, j)),
        out_shape=jax.ShapeDtypeStruct((depth, SUBLANES, n), F32),
        compiler_params=_cparams(("parallel", "parallel")),
        name="mod",
    )(cc, w_mod, b_mod.reshape(depth, 1, n))


def _normlin_kernel(*refs, has_gates):
    if has_gates:
        x_ref, g_ref, sc_ref, sh_ref, w_ref, wg_ref, o_ref, og_ref, h_ref = refs
    else:
        x_ref, g_ref, sc_ref, sh_ref, w_ref, o_ref, h_ref = refs

    @pl.when(pl.program_id(2) == 0)
    def _():
        y = _rms(x_ref[...]) * g_ref[...]
        h_ref[...] = (y * (1.0 + sc_ref[...]) + sh_ref[...]).astype(BF16)
        if has_gates:
            og_ref[...] = jnp.dot(h_ref[...], wg_ref[...], preferred_element_type=F32)

    o_ref[...] = jnp.dot(h_ref[...], w_ref[...], preferred_element_type=F32).astype(o_ref.dtype)


def _normlin_call(x, g, sc, sh, w, w_gates, tm):
    bsz, seq, d = x.shape
    n = w.shape[1]
    tn = _col_tile(n, 1024)
    has_gates = w_gates is not None
    in_specs = [pl.BlockSpec((None, tm, d), lambda b, i, j: (b, i, 0)),
                pl.BlockSpec((1, d), lambda b, i, j: (0, 0)),
                pl.BlockSpec((None, 1, d), lambda b, i, j: (b, 0, 0)),
                pl.BlockSpec((None, 1, d), lambda b, i, j: (b, 0, 0)),
                pl.BlockSpec((d, tn), lambda b, i, j: (0, j))]
    out_specs = [pl.BlockSpec((None, tm, tn), lambda b, i, j: (b, i, j))]
    out_shape = [jax.ShapeDtypeStruct((bsz, seq, n), BF16)]
    args = [x, g.reshape(1, d), sc, sh, w]
    if has_gates:
        ng = w_gates.shape[1]
        in_specs.append(pl.BlockSpec((d, ng), lambda b, i, j: (0, 0)))
        out_specs.append(pl.BlockSpec((None, tm, ng), lambda b, i, j: (b, i, 0)))
        out_shape.append(jax.ShapeDtypeStruct((bsz, seq, ng), F32))
        args.append(w_gates)
    out = pl.pallas_call(
        functools.partial(_normlin_kernel, has_gates=has_gates),
        grid=(bsz, seq // tm, n // tn),
        in_specs=in_specs,
        out_specs=out_specs,
        out_shape=out_shape,
        scratch_shapes=[pltpu.VMEM((tm, d), BF16)],
        compiler_params=_cparams(("parallel", "parallel", "arbitrary")),
        name="normlin",
    )(*args)
    return out if has_gates else out[0]


def _mlp_kernel(x_ref, g_ref, sc_ref, sh_ref, gate_ref, w1_ref, w2_ref, fg_ref, o_ref,
                h_ref, acc_ref, *, final_norm):
    k = pl.program_id(2)

    @pl.when(k == 0)
    def _():
        y = _rms(x_ref[...]) * g_ref[...]
        h_ref[...] = (y * (1.0 + sc_ref[...]) + sh_ref[...]).astype(BF16)
        acc_ref[...] = jnp.zeros_like(acc_ref)

    a = jnp.dot(h_ref[...], w1_ref[...], preferred_element_type=F32)
    a = jnp.square(jnp.maximum(a, 0.0)).astype(BF16)
    acc_ref[...] += jnp.dot(a, w2_ref[...], preferred_element_type=F32)

    @pl.when(k == pl.num_programs(2) - 1)
    def _():
        y = x_ref[...] + gate_ref[...] * acc_ref[...]
        if final_norm:
            y = _rms(y) * fg_ref[...]
        o_ref[...] = y


def _mlp_call(x, g, sc, sh, gate, w1, w2, fg, tm, final_norm):
    bsz, seq, d = x.shape
    f = w1.shape[1]
    tf = _col_tile(f, 1024)
    vec = pl.BlockSpec((None, 1, d), lambda b, i, k: (b, 0, 0))
    return pl.pallas_call(
        functools.partial(_mlp_kernel, final_norm=final_norm),
        grid=(bsz, seq // tm, f // tf),
        in_specs=[pl.BlockSpec((None, tm, d), lambda b, i, k: (b, i, 0)),
                  pl.BlockSpec((1, d), lambda b, i, k: (0, 0)),
                  vec, vec, vec,
                  pl.BlockSpec((d, tf), lambda b, i, k: (0, k)),
                  pl.BlockSpec((tf, d), lambda b, i, k: (k, 0)),
                  pl.BlockSpec((1, d), lambda b, i, k: (0, 0))],
        out_specs=pl.BlockSpec((None, tm, d), lambda b, i, k: (b, i, 0)),
        out_shape=jax.ShapeDtypeStruct((bsz, seq, d), F32),
        scratch_shapes=[pltpu.VMEM((tm, d), BF16), pltpu.VMEM((tm, d), F32)],
        compiler_params=_cparams(("parallel", "parallel", "arbitrary")),
        name="mlp",
    )(x, g.reshape(1, d), sc, sh, gate, w1, w2, fg.reshape(1, d))


CONV_PAD = SUBLANES


def _fill_padded(xpad_ref, x_ref, seq, rows):
    zeros = jnp.zeros((CONV_PAD, xpad_ref.shape[1]), F32)
    xpad_ref[0:CONV_PAD, :] = zeros
    xpad_ref[CONV_PAD + seq:CONV_PAD + seq + CONV_PAD, :] = zeros

    def body(t, carry):
        r0 = pl.multiple_of(t * rows, rows)
        xpad_ref[pl.ds(r0 + CONV_PAD, rows), :] = x_ref[pl.ds(r0, rows), :].astype(F32)
        return carry

    lax.fori_loop(0, seq // rows, body, 0)


def _conv_tile(xpad_ref, r0, rows, w):
    acc = None
    for j in range(CONV_WIDTH):
        off = j - CONV_LEFT
        term = xpad_ref[pl.ds(r0 + (CONV_PAD + off), rows), :] * w[j:j + 1]
        acc = term if acc is None else acc + term
    return acc


def _dnprep_kernel(x_ref, w_ref, o_ref, xpad_ref, *, seq, rows):
    j = pl.program_id(1)
    is_qk = j < 2 * DN_HEADS
    scale = jnp.where(j < DN_HEADS, DN_DK ** -0.5, 1.0).astype(F32)
    _fill_padded(xpad_ref, x_ref, seq, rows)
    w = w_ref[...]

    def body(t, carry):
        r0 = pl.multiple_of(t * rows, rows)
        y = _conv_tile(xpad_ref, r0, rows, w)
        y = y * _sigmoid(y)
        nrm = (y * lax.rsqrt(jnp.sum(y * y, axis=-1, keepdims=True) + EPS)) * scale
        o_ref[pl.ds(r0, rows), :] = jnp.where(is_qk, nrm, y).astype(o_ref.dtype)
        return carry

    lax.fori_loop(0, seq // rows, body, 0)


def _dnprep_call(proj, conv_w, col0_blocks, rows):
    bsz, seq, _ = proj.shape
    ncol = conv_w.shape[1] // LANES
    return pl.pallas_call(
        functools.partial(_dnprep_kernel, seq=seq, rows=rows),
        grid=(bsz, ncol),
        in_specs=[pl.BlockSpec((None, seq, LANES), lambda b, j: (b, 0, col0_blocks + j)),
                  pl.BlockSpec((CONV_WIDTH, LANES), lambda b, j: (0, j))],
        out_specs=pl.BlockSpec((None, seq, LANES), lambda b, j: (b, 0, j)),
        out_shape=jax.ShapeDtypeStruct((bsz, seq, ncol * LANES), BF16),
        scratch_shapes=[pltpu.VMEM((seq + 2 * CONV_PAD, LANES), F32)],
        compiler_params=_cparams(("parallel", "parallel")),
        name="dnprep",
    )(proj, conv_w)


def _scan_tile(a, b, carry, rev, rows):
    nv = rows // SUBLANES
    lanes = a.shape[1]
    a = a.reshape(nv, SUBLANES, lanes)
    b = b.reshape(nv, SUBLANES, lanes)
    row = lax.broadcasted_iota(jnp.int32, a.shape, 1)
    s = 1
    while s < SUBLANES:
        shift = SUBLANES - s if rev else s
        a_s = pltpu.roll(a, shift, 1)
        b_s = pltpu.roll(b, shift, 1)
        valid = (row < SUBLANES - s) if rev else (row >= s)
        b = b + a * jnp.where(valid, b_s, 0.0)
        a = a * jnp.where(valid, a_s, 1.0)
        s *= 2
    hs = [None] * nv
    for v in (range(nv - 1, -1, -1) if rev else range(nv)):
        h = a[v] * carry + b[v]
        carry = h[0:1] if rev else h[SUBLANES - 1:SUBLANES]
        hs[v] = h
    return jnp.concatenate(hs, axis=0), carry


def _gelu_tanh(x):
    return 0.5 * x * (1.0 + jnp.tanh(0.7978845608028654 * (x + 0.044715 * (x * x * x))))


def _lru_kernel(x_ref, y_ref, cw_ref, cb_ref, wa_ref, wi_ref, ba_ref, bi_ref, lam_ref, h0_ref,
                o_ref, ht_ref, xpad_ref, xc_ref, hf_ref, *, seq, rows):
    _fill_padded(xpad_ref, x_ref, seq, rows)
    cw = cw_ref[...]
    cb = cb_ref[...]
    nt = seq // rows

    def gates(xc, d):
        r = _sigmoid(_bdot(xc, wa_ref[d]) + ba_ref[d:d + 1, :])
        i = _sigmoid(_bdot(xc, wi_ref[d]) + bi_ref[d:d + 1, :])
        log_a = (-LRU_C * r) * _softplus(-lam_ref[d:d + 1, :])
        a = jnp.exp(log_a)
        b = jnp.sqrt(-jnp.tanh(log_a) * (a * a + 1.0)) * (i * xc)
        return a, b

    def fwd_body(t, carry):
        r0 = pl.multiple_of(t * rows, rows)
        xc = _conv_tile(xpad_ref, r0, rows, cw) + cb
        xc_ref[pl.ds(r0, rows), :] = xc
        a, b = gates(xc, 0)
        h, carry = _scan_tile(a, b, carry, False, rows)
        hf_ref[pl.ds(r0, rows), :] = h
        return carry

    carry_f = lax.fori_loop(0, nt, fwd_body, h0_ref[0:1, :])

    def bwd_body(t, carry):
        r0 = pl.multiple_of((nt - 1 - t) * rows, rows)
        xc = xc_ref[pl.ds(r0, rows), :]
        a, b = gates(xc, 1)
        h, carry = _scan_tile(a, b, carry, True, rows)
        y = y_ref[pl.ds(r0, rows), :].astype(F32)
        o_ref[pl.ds(r0, rows), :] = ((hf_ref[pl.ds(r0, rows), :] + h) * _gelu_tanh(y)).astype(o_ref.dtype)
        return carry

    carry_b = lax.fori_loop(0, nt, bwd_body, h0_ref[1:2, :])
    ht_ref[0:1, :] = carry_f
    ht_ref[1:2, :] = carry_b


def _lru_call(proj, cw, cb, wa, wi, ba, bi, lam, h0, rows):
    bsz, seq, _ = proj.shape
    width = cw.shape[1]
    ncol = width // LANES
    vec2 = pl.BlockSpec((2, LANES), lambda b, j: (0, j))
    wspec = pl.BlockSpec((2, None, LANES, LANES), lambda b, j: (0, j, 0, 0))
    return pl.pallas_call(
        functools.partial(_lru_kernel, seq=seq, rows=rows),
        grid=(bsz, ncol),
        in_specs=[pl.BlockSpec((None, seq, LANES), lambda b, j: (b, 0, j)),
                  pl.BlockSpec((None, seq, LANES), lambda b, j: (b, 0, ncol + j)),
                  pl.BlockSpec((CONV_WIDTH, LANES), lambda b, j: (0, j)),
                  pl.BlockSpec((1, LANES), lambda b, j: (0, j)),
                  wspec, wspec, vec2, vec2, vec2,
                  pl.BlockSpec((None, 2, LANES), lambda b, j: (b, 0, j))],
        out_specs=[pl.BlockSpec((None, seq, LANES), lambda b, j: (b, 0, j)),
                   pl.BlockSpec((None, 2, LANES), lambda b, j: (b, 0, j))],
        out_shape=[jax.ShapeDtypeStruct((bsz, seq, width), BF16),
                   jax.ShapeDtypeStruct((bsz, 2, width), F32)],
        scratch_shapes=[pltpu.VMEM((seq + 2 * CONV_PAD, LANES), F32),
                        pltpu.VMEM((seq, LANES), F32),
                        pltpu.VMEM((seq, LANES), F32)],
        compiler_params=_cparams(("parallel", "parallel")),
        name="lru",
    )(proj, proj, cw, cb.reshape(1, width), wa, wi, ba, bi, lam, h0)


DN_GROUP = 2 * DN_CHUNK


def _split3(x):
    x1 = x.astype(BF16)
    r1 = x - x1.astype(F32)
    x2 = r1.astype(BF16)
    x3 = (r1 - x2.astype(F32)).astype(BF16)
    return x1, x2, x3


def _delta_kernel(q_ref, k_ref, v_ref, gr_ref, grt_ref, cvec_ref, ccol_ref, cum_ref, cumt_ref,
                  ones_ref, s0_ref, o_ref, st_ref, s_ref, *, rev, rows):
    step = pl.program_id(1)
    d = 1 if rev else 0

    @pl.when(step == 0)
    def _():
        s_ref[...] = s0_ref[...]

    raw = gr_ref[...]
    sig = _sigmoid(raw)
    g = -jnp.exp(cvec_ref[0:1, :]) * _softplus(raw + cvec_ref[1:2, :])
    g1, g2, g3 = _split3(g)
    cum = cum_ref[...]
    ones = ones_ref[...]

    def mm(m, parts):
        return sum(jnp.dot(m, p, preferred_element_type=F32) for p in parts)

    gc = mm(cum, (g1, g2, g3))
    gl = mm(ones, (g1, g2, g3))
    e_gc = jnp.exp(gc)
    e_rest = jnp.exp(gl - gc)
    e_gl = jnp.exp(gl)
    gt = -jnp.exp(ccol_ref[0]) * _softplus(grt_ref[...] + ccol_ref[1])
    t1, t2, t3 = _split3(gt)
    cumt = cumt_ref[...]
    gct = sum(jnp.dot(p, cumt, preferred_element_type=F32) for p in (t1, t2, t3))

    ri = lax.broadcasted_iota(jnp.int32, (DN_GROUP, DN_GROUP), 0)
    ci = lax.broadcasted_iota(jnp.int32, (DN_GROUP, DN_GROUP), 1)
    same = (ri // DN_CHUNK) == (ci // DN_CHUNK)
    if rev:
        incl = same & (ri <= ci)
        strict = same & (ri < ci)
    else:
        incl = same & (ri >= ci)
        strict = same & (ri > ci)

    ngroups = rows // DN_GROUP
    zeros_c = jnp.zeros((DN_CHUNK, DN_DK), F32)

    def bcast(x, rsl, lane):
        return jnp.broadcast_to(x[rsl, lane:lane + 1], (DN_GROUP, LANES))

    heads = range(DN_HEADS)
    for gi in (range(ngroups - 1, -1, -1) if rev else range(ngroups)):
        rsl = slice(gi * DN_GROUP, (gi + 1) * DN_GROUP)
        qd, kd, egl_b, rhs, kk, qk = [], [], [], [], [], []
        for h in heads:
            lb = d * DN_HEADS + h
            lg = 2 * DN_HEADS + d * DN_HEADS + h
            csl = slice(h * DN_DK, (h + 1) * DN_DK)
            q16 = q_ref[rsl, csl]
            k16 = k_ref[rsl, csl]
            q_h = q16.astype(F32)
            k_h = k16.astype(F32)
            beta_b = bcast(sig, rsl, lb)
            egc_b = bcast(e_gc, rsl, lg)
            kb_h = k_h * beta_b
            kk.append(_bdot_nt(kb_h, k16))
            qk.append(_bdot_nt(q16, k16))
            rhs.append(jnp.concatenate([v_ref[rsl, csl].astype(F32) * beta_b, kb_h * egc_b], axis=1))
            qd.append(q_h * egc_b)
            kd.append(k_h * bcast(e_rest, rsl, lg))
            egl_b.append(bcast(e_gl, rsl, lg))
        p, a16, attn = [], [], []
        for h in heads:
            lg = 2 * DN_HEADS + d * DN_HEADS + h
            diff = bcast(gc, rsl, lg) - gct[lg:lg + 1, rsl]
            decay = jnp.exp(jnp.where(incl, diff, NEG_BIG))
            m = jnp.where(strict, kk[h] * decay, 0.0)
            attn.append(qk[h] * decay)
            p.append(-m)
            a16.append(m.astype(BF16))
        npow = 1
        while npow < DN_CHUNK // 2:
            a = [jnp.dot(a16[h], a16[h], preferred_element_type=F32) for h in heads]
            a16 = [a[h].astype(BF16) for h in heads]
            p = [p[h] + a[h] + jnp.dot(p[h].astype(BF16), a16[h], preferred_element_type=F32) for h in heads]
            npow *= 2
        uw = [rhs[h] + _bdot(p[h], rhs[h]) for h in heads]
        s = [s_ref[h] for h in heads]
        for cc in ((1, 0) if rev else (0, 1)):
            c0 = cc * DN_CHUNK
            cs = slice(c0, c0 + DN_CHUNK)
            r2 = [_bdot(jnp.concatenate([uw[h][cs, DN_DK:], qd[h][cs]], axis=0), s[h]) for h in heads]
            v_new = [uw[h][cs, :DN_DK] - r2[h][:DN_CHUNK] for h in heads]
            for h in heads:
                v_pad = (jnp.concatenate([v_new[h], zeros_c], axis=0) if cc == 0
                         else jnp.concatenate([zeros_c, v_new[h]], axis=0))
                o_c = r2[h][DN_CHUNK:] + _bdot(attn[h][cs], v_pad)
                o_ref[gi * DN_GROUP + c0:gi * DN_GROUP + c0 + DN_CHUNK,
                      h * DN_DK:(h + 1) * DN_DK] = o_c.astype(o_ref.dtype)
            for h in heads:
                egl = jnp.concatenate([egl_b[h][cs], egl_b[h][cs]], axis=0)
                s[h] = s[h] * egl + _bdot_tn(kd[h][cs], v_new[h])
        for h in heads:
            s_ref[h] = s[h]

    @pl.when(step == pl.num_programs(1) - 1)
    def _():
        st_ref[...] = s_ref[...]


def _delta_consts(rows, rev):
    t = jnp.arange(rows)
    same = (t[:, None] // DN_CHUNK) == (t[None, :] // DN_CHUNK)
    cum = same & ((t[:, None] <= t[None, :]) if rev else (t[:, None] >= t[None, :]))
    return cum.astype(BF16), cum.T.astype(BF16), same.astype(BF16)


def _delta_call(qkv, gates, gates_t, cvec, ccol, s0, rev, rows):
    bsz, seq, _ = qkv.shape
    width = DN_HEADS * DN_DK
    nsteps = seq // rows
    cum, cumt, ones = _delta_consts(rows, rev)

    def blk(i):
        return nsteps - 1 - i if rev else i

    full = lambda shape: pl.BlockSpec(shape, lambda b, i: (0,) * len(shape))
    state = pl.BlockSpec((None, DN_HEADS, DN_DK, DN_DK), lambda b, i: (b, 0, 0, 0))
    return pl.pallas_call(
        functools.partial(_delta_kernel, rev=rev, rows=rows),
        grid=(bsz, nsteps),
        in_specs=[pl.BlockSpec((None, rows, width), lambda b, i: (b, blk(i), 0)),
                  pl.BlockSpec((None, rows, width), lambda b, i: (b, blk(i), 1)),
                  pl.BlockSpec((None, rows, width), lambda b, i: (b, blk(i), 2)),
                  pl.BlockSpec((None, rows, LANES), lambda b, i: (b, blk(i), 0)),
                  pl.BlockSpec((None, 4 * DN_HEADS, rows), lambda b, i: (b, 0, blk(i))),
                  full((SUBLANES, LANES)),
                  full((2, 4 * DN_HEADS, rows)),
                  full((rows, rows)), full((rows, rows)), full((rows, rows)),
                  state],
        out_specs=[pl.BlockSpec((None, rows, width), lambda b, i: (b, blk(i), 0)), state],
        out_shape=[jax.ShapeDtypeStruct((bsz, seq, width), BF16),
                   jax.ShapeDtypeStruct((bsz, DN_HEADS, DN_DK, DN_DK), F32)],
        scratch_shapes=[pltpu.VMEM((DN_HEADS, DN_DK, DN_DK), F32)],
        compiler_params=_cparams(("parallel", "arbitrary")),
        name="delta_bwd" if rev else "delta_fwd",
    )(qkv, qkv, qkv, gates, gates_t, cvec, ccol, cum, cumt, ones, s0)


def _evenout_kernel(lru_ref, of_ref, ob_ref, z_ref, g_ref, w_ref, res_ref, gate_ref, o_ref, a_ref):
    width = lru_ref.shape[1]
    a_ref[:, 0:width] = lru_ref[...].astype(BF16)
    for h in range(DN_HEADS):
        csl = slice(h * DN_DK, (h + 1) * DN_DK)
        o = of_ref[:, csl].astype(F32) + ob_ref[:, csl].astype(F32)
        z = z_ref[:, csl].astype(F32)
        y = (_rms(o) * g_ref[:, csl]) * (z * _sigmoid(z))
        a_ref[:, width + h * DN_DK:width + (h + 1) * DN_DK] = y.astype(BF16)
    o_ref[...] = res_ref[...] + gate_ref[...] * jnp.dot(a_ref[...], w_ref[...], preferred_element_type=F32)


def _evenout_call(lru, o_f, o_b, proj, z_block, dn_g, w_out, res, gate, tm):
    bsz, seq, width = lru.shape
    dnw = o_f.shape[2]
    d = res.shape[2]
    row = lambda c: pl.BlockSpec((None, tm, c), lambda b, i: (b, i, 0))
    return pl.pallas_call(
        _evenout_kernel,
        grid=(bsz, seq // tm),
        in_specs=[row(width), row(dnw), row(dnw),
                  pl.BlockSpec((None, tm, dnw), lambda b, i: (b, i, z_block)),
                  pl.BlockSpec((1, dnw), lambda b, i: (0, 0)),
                  pl.BlockSpec((width + dnw, d), lambda b, i: (0, 0)),
                  row(d),
                  pl.BlockSpec((None, 1, d), lambda b, i: (b, 0, 0))],
        out_specs=row(d),
        out_shape=jax.ShapeDtypeStruct((bsz, seq, d), F32),
        scratch_shapes=[pltpu.VMEM((tm, width + dnw), BF16)],
        compiler_params=_cparams(("parallel", "parallel")),
        name="evenout",
    )(lru, o_f, o_b, proj, dn_g, w_out, res, gate)


def _linres_kernel(a_ref, w_ref, res_ref, gate_ref, o_ref):
    o_ref[...] = res_ref[...] + gate_ref[...] * jnp.dot(a_ref[...].astype(BF16), w_ref[...],
                                                        preferred_element_type=F32)


def _linres_call(a, w, res, gate, tm):
    bsz, seq, kdim = a.shape
    d = res.shape[2]
    return pl.pallas_call(
        _linres_kernel,
        grid=(bsz, seq // tm),
        in_specs=[pl.BlockSpec((None, tm, kdim), lambda b, i: (b, i, 0)),
                  pl.BlockSpec((kdim, d), lambda b, i: (0, 0)),
                  pl.BlockSpec((None, tm, d), lambda b, i: (b, i, 0)),
                  pl.BlockSpec((None, 1, d), lambda b, i: (b, 0, 0))],
        out_specs=pl.BlockSpec((None, tm, d), lambda b, i: (b, i, 0)),
        out_shape=jax.ShapeDtypeStruct((bsz, seq, d), F32),
        compiler_params=_cparams(("parallel", "parallel")),
        name="linres",
    )(a, w, res, gate)


def _aprep_kernel(*refs, rope, want_q):
    if rope:
        x_ref, qg_ref, kg_ref, cos_ref, sin_ref = refs[:5]
        outs = refs[5:]
    else:
        x_ref, qg_ref, kg_ref = refs[:3]
        outs = refs[3:]
    if want_q:
        q_ref, k_ref, v_ref = outs
    else:
        k_ref, v_ref = outs

    if rope:
        cos = cos_ref[...]
        sin = sin_ref[...]
        lane = lax.broadcasted_iota(jnp.int32, cos.shape, 1)
        first = (lane % (HEAD_DIM // 2)) < (HEAD_DIM // 4)

    def norm_rope(x, g):
        y = _rms(x) * g
        if rope:
            partner = jnp.where(first, pltpu.roll(y, HEAD_DIM - HEAD_DIM // 4, 1),
                                pltpu.roll(y, HEAD_DIM // 4, 1))
            y = y * cos + partner * sin
        return y

    if want_q:
        for h in range(N_Q_HEADS):
            x = x_ref[:, h * HEAD_DIM:(h + 1) * HEAD_DIM].astype(F32)
            q_ref[h] = (norm_rope(x, qg_ref[...]) * (HEAD_DIM ** -0.5 * LOG2E)).astype(BF16)
    for h in range(N_KV_HEADS):
        c0 = (N_Q_HEADS + h) * HEAD_DIM
        k_ref[h] = norm_rope(x_ref[:, c0:c0 + HEAD_DIM].astype(F32), kg_ref[...]).astype(BF16)
        c1 = (N_Q_HEADS + N_KV_HEADS + h) * HEAD_DIM
        v_ref[h] = x_ref[:, c1:c1 + HEAD_DIM].astype(F32).T.astype(BF16)


def _aprep_call(qkv, qg, kg, rope_tabs, want_q, tm):
    bsz, seq, cols = qkv.shape
    rope = rope_tabs is not None
    in_specs = [pl.BlockSpec((None, tm, cols), lambda b, i: (b, i, 0)),
                pl.BlockSpec((1, HEAD_DIM), lambda b, i: (0, 0)),
                pl.BlockSpec((1, HEAD_DIM), lambda b, i: (0, 0))]
    args = [qkv, qg.reshape(1, HEAD_DIM), kg.reshape(1, HEAD_DIM)]
    if rope:
        in_specs += [pl.BlockSpec((tm, HEAD_DIM), lambda b, i: (i, 0))] * 2
        args += list(rope_tabs)
    hspec = lambda n: pl.BlockSpec((None, n, tm, HEAD_DIM), lambda b, i: (b, 0, i, 0))
    hshape = lambda n: jax.ShapeDtypeStruct((bsz, n, seq, HEAD_DIM), BF16)
    tspec = lambda n: pl.BlockSpec((None, n, HEAD_DIM, tm), lambda b, i: (b, 0, 0, i))
    tshape = lambda n: jax.ShapeDtypeStruct((bsz, n, HEAD_DIM, seq), BF16)
    out_specs = [hspec(N_KV_HEADS), tspec(N_KV_HEADS)]
    out_shape = [hshape(N_KV_HEADS), tshape(N_KV_HEADS)]
    if want_q:
        out_specs = [hspec(N_Q_HEADS)] + out_specs
        out_shape = [hshape(N_Q_HEADS)] + out_shape
    return pl.pallas_call(
        functools.partial(_aprep_kernel, rope=rope, want_q=want_q),
        grid=(bsz, seq // tm),
        in_specs=in_specs,
        out_specs=out_specs,
        out_shape=out_shape,
        compiler_params=_cparams(("parallel", "parallel")),
        name="aprep_x" if rope else "aprep_ctx",
    )(*args)


def _rope_tables(n_tokens):
    rows = n_tokens // GRID_W
    axis = HEAD_DIM // 2
    row_id = jnp.repeat(jnp.arange(rows), GRID_W).astype(F32)
    col_id = jnp.tile(jnp.arange(GRID_W), rows).astype(F32)
    inv_freq = ROPE_THETA ** (-jnp.arange(0, axis, 2, dtype=F32) / axis)
    ang_r = row_id[:, None] * inv_freq[None]
    ang_c = col_id[:, None] * inv_freq[None]
    cr, sr, cc, sc = jnp.cos(ang_r), jnp.sin(ang_r), jnp.cos(ang_c), jnp.sin(ang_c)
    cos = jnp.concatenate([cr, cr, cc, cc], axis=-1)
    sin = jnp.concatenate([-sr, sr, -sc, sc], axis=-1)
    return cos, sin


def _flash_kernel(q_ref, k_ref, vt_ref, o_ref, sa_ref, sb_ref, *, tq, tk, nk):
    m_rows = Q_PER_KV * tq
    q = q_ref[...].reshape(m_rows, HEAD_DIM)

    def scores(j):
        c0 = pl.multiple_of(j * tk, tk)
        return lax.dot_general(k_ref[pl.ds(c0, tk), :], q, (((1,), (1,)), ((), ())),
                               preferred_element_type=F32)

    def update(j, st, carry):
        m, l, acc = carry
        m_new = jnp.maximum(m, jnp.max(st, axis=0, keepdims=True))
        alpha = jnp.exp2(m - m_new)
        p = jnp.exp2(st - m_new)
        l = alpha * l + jnp.sum(p, axis=0, keepdims=True)
        acc = alpha * acc + jnp.dot(vt_ref[j], p.astype(BF16), preferred_element_type=F32)
        return m_new, l, acc

    bufs = (sa_ref, sb_ref)
    sa_ref[...] = scores(0)

    def run(j0, count, carry, prefetch_last):
        for u in range(count):
            if u + 1 < count or prefetch_last:
                bufs[(u + 1) % 2][...] = scores(j0 + u + 1)
            carry = update(j0 + u, bufs[u % 2][...], carry)
        return carry

    init = (jnp.full((1, m_rows), NEG_BIG, F32), jnp.zeros((1, m_rows), F32),
            jnp.zeros((HEAD_DIM, m_rows), F32))
    niter = (nk - 1) // FLASH_UNROLL
    carry = lax.fori_loop(0, niter, lambda jj, c: run(jj * FLASH_UNROLL, FLASH_UNROLL, c, True), init)
    _, l, acc = run(niter * FLASH_UNROLL, nk - niter * FLASH_UNROLL, carry, False)
    o = (acc / l).T
    for r in range(Q_PER_KV):
        o_ref[:, r * HEAD_DIM:(r + 1) * HEAD_DIM] = o[r * tq:(r + 1) * tq].astype(o_ref.dtype)


def _flash_call(q, k, vt, tq, tk):
    bsz, _, seq, _ = q.shape
    lk = k.shape[2]
    nk = lk // tk
    q5 = q.reshape(bsz, N_KV_HEADS, Q_PER_KV, seq, HEAD_DIM)
    vt5 = jnp.transpose(vt.reshape(bsz, N_KV_HEADS, HEAD_DIM, nk, tk), (0, 1, 3, 2, 4))
    return pl.pallas_call(
        functools.partial(_flash_kernel, tq=tq, tk=tk, nk=nk),
        grid=(bsz, N_KV_HEADS, seq // tq),
        in_specs=[pl.BlockSpec((None, None, Q_PER_KV, tq, HEAD_DIM), lambda b, g, i: (b, g, 0, i, 0)),
                  pl.BlockSpec((None, None, lk, HEAD_DIM), lambda b, g, i: (b, g, 0, 0)),
                  pl.BlockSpec((None, None, nk, HEAD_DIM, tk), lambda b, g, i: (b, g, 0, 0, 0))],
        out_specs=pl.BlockSpec((None, tq, Q_PER_KV * HEAD_DIM), lambda b, g, i: (b, i, g)),
        out_shape=jax.ShapeDtypeStruct((bsz, seq, N_Q_HEADS * HEAD_DIM), BF16),
        scratch_shapes=[pltpu.VMEM((tk, Q_PER_KV * tq), F32)] * 2,
        compiler_params=_cparams(("parallel", "parallel", "parallel")),
        name="flash",
    )(q5, k, vt5)


def _row_tile(seq, cap):
    t = min(seq, cap)
    while seq % t:
        t //= 2
    return t


def _even_weights(w_in, lru_wa, lru_wi, dn_alog, dn_dtb, dn_g):
    d, cols = w_in.shape
    ngate = 4 * DN_HEADS
    main = cols - ngate
    w_gates = jnp.concatenate([w_in[:, main:], jnp.zeros((d, LANES - ngate), F32)], axis=1)

    def blockdiag(w):
        z = jnp.zeros_like(w[:, 0::2])
        top = jnp.concatenate([w[:, 0::2], z], axis=-1)
        bot = jnp.concatenate([z, w[:, 1::2]], axis=-1)
        return jnp.concatenate([top, bot], axis=-2).astype(BF16)

    alog = dn_alog.reshape(-1)
    dtb = dn_dtb.reshape(-1)
    cvec = jnp.zeros((SUBLANES, LANES), F32)
    cvec = cvec.at[0, 2 * DN_HEADS:ngate].set(alog).at[1, 2 * DN_HEADS:ngate].set(dtb)

    ccol = jnp.zeros((2, ngate), F32).at[0, 2 * DN_HEADS:].set(alog).at[1, 2 * DN_HEADS:].set(dtb)

    return dict(w_in=w_in[:, :main].astype(BF16), w_gates=w_gates.astype(BF16),
                wa=blockdiag(lru_wa), wi=blockdiag(lru_wi), cvec=cvec, ccol=ccol,
                dn_g=jnp.tile(dn_g, DN_HEADS).reshape(1, -1))


def _even_stream(h_in, res, gate, norm_g, sc, sh, ew, lru_cw, lru_cb, lru_ba, lru_bi, lru_lam, dn_cw,
                 w_out, h0, s0, tm_in, tm, rows, dn_rows):
    width = lru_cw.shape[1]
    ngate = 4 * DN_HEADS
    proj, gates = _normlin_call(h_in, norm_g, sc, sh, ew["w_in"], ew["w_gates"], tm_in)
    lru, h_t = _lru_call(proj, lru_cw, lru_cb, ew["wa"], ew["wi"], lru_ba, lru_bi, lru_lam, h0, rows)
    qkv = _dnprep_call(proj, dn_cw, 2 * width // LANES, rows)
    gates_t = jnp.transpose(gates[:, :, :ngate], (0, 2, 1))
    ccol = jnp.broadcast_to(ew["ccol"][:, :, None], (2, ngate, dn_rows))
    outs, states = [], []
    for rev in (False, True):
        o_d, s_d = _delta_call(qkv, gates, gates_t, ew["cvec"], ccol,
                               s0[:, :, 1 if rev else 0], rev, dn_rows)
        outs.append(o_d)
        states.append(s_d)
    z_block = (2 * width + dn_cw.shape[1]) // (DN_HEADS * DN_DK)
    out = _evenout_call(lru, outs[0], outs[1], proj, z_block, ew["dn_g"], w_out, res, gate, tm)
    return out, h_t, jnp.stack(states, axis=2)


def kernel(x, c, ctx, c_ctx, norm_mix_g, norm_mlp_g, w_mod, b_mod, mlp_w1, mlp_w2, even_w_in, lru_conv_w,
           lru_conv_b, lru_w_a, lru_b_a, lru_w_i, lru_b_i, lru_lambda, dn_conv_w, dn_a_log, dn_dt_bias,
           dn_norm_g, even_w_out, attn_w_qkv, attn_q_norm_g, attn_k_norm_g, attn_w_o, final_norm_g):
    bsz, seq, d = x.shape
    lc = ctx.shape[1]
    depth = w_mod.shape[0]
    tm_x = _row_tile(seq, 512)
    tm_c = _row_tile(lc, 512)
    tm_mlp = _row_tile(seq, 1024)
    rows_x = _row_tile(seq, 256)
    rows_c = _row_tile(lc, 256)

    cc = jnp.zeros((SUBLANES, d), F32).at[:bsz].set(c).at[bsz].set(c_ctx)
    mod = _mod_call(cc, w_mod, b_mod).reshape(depth, SUBLANES, 6, d)

    rope = _rope_tables(seq)
    xs, cs = x, ctx
    for l in range(depth):
        last = l == depth - 1
        j = l // 2
        mx = [mod[l, :bsz, k][:, None, :] for k in range(6)]
        mc = [jnp.broadcast_to(mod[l, bsz, k][None, None, :], (bsz, 1, d)) for k in range(6)]
        sh1, sc1, g1, sh2, sc2, g2 = mx
        csh1, csc1, cg1, csh2, csc2, cg2 = mc
        w1 = mlp_w1[l].astype(BF16)
        w2 = mlp_w2[l].astype(BF16)
        if l % 2 == 0:
            ew = _even_weights(even_w_in[j], lru_w_a[j], lru_w_i[j], dn_a_log[j], dn_dt_bias[j], dn_norm_g[j])
            w_out = even_w_out[j].astype(BF16)
            h0 = jnp.zeros((bsz, 2, lru_conv_w.shape[2]), F32)
            s0 = jnp.zeros((bsz, DN_HEADS, 2, DN_DK, DN_DK), F32)
            common = (ew, lru_conv_w[j], lru_conv_b[j], lru_b_a[j], lru_b_i[j], lru_lambda[j], dn_conv_w[j], w_out)
            cs_mix, h_t, s_t = _even_stream(cs, cs, cg1, norm_mix_g[l], csc1, csh1, *common, h0, s0,
                                            tm_c, tm_c, rows_c, rows_c)
            xs, _, _ = _even_stream(xs, xs, g1, norm_mix_g[l], sc1, sh1, *common, h_t, s_t,
                                    tm_mlp, tm_x, rows_x, rows_x)
        else:
            w_qkv = attn_w_qkv[j].astype(BF16)
            w_o = attn_w_o[j].astype(BF16)
            qkv_c = _normlin_call(cs, norm_mix_g[l], csc1, csh1, w_qkv, None, tm_c)
            qkv_x = _normlin_call(xs, norm_mix_g[l], sc1, sh1, w_qkv, None, tm_mlp)
            if last:
                k_c, vt_c = _aprep_call(qkv_c, attn_q_norm_g[j], attn_k_norm_g[j], None, False, rows_c)
            else:
                qt_c, k_c, vt_c = _aprep_call(qkv_c, attn_q_norm_g[j], attn_k_norm_g[j], None, True, rows_c)
            qt_x, k_x, vt_x = _aprep_call(qkv_x, attn_q_norm_g[j], attn_k_norm_g[j], rope, True, rows_x)
            k_all = jnp.concatenate([k_c, k_x], axis=2)
            vt_all = jnp.concatenate([vt_c, vt_x], axis=3)
            o_x = _flash_call(qt_x, k_all, vt_all, rows_x, _col_tile(lc + seq, FLASH_TK_CAP))
            xs = _linres_call(o_x, w_o, xs, g1, tm_x)
            if not last:
                o_c = _flash_call(qt_c, k_c, vt_c, rows_c, _col_tile(lc, FLASH_TK_CAP))
                cs_mix = _linres_call(o_c, w_o, cs, cg1, tm_c)
        xs = _mlp_call(xs, norm_mlp_g[l], sc2, sh2, g2, w1, w2, final_norm_g, tm_mlp, last)
        if not last:
            cs = _mlp_call(cs_mix, norm_mlp_g[l], csc2, csh2, cg2, w1, w2, final_norm_g, tm_c, False)
    return xs
```

```python
import functools

import jax
import jax.numpy as jnp
from jax import lax
from jax.experimental import pallas as pl
from jax.experimental.pallas import tpu as pltpu

F32 = jnp.float32
BF16 = jnp.bfloat16

EPS = 1e-6
LANES = 128
SUBLANES = 8
VMEM_LIMIT = 56 * 1024 * 1024

LRU_BLOCK = 64
LRU_C = 8.0
CONV_WIDTH = 4
CONV_LEFT = 2
DN_HEADS = 8
DN_DK = 128
DN_CHUNK = 64
N_Q_HEADS = 8
N_KV_HEADS = 4
Q_PER_KV = N_Q_HEADS // N_KV_HEADS
HEAD_DIM = 128
GRID_W = 64
ROPE_THETA = 10000.0
NEG_BIG = -1e30
LOG2E = 1.4426950408889634
DNPREP_ROWS = 1024
LRU_ROWS = 1024
FLASH_TK_CAP = 768
FLASH_UNROLL = 4


def _cparams(sem):
    return pltpu.CompilerParams(dimension_semantics=sem, vmem_limit_bytes=VMEM_LIMIT)


def _bdot(a, b):
    return jnp.dot(a.astype(BF16), b.astype(BF16), preferred_element_type=F32)


def _bdot_nt(a, b):
    return lax.dot_general(a.astype(BF16), b.astype(BF16), (((1,), (1,)), ((), ())),
                           preferred_element_type=F32)


def _bdot_tn(a, b):
    return lax.dot_general(a.astype(BF16), b.astype(BF16), (((0,), (0,)), ((), ())),
                           preferred_element_type=F32)


def _sigmoid(x):
    return jax.nn.sigmoid(x)


def _softplus(x):
    return jnp.maximum(x, 0.0) + jnp.log1p(jnp.exp(-jnp.abs(x)))


def _rms(x):
    return x * lax.rsqrt(jnp.mean(x * x, axis=-1, keepdims=True) + EPS)


def _col_tile(n, cap):
    best = LANES
    t = LANES
    while t <= min(n, cap):
        if n % t == 0:
            best = t
        t += LANES
    return best


def _mod_kernel(c_ref, w_ref, b_ref, o_ref):
    s = c_ref[...]
    s = s * _sigmoid(s)
    o_ref[...] = _bdot(s, w_ref[...]) + b_ref[...]


def _mod_call(cc, w_mod, b_mod):
    depth, d, n = w_mod.shape
    tn = _col_tile(n, 1536)
    return pl.pallas_call(
        _mod_kernel,
        grid=(depth, n // tn),
        in_specs=[pl.BlockSpec((SUBLANES, d), lambda l, j: (0, 0)),
                  pl.BlockSpec((None, d, tn), lambda l, j: (l, 0, j)),
                  pl.BlockSpec((None, 1, tn), lambda l, j: (l, 0, j))],
        out_specs=pl.BlockSpec((None, SUBLANES, tn), lambda l, j: (l, 0, j)),
        out_shape=jax.ShapeDtypeStruct((depth, SUBLANES, n), F32),
        compiler_params=_cparams(("parallel", "parallel")),
        name="mod",
    )(cc, w_mod, b_mod.reshape(depth, 1, n))


def _normlin_kernel(*refs, has_gates):
    if has_gates:
        x_ref, g_ref, sc_ref, sh_ref, w_ref, wg_ref, o_ref, og_ref, h_ref = refs
    else:
        x_ref, g_ref, sc_ref, sh_ref, w_ref, o_ref, h_ref = refs

    @pl.when(pl.program_id(2) == 0)
    def _():
        y = _rms(x_ref[...]) * g_ref[...]
        h_ref[...] = (y * (1.0 + sc_ref[...]) + sh_ref[...]).astype(BF16)
        if has_gates:
            og_ref[...] = jnp.dot(h_ref[...], wg_ref[...], preferred_element_type=F32)

    o_ref[...] = jnp.dot(h_ref[...], w_ref[...], preferred_element_type=F32).astype(o_ref.dtype)


def _normlin_call(x, g, sc, sh, w, w_gates, tm):
    bsz, seq, d = x.shape
    n = w.shape[1]
    tn = _col_tile(n, 1024)
    has_gates = w_gates is not None
    in_specs = [pl.BlockSpec((None, tm, d), lambda b, i, j: (b, i, 0)),
                pl.BlockSpec((1, d), lambda b, i, j: (0, 0)),
                pl.BlockSpec((None, 1, d), lambda b, i, j: (b, 0, 0)),
                pl.BlockSpec((None, 1, d), lambda b, i, j: (b, 0, 0)),
                pl.BlockSpec((d, tn), lambda b, i, j: (0, j))]
    out_specs = [pl.BlockSpec((None, tm, tn), lambda b, i, j: (b, i, j))]
    out_shape = [jax.ShapeDtypeStruct((bsz, seq, n), BF16)]
    args = [x, g.reshape(1, d), sc, sh, w]
    if has_gates:
        ng = w_gates.shape[1]
        in_specs.append(pl.BlockSpec((d, ng), lambda b, i, j: (0, 0)))
        out_specs.append(pl.BlockSpec((None, tm, ng), lambda b, i, j: (b, i, 0)))
        out_shape.append(jax.ShapeDtypeStruct((bsz, seq, ng), F32))
        args.append(w_gates)
    out = pl.pallas_call(
        functools.partial(_normlin_kernel, has_gates=has_gates),
        grid=(bsz, seq // tm, n // tn),
        in_specs=in_specs,
        out_specs=out_specs,
        out_shape=out_shape,
        scratch_shapes=[pltpu.VMEM((tm, d), BF16)],
        compiler_params=_cparams(("parallel", "parallel", "arbitrary")),
        name="normlin",
    )(*args)
    return out if has_gates else out[0]


def _mlp_kernel(x_ref, g_ref, sc_ref, sh_ref, gate_ref, w1_ref, w2_ref, fg_ref, o_ref,
                h_ref, acc_ref, *, final_norm):
    k = pl.program_id(2)

    @pl.when(k == 0)
    def _():
        y = _rms(x_ref[...]) * g_ref[...]
        h_ref[...] = (y * (1.0 + sc_ref[...]) + sh_ref[...]).astype(BF16)
        acc_ref[...] = jnp.zeros_like(acc_ref)

    a = jnp.dot(h_ref[...], w1_ref[...], preferred_element_type=F32)
    a = jnp.square(jnp.maximum(a, 0.0)).astype(BF16)
    acc_ref[...] += jnp.dot(a, w2_ref[...], preferred_element_type=F32)

    @pl.when(k == pl.num_programs(2) - 1)
    def _():
        y = x_ref[...] + gate_ref[...] * acc_ref[...]
        if final_norm:
            y = _rms(y) * fg_ref[...]
        o_ref[...] = y


def _mlp_call(x, g, sc, sh, gate, w1, w2, fg, tm, final_norm):
    bsz, seq, d = x.shape
    f = w1.shape[1]
    tf = _col_tile(f, 1024)
    vec = pl.BlockSpec((None, 1, d), lambda b, i, k: (b, 0, 0))
    return pl.pallas_call(
        functools.partial(_mlp_kernel, final_norm=final_norm),
        grid=(bsz, seq // tm, f // tf),
        in_specs=[pl.BlockSpec((None, tm, d), lambda b, i, k: (b, i, 0)),
                  pl.BlockSpec((1, d), lambda b, i, k: (0, 0)),
                  vec, vec, vec,
                  pl.BlockSpec((d, tf), lambda b, i, k: (0, k)),
                  pl.BlockSpec((tf, d), lambda b, i, k: (k, 0)),
                  pl.BlockSpec((1, d), lambda b, i, k: (0, 0))],
        out_specs=pl.BlockSpec((None, tm, d), lambda b, i, k: (b, i, 0)),
        out_shape=jax.ShapeDtypeStruct((bsz, seq, d), F32),
        scratch_shapes=[pltpu.VMEM((tm, d), BF16), pltpu.VMEM((tm, d), F32)],
        compiler_params=_cparams(("parallel", "parallel", "arbitrary")),
        name="mlp",
    )(x, g.reshape(1, d), sc, sh, gate, w1, w2, fg.reshape(1, d))


CONV_PAD = SUBLANES


def _fill_padded(xpad_ref, x_ref, seq, rows):
    zeros = jnp.zeros((CONV_PAD, xpad_ref.shape[1]), F32)
    xpad_ref[0:CONV_PAD, :] = zeros
    xpad_ref[CONV_PAD + seq:CONV_PAD + seq + CONV_PAD, :] = zeros

    def body(t, carry):
        r0 = pl.multiple_of(t * rows, rows)
        xpad_ref[pl.ds(r0 + CONV_PAD, rows), :] = x_ref[pl.ds(r0, rows), :].astype(F32)
        return carry

    lax.fori_loop(0, seq // rows, body, 0)


def _conv_tile(xpad_ref, r0, rows, w):
    acc = None
    for j in range(CONV_WIDTH):
        off = j - CONV_LEFT
        term = xpad_ref[pl.ds(r0 + (CONV_PAD + off), rows), :] * w[j:j + 1]
        acc = term if acc is None else acc + term
    return acc


def _dnprep_kernel(x_ref, w_ref, o_ref, xpad_ref, *, seq, rows):
    j = pl.program_id(1)
    is_qk = j < 2 * DN_HEADS
    scale = jnp.where(j < DN_HEADS, DN_DK ** -0.5, 1.0).astype(F32)
    _fill_padded(xpad_ref, x_ref, seq, rows)
    w = w_ref[...]

    def body(t, carry):
        r0 = pl.multiple_of(t * rows, rows)
        y = _conv_tile(xpad_ref, r0, rows, w)
        y = y * _sigmoid(y)
        nrm = (y * lax.rsqrt(jnp.sum(y * y, axis=-1, keepdims=True) + EPS)) * scale
        o_ref[pl.ds(r0, rows), :] = jnp.where(is_qk, nrm, y).astype(o_ref.dtype)
        return carry

    lax.fori_loop(0, seq // rows, body, 0)


def _dnprep_call(proj, conv_w, col0_blocks, rows):
    bsz, seq, _ = proj.shape
    ncol = conv_w.shape[1] // LANES
    return pl.pallas_call(
        functools.partial(_dnprep_kernel, seq=seq, rows=rows),
        grid=(bsz, ncol),
        in_specs=[pl.BlockSpec((None, seq, LANES), lambda b, j: (b, 0, col0_blocks + j)),
                  pl.BlockSpec((CONV_WIDTH, LANES), lambda b, j: (0, j))],
        out_specs=pl.BlockSpec((None, seq, LANES), lambda b, j: (b, 0, j)),
        out_shape=jax.ShapeDtypeStruct((bsz, seq, ncol * LANES), BF16),
        scratch_shapes=[pltpu.VMEM((seq + 2 * CONV_PAD, LANES), F32)],
        compiler_params=_cparams(("parallel", "parallel")),
        name="dnprep",
    )(proj, conv_w)


def _scan_tile(a, b, carry, rev, rows):
    nv = rows // SUBLANES
    lanes = a.shape[1]
    a = a.reshape(nv, SUBLANES, lanes)
    b = b.reshape(nv, SUBLANES, lanes)
    row = lax.broadcasted_iota(jnp.int32, a.shape, 1)
    s = 1
    while s < SUBLANES:
        shift = SUBLANES - s if rev else s
        a_s = pltpu.roll(a, shift, 1)
        b_s = pltpu.roll(b, shift, 1)
        valid = (row < SUBLANES - s) if rev else (row >= s)
        b = b + a * jnp.where(valid, b_s, 0.0)
        a = a * jnp.where(valid, a_s, 1.0)
        s *= 2
    hs = [None] * nv
    for v in (range(nv - 1, -1, -1) if rev else range(nv)):
        h = a[v] * carry + b[v]
        carry = h[0:1] if rev else h[SUBLANES - 1:SUBLANES]
        hs[v] = h
    return jnp.concatenate(hs, axis=0), carry


def _gelu_tanh(x):
    return 0.5 * x * (1.0 + jnp.tanh(0.7978845608028654 * (x + 0.044715 * (x * x * x))))


def _lru_kernel(x_ref, y_ref, cw_ref, cb_ref, wa_ref, wi_ref, ba_ref, bi_ref, lam_ref, h0_ref,
                o_ref, ht_ref, xpad_ref, xc_ref, hf_ref, *, seq, rows):
    _fill_padded(xpad_ref, x_ref, seq, rows)
    cw = cw_ref[...]
    cb = cb_ref[...]
    nt = seq // rows

    def gates(xc, d):
        r = _sigmoid(_bdot(xc, wa_ref[d]) + ba_ref[d:d + 1, :])
        i = _sigmoid(_bdot(xc, wi_ref[d]) + bi_ref[d:d + 1, :])
        log_a = (-LRU_C * r) * _softplus(-lam_ref[d:d + 1, :])
        a = jnp.exp(log_a)
        b = jnp.sqrt(-jnp.tanh(log_a) * (a * a + 1.0)) * (i * xc)
        return a, b

    def fwd_body(t, carry):
        r0 = pl.multiple_of(t * rows, rows)
        xc = _conv_tile(xpad_ref, r0, rows, cw) + cb
        xc_ref[pl.ds(r0, rows), :] = xc
        a, b = gates(xc, 0)
        h, carry = _scan_tile(a, b, carry, False, rows)
        hf_ref[pl.ds(r0, rows), :] = h
        return carry

    carry_f = lax.fori_loop(0, nt, fwd_body, h0_ref[0:1, :])

    def bwd_body(t, carry):
        r0 = pl.multiple_of((nt - 1 - t) * rows, rows)
        xc = xc_ref[pl.ds(r0, rows), :]
        a, b = gates(xc, 1)
        h, carry = _scan_tile(a, b, carry, True, rows)
        y = y_ref[pl.ds(r0, rows), :].astype(F32)
        o_ref[pl.ds(r0, rows), :] = ((hf_ref[pl.ds(r0, rows), :] + h) * _gelu_tanh(y)).astype(o_ref.dtype)
        return carry

    carry_b = lax.fori_loop(0, nt, bwd_body, h0_ref[1:2, :])
    ht_ref[0:1, :] = carry_f
    ht_ref[1:2, :] = carry_b


def _lru_call(proj, cw, cb, wa, wi, ba, bi, lam, h0, rows):
    bsz, seq, _ = proj.shape
    width = cw.shape[1]
    ncol = width // LANES
    vec2 = pl.BlockSpec((2, LANES), lambda b, j: (0, j))
    wspec = pl.BlockSpec((2, None, LANES, LANES), lambda b, j: (0, j, 0, 0))
    return pl.pallas_call(
        functools.partial(_lru_kernel, seq=seq, rows=rows),
        grid=(bsz, ncol),
        in_specs=[pl.BlockSpec((None, seq, LANES), lambda b, j: (b, 0, j)),
                  pl.BlockSpec((None, seq, LANES), lambda b, j: (b, 0, ncol + j)),
                  pl.BlockSpec((CONV_WIDTH, LANES), lambda b, j: (0, j)),
                  pl.BlockSpec((1, LANES), lambda b, j: (0, j)),
                  wspec, wspec, vec2, vec2, vec2,
                  pl.BlockSpec((None, 2, LANES), lambda b, j: (b, 0, j))],
        out_specs=[pl.BlockSpec((None, seq, LANES), lambda b, j: (b, 0, j)),
                   pl.BlockSpec((None, 2, LANES), lambda b, j: (b, 0, j))],
        out_shape=[jax.ShapeDtypeStruct((bsz, seq, width), BF16),
                   jax.ShapeDtypeStruct((bsz, 2, width), F32)],
        scratch_shapes=[pltpu.VMEM((seq + 2 * CONV_PAD, LANES), F32),
                        pltpu.VMEM((seq, LANES), F32),
                        pltpu.VMEM((seq, LANES), F32)],
        compiler_params=_cparams(("parallel", "parallel")),
        name="lru",
    )(proj, proj, cw, cb.reshape(1, width), wa, wi, ba, bi, lam, h0)


DN_GROUP = 2 * DN_CHUNK


def _split3(x):
    x1 = x.astype(BF16)
    r1 = x - x1.astype(F32)
    x2 = r1.astype(BF16)
    x3 = (r1 - x2.astype(F32)).astype(BF16)
    return x1, x2, x3


def _delta_kernel(q_ref, k_ref, v_ref, gr_ref, grt_ref, cvec_ref, ccol_ref, cum_ref, cumt_ref,
                  ones_ref, s0_ref, o_ref, st_ref, s_ref, *, rev, rows):
    step = pl.program_id(1)
    d = 1 if rev else 0

    @pl.when(step == 0)
    def _():
        s_ref[...] = s0_ref[...]

    raw = gr_ref[...]
    sig = _sigmoid(raw)
    g = -jnp.exp(cvec_ref[0:1, :]) * _softplus(raw + cvec_ref[1:2, :])
    g1, g2, g3 = _split3(g)
    cum = cum_ref[...]
    ones = ones_ref[...]

    def mm(m, parts):
        return sum(jnp.dot(m, p, preferred_element_type=F32) for p in parts)

    gc = mm(cum, (g1, g2, g3))
    gl = mm(ones, (g1, g2, g3))
    e_gc = jnp.exp(gc)
    e_rest = jnp.exp(gl - gc)
    e_gl = jnp.exp(gl)
    gt = -jnp.exp(ccol_ref[0]) * _softplus(grt_ref[...] + ccol_ref[1])
    t1, t2, t3 = _split3(gt)
    cumt = cumt_ref[...]
    gct = sum(jnp.dot(p, cumt, preferred_element_type=F32) for p in (t1, t2, t3))

    ri = lax.broadcasted_iota(jnp.int32, (DN_GROUP, DN_GROUP), 0)
    ci = lax.broadcasted_iota(jnp.int32, (DN_GROUP, DN_GROUP), 1)
    same = (ri // DN_CHUNK) == (ci // DN_CHUNK)
    if rev:
        incl = same & (ri <= ci)
        strict = same & (ri < ci)
    else:
        incl = same & (ri >= ci)
        strict = same & (ri > ci)

    ngroups = rows // DN_GROUP
    zeros_c = jnp.zeros((DN_CHUNK, DN_DK), F32)

    def bcast(x, rsl, lane):
        return jnp.broadcast_to(x[rsl, lane:lane + 1], (DN_GROUP, LANES))

    heads = range(DN_HEADS)
    groups = list(range(ngroups - 1, -1, -1) if rev else range(ngroups))
    units = [(gi, h) for gi in groups for h in heads]
    qd, kd, egl_b, rhs, kk, qk = {}, {}, {}, {}, {}, {}
    for u in units:
        gi, h = u
        rsl = slice(gi * DN_GROUP, (gi + 1) * DN_GROUP)
        lb = d * DN_HEADS + h
        lg = 2 * DN_HEADS + d * DN_HEADS + h
        csl = slice(h * DN_DK, (h + 1) * DN_DK)
        q16 = q_ref[rsl, csl]
        k16 = k_ref[rsl, csl]
        q_h = q16.astype(F32)
        k_h = k16.astype(F32)
        beta_b = bcast(sig, rsl, lb)
        egc_b = bcast(e_gc, rsl, lg)
        kb_h = k_h * beta_b
        kk[u] = _bdot_nt(kb_h, k16)
        qk[u] = _bdot_nt(q16, k16)
        rhs[u] = jnp.concatenate([v_ref[rsl, csl].astype(F32) * beta_b, kb_h * egc_b], axis=1)
        qd[u] = q_h * egc_b
        kd[u] = k_h * bcast(e_rest, rsl, lg)
        egl_b[u] = bcast(e_gl, rsl, lg)
    p, a16, attn = {}, {}, {}
    for u in units:
        gi, h = u
        rsl = slice(gi * DN_GROUP, (gi + 1) * DN_GROUP)
        lg = 2 * DN_HEADS + d * DN_HEADS + h
        diff = bcast(gc, rsl, lg) - gct[lg:lg + 1, rsl]
        decay = jnp.exp(jnp.where(incl, diff, NEG_BIG))
        m = jnp.where(strict, kk[u] * decay, 0.0)
        attn[u] = qk[u] * decay
        p[u] = -m
        a16[u] = m.astype(BF16)
    npow = 1
    while npow < DN_CHUNK // 2:
        a = {u: jnp.dot(a16[u], a16[u], preferred_element_type=F32) for u in units}
        a16 = {u: a[u].astype(BF16) for u in units}
        p = {u: p[u] + a[u] + jnp.dot(p[u].astype(BF16), a16[u], preferred_element_type=F32) for u in units}
        npow *= 2
    uw = {u: rhs[u] + _bdot(p[u], rhs[u]) for u in units}
    s = [s_ref[h] for h in heads]
    for gi in groups:
        for cc in ((1, 0) if rev else (0, 1)):
            c0 = cc * DN_CHUNK
            cs = slice(c0, c0 + DN_CHUNK)
            r2 = [_bdot(jnp.concatenate([uw[gi, h][cs, DN_DK:], qd[gi, h][cs]], axis=0), s[h]) for h in heads]
            v_new = [uw[gi, h][cs, :DN_DK] - r2[h][:DN_CHUNK] for h in heads]
            for h in heads:
                v_pad = (jnp.concatenate([v_new[h], zeros_c], axis=0) if cc == 0
                         else jnp.concatenate([zeros_c, v_new[h]], axis=0))
                o_c = r2[h][DN_CHUNK:] + _bdot(attn[gi, h][cs], v_pad)
                o_ref[gi * DN_GROUP + c0:gi * DN_GROUP + c0 + DN_CHUNK,
                      h * DN_DK:(h + 1) * DN_DK] = o_c.astype(o_ref.dtype)
            for h in heads:
                egl = jnp.concatenate([egl_b[gi, h][cs], egl_b[gi, h][cs]], axis=0)
                s[h] = s[h] * egl + _bdot_tn(kd[gi, h][cs], v_new[h])
    for h in heads:
        s_ref[h] = s[h]

    @pl.when(step == pl.num_programs(1) - 1)
    def _():
        st_ref[...] = s_ref[...]


def _delta_consts(rows, rev):
    t = jnp.arange(rows)
    same = (t[:, None] // DN_CHUNK) == (t[None, :] // DN_CHUNK)
    cum = same & ((t[:, None] <= t[None, :]) if rev else (t[:, None] >= t[None, :]))
    return cum.astype(BF16), cum.T.astype(BF16), same.astype(BF16)


def _delta_call(qkv, gates, gates_t, cvec, ccol, s0, rev, rows):
    bsz, seq, _ = qkv.shape
    width = DN_HEADS * DN_DK
    nsteps = seq // rows
    cum, cumt, ones = _delta_consts(rows, rev)

    def blk(i):
        return nsteps - 1 - i if rev else i

    full = lambda shape: pl.BlockSpec(shape, lambda b, i: (0,) * len(shape))
    state = pl.BlockSpec((None, DN_HEADS, DN_DK, DN_DK), lambda b, i: (b, 0, 0, 0))
    return pl.pallas_call(
        functools.partial(_delta_kernel, rev=rev, rows=rows),
        grid=(bsz, nsteps),
        in_specs=[pl.BlockSpec((None, rows, width), lambda b, i: (b, blk(i), 0)),
                  pl.BlockSpec((None, rows, width), lambda b, i: (b, blk(i), 1)),
                  pl.BlockSpec((None, rows, width), lambda b, i: (b, blk(i), 2)),
                  pl.BlockSpec((None, rows, LANES), lambda b, i: (b, blk(i), 0)),
                  pl.BlockSpec((None, 4 * DN_HEADS, rows), lambda b, i: (b, 0, blk(i))),
                  full((SUBLANES, LANES)),
                  full((2, 4 * DN_HEADS, rows)),
                  full((rows, rows)), full((rows, rows)), full((rows, rows)),
                  state],
        out_specs=[pl.BlockSpec((None, rows, width), lambda b, i: (b, blk(i), 0)), state],
        out_shape=[jax.ShapeDtypeStruct((bsz, seq, width), BF16),
                   jax.ShapeDtypeStruct((bsz, DN_HEADS, DN_DK, DN_DK), F32)],
        scratch_shapes=[pltpu.VMEM((DN_HEADS, DN_DK, DN_DK), F32)],
        compiler_params=_cparams(("parallel", "arbitrary")),
        name="delta_bwd" if rev else "delta_fwd",
    )(qkv, qkv, qkv, gates, gates_t, cvec, ccol, cum, cumt, ones, s0)


def _evenout_kernel(lru_ref, of_ref, ob_ref, z_ref, g_ref, w_ref, res_ref, gate_ref, o_ref, a_ref):
    width = lru_ref.shape[1]
    a_ref[:, 0:width] = lru_ref[...].astype(BF16)
    for h in range(DN_HEADS):
        csl = slice(h * DN_DK, (h + 1) * DN_DK)
        o = of_ref[:, csl].astype(F32) + ob_ref[:, csl].astype(F32)
        z = z_ref[:, csl].astype(F32)
        y = (_rms(o) * g_ref[:, csl]) * (z * _sigmoid(z))
        a_ref[:, width + h * DN_DK:width + (h + 1) * DN_DK] = y.astype(BF16)
    o_ref[...] = res_ref[...] + gate_ref[...] * jnp.dot(a_ref[...], w_ref[...], preferred_element_type=F32)


def _evenout_call(lru, o_f, o_b, proj, z_block, dn_g, w_out, res, gate, tm):
    bsz, seq, width = lru.shape
    dnw = o_f.shape[2]
    d = res.shape[2]
    row = lambda c: pl.BlockSpec((None, tm, c), lambda b, i: (b, i, 0))
    return pl.pallas_call(
        _evenout_kernel,
        grid=(bsz, seq // tm),
        in_specs=[row(width), row(dnw), row(dnw),
                  pl.BlockSpec((None, tm, dnw), lambda b, i: (b, i, z_block)),
                  pl.BlockSpec((1, dnw), lambda b, i: (0, 0)),
                  pl.BlockSpec((width + dnw, d), lambda b, i: (0, 0)),
                  row(d),
                  pl.BlockSpec((None, 1, d), lambda b, i: (b, 0, 0))],
        out_specs=row(d),
        out_shape=jax.ShapeDtypeStruct((bsz, seq, d), F32),
        scratch_shapes=[pltpu.VMEM((tm, width + dnw), BF16)],
        compiler_params=_cparams(("parallel", "parallel")),
        name="evenout",
    )(lru, o_f, o_b, proj, dn_g, w_out, res, gate)


def _linres_kernel(a_ref, w_ref, res_ref, gate_ref, o_ref):
    o_ref[...] = res_ref[...] + gate_ref[...] * jnp.dot(a_ref[...].astype(BF16), w_ref[...],
                                                        preferred_element_type=F32)


def _linres_call(a, w, res, gate, tm):
    bsz, seq, kdim = a.shape
    d = res.shape[2]
    return pl.pallas_call(
        _linres_kernel,
        grid=(bsz, seq // tm),
        in_specs=[pl.BlockSpec((None, tm, kdim), lambda b, i: (b, i, 0)),
                  pl.BlockSpec((kdim, d), lambda b, i: (0, 0)),
                  pl.BlockSpec((None, tm, d), lambda b, i: (b, i, 0)),
                  pl.BlockSpec((None, 1, d), lambda b, i: (b, 0, 0))],
        out_specs=pl.BlockSpec((None, tm, d), lambda b, i: (b, i, 0)),
        out_shape=jax.ShapeDtypeStruct((bsz, seq, d), F32),
        compiler_params=_cparams(("parallel", "parallel")),
        name="linres",
    )(a, w, res, gate)


def _aprep_kernel(*refs, rope, want_q):
    if rope:
        x_ref, qg_ref, kg_ref, cos_ref, sin_ref = refs[:5]
        outs = refs[5:]
    else:
        x_ref, qg_ref, kg_ref = refs[:3]
        outs = refs[3:]
    if want_q:
        q_ref, k_ref, v_ref = outs
    else:
        k_ref, v_ref = outs

    if rope:
        cos = cos_ref[...]
        sin = sin_ref[...]

    def norm_rope(x, g):
        y = _rms(x) * g
        if rope:
            y = y * cos + pltpu.roll(y, HEAD_DIM // 2, 1) * sin
        return y

    if want_q:
        for h in range(N_Q_HEADS):
            x = x_ref[:, h * HEAD_DIM:(h + 1) * HEAD_DIM].astype(F32)
            q_ref[h] = (norm_rope(x, qg_ref[...]) * (HEAD_DIM ** -0.5 * LOG2E)).astype(BF16)
    for h in range(N_KV_HEADS):
        c0 = (N_Q_HEADS + h) * HEAD_DIM
        k_ref[h] = norm_rope(x_ref[:, c0:c0 + HEAD_DIM].astype(F32), kg_ref[...]).astype(BF16)
        c1 = (N_Q_HEADS + N_KV_HEADS + h) * HEAD_DIM
        v_ref[h] = x_ref[:, c1:c1 + HEAD_DIM].astype(F32).T.astype(BF16)


def _aprep_call(qkv, qg, kg, rope_tabs, want_q, tm):
    bsz, seq, cols = qkv.shape
    rope = rope_tabs is not None
    in_specs = [pl.BlockSpec((None, tm, cols), lambda b, i: (b, i, 0)),
                pl.BlockSpec((1, HEAD_DIM), lambda b, i: (0, 0)),
                pl.BlockSpec((1, HEAD_DIM), lambda b, i: (0, 0))]
    args = [qkv, qg.reshape(1, HEAD_DIM), kg.reshape(1, HEAD_DIM)]
    if rope:
        in_specs += [pl.BlockSpec((tm, HEAD_DIM), lambda b, i: (i, 0))] * 2
        args += list(rope_tabs)
    hspec = lambda n: pl.BlockSpec((None, n, tm, HEAD_DIM), lambda b, i: (b, 0, i, 0))
    hshape = lambda n: jax.ShapeDtypeStruct((bsz, n, seq, HEAD_DIM), BF16)
    tspec = lambda n: pl.BlockSpec((None, n, HEAD_DIM, tm), lambda b, i: (b, 0, 0, i))
    tshape = lambda n: jax.ShapeDtypeStruct((bsz, n, HEAD_DIM, seq), BF16)
    out_specs = [hspec(N_KV_HEADS), tspec(N_KV_HEADS)]
    out_shape = [hshape(N_KV_HEADS), tshape(N_KV_HEADS)]
    if want_q:
        out_specs = [hspec(N_Q_HEADS)] + out_specs
        out_shape = [hshape(N_Q_HEADS)] + out_shape
    return pl.pallas_call(
        functools.partial(_aprep_kernel, rope=rope, want_q=want_q),
        grid=(bsz, seq // tm),
        in_specs=in_specs,
        out_specs=out_specs,
        out_shape=out_shape,
        compiler_params=_cparams(("parallel", "parallel")),
        name="aprep_x" if rope else "aprep_ctx",
    )(*args)


def _rope_tables(n_tokens):
    rows = n_tokens // GRID_W
    axis = HEAD_DIM // 2
    row_id = jnp.repeat(jnp.arange(rows), GRID_W).astype(F32)
    col_id = jnp.tile(jnp.arange(GRID_W), rows).astype(F32)
    inv_freq = ROPE_THETA ** (-jnp.arange(0, axis, 2, dtype=F32) / axis)
    ang_r = row_id[:, None] * inv_freq[None]
    ang_c = col_id[:, None] * inv_freq[None]
    cr, sr, cc, sc = jnp.cos(ang_r), jnp.sin(ang_r), jnp.cos(ang_c), jnp.sin(ang_c)
    cos = jnp.concatenate([cr, cc, cr, cc], axis=-1)
    sin = jnp.concatenate([-sr, -sc, sr, sc], axis=-1)
    return cos, sin


def _rope_perm():
    quarter = HEAD_DIM // 4
    return jnp.concatenate([jnp.arange(0, quarter), jnp.arange(2 * quarter, 3 * quarter),
                            jnp.arange(quarter, 2 * quarter), jnp.arange(3 * quarter, 4 * quarter)])


def _flash_kernel(q_ref, k_ref, vt_ref, o_ref, sa_ref, sb_ref, *, tq, tk, nk):
    m_rows = Q_PER_KV * tq
    q = q_ref[...].reshape(m_rows, HEAD_DIM)

    def scores(j):
        c0 = pl.multiple_of(j * tk, tk)
        return lax.dot_general(k_ref[pl.ds(c0, tk), :], q, (((1,), (1,)), ((), ())),
                               preferred_element_type=F32)

    def update(j, st, carry):
        m, l, acc = carry
        m_new = jnp.maximum(m, jnp.max(st, axis=0, keepdims=True))
        alpha = jnp.exp2(m - m_new)
        p = jnp.exp2(st - m_new)
        l = alpha * l + jnp.sum(p, axis=0, keepdims=True)
        acc = alpha * acc + jnp.dot(vt_ref[j], p.astype(BF16), preferred_element_type=F32)
        return m_new, l, acc

    bufs = (sa_ref, sb_ref)
    sa_ref[...] = scores(0)

    def run(j0, count, carry, prefetch_last):
        for u in range(count):
            if u + 1 < count or prefetch_last:
                bufs[(u + 1) % 2][...] = scores(j0 + u + 1)
            carry = update(j0 + u, bufs[u % 2][...], carry)
        return carry

    init = (jnp.full((1, m_rows), NEG_BIG, F32), jnp.zeros((1, m_rows), F32),
            jnp.zeros((HEAD_DIM, m_rows), F32))
    niter = (nk - 1) // FLASH_UNROLL
    carry = lax.fori_loop(0, niter, lambda jj, c: run(jj * FLASH_UNROLL, FLASH_UNROLL, c, True), init)
    _, l, acc = run(niter * FLASH_UNROLL, nk - niter * FLASH_UNROLL, carry, False)
    o = (acc / l).T
    for r in range(Q_PER_KV):
        o_ref[:, r * HEAD_DIM:(r + 1) * HEAD_DIM] = o[r * tq:(r + 1) * tq].astype(o_ref.dtype)


def _flash_call(q, k, vt, tq, tk):
    bsz, _, seq, _ = q.shape
    lk = k.shape[2]
    nk = lk // tk
    q5 = q.reshape(bsz, N_KV_HEADS, Q_PER_KV, seq, HEAD_DIM)
    vt5 = jnp.transpose(vt.reshape(bsz, N_KV_HEADS, HEAD_DIM, nk, tk), (0, 1, 3, 2, 4))
    return pl.pallas_call(
        functools.partial(_flash_kernel, tq=tq, tk=tk, nk=nk),
        grid=(bsz, N_KV_HEADS, seq // tq),
        in_specs=[pl.BlockSpec((None, None, Q_PER_KV, tq, HEAD_DIM), lambda b, g, i: (b, g, 0, i, 0)),
                  pl.BlockSpec((None, None, lk, HEAD_DIM), lambda b, g, i: (b, g, 0, 0)),
                  pl.BlockSpec((None, None, nk, HEAD_DIM, tk), lambda b, g, i: (b, g, 0, 0, 0))],
        out_specs=pl.BlockSpec((None, tq, Q_PER_KV * HEAD_DIM), lambda b, g, i: (b, i, g)),
        out_shape=jax.ShapeDtypeStruct((bsz, seq, N_Q_HEADS * HEAD_DIM), BF16),
        scratch_shapes=[pltpu.VMEM((tk, Q_PER_KV * tq), F32)] * 2,
        compiler_params=_cparams(("parallel", "parallel", "parallel")),
        name="flash",
    )(q5, k, vt5)


def _row_tile(seq, cap):
    t = min(seq, cap)
    while seq % t:
        t //= 2
    return t


def _even_weights(w_in, lru_wa, lru_wi, dn_alog, dn_dtb, dn_g):
    d, cols = w_in.shape
    ngate = 4 * DN_HEADS
    main = cols - ngate
    w_gates = jnp.concatenate([w_in[:, main:], jnp.zeros((d, LANES - ngate), F32)], axis=1)

    def blockdiag(w):
        z = jnp.zeros_like(w[:, 0::2])
        top = jnp.concatenate([w[:, 0::2], z], axis=-1)
        bot = jnp.concatenate([z, w[:, 1::2]], axis=-1)
        return jnp.concatenate([top, bot], axis=-2).astype(BF16)

    alog = dn_alog.reshape(-1)
    dtb = dn_dtb.reshape(-1)
    cvec = jnp.zeros((SUBLANES, LANES), F32)
    cvec = cvec.at[0, 2 * DN_HEADS:ngate].set(alog).at[1, 2 * DN_HEADS:ngate].set(dtb)

    ccol = jnp.zeros((2, ngate), F32).at[0, 2 * DN_HEADS:].set(alog).at[1, 2 * DN_HEADS:].set(dtb)

    return dict(w_in=w_in[:, :main].astype(BF16), w_gates=w_gates.astype(BF16),
                wa=blockdiag(lru_wa), wi=blockdiag(lru_wi), cvec=cvec, ccol=ccol,
                dn_g=jnp.tile(dn_g, DN_HEADS).reshape(1, -1))


def _even_stream(h_in, res, gate, norm_g, sc, sh, ew, lru_cw, lru_cb, lru_ba, lru_bi, lru_lam, dn_cw,
                 w_out, h0, s0, tm_in, tm, rows, dn_rows):
    width = lru_cw.shape[1]
    ngate = 4 * DN_HEADS
    proj, gates = _normlin_call(h_in, norm_g, sc, sh, ew["w_in"], ew["w_gates"], tm_in)
    lru, h_t = _lru_call(proj, lru_cw, lru_cb, ew["wa"], ew["wi"], lru_ba, lru_bi, lru_lam, h0,
                         _row_tile(proj.shape[1], LRU_ROWS))
    qkv = _dnprep_call(proj, dn_cw, 2 * width // LANES, _row_tile(proj.shape[1], DNPREP_ROWS))
    gates_t = jnp.transpose(gates[:, :, :ngate], (0, 2, 1))
    ccol = jnp.broadcast_to(ew["ccol"][:, :, None], (2, ngate, dn_rows))
    outs, states = [], []
    for rev in (False, True):
        o_d, s_d = _delta_call(qkv, gates, gates_t, ew["cvec"], ccol,
                               s0[:, :, 1 if rev else 0], rev, dn_rows)
        outs.append(o_d)
        states.append(s_d)
    z_block = (2 * width + dn_cw.shape[1]) // (DN_HEADS * DN_DK)
    out = _evenout_call(lru, outs[0], outs[1], proj, z_block, ew["dn_g"], w_out, res, gate, tm)
    return out, h_t, jnp.stack(states, axis=2)


def kernel(x, c, ctx, c_ctx, norm_mix_g, norm_mlp_g, w_mod, b_mod, mlp_w1, mlp_w2, even_w_in, lru_conv_w,
           lru_conv_b, lru_w_a, lru_b_a, lru_w_i, lru_b_i, lru_lambda, dn_conv_w, dn_a_log, dn_dt_bias,
           dn_norm_g, even_w_out, attn_w_qkv, attn_q_norm_g, attn_k_norm_g, attn_w_o, final_norm_g):
    bsz, seq, d = x.shape
    lc = ctx.shape[1]
    depth = w_mod.shape[0]
    tm_x = _row_tile(seq, 512)
    tm_c = _row_tile(lc, 512)
    tm_mlp = _row_tile(seq, 1024)
    rows_x = _row_tile(seq, 256)
    rows_c = _row_tile(lc, 256)

    cc = jnp.zeros((SUBLANES, d), F32).at[:bsz].set(c).at[bsz].set(c_ctx)
    mod = _mod_call(cc, w_mod, b_mod).reshape(depth, SUBLANES, 6, d)

    rope = _rope_tables(seq)
    xs, cs = x, ctx
    for l in range(depth):
        last = l == depth - 1
        j = l // 2
        mx = [mod[l, :bsz, k][:, None, :] for k in range(6)]
        mc = [jnp.broadcast_to(mod[l, bsz, k][None, None, :], (bsz, 1, d)) for k in range(6)]
        sh1, sc1, g1, sh2, sc2, g2 = mx
        csh1, csc1, cg1, csh2, csc2, cg2 = mc
        w1 = mlp_w1[l].astype(BF16)
        w2 = mlp_w2[l].astype(BF16)
        if l % 2 == 0:
            ew = _even_weights(even_w_in[j], lru_w_a[j], lru_w_i[j], dn_a_log[j], dn_dt_bias[j], dn_norm_g[j])
            w_out = even_w_out[j].astype(BF16)
            h0 = jnp.zeros((bsz, 2, lru_conv_w.shape[2]), F32)
            s0 = jnp.zeros((bsz, DN_HEADS, 2, DN_DK, DN_DK), F32)
            common = (ew, lru_conv_w[j], lru_conv_b[j], lru_b_a[j], lru_b_i[j], lru_lambda[j], dn_conv_w[j], w_out)
            cs_mix, h_t, s_t = _even_stream(cs, cs, cg1, norm_mix_g[l], csc1, csh1, *common, h0, s0,
                                            tm_c, tm_c, rows_c, rows_c)
            xs, _, _ = _even_stream(xs, xs, g1, norm_mix_g[l], sc1, sh1, *common, h_t, s_t,
                                    tm_mlp, tm_x, rows_x, rows_x)
        else:
            perm = _rope_perm()
            nqk = (N_Q_HEADS + N_KV_HEADS) * HEAD_DIM
            w_qk = attn_w_qkv[j][:, :nqk].reshape(d, N_Q_HEADS + N_KV_HEADS, HEAD_DIM)[:, :, perm]
            w_qkv = jnp.concatenate([w_qk.reshape(d, nqk), attn_w_qkv[j][:, nqk:]], axis=1).astype(BF16)
            q_g = attn_q_norm_g[j][perm]
            k_g = attn_k_norm_g[j][perm]
            w_o = attn_w_o[j].astype(BF16)
            qkv_c = _normlin_call(cs, norm_mix_g[l], csc1, csh1, w_qkv, None, tm_c)
            qkv_x = _normlin_call(xs, norm_mix_g[l], sc1, sh1, w_qkv, None, tm_mlp)
            if last:
                k_c, vt_c = _aprep_call(qkv_c, q_g, k_g,None, False, rows_c)
            else:
                qt_c, k_c, vt_c = _aprep_call(qkv_c, q_g, k_g,None, True, rows_c)
            qt_x, k_x, vt_x = _aprep_call(qkv_x, q_g, k_g,rope, True, tm_x)
            k_all = jnp.concatenate([k_c, k_x], axis=2)
            vt_all = jnp.concatenate([vt_c, vt_x], axis=3)
            o_x = _flash_call(qt_x, k_all, vt_all, rows_x, _col_tile(lc + seq, FLASH_TK_CAP))
            xs = _linres_call(o_x, w_o, xs, g1, tm_x)
            if not last:
                o_c = _flash_call(qt_c, k_c, vt_c, rows_c, _col_tile(lc, FLASH_TK_CAP))
                cs_mix = _linres_call(o_c, w_o, cs, cg1, tm_c)
        xs = _mlp_call(xs, norm_mlp_g[l], sc2, sh2, g2, w1, w2, final_norm_g, tm_mlp, last)
        if not last:
            cs = _mlp_call(cs_mix, norm_mlp_g[l], csc2, csh2, cg2, w1, w2, final_norm_g, tm_c, False)
    return xs
```

```python
import functools

import jax
import jax.numpy as jnp
from jax import lax
from jax.experimental import pallas as pl
from jax.experimental.pallas import tpu as pltpu

F32 = jnp.float32
BF16 = jnp.bfloat16

EPS = 1e-6
LANES = 128
SUBLANES = 8
VMEM_LIMIT = 56 * 1024 * 1024

LRU_BLOCK = 64
LRU_C = 8.0
CONV_WIDTH = 4
CONV_LEFT = 2
DN_HEADS = 8
DN_DK = 128
DN_CHUNK = 64
N_Q_HEADS = 8
N_KV_HEADS = 4
Q_PER_KV = N_Q_HEADS // N_KV_HEADS
HEAD_DIM = 128
GRID_W = 64
ROPE_THETA = 10000.0
NEG_BIG = -1e30
LOG2E = 1.4426950408889634
DNPREP_ROWS = 1024
LRU_ROWS = 1024
FLASH_TK_CAP = 768
FLASH_UNROLL = 4


def _cparams(sem):
    return pltpu.CompilerParams(dimension_semantics=sem, vmem_limit_bytes=VMEM_LIMIT)


def _bdot(a, b):
    return jnp.dot(a.astype(BF16), b.astype(BF16), preferred_element_type=F32)


def _bdot_nt(a, b):
    return lax.dot_general(a.astype(BF16), b.astype(BF16), (((1,), (1,)), ((), ())),
                           preferred_element_type=F32)


def _bdot_tn(a, b):
    return lax.dot_general(a.astype(BF16), b.astype(BF16), (((0,), (0,)), ((), ())),
                           preferred_element_type=F32)


def _sigmoid(x):
    return jax.nn.sigmoid(x)


def _softplus(x):
    return jnp.maximum(x, 0.0) + jnp.log1p(jnp.exp(-jnp.abs(x)))


def _rms(x):
    return x * lax.rsqrt(jnp.mean(x * x, axis=-1, keepdims=True) + EPS)


def _col_tile(n, cap):
    best = LANES
    t = LANES
    while t <= min(n, cap):
        if n % t == 0:
            best = t
        t += LANES
    return best


def _mod_kernel(c_ref, w_ref, b_ref, o_ref):
    s = c_ref[...]
    s = s * _sigmoid(s)
    o_ref[...] = _bdot(s, w_ref[...]) + b_ref[...]


def _mod_call(cc, w_mod, b_mod):
    depth, d, n = w_mod.shape
    tn = _col_tile(n, 1536)
    return pl.pallas_call(
        _mod_kernel,
        grid=(depth, n // tn),
        in_specs=[pl.BlockSpec((SUBLANES, d), lambda l, j: (0, 0)),
                  pl.BlockSpec((None, d, tn), lambda l, j: (l, 0, j)),
                  pl.BlockSpec((None, 1, tn), lambda l, j: (l, 0, j))],
        out_specs=pl.BlockSpec((None, SUBLANES, tn), lambda l, j: (l, 0, j)),
        out_shape=jax.ShapeDtypeStruct((depth, SUBLANES, n), F32),
        compiler_params=_cparams(("parallel", "parallel")),
        name="mod",
    )(cc, w_mod, b_mod.reshape(depth, 1, n))


def _normlin_kernel(*refs, has_gates):
    if has_gates:
        x_ref, g_ref, sc_ref, sh_ref, w_ref, wg_ref, wgt_ref, o_ref, og_ref, ogt_ref, h_ref = refs
    else:
        x_ref, g_ref, sc_ref, sh_ref, w_ref, o_ref, h_ref = refs

    @pl.when(pl.program_id(2) == 0)
    def _():
        y = _rms(x_ref[...]) * g_ref[...]
        h_ref[...] = (y * (1.0 + sc_ref[...]) + sh_ref[...]).astype(BF16)
        if has_gates:
            og_ref[...] = jnp.dot(h_ref[...], wg_ref[...], preferred_element_type=F32)
            ogt_ref[...] = lax.dot_general(wgt_ref[...], h_ref[...], (((1,), (1,)), ((), ())),
                                           preferred_element_type=F32)

    o_ref[...] = jnp.dot(h_ref[...], w_ref[...], preferred_element_type=F32).astype(o_ref.dtype)


def _normlin_call(x, g, sc, sh, w, w_gates, w_gates_t, tm):
    bsz, seq, d = x.shape
    n = w.shape[1]
    tn = _col_tile(n, 1024)
    has_gates = w_gates is not None
    in_specs = [pl.BlockSpec((None, tm, d), lambda b, i, j: (b, i, 0)),
                pl.BlockSpec((1, d), lambda b, i, j: (0, 0)),
                pl.BlockSpec((None, 1, d), lambda b, i, j: (b, 0, 0)),
                pl.BlockSpec((None, 1, d), lambda b, i, j: (b, 0, 0)),
                pl.BlockSpec((d, tn), lambda b, i, j: (0, j))]
    out_specs = [pl.BlockSpec((None, tm, tn), lambda b, i, j: (b, i, j))]
    out_shape = [jax.ShapeDtypeStruct((bsz, seq, n), BF16)]
    args = [x, g.reshape(1, d), sc, sh, w]
    if has_gates:
        ng = w_gates.shape[1]
        in_specs.append(pl.BlockSpec((d, ng), lambda b, i, j: (0, 0)))
        out_specs.append(pl.BlockSpec((None, tm, ng), lambda b, i, j: (b, i, 0)))
        out_shape.append(jax.ShapeDtypeStruct((bsz, seq, ng), F32))
        ngt = w_gates_t.shape[0]
        in_specs.append(pl.BlockSpec((ngt, d), lambda b, i, j: (0, 0)))
        out_specs.append(pl.BlockSpec((None, ngt, tm), lambda b, i, j: (b, 0, i)))
        out_shape.append(jax.ShapeDtypeStruct((bsz, ngt, seq), F32))
        args += [w_gates, w_gates_t]
    out = pl.pallas_call(
        functools.partial(_normlin_kernel, has_gates=has_gates),
        grid=(bsz, seq // tm, n // tn),
        in_specs=in_specs,
        out_specs=out_specs,
        out_shape=out_shape,
        scratch_shapes=[pltpu.VMEM((tm, d), BF16)],
        compiler_params=_cparams(("parallel", "parallel", "arbitrary")),
        name="normlin",
    )(*args)
    return out if has_gates else out[0]


def _mlp_kernel(x_ref, g_ref, sc_ref, sh_ref, gate_ref, w1_ref, w2_ref, fg_ref, o_ref,
                h_ref, acc_ref, *, final_norm):
    k = pl.program_id(2)

    @pl.when(k == 0)
    def _():
        y = _rms(x_ref[...]) * g_ref[...]
        h_ref[...] = (y * (1.0 + sc_ref[...]) + sh_ref[...]).astype(BF16)
        acc_ref[...] = jnp.zeros_like(acc_ref)

    a = jnp.dot(h_ref[...], w1_ref[...], preferred_element_type=F32)
    a = jnp.square(jnp.maximum(a, 0.0)).astype(BF16)
    acc_ref[...] += jnp.dot(a, w2_ref[...], preferred_element_type=F32)

    @pl.when(k == pl.num_programs(2) - 1)
    def _():
        y = x_ref[...] + gate_ref[...] * acc_ref[...]
        if final_norm:
            y = _rms(y) * fg_ref[...]
        o_ref[...] = y


def _mlp_call(x, g, sc, sh, gate, w1, w2, fg, tm, final_norm):
    bsz, seq, d = x.shape
    f = w1.shape[1]
    tf = _col_tile(f, 1024)
    vec = pl.BlockSpec((None, 1, d), lambda b, i, k: (b, 0, 0))
    return pl.pallas_call(
        functools.partial(_mlp_kernel, final_norm=final_norm),
        grid=(bsz, seq // tm, f // tf),
        in_specs=[pl.BlockSpec((None, tm, d), lambda b, i, k: (b, i, 0)),
                  pl.BlockSpec((1, d), lambda b, i, k: (0, 0)),
                  vec, vec, vec,
                  pl.BlockSpec((d, tf), lambda b, i, k: (0, k)),
                  pl.BlockSpec((tf, d), lambda b, i, k: (k, 0)),
                  pl.BlockSpec((1, d), lambda b, i, k: (0, 0))],
        out_specs=pl.BlockSpec((None, tm, d), lambda b, i, k: (b, i, 0)),
        out_shape=jax.ShapeDtypeStruct((bsz, seq, d), F32),
        scratch_shapes=[pltpu.VMEM((tm, d), BF16), pltpu.VMEM((tm, d), F32)],
        compiler_params=_cparams(("parallel", "parallel", "arbitrary")),
        name="mlp",
    )(x, g.reshape(1, d), sc, sh, gate, w1, w2, fg.reshape(1, d))


CONV_PAD = SUBLANES


def _fill_padded(xpad_ref, x_ref, seq, rows):
    zeros = jnp.zeros((CONV_PAD, xpad_ref.shape[1]), F32)
    xpad_ref[0:CONV_PAD, :] = zeros
    xpad_ref[CONV_PAD + seq:CONV_PAD + seq + CONV_PAD, :] = zeros

    def body(t, carry):
        r0 = pl.multiple_of(t * rows, rows)
        xpad_ref[pl.ds(r0 + CONV_PAD, rows), :] = x_ref[pl.ds(r0, rows), :].astype(F32)
        return carry

    lax.fori_loop(0, seq // rows, body, 0)


def _conv_tile(xpad_ref, r0, rows, w):
    acc = None
    for j in range(CONV_WIDTH):
        off = j - CONV_LEFT
        term = xpad_ref[pl.ds(r0 + (CONV_PAD + off), rows), :] * w[j:j + 1]
        acc = term if acc is None else acc + term
    return acc


def _dnprep_kernel(x_ref, w_ref, o_ref, xpad_ref, *, seq, rows):
    j = pl.program_id(1)
    is_qk = j < 2 * DN_HEADS
    scale = jnp.where(j < DN_HEADS, DN_DK ** -0.5, 1.0).astype(F32)
    _fill_padded(xpad_ref, x_ref, seq, rows)
    w = w_ref[...]

    def body(t, carry):
        r0 = pl.multiple_of(t * rows, rows)
        y = _conv_tile(xpad_ref, r0, rows, w)
        y = y * _sigmoid(y)
        nrm = (y * lax.rsqrt(jnp.sum(y * y, axis=-1, keepdims=True) + EPS)) * scale
        o_ref[pl.ds(r0, rows), :] = jnp.where(is_qk, nrm, y).astype(o_ref.dtype)
        return carry

    lax.fori_loop(0, seq // rows, body, 0)


def _dnprep_call(proj, conv_w, col0_blocks, rows):
    bsz, seq, _ = proj.shape
    ncol = conv_w.shape[1] // LANES
    return pl.pallas_call(
        functools.partial(_dnprep_kernel, seq=seq, rows=rows),
        grid=(bsz, ncol),
        in_specs=[pl.BlockSpec((None, seq, LANES), lambda b, j: (b, 0, col0_blocks + j)),
                  pl.BlockSpec((CONV_WIDTH, LANES), lambda b, j: (0, j))],
        out_specs=pl.BlockSpec((None, seq, LANES), lambda b, j: (b, 0, j)),
        out_shape=jax.ShapeDtypeStruct((bsz, seq, ncol * LANES), BF16),
        scratch_shapes=[pltpu.VMEM((seq + 2 * CONV_PAD, LANES), F32)],
        compiler_params=_cparams(("parallel", "parallel")),
        name="dnprep",
    )(proj, conv_w)


def _scan_tile(a, b, carry, rev, rows):
    nv = rows // SUBLANES
    lanes = a.shape[1]
    a = a.reshape(nv, SUBLANES, lanes)
    b = b.reshape(nv, SUBLANES, lanes)
    row = lax.broadcasted_iota(jnp.int32, a.shape, 1)
    s = 1
    while s < SUBLANES:
        shift = SUBLANES - s if rev else s
        a_s = pltpu.roll(a, shift, 1)
        b_s = pltpu.roll(b, shift, 1)
        valid = (row < SUBLANES - s) if rev else (row >= s)
        b = b + a * jnp.where(valid, b_s, 0.0)
        a = a * jnp.where(valid, a_s, 1.0)
        s *= 2
    hs = [None] * nv
    for v in (range(nv - 1, -1, -1) if rev else range(nv)):
        h = a[v] * carry + b[v]
        carry = h[0:1] if rev else h[SUBLANES - 1:SUBLANES]
        hs[v] = h
    return jnp.concatenate(hs, axis=0), carry


def _gelu_tanh(x):
    return 0.5 * x * (1.0 + jnp.tanh(0.7978845608028654 * (x + 0.044715 * (x * x * x))))


def _lru_kernel(x_ref, y_ref, cw_ref, cb_ref, wa_ref, wi_ref, ba_ref, bi_ref, lam_ref, h0_ref,
                o_ref, ht_ref, xpad_ref, xc_ref, hf_ref, *, seq, rows):
    _fill_padded(xpad_ref, x_ref, seq, rows)
    cw = cw_ref[...]
    cb = cb_ref[...]
    nt = seq // rows

    def gates(xc, d):
        r = _sigmoid(_bdot(xc, wa_ref[d]) + ba_ref[d:d + 1, :])
        i = _sigmoid(_bdot(xc, wi_ref[d]) + bi_ref[d:d + 1, :])
        log_a = (-LRU_C * r) * _softplus(-lam_ref[d:d + 1, :])
        a = jnp.exp(log_a)
        b = jnp.sqrt(-jnp.tanh(log_a) * (a * a + 1.0)) * (i * xc)
        return a, b

    def fwd_body(t, carry):
        r0 = pl.multiple_of(t * rows, rows)
        xc = _conv_tile(xpad_ref, r0, rows, cw) + cb
        xc_ref[pl.ds(r0, rows), :] = xc
        a, b = gates(xc, 0)
        h, carry = _scan_tile(a, b, carry, False, rows)
        hf_ref[pl.ds(r0, rows), :] = h
        return carry

    carry_f = lax.fori_loop(0, nt, fwd_body, h0_ref[0:1, :])

    def bwd_body(t, carry):
        r0 = pl.multiple_of((nt - 1 - t) * rows, rows)
        xc = xc_ref[pl.ds(r0, rows), :]
        a, b = gates(xc, 1)
        h, carry = _scan_tile(a, b, carry, True, rows)
        y = y_ref[pl.ds(r0, rows), :].astype(F32)
        o_ref[pl.ds(r0, rows), :] = ((hf_ref[pl.ds(r0, rows), :] + h) * _gelu_tanh(y)).astype(o_ref.dtype)
        return carry

    carry_b = lax.fori_loop(0, nt, bwd_body, h0_ref[1:2, :])
    ht_ref[0:1, :] = carry_f
    ht_ref[1:2, :] = carry_b


def _lru_call(proj, cw, cb, wa, wi, ba, bi, lam, h0, rows):
    bsz, seq, _ = proj.shape
    width = cw.shape[1]
    ncol = width // LANES
    vec2 = pl.BlockSpec((2, LANES), lambda b, j: (0, j))
    wspec = pl.BlockSpec((2, None, LANES, LANES), lambda b, j: (0, j, 0, 0))
    return pl.pallas_call(
        functools.partial(_lru_kernel, seq=seq, rows=rows),
        grid=(bsz, ncol),
        in_specs=[pl.BlockSpec((None, seq, LANES), lambda b, j: (b, 0, j)),
                  pl.BlockSpec((None, seq, LANES), lambda b, j: (b, 0, ncol + j)),
                  pl.BlockSpec((CONV_WIDTH, LANES), lambda b, j: (0, j)),
                  pl.BlockSpec((1, LANES), lambda b, j: (0, j)),
                  wspec, wspec, vec2, vec2, vec2,
                  pl.BlockSpec((None, 2, LANES), lambda b, j: (b, 0, j))],
        out_specs=[pl.BlockSpec((None, seq, LANES), lambda b, j: (b, 0, j)),
                   pl.BlockSpec((None, 2, LANES), lambda b, j: (b, 0, j))],
        out_shape=[jax.ShapeDtypeStruct((bsz, seq, width), BF16),
                   jax.ShapeDtypeStruct((bsz, 2, width), F32)],
        scratch_shapes=[pltpu.VMEM((seq + 2 * CONV_PAD, LANES), F32),
                        pltpu.VMEM((seq, LANES), F32),
                        pltpu.VMEM((seq, LANES), F32)],
        compiler_params=_cparams(("parallel", "parallel")),
        name="lru",
    )(proj, proj, cw, cb.reshape(1, width), wa, wi, ba, bi, lam, h0)


DN_GROUP = 2 * DN_CHUNK


def _split3(x):
    x1 = x.astype(BF16)
    r1 = x - x1.astype(F32)
    x2 = r1.astype(BF16)
    x3 = (r1 - x2.astype(F32)).astype(BF16)
    return x1, x2, x3


def _delta_kernel(q_ref, k_ref, v_ref, gr_ref, grt_ref, cvec_ref, ccol_ref, cum_ref, cumt_ref,
                  ones_ref, s0_ref, o_ref, st_ref, s_ref, *, rev, rows):
    step = pl.program_id(1)
    d = 1 if rev else 0

    @pl.when(step == 0)
    def _():
        s_ref[...] = s0_ref[...]

    raw = gr_ref[...]
    sig = _sigmoid(raw)
    g = -jnp.exp(cvec_ref[0:1, :]) * _softplus(raw + cvec_ref[1:2, :])
    g1, g2, g3 = _split3(g)
    cum = cum_ref[...]
    ones = ones_ref[...]

    def mm(m, parts):
        return sum(jnp.dot(m, p, preferred_element_type=F32) for p in parts)

    gc = mm(cum, (g1, g2, g3))
    gl = mm(ones, (g1, g2, g3))
    e_gc = jnp.exp(gc)
    e_rest = jnp.exp(gl - gc)
    e_gl = jnp.exp(gl)
    gt = -jnp.exp(ccol_ref[0]) * _softplus(grt_ref[...] + ccol_ref[1])
    t1, t2, t3 = _split3(gt)
    cumt = cumt_ref[...]
    gct = sum(jnp.dot(p, cumt, preferred_element_type=F32) for p in (t1, t2, t3))

    ri = lax.broadcasted_iota(jnp.int32, (DN_GROUP, DN_GROUP), 0)
    ci = lax.broadcasted_iota(jnp.int32, (DN_GROUP, DN_GROUP), 1)
    same = (ri // DN_CHUNK) == (ci // DN_CHUNK)
    if rev:
        incl = same & (ri <= ci)
        strict = same & (ri < ci)
    else:
        incl = same & (ri >= ci)
        strict = same & (ri > ci)

    ngroups = rows // DN_GROUP
    zeros_c = jnp.zeros((DN_CHUNK, DN_DK), F32)

    def bcast(x, rsl, lane):
        return jnp.broadcast_to(x[rsl, lane:lane + 1], (DN_GROUP, LANES))

    heads = range(DN_HEADS)
    groups = list(range(ngroups - 1, -1, -1) if rev else range(ngroups))
    units = [(gi, h) for gi in groups for h in heads]
    qd, kd, egl_b, rhs, kk, qk = {}, {}, {}, {}, {}, {}
    for u in units:
        gi, h = u
        rsl = slice(gi * DN_GROUP, (gi + 1) * DN_GROUP)
        lb = d * DN_HEADS + h
        lg = 2 * DN_HEADS + d * DN_HEADS + h
        csl = slice(h * DN_DK, (h + 1) * DN_DK)
        q16 = q_ref[rsl, csl]
        k16 = k_ref[rsl, csl]
        q_h = q16.astype(F32)
        k_h = k16.astype(F32)
        beta_b = bcast(sig, rsl, lb)
        egc_b = bcast(e_gc, rsl, lg)
        kb_h = k_h * beta_b
        kk[u] = _bdot_nt(kb_h, k16)
        qk[u] = _bdot_nt(q16, k16)
        rhs[u] = jnp.concatenate([v_ref[rsl, csl].astype(F32) * beta_b, kb_h * egc_b], axis=1)
        qd[u] = q_h * egc_b
        kd[u] = k_h * bcast(e_rest, rsl, lg)
        egl_b[u] = bcast(e_gl, rsl, lg)
    p, a16, attn = {}, {}, {}
    for u in units:
        gi, h = u
        rsl = slice(gi * DN_GROUP, (gi + 1) * DN_GROUP)
        lg = 2 * DN_HEADS + d * DN_HEADS + h
        diff = bcast(gc, rsl, lg) - gct[lg:lg + 1, rsl]
        decay = jnp.exp(jnp.where(incl, diff, NEG_BIG))
        m = jnp.where(strict, kk[u] * decay, 0.0)
        attn[u] = qk[u] * decay
        p[u] = -m
        a16[u] = m.astype(BF16)
    npow = 1
    while npow < DN_CHUNK // 2:
        a = {u: jnp.dot(a16[u], a16[u], preferred_element_type=F32) for u in units}
        a16 = {u: a[u].astype(BF16) for u in units}
        p = {u: p[u] + a[u] + jnp.dot(p[u].astype(BF16), a16[u], preferred_element_type=F32) for u in units}
        npow *= 2
    uw = {u: rhs[u] + _bdot(p[u], rhs[u]) for u in units}
    s = [s_ref[h] for h in heads]
    for gi in groups:
        for cc in ((1, 0) if rev else (0, 1)):
            c0 = cc * DN_CHUNK
            cs = slice(c0, c0 + DN_CHUNK)
            r2 = [_bdot(jnp.concatenate([uw[gi, h][cs, DN_DK:], qd[gi, h][cs]], axis=0), s[h]) for h in heads]
            v_new = [uw[gi, h][cs, :DN_DK] - r2[h][:DN_CHUNK] for h in heads]
            for h in heads:
                v_pad = (jnp.concatenate([v_new[h], zeros_c], axis=0) if cc == 0
                         else jnp.concatenate([zeros_c, v_new[h]], axis=0))
                o_c = r2[h][DN_CHUNK:] + _bdot(attn[gi, h][cs], v_pad)
                o_ref[gi * DN_GROUP + c0:gi * DN_GROUP + c0 + DN_CHUNK,
                      h * DN_DK:(h + 1) * DN_DK] = o_c.astype(o_ref.dtype)
            for h in heads:
                egl = jnp.concatenate([egl_b[gi, h][cs], egl_b[gi, h][cs]], axis=0)
                s[h] = s[h] * egl + _bdot_tn(kd[gi, h][cs], v_new[h])
    for h in heads:
        s_ref[h] = s[h]

    @pl.when(step == pl.num_programs(1) - 1)
    def _():
        st_ref[...] = s_ref[...]


def _delta_consts(rows, rev):
    t = jnp.arange(rows)
    same = (t[:, None] // DN_CHUNK) == (t[None, :] // DN_CHUNK)
    cum = same & ((t[:, None] <= t[None, :]) if rev else (t[:, None] >= t[None, :]))
    return cum.astype(BF16), cum.T.astype(BF16), same.astype(BF16)


def _delta_call(qkv, gates, gates_t, cvec, ccol, s0, rev, rows):
    bsz, seq, _ = qkv.shape
    width = DN_HEADS * DN_DK
    nsteps = seq // rows
    cum, cumt, ones = _delta_consts(rows, rev)

    def blk(i):
        return nsteps - 1 - i if rev else i

    full = lambda shape: pl.BlockSpec(shape, lambda b, i: (0,) * len(shape))
    state = pl.BlockSpec((None, DN_HEADS, DN_DK, DN_DK), lambda b, i: (b, 0, 0, 0))
    return pl.pallas_call(
        functools.partial(_delta_kernel, rev=rev, rows=rows),
        grid=(bsz, nsteps),
        in_specs=[pl.BlockSpec((None, rows, width), lambda b, i: (b, blk(i), 0)),
                  pl.BlockSpec((None, rows, width), lambda b, i: (b, blk(i), 1)),
                  pl.BlockSpec((None, rows, width), lambda b, i: (b, blk(i), 2)),
                  pl.BlockSpec((None, rows, LANES), lambda b, i: (b, blk(i), 0)),
                  pl.BlockSpec((None, 4 * DN_HEADS, rows), lambda b, i: (b, 0, blk(i))),
                  full((SUBLANES, LANES)),
                  full((2, 4 * DN_HEADS, rows)),
                  full((rows, rows)), full((rows, rows)), full((rows, rows)),
                  state],
        out_specs=[pl.BlockSpec((None, rows, width), lambda b, i: (b, blk(i), 0)), state],
        out_shape=[jax.ShapeDtypeStruct((bsz, seq, width), BF16),
                   jax.ShapeDtypeStruct((bsz, DN_HEADS, DN_DK, DN_DK), F32)],
        scratch_shapes=[pltpu.VMEM((DN_HEADS, DN_DK, DN_DK), F32)],
        compiler_params=_cparams(("parallel", "arbitrary")),
        name="delta_bwd" if rev else "delta_fwd",
    )(qkv, qkv, qkv, gates, gates_t, cvec, ccol, cum, cumt, ones, s0)


def _evenout_kernel(lru_ref, of_ref, ob_ref, z_ref, g_ref, w_ref, res_ref, gate_ref, o_ref, a_ref):
    width = lru_ref.shape[1]
    a_ref[:, 0:width] = lru_ref[...].astype(BF16)
    for h in range(DN_HEADS):
        csl = slice(h * DN_DK, (h + 1) * DN_DK)
        o = of_ref[:, csl].astype(F32) + ob_ref[:, csl].astype(F32)
        z = z_ref[:, csl].astype(F32)
        y = (_rms(o) * g_ref[:, csl]) * (z * _sigmoid(z))
        a_ref[:, width + h * DN_DK:width + (h + 1) * DN_DK] = y.astype(BF16)
    o_ref[...] = res_ref[...] + gate_ref[...] * jnp.dot(a_ref[...], w_ref[...], preferred_element_type=F32)


def _evenout_call(lru, o_f, o_b, proj, z_block, dn_g, w_out, res, gate, tm):
    bsz, seq, width = lru.shape
    dnw = o_f.shape[2]
    d = res.shape[2]
    row = lambda c: pl.BlockSpec((None, tm, c), lambda b, i: (b, i, 0))
    return pl.pallas_call(
        _evenout_kernel,
        grid=(bsz, seq // tm),
        in_specs=[row(width), row(dnw), row(dnw),
                  pl.BlockSpec((None, tm, dnw), lambda b, i: (b, i, z_block)),
                  pl.BlockSpec((1, dnw), lambda b, i: (0, 0)),
                  pl.BlockSpec((width + dnw, d), lambda b, i: (0, 0)),
                  row(d),
                  pl.BlockSpec((None, 1, d), lambda b, i: (b, 0, 0))],
        out_specs=row(d),
        out_shape=jax.ShapeDtypeStruct((bsz, seq, d), F32),
        scratch_shapes=[pltpu.VMEM((tm, width + dnw), BF16)],
        compiler_params=_cparams(("parallel", "parallel")),
        name="evenout",
    )(lru, o_f, o_b, proj, dn_g, w_out, res, gate)


def _linres_kernel(a_ref, w_ref, res_ref, gate_ref, o_ref):
    o_ref[...] = res_ref[...] + gate_ref[...] * jnp.dot(a_ref[...].astype(BF16), w_ref[...],
                                                        preferred_element_type=F32)


def _linres_call(a, w, res, gate, tm):
    bsz, seq, kdim = a.shape
    d = res.shape[2]
    return pl.pallas_call(
        _linres_kernel,
        grid=(bsz, seq // tm),
        in_specs=[pl.BlockSpec((None, tm, kdim), lambda b, i: (b, i, 0)),
                  pl.BlockSpec((kdim, d), lambda b, i: (0, 0)),
                  pl.BlockSpec((None, tm, d), lambda b, i: (b, i, 0)),
                  pl.BlockSpec((None, 1, d), lambda b, i: (b, 0, 0))],
        out_specs=pl.BlockSpec((None, tm, d), lambda b, i: (b, i, 0)),
        out_shape=jax.ShapeDtypeStruct((bsz, seq, d), F32),
        compiler_params=_cparams(("parallel", "parallel")),
        name="linres",
    )(a, w, res, gate)


def _aprep_kernel(*refs, rope, want_q):
    if rope:
        x_ref, qg_ref, kg_ref, cos_ref, sin_ref = refs[:5]
        outs = refs[5:]
    else:
        x_ref, qg_ref, kg_ref = refs[:3]
        outs = refs[3:]
    if want_q:
        q_ref, k_ref, v_ref = outs
    else:
        k_ref, v_ref = outs

    if rope:
        cos = cos_ref[...]
        sin = sin_ref[...]

    def norm_rope(x, g):
        y = _rms(x) * g
        if rope:
            y = y * cos + pltpu.roll(y, HEAD_DIM // 2, 1) * sin
        return y

    if want_q:
        for h in range(N_Q_HEADS):
            x = x_ref[:, h * HEAD_DIM:(h + 1) * HEAD_DIM].astype(F32)
            q_ref[h] = (norm_rope(x, qg_ref[...]) * (HEAD_DIM ** -0.5 * LOG2E)).astype(BF16)
    for h in range(N_KV_HEADS):
        c0 = (N_Q_HEADS + h) * HEAD_DIM
        k_ref[h] = norm_rope(x_ref[:, c0:c0 + HEAD_DIM].astype(F32), kg_ref[...]).astype(BF16)
        c1 = (N_Q_HEADS + N_KV_HEADS + h) * HEAD_DIM
        v_ref[h] = x_ref[:, c1:c1 + HEAD_DIM].astype(F32).T.astype(BF16)


def _aprep_call(qkv, qg, kg, rope_tabs, want_q, tm):
    bsz, seq, cols = qkv.shape
    rope = rope_tabs is not None
    in_specs = [pl.BlockSpec((None, tm, cols), lambda b, i: (b, i, 0)),
                pl.BlockSpec((1, HEAD_DIM), lambda b, i: (0, 0)),
                pl.BlockSpec((1, HEAD_DIM), lambda b, i: (0, 0))]
    args = [qkv, qg.reshape(1, HEAD_DIM), kg.reshape(1, HEAD_DIM)]
    if rope:
        in_specs += [pl.BlockSpec((tm, HEAD_DIM), lambda b, i: (i, 0))] * 2
        args += list(rope_tabs)
    hspec = lambda n: pl.BlockSpec((None, n, tm, HEAD_DIM), lambda b, i: (b, 0, i, 0))
    hshape = lambda n: jax.ShapeDtypeStruct((bsz, n, seq, HEAD_DIM), BF16)
    tspec = lambda n: pl.BlockSpec((None, n, HEAD_DIM, tm), lambda b, i: (b, 0, 0, i))
    tshape = lambda n: jax.ShapeDtypeStruct((bsz, n, HEAD_DIM, seq), BF16)
    out_specs = [hspec(N_KV_HEADS), tspec(N_KV_HEADS)]
    out_shape = [hshape(N_KV_HEADS), tshape(N_KV_HEADS)]
    if want_q:
        out_specs = [hspec(N_Q_HEADS)] + out_specs
        out_shape = [hshape(N_Q_HEADS)] + out_shape
    return pl.pallas_call(
        functools.partial(_aprep_kernel, rope=rope, want_q=want_q),
        grid=(bsz, seq // tm),
        in_specs=in_specs,
        out_specs=out_specs,
        out_shape=out_shape,
        compiler_params=_cparams(("parallel", "parallel")),
        name="aprep_x" if rope else "aprep_ctx",
    )(*args)


def _rope_tables(n_tokens):
    rows = n_tokens // GRID_W
    axis = HEAD_DIM // 2
    row_id = jnp.repeat(jnp.arange(rows), GRID_W).astype(F32)
    col_id = jnp.tile(jnp.arange(GRID_W), rows).astype(F32)
    inv_freq = ROPE_THETA ** (-jnp.arange(0, axis, 2, dtype=F32) / axis)
    ang_r = row_id[:, None] * inv_freq[None]
    ang_c = col_id[:, None] * inv_freq[None]
    cr, sr, cc, sc = jnp.cos(ang_r), jnp.sin(ang_r), jnp.cos(ang_c), jnp.sin(ang_c)
    cos = jnp.concatenate([cr, cc, cr, cc], axis=-1)
    sin = jnp.concatenate([-sr, -sc, sr, sc], axis=-1)
    return cos, sin


def _rope_perm():
    quarter = HEAD_DIM // 4
    return jnp.concatenate([jnp.arange(0, quarter), jnp.arange(2 * quarter, 3 * quarter),
                            jnp.arange(quarter, 2 * quarter), jnp.arange(3 * quarter, 4 * quarter)])


def _flash_kernel(q_ref, k_ref, vt_ref, o_ref, s0_ref, sa_ref, sb_ref, *, tq, tk, nk, nt):
    m_rows = Q_PER_KV * tq

    def q_tile(i):
        r0 = pl.multiple_of(i * tq, tq)
        return q_ref[:, pl.ds(r0, tq), :].reshape(m_rows, HEAD_DIM)

    def scores(q, j):
        c0 = pl.multiple_of(j * tk, tk)
        return lax.dot_general(k_ref[pl.ds(c0, tk), :], q, (((1,), (1,)), ((), ())),
                               preferred_element_type=F32)

    def update(j, st, carry):
        m, l, acc = carry
        m_new = jnp.maximum(m, jnp.max(st, axis=0, keepdims=True))
        alpha = jnp.exp2(m - m_new)
        p = jnp.exp2(st - m_new)
        l = alpha * l + jnp.sum(p, axis=0, keepdims=True)
        vt = vt_ref[:, pl.ds(pl.multiple_of(j * tk, tk), tk)]
        acc = alpha * acc + jnp.dot(vt, p.astype(BF16), preferred_element_type=F32)
        return m_new, l, acc

    bufs = (sa_ref, sb_ref)

    def tile_body(i, _):
        q = q_tile(i)
        q_next = q_tile(jnp.minimum(i + 1, nt - 1))

        def run(j0, count, carry, last):
            for u in range(count):
                j = j0 + u
                if u + 1 < count or not last:
                    bufs[(u + 1) % 2][...] = scores(q, j + 1)
                else:
                    s0_ref[...] = scores(q_next, 0)
                carry = update(j, bufs[u % 2][...], carry)
            return carry

        carry = (jnp.full((1, m_rows), NEG_BIG, F32), jnp.zeros((1, m_rows), F32),
                 jnp.zeros((HEAD_DIM, m_rows), F32))
        sa_ref[...] = scores(q, 1)
        carry = update(0, s0_ref[...], carry)
        rest = nk - 1
        niter = (rest - 1) // FLASH_UNROLL
        carry = lax.fori_loop(0, niter, lambda jj, c: run(1 + jj * FLASH_UNROLL, FLASH_UNROLL, c, False),
                              carry)
        _, l, acc = run(1 + niter * FLASH_UNROLL, rest - niter * FLASH_UNROLL, carry, True)
        o = (acc / l).T
        r0 = pl.multiple_of(i * tq, tq)
        for r in range(Q_PER_KV):
            o_ref[pl.ds(r0, tq), r * HEAD_DIM:(r + 1) * HEAD_DIM] = o[r * tq:(r + 1) * tq].astype(o_ref.dtype)
        return 0

    s0_ref[...] = scores(q_tile(0), 0)
    lax.fori_loop(0, nt, tile_body, 0)


def _flash_call(q, k, vt, tq, tk):
    bsz, _, seq, _ = q.shape
    lk = k.shape[2]
    nk = lk // tk
    assert nk >= 2, "the score pipeline needs at least two key blocks"
    q5 = q.reshape(bsz, N_KV_HEADS, Q_PER_KV, seq, HEAD_DIM)
    return pl.pallas_call(
        functools.partial(_flash_kernel, tq=tq, tk=tk, nk=nk, nt=seq // tq),
        grid=(bsz, N_KV_HEADS),
        in_specs=[pl.BlockSpec((None, None, Q_PER_KV, seq, HEAD_DIM), lambda b, g: (b, g, 0, 0, 0)),
                  pl.BlockSpec((None, None, lk, HEAD_DIM), lambda b, g: (b, g, 0, 0)),
                  pl.BlockSpec((None, None, HEAD_DIM, lk), lambda b, g: (b, g, 0, 0))],
        out_specs=pl.BlockSpec((None, seq, Q_PER_KV * HEAD_DIM), lambda b, g: (b, 0, g)),
        out_shape=jax.ShapeDtypeStruct((bsz, seq, N_Q_HEADS * HEAD_DIM), BF16),
        scratch_shapes=[pltpu.VMEM((tk, Q_PER_KV * tq), F32)] * 3,
        compiler_params=_cparams(("parallel", "parallel")),
        name="flash",
    )(q5, k, vt)


def _row_tile(seq, cap):
    t = min(seq, cap)
    while seq % t:
        t //= 2
    return t


def _even_weights(w_in, lru_wa, lru_wi, dn_alog, dn_dtb, dn_g):
    d, cols = w_in.shape
    ngate = 4 * DN_HEADS
    main = cols - ngate
    w_gates = jnp.concatenate([w_in[:, main:], jnp.zeros((d, LANES - ngate), F32)], axis=1)

    def blockdiag(w):
        z = jnp.zeros_like(w[:, 0::2])
        top = jnp.concatenate([w[:, 0::2], z], axis=-1)
        bot = jnp.concatenate([z, w[:, 1::2]], axis=-1)
        return jnp.concatenate([top, bot], axis=-2).astype(BF16)

    alog = dn_alog.reshape(-1)
    dtb = dn_dtb.reshape(-1)
    cvec = jnp.zeros((SUBLANES, LANES), F32)
    cvec = cvec.at[0, 2 * DN_HEADS:ngate].set(alog).at[1, 2 * DN_HEADS:ngate].set(dtb)

    ccol = jnp.zeros((2, ngate), F32).at[0, 2 * DN_HEADS:].set(alog).at[1, 2 * DN_HEADS:].set(dtb)

    return dict(w_in=w_in[:, :main].astype(BF16), w_gates=w_gates.astype(BF16),
                w_gates_t=w_in[:, main:].T.astype(BF16),
                wa=blockdiag(lru_wa), wi=blockdiag(lru_wi), cvec=cvec, ccol=ccol,
                dn_g=jnp.tile(dn_g, DN_HEADS).reshape(1, -1))


def _even_stream(h_in, res, gate, norm_g, sc, sh, ew, lru_cw, lru_cb, lru_ba, lru_bi, lru_lam, dn_cw,
                 w_out, h0, s0, tm_in, tm, rows, dn_rows):
    width = lru_cw.shape[1]
    ngate = 4 * DN_HEADS
    proj, gates, gates_t = _normlin_call(h_in, norm_g, sc, sh, ew["w_in"], ew["w_gates"], ew["w_gates_t"],
                                         tm_in)
    lru, h_t = _lru_call(proj, lru_cw, lru_cb, ew["wa"], ew["wi"], lru_ba, lru_bi, lru_lam, h0,
                         _row_tile(proj.shape[1], LRU_ROWS))
    qkv = _dnprep_call(proj, dn_cw, 2 * width // LANES, _row_tile(proj.shape[1], DNPREP_ROWS))
    ccol = jnp.broadcast_to(ew["ccol"][:, :, None], (2, ngate, dn_rows))
    outs, states = [], []
    for rev in (False, True):
        o_d, s_d = _delta_call(qkv, gates, gates_t, ew["cvec"], ccol,
                               s0[:, :, 1 if rev else 0], rev, dn_rows)
        outs.append(o_d)
        states.append(s_d)
    z_block = (2 * width + dn_cw.shape[1]) // (DN_HEADS * DN_DK)
    out = _evenout_call(lru, outs[0], outs[1], proj, z_block, ew["dn_g"], w_out, res, gate, tm)
    return out, h_t, jnp.stack(states, axis=2)


def kernel(x, c, ctx, c_ctx, norm_mix_g, norm_mlp_g, w_mod, b_mod, mlp_w1, mlp_w2, even_w_in, lru_conv_w,
           lru_conv_b, lru_w_a, lru_b_a, lru_w_i, lru_b_i, lru_lambda, dn_conv_w, dn_a_log, dn_dt_bias,
           dn_norm_g, even_w_out, attn_w_qkv, attn_q_norm_g, attn_k_norm_g, attn_w_o, final_norm_g):
    bsz, seq, d = x.shape
    lc = ctx.shape[1]
    depth = w_mod.shape[0]
    tm_x = _row_tile(seq, 512)
    tm_c = _row_tile(lc, 512)
    tm_mlp = _row_tile(seq, 1024)
    rows_x = _row_tile(seq, 256)
    rows_c = _row_tile(lc, 256)

    cc = jnp.zeros((SUBLANES, d), F32).at[:bsz].set(c).at[bsz].set(c_ctx)
    mod = _mod_call(cc, w_mod, b_mod).reshape(depth, SUBLANES, 6, d)

    rope = _rope_tables(seq)
    xs, cs = x, ctx
    for l in range(depth):
        last = l == depth - 1
        j = l // 2
        mx = [mod[l, :bsz, k][:, None, :] for k in range(6)]
        mc = [jnp.broadcast_to(mod[l, bsz, k][None, None, :], (bsz, 1, d)) for k in range(6)]
        sh1, sc1, g1, sh2, sc2, g2 = mx
        csh1, csc1, cg1, csh2, csc2, cg2 = mc
        w1 = mlp_w1[l].astype(BF16)
        w2 = mlp_w2[l].astype(BF16)
        if l % 2 == 0:
            ew = _even_weights(even_w_in[j], lru_w_a[j], lru_w_i[j], dn_a_log[j], dn_dt_bias[j], dn_norm_g[j])
            w_out = even_w_out[j].astype(BF16)
            h0 = jnp.zeros((bsz, 2, lru_conv_w.shape[2]), F32)
            s0 = jnp.zeros((bsz, DN_HEADS, 2, DN_DK, DN_DK), F32)
            common = (ew, lru_conv_w[j], lru_conv_b[j], lru_b_a[j], lru_b_i[j], lru_lambda[j], dn_conv_w[j], w_out)
            cs_mix, h_t, s_t = _even_stream(cs, cs, cg1, norm_mix_g[l], csc1, csh1, *common, h0, s0,
                                            tm_c, tm_c, rows_c, rows_c)
            xs, _, _ = _even_stream(xs, xs, g1, norm_mix_g[l], sc1, sh1, *common, h_t, s_t,
                                    tm_mlp, tm_x, rows_x, rows_x)
        else:
            perm = _rope_perm()
            nqk = (N_Q_HEADS + N_KV_HEADS) * HEAD_DIM
            w_qk = attn_w_qkv[j][:, :nqk].reshape(d, N_Q_HEADS + N_KV_HEADS, HEAD_DIM)[:, :, perm]
            w_qkv = jnp.concatenate([w_qk.reshape(d, nqk), attn_w_qkv[j][:, nqk:]], axis=1).astype(BF16)
            q_g = attn_q_norm_g[j][perm]
            k_g = attn_k_norm_g[j][perm]
            w_o = attn_w_o[j].astype(BF16)
            qkv_c = _normlin_call(cs, norm_mix_g[l], csc1, csh1, w_qkv, None, None, tm_c)
            qkv_x = _normlin_call(xs, norm_mix_g[l], sc1, sh1, w_qkv, None, None, tm_mlp)
            if last:
                k_c, vt_c = _aprep_call(qkv_c, q_g, k_g,None, False, rows_c)
            else:
                qt_c, k_c, vt_c = _aprep_call(qkv_c, q_g, k_g,None, True, rows_c)
            qt_x, k_x, vt_x = _aprep_call(qkv_x, q_g, k_g,rope, True, tm_x)
            k_all = jnp.concatenate([k_c, k_x], axis=2)
            vt_all = jnp.concatenate([vt_c, vt_x], axis=3)
            o_x = _flash_call(qt_x, k_all, vt_all, rows_x, _col_tile(lc + seq, FLASH_TK_CAP))
            xs = _linres_call(o_x, w_o, xs, g1, tm_x)
            if not last:
                o_c = _flash_call(qt_c, k_c, vt_c, rows_c, _col_tile(lc, min(FLASH_TK_CAP, lc // 2)))
                cs_mix = _linres_call(o_c, w_o, cs, cg1, tm_c)
        xs = _mlp_call(xs, norm_mlp_g[l], sc2, sh2, g2, w1, w2, final_norm_g, tm_mlp, last)
        if not last:
            cs = _mlp_call(cs_mix, norm_mlp_g[l], csc2, csh2, cg2, w1, w2, final_norm_g, tm_c, False)
    return xs
```

```python
import functools

import jax
import jax.numpy as jnp
from jax import lax
from jax.experimental import pallas as pl
from jax.experimental.pallas import tpu as pltpu

F32 = jnp.float32
BF16 = jnp.bfloat16

EPS = 1e-6
LANES = 128
SUBLANES = 8
VMEM_LIMIT = 56 * 1024 * 1024

LRU_BLOCK = 64
LRU_C = 8.0
CONV_WIDTH = 4
CONV_LEFT = 2
DN_HEADS = 8
DN_DK = 128
DN_CHUNK = 64
N_Q_HEADS = 8
N_KV_HEADS = 4
Q_PER_KV = N_Q_HEADS // N_KV_HEADS
HEAD_DIM = 128
GRID_W = 64
ROPE_THETA = 10000.0
NEG_BIG = -1e30
LOG2E = 1.4426950408889634
DNPREP_ROWS = 1024
LRU_ROWS = 1024
FLASH_TK_CAP = 768
FLASH_UNROLL = 4


def _cparams(sem):
    return pltpu.CompilerParams(dimension_semantics=sem, vmem_limit_bytes=VMEM_LIMIT)


def _bdot(a, b):
    return jnp.dot(a.astype(BF16), b.astype(BF16), preferred_element_type=F32)


def _bdot_nt(a, b):
    return lax.dot_general(a.astype(BF16), b.astype(BF16), (((1,), (1,)), ((), ())),
                           preferred_element_type=F32)


def _bdot_tn(a, b):
    return lax.dot_general(a.astype(BF16), b.astype(BF16), (((0,), (0,)), ((), ())),
                           preferred_element_type=F32)


def _sigmoid(x):
    return jax.nn.sigmoid(x)


def _softplus(x):
    return jnp.maximum(x, 0.0) + jnp.log1p(jnp.exp(-jnp.abs(x)))


def _rms(x):
    return x * lax.rsqrt(jnp.mean(x * x, axis=-1, keepdims=True) + EPS)


def _col_tile(n, cap):
    best = LANES
    t = LANES
    while t <= min(n, cap):
        if n % t == 0:
            best = t
        t += LANES
    return best


def _mod_kernel(c_ref, w_ref, b_ref, o_ref):
    s = c_ref[...]
    s = s * _sigmoid(s)
    o_ref[...] = _bdot(s, w_ref[...]) + b_ref[...]


def _mod_call(cc, w_mod, b_mod):
    depth, d, n = w_mod.shape
    tn = _col_tile(n, 1536)
    return pl.pallas_call(
        _mod_kernel,
        grid=(depth, n // tn),
        in_specs=[pl.BlockSpec((SUBLANES, d), lambda l, j: (0, 0)),
                  pl.BlockSpec((None, d, tn), lambda l, j: (l, 0, j)),
                  pl.BlockSpec((None, 1, tn), lambda l, j: (l, 0, j))],
        out_specs=pl.BlockSpec((None, SUBLANES, tn), lambda l, j: (l, 0, j)),
        out_shape=jax.ShapeDtypeStruct((depth, SUBLANES, n), F32),
        compiler_params=_cparams(("parallel", "parallel")),
        name="mod",
    )(cc, w_mod, b_mod.reshape(depth, 1, n))


def _normlin_kernel(*refs, has_gates):
    if has_gates:
        x_ref, g_ref, sc_ref, sh_ref, w_ref, wg_ref, wgt_ref, o_ref, og_ref, ogt_ref, h_ref = refs
    else:
        x_ref, g_ref, sc_ref, sh_ref, w_ref, o_ref, h_ref = refs

    @pl.when(pl.program_id(2) == 0)
    def _():
        y = _rms(x_ref[...]) * g_ref[...]
        h_ref[...] = (y * (1.0 + sc_ref[...]) + sh_ref[...]).astype(BF16)
        if has_gates:
            og_ref[...] = jnp.dot(h_ref[...], wg_ref[...], preferred_element_type=F32)
            ogt_ref[...] = lax.dot_general(wgt_ref[...], h_ref[...], (((1,), (1,)), ((), ())),
                                           preferred_element_type=F32)

    o_ref[...] = jnp.dot(h_ref[...], w_ref[...], preferred_element_type=F32).astype(o_ref.dtype)


def _normlin_call(x, g, sc, sh, w, w_gates, w_gates_t, tm):
    bsz, seq, d = x.shape
    n = w.shape[1]
    tn = _col_tile(n, 1024)
    has_gates = w_gates is not None
    in_specs = [pl.BlockSpec((None, tm, d), lambda b, i, j: (b, i, 0)),
                pl.BlockSpec((1, d), lambda b, i, j: (0, 0)),
                pl.BlockSpec((None, 1, d), lambda b, i, j: (b, 0, 0)),
                pl.BlockSpec((None, 1, d), lambda b, i, j: (b, 0, 0)),
                pl.BlockSpec((d, tn), lambda b, i, j: (0, j))]
    out_specs = [pl.BlockSpec((None, tm, tn), lambda b, i, j: (b, i, j))]
    out_shape = [jax.ShapeDtypeStruct((bsz, seq, n), BF16)]
    args = [x, g.reshape(1, d), sc, sh, w]
    if has_gates:
        ng = w_gates.shape[1]
        in_specs.append(pl.BlockSpec((d, ng), lambda b, i, j: (0, 0)))
        out_specs.append(pl.BlockSpec((None, tm, ng), lambda b, i, j: (b, i, 0)))
        out_shape.append(jax.ShapeDtypeStruct((bsz, seq, ng), F32))
        ngt = w_gates_t.shape[0]
        in_specs.append(pl.BlockSpec((ngt, d), lambda b, i, j: (0, 0)))
        out_specs.append(pl.BlockSpec((None, ngt, tm), lambda b, i, j: (b, 0, i)))
        out_shape.append(jax.ShapeDtypeStruct((bsz, ngt, seq), F32))
        args += [w_gates, w_gates_t]
    out = pl.pallas_call(
        functools.partial(_normlin_kernel, has_gates=has_gates),
        grid=(bsz, seq // tm, n // tn),
        in_specs=in_specs,
        out_specs=out_specs,
        out_shape=out_shape,
        scratch_shapes=[pltpu.VMEM((tm, d), BF16)],
        compiler_params=_cparams(("parallel", "parallel", "arbitrary")),
        name="normlin",
    )(*args)
    return out if has_gates else out[0]


def _mlp_kernel(x_ref, g_ref, sc_ref, sh_ref, gate_ref, w1_ref, w2_ref, fg_ref, o_ref,
                h_ref, acc_ref, *, final_norm):
    k = pl.program_id(2)

    @pl.when(k == 0)
    def _():
        y = _rms(x_ref[...]) * g_ref[...]
        h_ref[...] = (y * (1.0 + sc_ref[...]) + sh_ref[...]).astype(BF16)
        acc_ref[...] = jnp.zeros_like(acc_ref)

    a = jnp.dot(h_ref[...], w1_ref[...], preferred_element_type=F32)
    a = jnp.square(jnp.maximum(a, 0.0)).astype(BF16)
    acc_ref[...] += jnp.dot(a, w2_ref[...], preferred_element_type=F32)

    @pl.when(k == pl.num_programs(2) - 1)
    def _():
        y = x_ref[...] + gate_ref[...] * acc_ref[...]
        if final_norm:
            y = _rms(y) * fg_ref[...]
        o_ref[...] = y


def _mlp_call(x, g, sc, sh, gate, w1, w2, fg, tm, final_norm):
    bsz, seq, d = x.shape
    f = w1.shape[1]
    tf = _col_tile(f, 1024)
    vec = pl.BlockSpec((None, 1, d), lambda b, i, k: (b, 0, 0))
    return pl.pallas_call(
        functools.partial(_mlp_kernel, final_norm=final_norm),
        grid=(bsz, seq // tm, f // tf),
        in_specs=[pl.BlockSpec((None, tm, d), lambda b, i, k: (b, i, 0)),
                  pl.BlockSpec((1, d), lambda b, i, k: (0, 0)),
                  vec, vec, vec,
                  pl.BlockSpec((d, tf), lambda b, i, k: (0, k)),
                  pl.BlockSpec((tf, d), lambda b, i, k: (k, 0)),
                  pl.BlockSpec((1, d), lambda b, i, k: (0, 0))],
        out_specs=pl.BlockSpec((None, tm, d), lambda b, i, k: (b, i, 0)),
        out_shape=jax.ShapeDtypeStruct((bsz, seq, d), F32),
        scratch_shapes=[pltpu.VMEM((tm, d), BF16), pltpu.VMEM((tm, d), F32)],
        compiler_params=_cparams(("parallel", "parallel", "arbitrary")),
        name="mlp",
    )(x, g.reshape(1, d), sc, sh, gate, w1, w2, fg.reshape(1, d))


CONV_PAD = SUBLANES


def _fill_padded(xpad_ref, x_ref, seq, rows):
    zeros = jnp.zeros((CONV_PAD, xpad_ref.shape[1]), F32)
    xpad_ref[0:CONV_PAD, :] = zeros
    xpad_ref[CONV_PAD + seq:CONV_PAD + seq + CONV_PAD, :] = zeros

    def body(t, carry):
        r0 = pl.multiple_of(t * rows, rows)
        xpad_ref[pl.ds(r0 + CONV_PAD, rows), :] = x_ref[pl.ds(r0, rows), :].astype(F32)
        return carry

    lax.fori_loop(0, seq // rows, body, 0)


def _conv_tile(xpad_ref, r0, rows, w):
    acc = None
    for j in range(CONV_WIDTH):
        off = j - CONV_LEFT
        term = xpad_ref[pl.ds(r0 + (CONV_PAD + off), rows), :] * w[j:j + 1]
        acc = term if acc is None else acc + term
    return acc


def _dnprep_kernel(x_ref, w_ref, o_ref, xpad_ref, *, seq, rows):
    j = pl.program_id(1)
    is_qk = j < 2 * DN_HEADS
    scale = jnp.where(j < DN_HEADS, DN_DK ** -0.5, 1.0).astype(F32)
    _fill_padded(xpad_ref, x_ref, seq, rows)
    w = w_ref[...]

    def body(t, carry):
        r0 = pl.multiple_of(t * rows, rows)
        y = _conv_tile(xpad_ref, r0, rows, w)
        y = y * _sigmoid(y)
        nrm = (y * lax.rsqrt(jnp.sum(y * y, axis=-1, keepdims=True) + EPS)) * scale
        o_ref[pl.ds(r0, rows), :] = jnp.where(is_qk, nrm, y).astype(o_ref.dtype)
        return carry

    lax.fori_loop(0, seq // rows, body, 0)


def _dnprep_call(proj, conv_w, col0_blocks, rows):
    bsz, seq, _ = proj.shape
    ncol = conv_w.shape[1] // LANES
    return pl.pallas_call(
        functools.partial(_dnprep_kernel, seq=seq, rows=rows),
        grid=(bsz, ncol),
        in_specs=[pl.BlockSpec((None, seq, LANES), lambda b, j: (b, 0, col0_blocks + j)),
                  pl.BlockSpec((CONV_WIDTH, LANES), lambda b, j: (0, j))],
        out_specs=pl.BlockSpec((None, seq, LANES), lambda b, j: (b, 0, j)),
        out_shape=jax.ShapeDtypeStruct((bsz, seq, ncol * LANES), BF16),
        scratch_shapes=[pltpu.VMEM((seq + 2 * CONV_PAD, LANES), F32)],
        compiler_params=_cparams(("parallel", "parallel")),
        name="dnprep",
    )(proj, conv_w)


def _scan_tile(a, b, carry, rev, rows):
    nv = rows // SUBLANES
    lanes = a.shape[1]
    a = a.reshape(nv, SUBLANES, lanes)
    b = b.reshape(nv, SUBLANES, lanes)
    row = lax.broadcasted_iota(jnp.int32, a.shape, 1)
    s = 1
    while s < SUBLANES:
        shift = SUBLANES - s if rev else s
        a_s = pltpu.roll(a, shift, 1)
        b_s = pltpu.roll(b, shift, 1)
        valid = (row < SUBLANES - s) if rev else (row >= s)
        b = b + a * jnp.where(valid, b_s, 0.0)
        a = a * jnp.where(valid, a_s, 1.0)
        s *= 2
    hs = [None] * nv
    for v in (range(nv - 1, -1, -1) if rev else range(nv)):
        h = a[v] * carry + b[v]
        carry = h[0:1] if rev else h[SUBLANES - 1:SUBLANES]
        hs[v] = h
    return jnp.concatenate(hs, axis=0), carry


def _gelu_tanh(x):
    return 0.5 * x * (1.0 + jnp.tanh(0.7978845608028654 * (x + 0.044715 * (x * x * x))))


def _lru_kernel(x_ref, y_ref, cw_ref, cb_ref, wa_ref, wi_ref, ba_ref, bi_ref, lam_ref, h0_ref,
                o_ref, ht_ref, xpad_ref, xc_ref, hf_ref, *, seq, rows):
    _fill_padded(xpad_ref, x_ref, seq, rows)
    cw = cw_ref[...]
    cb = cb_ref[...]
    nt = seq // rows

    def gates(xc, d):
        r = 1.0 / (1.0 + jnp.exp2(_bdot(xc, wa_ref[d]) - LOG2E * ba_ref[d:d + 1, :]))
        i = 1.0 / (1.0 + jnp.exp2(_bdot(xc, wi_ref[d]) - LOG2E * bi_ref[d:d + 1, :]))
        neg_rate = LRU_C * _softplus(-lam_ref[d:d + 1, :])
        a = jnp.exp2(r * (-LOG2E * neg_rate))
        b = jnp.sqrt(jnp.tanh(r * neg_rate) * (a * a + 1.0)) * (i * xc)
        return a, b

    def fwd_body(t, carry):
        r0 = pl.multiple_of(t * rows, rows)
        xc = _conv_tile(xpad_ref, r0, rows, cw) + cb
        xc_ref[pl.ds(r0, rows), :] = xc
        a, b = gates(xc, 0)
        h, carry = _scan_tile(a, b, carry, False, rows)
        hf_ref[pl.ds(r0, rows), :] = h
        return carry

    carry_f = lax.fori_loop(0, nt, fwd_body, h0_ref[0:1, :])

    def bwd_body(t, carry):
        r0 = pl.multiple_of((nt - 1 - t) * rows, rows)
        xc = xc_ref[pl.ds(r0, rows), :]
        a, b = gates(xc, 1)
        h, carry = _scan_tile(a, b, carry, True, rows)
        y = y_ref[pl.ds(r0, rows), :].astype(F32)
        o_ref[pl.ds(r0, rows), :] = ((hf_ref[pl.ds(r0, rows), :] + h) * _gelu_tanh(y)).astype(o_ref.dtype)
        return carry

    carry_b = lax.fori_loop(0, nt, bwd_body, h0_ref[1:2, :])
    ht_ref[0:1, :] = carry_f
    ht_ref[1:2, :] = carry_b


def _lru_call(proj, cw, cb, wa, wi, ba, bi, lam, h0, rows):
    bsz, seq, _ = proj.shape
    width = cw.shape[1]
    ncol = width // LANES
    vec2 = pl.BlockSpec((2, LANES), lambda b, j: (0, j))
    wspec = pl.BlockSpec((2, None, LANES, LANES), lambda b, j: (0, j, 0, 0))
    return pl.pallas_call(
        functools.partial(_lru_kernel, seq=seq, rows=rows),
        grid=(bsz, ncol),
        in_specs=[pl.BlockSpec((None, seq, LANES), lambda b, j: (b, 0, j)),
                  pl.BlockSpec((None, seq, LANES), lambda b, j: (b, 0, ncol + j)),
                  pl.BlockSpec((CONV_WIDTH, LANES), lambda b, j: (0, j)),
                  pl.BlockSpec((1, LANES), lambda b, j: (0, j)),
                  wspec, wspec, vec2, vec2, vec2,
                  pl.BlockSpec((None, 2, LANES), lambda b, j: (b, 0, j))],
        out_specs=[pl.BlockSpec((None, seq, LANES), lambda b, j: (b, 0, j)),
                   pl.BlockSpec((None, 2, LANES), lambda b, j: (b, 0, j))],
        out_shape=[jax.ShapeDtypeStruct((bsz, seq, width), BF16),
                   jax.ShapeDtypeStruct((bsz, 2, width), F32)],
        scratch_shapes=[pltpu.VMEM((seq + 2 * CONV_PAD, LANES), F32),
                        pltpu.VMEM((seq, LANES), F32),
                        pltpu.VMEM((seq, LANES), F32)],
        compiler_params=_cparams(("parallel", "parallel")),
        name="lru",
    )(proj, proj, cw, cb.reshape(1, width), wa, wi, ba, bi, lam, h0)


DN_GROUP = 2 * DN_CHUNK


def _split3(x):
    x1 = x.astype(BF16)
    r1 = x - x1.astype(F32)
    x2 = r1.astype(BF16)
    x3 = (r1 - x2.astype(F32)).astype(BF16)
    return x1, x2, x3


def _delta_kernel(q_ref, k_ref, v_ref, gr_ref, grt_ref, cvec_ref, ccol_ref, cum_ref, cumt_ref,
                  ones_ref, s0_ref, o_ref, st_ref, s_ref, *, rev, rows):
    step = pl.program_id(1)
    d = 1 if rev else 0

    @pl.when(step == 0)
    def _():
        s_ref[...] = s0_ref[...]

    raw = gr_ref[...]
    sig = _sigmoid(raw)
    g = -jnp.exp(cvec_ref[0:1, :]) * _softplus(raw + cvec_ref[1:2, :])
    g1, g2, g3 = _split3(g)
    cum = cum_ref[...]
    ones = ones_ref[...]

    def mm(m, parts):
        return sum(jnp.dot(m, p, preferred_element_type=F32) for p in parts)

    gc = mm(cum, (g1, g2, g3))
    gl = mm(ones, (g1, g2, g3))
    e_gc = jnp.exp(gc)
    e_rest = jnp.exp(gl - gc)
    e_gl = jnp.exp(gl)
    gt = -jnp.exp(ccol_ref[0]) * _softplus(grt_ref[...] + ccol_ref[1])
    t1, t2, t3 = _split3(gt)
    cumt = cumt_ref[...]
    gct = sum(jnp.dot(p, cumt, preferred_element_type=F32) for p in (t1, t2, t3))

    ri = lax.broadcasted_iota(jnp.int32, (DN_GROUP, DN_GROUP), 0)
    ci = lax.broadcasted_iota(jnp.int32, (DN_GROUP, DN_GROUP), 1)
    same = (ri // DN_CHUNK) == (ci // DN_CHUNK)
    if rev:
        incl = same & (ri <= ci)
        strict = same & (ri < ci)
    else:
        incl = same & (ri >= ci)
        strict = same & (ri > ci)

    ngroups = rows // DN_GROUP
    zeros_c = jnp.zeros((DN_CHUNK, DN_DK), F32)

    def bcast(x, rsl, lane):
        return jnp.broadcast_to(x[rsl, lane:lane + 1], (DN_GROUP, LANES))

    heads = range(DN_HEADS)
    groups = list(range(ngroups - 1, -1, -1) if rev else range(ngroups))
    units = [(gi, h) for gi in groups for h in heads]
    qd, kd, egl_b, rhs, kk, qk = {}, {}, {}, {}, {}, {}
    for u in units:
        gi, h = u
        rsl = slice(gi * DN_GROUP, (gi + 1) * DN_GROUP)
        lb = d * DN_HEADS + h
        lg = 2 * DN_HEADS + d * DN_HEADS + h
        csl = slice(h * DN_DK, (h + 1) * DN_DK)
        q16 = q_ref[rsl, csl]
        k16 = k_ref[rsl, csl]
        q_h = q16.astype(F32)
        k_h = k16.astype(F32)
        beta_b = bcast(sig, rsl, lb)
        egc_b = bcast(e_gc, rsl, lg)
        kb_h = k_h * beta_b
        kk[u] = _bdot_nt(kb_h, k16)
        qk[u] = _bdot_nt(q16, k16)
        rhs[u] = jnp.concatenate([v_ref[rsl, csl].astype(F32) * beta_b, kb_h * egc_b], axis=1)
        qd[u] = q_h * egc_b
        kd[u] = k_h * bcast(e_rest, rsl, lg)
        egl_b[u] = bcast(e_gl, rsl, lg)
    p, a16, attn = {}, {}, {}
    for u in units:
        gi, h = u
        rsl = slice(gi * DN_GROUP, (gi + 1) * DN_GROUP)
        lg = 2 * DN_HEADS + d * DN_HEADS + h
        diff = bcast(gc, rsl, lg) - gct[lg:lg + 1, rsl]
        decay = jnp.exp(jnp.where(incl, diff, NEG_BIG))
        m = jnp.where(strict, kk[u] * decay, 0.0)
        attn[u] = qk[u] * decay
        p[u] = -m
        a16[u] = m.astype(BF16)
    npow = 1
    while npow < DN_CHUNK // 2:
        a = {u: jnp.dot(a16[u], a16[u], preferred_element_type=F32) for u in units}
        a16 = {u: a[u].astype(BF16) for u in units}
        p = {u: p[u] + a[u] + jnp.dot(p[u].astype(BF16), a16[u], preferred_element_type=F32) for u in units}
        npow *= 2
    uw = {u: rhs[u] + _bdot(p[u], rhs[u]) for u in units}
    s = [s_ref[h] for h in heads]
    for gi in groups:
        for cc in ((1, 0) if rev else (0, 1)):
            c0 = cc * DN_CHUNK
            cs = slice(c0, c0 + DN_CHUNK)
            r2 = [_bdot(jnp.concatenate([uw[gi, h][cs, DN_DK:], qd[gi, h][cs]], axis=0), s[h]) for h in heads]
            v_new = [uw[gi, h][cs, :DN_DK] - r2[h][:DN_CHUNK] for h in heads]
            for h in heads:
                v_pad = (jnp.concatenate([v_new[h], zeros_c], axis=0) if cc == 0
                         else jnp.concatenate([zeros_c, v_new[h]], axis=0))
                o_c = r2[h][DN_CHUNK:] + _bdot(attn[gi, h][cs], v_pad)
                o_ref[gi * DN_GROUP + c0:gi * DN_GROUP + c0 + DN_CHUNK,
                      h * DN_DK:(h + 1) * DN_DK] = o_c.astype(o_ref.dtype)
            for h in heads:
                egl = jnp.concatenate([egl_b[gi, h][cs], egl_b[gi, h][cs]], axis=0)
                s[h] = s[h] * egl + _bdot_tn(kd[gi, h][cs], v_new[h])
    for h in heads:
        s_ref[h] = s[h]

    @pl.when(step == pl.num_programs(1) - 1)
    def _():
        st_ref[...] = s_ref[...]


def _delta_consts(rows, rev):
    t = jnp.arange(rows)
    same = (t[:, None] // DN_CHUNK) == (t[None, :] // DN_CHUNK)
    cum = same & ((t[:, None] <= t[None, :]) if rev else (t[:, None] >= t[None, :]))
    return cum.astype(BF16), cum.T.astype(BF16), same.astype(BF16)


def _delta_call(qkv, gates, gates_t, cvec, ccol, s0, rev, rows):
    bsz, seq, _ = qkv.shape
    width = DN_HEADS * DN_DK
    nsteps = seq // rows
    cum, cumt, ones = _delta_consts(rows, rev)

    def blk(i):
        return nsteps - 1 - i if rev else i

    full = lambda shape: pl.BlockSpec(shape, lambda b, i: (0,) * len(shape))
    state = pl.BlockSpec((None, DN_HEADS, DN_DK, DN_DK), lambda b, i: (b, 0, 0, 0))
    return pl.pallas_call(
        functools.partial(_delta_kernel, rev=rev, rows=rows),
        grid=(bsz, nsteps),
        in_specs=[pl.BlockSpec((None, rows, width), lambda b, i: (b, blk(i), 0)),
                  pl.BlockSpec((None, rows, width), lambda b, i: (b, blk(i), 1)),
                  pl.BlockSpec((None, rows, width), lambda b, i: (b, blk(i), 2)),
                  pl.BlockSpec((None, rows, LANES), lambda b, i: (b, blk(i), 0)),
                  pl.BlockSpec((None, 4 * DN_HEADS, rows), lambda b, i: (b, 0, blk(i))),
                  full((SUBLANES, LANES)),
                  full((2, 4 * DN_HEADS, rows)),
                  full((rows, rows)), full((rows, rows)), full((rows, rows)),
                  state],
        out_specs=[pl.BlockSpec((None, rows, width), lambda b, i: (b, blk(i), 0)), state],
        out_shape=[jax.ShapeDtypeStruct((bsz, seq, width), BF16),
                   jax.ShapeDtypeStruct((bsz, DN_HEADS, DN_DK, DN_DK), F32)],
        scratch_shapes=[pltpu.VMEM((DN_HEADS, DN_DK, DN_DK), F32)],
        compiler_params=_cparams(("parallel", "arbitrary")),
        name="delta_bwd" if rev else "delta_fwd",
    )(qkv, qkv, qkv, gates, gates_t, cvec, ccol, cum, cumt, ones, s0)


def _evenout_kernel(lru_ref, of_ref, ob_ref, z_ref, g_ref, w_ref, res_ref, gate_ref, o_ref, a_ref):
    width = lru_ref.shape[1]
    a_ref[:, 0:width] = lru_ref[...].astype(BF16)
    for h in range(DN_HEADS):
        csl = slice(h * DN_DK, (h + 1) * DN_DK)
        o = of_ref[:, csl].astype(F32) + ob_ref[:, csl].astype(F32)
        z = z_ref[:, csl].astype(F32)
        y = (_rms(o) * g_ref[:, csl]) * (z * _sigmoid(z))
        a_ref[:, width + h * DN_DK:width + (h + 1) * DN_DK] = y.astype(BF16)
    o_ref[...] = res_ref[...] + gate_ref[...] * jnp.dot(a_ref[...], w_ref[...], preferred_element_type=F32)


def _evenout_call(lru, o_f, o_b, proj, z_block, dn_g, w_out, res, gate, tm):
    bsz, seq, width = lru.shape
    dnw = o_f.shape[2]
    d = res.shape[2]
    row = lambda c: pl.BlockSpec((None, tm, c), lambda b, i: (b, i, 0))
    return pl.pallas_call(
        _evenout_kernel,
        grid=(bsz, seq // tm),
        in_specs=[row(width), row(dnw), row(dnw),
                  pl.BlockSpec((None, tm, dnw), lambda b, i: (b, i, z_block)),
                  pl.BlockSpec((1, dnw), lambda b, i: (0, 0)),
                  pl.BlockSpec((width + dnw, d), lambda b, i: (0, 0)),
                  row(d),
                  pl.BlockSpec((None, 1, d), lambda b, i: (b, 0, 0))],
        out_specs=row(d),
        out_shape=jax.ShapeDtypeStruct((bsz, seq, d), F32),
        scratch_shapes=[pltpu.VMEM((tm, width + dnw), BF16)],
        compiler_params=_cparams(("parallel", "parallel")),
        name="evenout",
    )(lru, o_f, o_b, proj, dn_g, w_out, res, gate)


def _linres_kernel(a_ref, w_ref, res_ref, gate_ref, o_ref):
    o_ref[...] = res_ref[...] + gate_ref[...] * jnp.dot(a_ref[...].astype(BF16), w_ref[...],
                                                        preferred_element_type=F32)


def _linres_call(a, w, res, gate, tm):
    bsz, seq, kdim = a.shape
    d = res.shape[2]
    return pl.pallas_call(
        _linres_kernel,
        grid=(bsz, seq // tm),
        in_specs=[pl.BlockSpec((None, tm, kdim), lambda b, i: (b, i, 0)),
                  pl.BlockSpec((kdim, d), lambda b, i: (0, 0)),
                  pl.BlockSpec((None, tm, d), lambda b, i: (b, i, 0)),
                  pl.BlockSpec((None, 1, d), lambda b, i: (b, 0, 0))],
        out_specs=pl.BlockSpec((None, tm, d), lambda b, i: (b, i, 0)),
        out_shape=jax.ShapeDtypeStruct((bsz, seq, d), F32),
        compiler_params=_cparams(("parallel", "parallel")),
        name="linres",
    )(a, w, res, gate)


def _aprep_kernel(*refs, rope, want_q):
    if rope:
        x_ref, qg_ref, kg_ref, cos_ref, sin_ref = refs[:5]
        outs = refs[5:]
    else:
        x_ref, qg_ref, kg_ref = refs[:3]
        outs = refs[3:]
    if want_q:
        q_ref, k_ref, v_ref = outs
    else:
        k_ref, v_ref = outs

    if rope:
        cos = cos_ref[...]
        sin = sin_ref[...]

    def norm_rope(x, g):
        y = _rms(x) * g
        if rope:
            y = y * cos + pltpu.roll(y, HEAD_DIM // 2, 1) * sin
        return y

    if want_q:
        for h in range(N_Q_HEADS):
            x = x_ref[:, h * HEAD_DIM:(h + 1) * HEAD_DIM].astype(F32)
            q_ref[h] = (norm_rope(x, qg_ref[...]) * (HEAD_DIM ** -0.5 * LOG2E)).astype(BF16)
    for h in range(N_KV_HEADS):
        c0 = (N_Q_HEADS + h) * HEAD_DIM
        k_ref[h] = norm_rope(x_ref[:, c0:c0 + HEAD_DIM].astype(F32), kg_ref[...]).astype(BF16)
        c1 = (N_Q_HEADS + N_KV_HEADS + h) * HEAD_DIM
        v_ref[h] = x_ref[:, c1:c1 + HEAD_DIM].astype(F32).T.astype(BF16)


def _aprep_call(qkv, qg, kg, rope_tabs, want_q, tm):
    bsz, seq, cols = qkv.shape
    rope = rope_tabs is not None
    in_specs = [pl.BlockSpec((None, tm, cols), lambda b, i: (b, i, 0)),
                pl.BlockSpec((1, HEAD_DIM), lambda b, i: (0, 0)),
                pl.BlockSpec((1, HEAD_DIM), lambda b, i: (0, 0))]
    args = [qkv, qg.reshape(1, HEAD_DIM), kg.reshape(1, HEAD_DIM)]
    if rope:
        in_specs += [pl.BlockSpec((tm, HEAD_DIM), lambda b, i: (i, 0))] * 2
        args += list(rope_tabs)
    hspec = lambda n: pl.BlockSpec((None, n, tm, HEAD_DIM), lambda b, i: (b, 0, i, 0))
    hshape = lambda n: jax.ShapeDtypeStruct((bsz, n, seq, HEAD_DIM), BF16)
    tspec = lambda n: pl.BlockSpec((None, n, HEAD_DIM, tm), lambda b, i: (b, 0, 0, i))
    tshape = lambda n: jax.ShapeDtypeStruct((bsz, n, HEAD_DIM, seq), BF16)
    out_specs = [hspec(N_KV_HEADS), tspec(N_KV_HEADS)]
    out_shape = [hshape(N_KV_HEADS), tshape(N_KV_HEADS)]
    if want_q:
        out_specs = [hspec(N_Q_HEADS)] + out_specs
        out_shape = [hshape(N_Q_HEADS)] + out_shape
    return pl.pallas_call(
        functools.partial(_aprep_kernel, rope=rope, want_q=want_q),
        grid=(bsz, seq // tm),
        in_specs=in_specs,
        out_specs=out_specs,
        out_shape=out_shape,
        compiler_params=_cparams(("parallel", "parallel")),
        name="aprep_x" if rope else "aprep_ctx",
    )(*args)


def _rope_tables(n_tokens):
    rows = n_tokens // GRID_W
    axis = HEAD_DIM // 2
    row_id = jnp.repeat(jnp.arange(rows), GRID_W).astype(F32)
    col_id = jnp.tile(jnp.arange(GRID_W), rows).astype(F32)
    inv_freq = ROPE_THETA ** (-jnp.arange(0, axis, 2, dtype=F32) / axis)
    ang_r = row_id[:, None] * inv_freq[None]
    ang_c = col_id[:, None] * inv_freq[None]
    cr, sr, cc, sc = jnp.cos(ang_r), jnp.sin(ang_r), jnp.cos(ang_c), jnp.sin(ang_c)
    cos = jnp.concatenate([cr, cc, cr, cc], axis=-1)
    sin = jnp.concatenate([-sr, -sc, sr, sc], axis=-1)
    return cos, sin


def _rope_perm():
    quarter = HEAD_DIM // 4
    return jnp.concatenate([jnp.arange(0, quarter), jnp.arange(2 * quarter, 3 * quarter),
                            jnp.arange(quarter, 2 * quarter), jnp.arange(3 * quarter, 4 * quarter)])


def _flash_kernel(q_ref, k_ref, vt_ref, o_ref, s0_ref, sa_ref, sb_ref, m0_ref, ma_ref, mb_ref, *,
                  tq, tk, nk, nt):
    m_rows = Q_PER_KV * tq

    def q_tile(i):
        r0 = pl.multiple_of(i * tq, tq)
        return q_ref[:, pl.ds(r0, tq), :].reshape(m_rows, HEAD_DIM)

    def scores(q, j, s_ref, cmax_ref):
        c0 = pl.multiple_of(j * tk, tk)
        st = lax.dot_general(k_ref[pl.ds(c0, tk), :], q, (((1,), (1,)), ((), ())),
                             preferred_element_type=F32)
        s_ref[...] = st
        cmax_ref[...] = jnp.max(st, axis=0, keepdims=True)

    def update(j, s_ref, cmax_ref, carry):
        m, l, acc = carry
        m_new = jnp.maximum(m, cmax_ref[...])
        alpha = jnp.exp2(m - m_new)
        p = jnp.exp2(s_ref[...] - m_new)
        l = alpha * l + jnp.sum(p, axis=0, keepdims=True)
        vt = vt_ref[:, pl.ds(pl.multiple_of(j * tk, tk), tk)]
        acc = alpha * acc + jnp.dot(vt, p.astype(BF16), preferred_element_type=F32)
        return m_new, l, acc

    bufs = ((sa_ref, ma_ref), (sb_ref, mb_ref))

    def tile_body(i, _):
        q = q_tile(i)
        q_next = q_tile(jnp.minimum(i + 1, nt - 1))

        def run(j0, count, carry, last):
            for u in range(count):
                j = j0 + u
                if u + 1 < count or not last:
                    scores(q, j + 1, *bufs[(u + 1) % 2])
                else:
                    scores(q_next, 0, s0_ref, m0_ref)
                carry = update(j, *bufs[u % 2], carry)
            return carry

        carry = (jnp.full((1, m_rows), NEG_BIG, F32), jnp.zeros((1, m_rows), F32),
                 jnp.zeros((HEAD_DIM, m_rows), F32))
        scores(q, 1, sa_ref, ma_ref)
        carry = update(0, s0_ref, m0_ref, carry)
        rest = nk - 1
        niter = (rest - 1) // FLASH_UNROLL
        carry = lax.fori_loop(0, niter, lambda jj, c: run(1 + jj * FLASH_UNROLL, FLASH_UNROLL, c, False),
                              carry)
        _, l, acc = run(1 + niter * FLASH_UNROLL, rest - niter * FLASH_UNROLL, carry, True)
        o = (acc / l).T
        r0 = pl.multiple_of(i * tq, tq)
        for r in range(Q_PER_KV):
            o_ref[pl.ds(r0, tq), r * HEAD_DIM:(r + 1) * HEAD_DIM] = o[r * tq:(r + 1) * tq].astype(o_ref.dtype)
        return 0

    scores(q_tile(0), 0, s0_ref, m0_ref)
    lax.fori_loop(0, nt, tile_body, 0)


def _flash_call(q, k, vt, tq, tk):
    bsz, _, seq, _ = q.shape
    lk = k.shape[2]
    nk = lk // tk
    assert nk >= 2, "the score pipeline needs at least two key blocks"
    q5 = q.reshape(bsz, N_KV_HEADS, Q_PER_KV, seq, HEAD_DIM)
    return pl.pallas_call(
        functools.partial(_flash_kernel, tq=tq, tk=tk, nk=nk, nt=seq // tq),
        grid=(bsz, N_KV_HEADS),
        in_specs=[pl.BlockSpec((None, None, Q_PER_KV, seq, HEAD_DIM), lambda b, g: (b, g, 0, 0, 0)),
                  pl.BlockSpec((None, None, lk, HEAD_DIM), lambda b, g: (b, g, 0, 0)),
                  pl.BlockSpec((None, None, HEAD_DIM, lk), lambda b, g: (b, g, 0, 0))],
        out_specs=pl.BlockSpec((None, seq, Q_PER_KV * HEAD_DIM), lambda b, g: (b, 0, g)),
        out_shape=jax.ShapeDtypeStruct((bsz, seq, N_Q_HEADS * HEAD_DIM), BF16),
        scratch_shapes=([pltpu.VMEM((tk, Q_PER_KV * tq), F32)] * 3
                        + [pltpu.VMEM((1, Q_PER_KV * tq), F32)] * 3),
        compiler_params=_cparams(("parallel", "parallel")),
        name="flash",
    )(q5, k, vt)


def _row_tile(seq, cap):
    t = min(seq, cap)
    while seq % t:
        t //= 2
    return t


def _even_weights(w_in, lru_wa, lru_wi, dn_alog, dn_dtb, dn_g):
    d, cols = w_in.shape
    ngate = 4 * DN_HEADS
    main = cols - ngate
    w_gates = jnp.concatenate([w_in[:, main:], jnp.zeros((d, LANES - ngate), F32)], axis=1)

    def blockdiag(w):
        w = w * (-LOG2E)
        z = jnp.zeros_like(w[:, 0::2])
        top = jnp.concatenate([w[:, 0::2], z], axis=-1)
        bot = jnp.concatenate([z, w[:, 1::2]], axis=-1)
        return jnp.concatenate([top, bot], axis=-2).astype(BF16)

    alog = dn_alog.reshape(-1)
    dtb = dn_dtb.reshape(-1)
    cvec = jnp.zeros((SUBLANES, LANES), F32)
    cvec = cvec.at[0, 2 * DN_HEADS:ngate].set(alog).at[1, 2 * DN_HEADS:ngate].set(dtb)

    ccol = jnp.zeros((2, ngate), F32).at[0, 2 * DN_HEADS:].set(alog).at[1, 2 * DN_HEADS:].set(dtb)

    return dict(w_in=w_in[:, :main].astype(BF16), w_gates=w_gates.astype(BF16),
                w_gates_t=w_in[:, main:].T.astype(BF16),
                wa=blockdiag(lru_wa), wi=blockdiag(lru_wi), cvec=cvec, ccol=ccol,
                dn_g=jnp.tile(dn_g, DN_HEADS).reshape(1, -1))


def _even_stream(h_in, res, gate, norm_g, sc, sh, ew, lru_cw, lru_cb, lru_ba, lru_bi, lru_lam, dn_cw,
                 w_out, h0, s0, tm_in, tm, rows, dn_rows):
    width = lru_cw.shape[1]
    ngate = 4 * DN_HEADS
    proj, gates, gates_t = _normlin_call(h_in, norm_g, sc, sh, ew["w_in"], ew["w_gates"], ew["w_gates_t"],
                                         tm_in)
    lru, h_t = _lru_call(proj, lru_cw, lru_cb, ew["wa"], ew["wi"], lru_ba, lru_bi, lru_lam, h0,
                         _row_tile(proj.shape[1], LRU_ROWS))
    qkv = _dnprep_call(proj, dn_cw, 2 * width // LANES, _row_tile(proj.shape[1], DNPREP_ROWS))
    ccol = jnp.broadcast_to(ew["ccol"][:, :, None], (2, ngate, dn_rows))
    outs, states = [], []
    for rev in (False, True):
        o_d, s_d = _delta_call(qkv, gates, gates_t, ew["cvec"], ccol,
                               s0[:, :, 1 if rev else 0], rev, dn_rows)
        outs.append(o_d)
        states.append(s_d)
    z_block = (2 * width + dn_cw.shape[1]) // (DN_HEADS * DN_DK)
    out = _evenout_call(lru, outs[0], outs[1], proj, z_block, ew["dn_g"], w_out, res, gate, tm)
    return out, h_t, jnp.stack(states, axis=2)


def kernel(x, c, ctx, c_ctx, norm_mix_g, norm_mlp_g, w_mod, b_mod, mlp_w1, mlp_w2, even_w_in, lru_conv_w,
           lru_conv_b, lru_w_a, lru_b_a, lru_w_i, lru_b_i, lru_lambda, dn_conv_w, dn_a_log, dn_dt_bias,
           dn_norm_g, even_w_out, attn_w_qkv, attn_q_norm_g, attn_k_norm_g, attn_w_o, final_norm_g):
    bsz, seq, d = x.shape
    lc = ctx.shape[1]
    depth = w_mod.shape[0]
    tm_x = _row_tile(seq, 512)
    tm_c = _row_tile(lc, 512)
    tm_mlp = _row_tile(seq, 1024)
    rows_x = _row_tile(seq, 256)
    rows_c = _row_tile(lc, 256)

    cc = jnp.zeros((SUBLANES, d), F32).at[:bsz].set(c).at[bsz].set(c_ctx)
    mod = _mod_call(cc, w_mod, b_mod).reshape(depth, SUBLANES, 6, d)

    rope = _rope_tables(seq)
    xs, cs = x, ctx
    for l in range(depth):
        last = l == depth - 1
        j = l // 2
        mx = [mod[l, :bsz, k][:, None, :] for k in range(6)]
        mc = [jnp.broadcast_to(mod[l, bsz, k][None, None, :], (bsz, 1, d)) for k in range(6)]
        sh1, sc1, g1, sh2, sc2, g2 = mx
        csh1, csc1, cg1, csh2, csc2, cg2 = mc
        w1 = mlp_w1[l].astype(BF16)
        w2 = mlp_w2[l].astype(BF16)
        if l % 2 == 0:
            ew = _even_weights(even_w_in[j], lru_w_a[j], lru_w_i[j], dn_a_log[j], dn_dt_bias[j], dn_norm_g[j])
            w_out = even_w_out[j].astype(BF16)
            h0 = jnp.zeros((bsz, 2, lru_conv_w.shape[2]), F32)
            s0 = jnp.zeros((bsz, DN_HEADS, 2, DN_DK, DN_DK), F32)
            common = (ew, lru_conv_w[j], lru_conv_b[j], lru_b_a[j], lru_b_i[j], lru_lambda[j], dn_conv_w[j], w_out)
            cs_mix, h_t, s_t = _even_stream(cs, cs, cg1, norm_mix_g[l], csc1, csh1, *common, h0, s0,
                                            tm_c, tm_c, rows_c, rows_c)
            xs, _, _ = _even_stream(xs, xs, g1, norm_mix_g[l], sc1, sh1, *common, h_t, s_t,
                                    tm_mlp, tm_x, rows_x, rows_x)
        else:
            perm = _rope_perm()
            nqk = (N_Q_HEADS + N_KV_HEADS) * HEAD_DIM
            w_qk = attn_w_qkv[j][:, :nqk].reshape(d, N_Q_HEADS + N_KV_HEADS, HEAD_DIM)[:, :, perm]
            w_qkv = jnp.concatenate([w_qk.reshape(d, nqk), attn_w_qkv[j][:, nqk:]], axis=1).astype(BF16)
            q_g = attn_q_norm_g[j][perm]
            k_g = attn_k_norm_g[j][perm]
            w_o = attn_w_o[j].astype(BF16)
            qkv_c = _normlin_call(cs, norm_mix_g[l], csc1, csh1, w_qkv, None, None, tm_c)
            qkv_x = _normlin_call(xs, norm_mix_g[l], sc1, sh1, w_qkv, None, None, tm_mlp)
            if last:
                k_c, vt_c = _aprep_call(qkv_c, q_g, k_g,None, False, rows_c)
            else:
                qt_c, k_c, vt_c = _aprep_call(qkv_c, q_g, k_g,None, True, rows_c)
            qt_x, k_x, vt_x = _aprep_call(qkv_x, q_g, k_g,rope, True, tm_x)
            k_all = jnp.concatenate([k_c, k_x], axis=2)
            vt_all = jnp.concatenate([vt_c, vt_x], axis=3)
            o_x = _flash_call(qt_x, k_all, vt_all, rows_x, _col_tile(lc + seq, FLASH_TK_CAP))
            xs = _linres_call(o_x, w_o, xs, g1, tm_x)
            if not last:
                o_c = _flash_call(qt_c, k_c, vt_c, rows_c, _col_tile(lc, min(FLASH_TK_CAP, lc // 2)))
                cs_mix = _linres_call(o_c, w_o, cs, cg1, tm_c)
        xs = _mlp_call(xs, norm_mlp_g[l], sc2, sh2, g2, w1, w2, final_norm_g, tm_mlp, last)
        if not last:
            cs = _mlp_call(cs_mix, norm_mlp_g[l], csc2, csh2, cg2, w1, w2, final_norm_g, tm_c, False)
    return xs
```

```python
import functools

import jax
import jax.numpy as jnp
from jax import lax
from jax.experimental import pallas as pl
from jax.experimental.pallas import tpu as pltpu

F32 = jnp.float32
BF16 = jnp.bfloat16

EPS = 1e-6
LANES = 128
SUBLANES = 8
VMEM_LIMIT = 56 * 1024 * 1024

LRU_BLOCK = 64
LRU_C = 8.0
CONV_WIDTH = 4
CONV_LEFT = 2
DN_HEADS = 8
DN_DK = 128
DN_CHUNK = 64
N_Q_HEADS = 8
N_KV_HEADS = 4
Q_PER_KV = N_Q_HEADS // N_KV_HEADS
HEAD_DIM = 128
GRID_W = 64
ROPE_THETA = 10000.0
NEG_BIG = -1e30
LOG2E = 1.4426950408889634
DNPREP_ROWS = 1024
LRU_ROWS = 1024
DELTA_ROWS = 256
FLASH_TK_CAP = 768
VT_PAD = 16
FLASH_UNROLL = 4


def _cparams(sem):
    return pltpu.CompilerParams(dimension_semantics=sem, vmem_limit_bytes=VMEM_LIMIT)


def _bdot(a, b):
    return jnp.dot(a.astype(BF16), b.astype(BF16), preferred_element_type=F32)


def _bdot_nt(a, b):
    return lax.dot_general(a.astype(BF16), b.astype(BF16), (((1,), (1,)), ((), ())),
                           preferred_element_type=F32)


def _bdot_tn(a, b):
    return lax.dot_general(a.astype(BF16), b.astype(BF16), (((0,), (0,)), ((), ())),
                           preferred_element_type=F32)


def _sigmoid(x):
    return jax.nn.sigmoid(x)


def _softplus(x):
    return jnp.maximum(x, 0.0) + jnp.log1p(jnp.exp(-jnp.abs(x)))


def _rms(x):
    return x * lax.rsqrt(jnp.mean(x * x, axis=-1, keepdims=True) + EPS)


def _col_tile(n, cap):
    best = LANES
    t = LANES
    while t <= min(n, cap):
        if n % t == 0:
            best = t
        t += LANES
    return best


def _mod_kernel(c_ref, w_ref, b_ref, o_ref):
    s = c_ref[...]
    s = s * _sigmoid(s)
    o_ref[...] = _bdot(s, w_ref[...]) + b_ref[...]


def _mod_call(cc, w_mod, b_mod):
    depth, d, n = w_mod.shape
    tn = _col_tile(n, 1536)
    return pl.pallas_call(
        _mod_kernel,
        grid=(depth, n // tn),
        in_specs=[pl.BlockSpec((SUBLANES, d), lambda l, j: (0, 0)),
                  pl.BlockSpec((None, d, tn), lambda l, j: (l, 0, j)),
                  pl.BlockSpec((None, 1, tn), lambda l, j: (l, 0, j))],
        out_specs=pl.BlockSpec((None, SUBLANES, tn), lambda l, j: (l, 0, j)),
        out_shape=jax.ShapeDtypeStruct((depth, SUBLANES, n), F32),
        compiler_params=_cparams(("parallel", "parallel")),
        name="mod",
    )(cc, w_mod, b_mod.reshape(depth, 1, n))


def _normlin_kernel(*refs, has_gates):
    if has_gates:
        x_ref, g_ref, sc_ref, sh_ref, w_ref, wg_ref, wgt_ref, o_ref, og_ref, ogt_ref, h_ref = refs
    else:
        x_ref, g_ref, sc_ref, sh_ref, w_ref, o_ref, h_ref = refs

    @pl.when(pl.program_id(2) == 0)
    def _():
        y = _rms(x_ref[...]) * g_ref[...]
        h_ref[...] = (y * (1.0 + sc_ref[...]) + sh_ref[...]).astype(BF16)
        if has_gates:
            og_ref[...] = jnp.dot(h_ref[...], wg_ref[...], preferred_element_type=F32)
            ogt_ref[...] = lax.dot_general(wgt_ref[...], h_ref[...], (((1,), (1,)), ((), ())),
                                           preferred_element_type=F32)

    o_ref[...] = jnp.dot(h_ref[...], w_ref[...], preferred_element_type=F32).astype(o_ref.dtype)


def _normlin_call(x, g, sc, sh, w, w_gates, w_gates_t, tm):
    bsz, seq, d = x.shape
    n = w.shape[1]
    tn = _col_tile(n, 2048)
    has_gates = w_gates is not None
    in_specs = [pl.BlockSpec((None, tm, d), lambda b, i, j: (b, i, 0)),
                pl.BlockSpec((1, d), lambda b, i, j: (0, 0)),
                pl.BlockSpec((None, 1, d), lambda b, i, j: (b, 0, 0)),
                pl.BlockSpec((None, 1, d), lambda b, i, j: (b, 0, 0)),
                pl.BlockSpec((d, tn), lambda b, i, j: (0, j))]
    out_specs = [pl.BlockSpec((None, tm, tn), lambda b, i, j: (b, i, j))]
    out_shape = [jax.ShapeDtypeStruct((bsz, seq, n), BF16)]
    args = [x, g.reshape(1, d), sc, sh, w]
    if has_gates:
        ng = w_gates.shape[1]
        in_specs.append(pl.BlockSpec((d, ng), lambda b, i, j: (0, 0)))
        out_specs.append(pl.BlockSpec((None, tm, ng), lambda b, i, j: (b, i, 0)))
        out_shape.append(jax.ShapeDtypeStruct((bsz, seq, ng), F32))
        ngt = w_gates_t.shape[0]
        in_specs.append(pl.BlockSpec((ngt, d), lambda b, i, j: (0, 0)))
        out_specs.append(pl.BlockSpec((None, ngt, tm), lambda b, i, j: (b, 0, i)))
        out_shape.append(jax.ShapeDtypeStruct((bsz, ngt, seq), F32))
        args += [w_gates, w_gates_t]
    out = pl.pallas_call(
        functools.partial(_normlin_kernel, has_gates=has_gates),
        grid=(bsz, seq // tm, n // tn),
        in_specs=in_specs,
        out_specs=out_specs,
        out_shape=out_shape,
        scratch_shapes=[pltpu.VMEM((tm, d), BF16)],
        compiler_params=_cparams(("parallel", "parallel", "arbitrary")),
        name="normlin",
    )(*args)
    return out if has_gates else out[0]


def _mlp_kernel(x_ref, g_ref, sc_ref, sh_ref, gate_ref, w1_ref, w2_ref, fg_ref, o_ref,
                h_ref, acc_ref, *, final_norm):
    k = pl.program_id(2)

    @pl.when(k == 0)
    def _():
        y = _rms(x_ref[...]) * g_ref[...]
        h_ref[...] = (y * (1.0 + sc_ref[...]) + sh_ref[...]).astype(BF16)
        acc_ref[...] = jnp.zeros_like(acc_ref)

    a = jnp.dot(h_ref[...], w1_ref[...], preferred_element_type=F32)
    a = jnp.square(jnp.maximum(a, 0.0)).astype(BF16)
    acc_ref[...] += jnp.dot(a, w2_ref[...], preferred_element_type=F32)

    @pl.when(k == pl.num_programs(2) - 1)
    def _():
        y = x_ref[...] + gate_ref[...] * acc_ref[...]
        if final_norm:
            y = _rms(y) * fg_ref[...]
        o_ref[...] = y


def _mlp_call(x, g, sc, sh, gate, w1, w2, fg, tm, final_norm):
    bsz, seq, d = x.shape
    f = w1.shape[1]
    tf = _col_tile(f, 1024)
    vec = pl.BlockSpec((None, 1, d), lambda b, i, k: (b, 0, 0))
    return pl.pallas_call(
        functools.partial(_mlp_kernel, final_norm=final_norm),
        grid=(bsz, seq // tm, f // tf),
        in_specs=[pl.BlockSpec((None, tm, d), lambda b, i, k: (b, i, 0)),
                  pl.BlockSpec((1, d), lambda b, i, k: (0, 0)),
                  vec, vec, vec,
                  pl.BlockSpec((d, tf), lambda b, i, k: (0, k)),
                  pl.BlockSpec((tf, d), lambda b, i, k: (k, 0)),
                  pl.BlockSpec((1, d), lambda b, i, k: (0, 0))],
        out_specs=pl.BlockSpec((None, tm, d), lambda b, i, k: (b, i, 0)),
        out_shape=jax.ShapeDtypeStruct((bsz, seq, d), F32),
        scratch_shapes=[pltpu.VMEM((tm, d), BF16), pltpu.VMEM((tm, d), F32)],
        compiler_params=_cparams(("parallel", "parallel", "arbitrary")),
        name="mlp",
    )(x, g.reshape(1, d), sc, sh, gate, w1, w2, fg.reshape(1, d))


CONV_PAD = SUBLANES


def _fill_padded(xpad_ref, x_ref, seq, rows):
    zeros = jnp.zeros((CONV_PAD, xpad_ref.shape[1]), F32)
    xpad_ref[0:CONV_PAD, :] = zeros
    xpad_ref[CONV_PAD + seq:CONV_PAD + seq + CONV_PAD, :] = zeros

    def body(t, carry):
        r0 = pl.multiple_of(t * rows, rows)
        xpad_ref[pl.ds(r0 + CONV_PAD, rows), :] = x_ref[pl.ds(r0, rows), :].astype(F32)
        return carry

    lax.fori_loop(0, seq // rows, body, 0)


def _conv_tile(xpad_ref, r0, rows, w):
    acc = None
    for j in range(CONV_WIDTH):
        off = j - CONV_LEFT
        term = xpad_ref[pl.ds(r0 + (CONV_PAD + off), rows), :] * w[j:j + 1]
        acc = term if acc is None else acc + term
    return acc


def _dnprep_kernel(x_ref, w_ref, o_ref, xpad_ref, *, seq, rows):
    j = pl.program_id(1)
    is_qk = j < 2 * DN_HEADS
    scale = jnp.where(j < DN_HEADS, DN_DK ** -0.5, 1.0).astype(F32)
    _fill_padded(xpad_ref, x_ref, seq, rows)
    w = w_ref[...]

    def body(t, carry):
        r0 = pl.multiple_of(t * rows, rows)
        y = _conv_tile(xpad_ref, r0, rows, w)
        y = y * _sigmoid(y)
        nrm = (y * lax.rsqrt(jnp.sum(y * y, axis=-1, keepdims=True) + EPS)) * scale
        o_ref[pl.ds(r0, rows), :] = jnp.where(is_qk, nrm, y).astype(o_ref.dtype)
        return carry

    lax.fori_loop(0, seq // rows, body, 0)


def _dnprep_call(proj, conv_w, col0_blocks, rows):
    bsz, seq, _ = proj.shape
    ncol = conv_w.shape[1] // LANES
    return pl.pallas_call(
        functools.partial(_dnprep_kernel, seq=seq, rows=rows),
        grid=(bsz, ncol),
        in_specs=[pl.BlockSpec((None, seq, LANES), lambda b, j: (b, 0, col0_blocks + j)),
                  pl.BlockSpec((CONV_WIDTH, LANES), lambda b, j: (0, j))],
        out_specs=pl.BlockSpec((None, seq, LANES), lambda b, j: (b, 0, j)),
        out_shape=jax.ShapeDtypeStruct((bsz, seq, ncol * LANES), BF16),
        scratch_shapes=[pltpu.VMEM((seq + 2 * CONV_PAD, LANES), F32)],
        compiler_params=_cparams(("parallel", "parallel")),
        name="dnprep",
    )(proj, conv_w)


def _scan_tile(a, b, carry, rev, rows):
    nv = rows // SUBLANES
    lanes = a.shape[1]
    a = a.reshape(nv, SUBLANES, lanes)
    b = b.reshape(nv, SUBLANES, lanes)
    row = lax.broadcasted_iota(jnp.int32, a.shape, 1)
    s = 1
    while s < SUBLANES:
        shift = SUBLANES - s if rev else s
        a_s = pltpu.roll(a, shift, 1)
        b_s = pltpu.roll(b, shift, 1)
        valid = (row < SUBLANES - s) if rev else (row >= s)
        b = b + a * jnp.where(valid, b_s, 0.0)
        a = a * jnp.where(valid, a_s, 1.0)
        s *= 2
    hs = [None] * nv
    for v in (range(nv - 1, -1, -1) if rev else range(nv)):
        h = a[v] * carry + b[v]
        carry = h[0:1] if rev else h[SUBLANES - 1:SUBLANES]
        hs[v] = h
    return jnp.concatenate(hs, axis=0), carry


def _gelu_tanh(x):
    return 0.5 * x * (1.0 + jnp.tanh(0.7978845608028654 * (x + 0.044715 * (x * x * x))))


def _lru_kernel(x_ref, y_ref, cw_ref, cb_ref, wa_ref, wi_ref, ba_ref, bi_ref, lam_ref, h0_ref,
                o_ref, ht_ref, xpad_ref, xc_ref, hf_ref, *, seq, rows):
    _fill_padded(xpad_ref, x_ref, seq, rows)
    cw = cw_ref[...]
    cb = cb_ref[...]
    nt = seq // rows

    def gates(xc, d):
        r = 1.0 / (1.0 + jnp.exp2(_bdot(xc, wa_ref[d]) - LOG2E * ba_ref[d:d + 1, :]))
        i = 1.0 / (1.0 + jnp.exp2(_bdot(xc, wi_ref[d]) - LOG2E * bi_ref[d:d + 1, :]))
        neg_rate = LRU_C * _softplus(-lam_ref[d:d + 1, :])
        a = jnp.exp2(r * (-LOG2E * neg_rate))
        b = jnp.sqrt(jnp.tanh(r * neg_rate) * (a * a + 1.0)) * (i * xc)
        return a, b

    def fwd_body(t, carry):
        r0 = pl.multiple_of(t * rows, rows)
        xc = _conv_tile(xpad_ref, r0, rows, cw) + cb
        xc_ref[pl.ds(r0, rows), :] = xc
        a, b = gates(xc, 0)
        h, carry = _scan_tile(a, b, carry, False, rows)
        hf_ref[pl.ds(r0, rows), :] = h
        return carry

    carry_f = lax.fori_loop(0, nt, fwd_body, h0_ref[0:1, :])

    def bwd_body(t, carry):
        r0 = pl.multiple_of((nt - 1 - t) * rows, rows)
        xc = xc_ref[pl.ds(r0, rows), :]
        a, b = gates(xc, 1)
        h, carry = _scan_tile(a, b, carry, True, rows)
        y = y_ref[pl.ds(r0, rows), :].astype(F32)
        o_ref[pl.ds(r0, rows), :] = ((hf_ref[pl.ds(r0, rows), :] + h) * _gelu_tanh(y)).astype(o_ref.dtype)
        return carry

    carry_b = lax.fori_loop(0, nt, bwd_body, h0_ref[1:2, :])
    ht_ref[0:1, :] = carry_f
    ht_ref[1:2, :] = carry_b


def _lru_call(proj, cw, cb, wa, wi, ba, bi, lam, h0, rows):
    bsz, seq, _ = proj.shape
    width = cw.shape[1]
    ncol = width // LANES
    vec2 = pl.BlockSpec((2, LANES), lambda b, j: (0, j))
    wspec = pl.BlockSpec((2, None, LANES, LANES), lambda b, j: (0, j, 0, 0))
    return pl.pallas_call(
        functools.partial(_lru_kernel, seq=seq, rows=rows),
        grid=(bsz, ncol),
        in_specs=[pl.BlockSpec((None, seq, LANES), lambda b, j: (b, 0, j)),
                  pl.BlockSpec((None, seq, LANES), lambda b, j: (b, 0, ncol + j)),
                  pl.BlockSpec((CONV_WIDTH, LANES), lambda b, j: (0, j)),
                  pl.BlockSpec((1, LANES), lambda b, j: (0, j)),
                  wspec, wspec, vec2, vec2, vec2,
                  pl.BlockSpec((None, 2, LANES), lambda b, j: (b, 0, j))],
        out_specs=[pl.BlockSpec((None, seq, LANES), lambda b, j: (b, 0, j)),
                   pl.BlockSpec((None, 2, LANES), lambda b, j: (b, 0, j))],
        out_shape=[jax.ShapeDtypeStruct((bsz, seq, width), BF16),
                   jax.ShapeDtypeStruct((bsz, 2, width), F32)],
        scratch_shapes=[pltpu.VMEM((seq + 2 * CONV_PAD, LANES), F32),
                        pltpu.VMEM((seq, LANES), F32),
                        pltpu.VMEM((seq, LANES), F32)],
        compiler_params=_cparams(("parallel", "parallel")),
        name="lru",
    )(proj, proj, cw, cb.reshape(1, width), wa, wi, ba, bi, lam, h0)


DN_GROUP = 2 * DN_CHUNK


def _split3(x):
    x1 = x.astype(BF16)
    r1 = x - x1.astype(F32)
    x2 = r1.astype(BF16)
    x3 = (r1 - x2.astype(F32)).astype(BF16)
    return x1, x2, x3


def _delta_kernel(q_ref, k_ref, v_ref, gr_ref, grt_ref, cvec_ref, ccol_ref, cum_ref, cumt_ref,
                  ones_ref, s0_ref, o_ref, st_ref, s_ref, *, rev, rows):
    step = pl.program_id(1)
    d = 1 if rev else 0

    @pl.when(step == 0)
    def _():
        s_ref[...] = s0_ref[...]

    raw = gr_ref[...]
    sig = _sigmoid(raw)
    g = -jnp.exp(cvec_ref[0:1, :]) * _softplus(raw + cvec_ref[1:2, :])
    g1, g2, g3 = _split3(g)
    cum = cum_ref[...]
    ones = ones_ref[...]

    def mm(m, parts):
        return sum(jnp.dot(m, p, preferred_element_type=F32) for p in parts)

    gc = mm(cum, (g1, g2, g3))
    gl = mm(ones, (g1, g2, g3))
    e_gc = jnp.exp(gc)
    e_rest = jnp.exp(gl - gc)
    e_gl = jnp.exp(gl)
    gt = -jnp.exp(ccol_ref[0]) * _softplus(grt_ref[...] + ccol_ref[1])
    t1, t2, t3 = _split3(gt)
    cumt = cumt_ref[...]
    gct = sum(jnp.dot(p, cumt, preferred_element_type=F32) for p in (t1, t2, t3))

    ri = lax.broadcasted_iota(jnp.int32, (DN_GROUP, DN_GROUP), 0)
    ci = lax.broadcasted_iota(jnp.int32, (DN_GROUP, DN_GROUP), 1)
    same = (ri // DN_CHUNK) == (ci // DN_CHUNK)
    if rev:
        incl = same & (ri <= ci)
        strict = same & (ri < ci)
    else:
        incl = same & (ri >= ci)
        strict = same & (ri > ci)

    ngroups = rows // DN_GROUP
    zeros_c = jnp.zeros((DN_CHUNK, DN_DK), F32)

    def bcast(x, rsl, lane):
        return jnp.broadcast_to(x[rsl, lane:lane + 1], (DN_GROUP, LANES))

    heads = range(DN_HEADS)
    groups = list(range(ngroups - 1, -1, -1) if rev else range(ngroups))
    units = [(gi, h) for gi in groups for h in heads]
    qd, kd, egl_b, rhs, kk, qk = {}, {}, {}, {}, {}, {}
    for u in units:
        gi, h = u
        rsl = slice(gi * DN_GROUP, (gi + 1) * DN_GROUP)
        lb = d * DN_HEADS + h
        lg = 2 * DN_HEADS + d * DN_HEADS + h
        csl = slice(h * DN_DK, (h + 1) * DN_DK)
        q16 = q_ref[rsl, csl]
        k16 = k_ref[rsl, csl]
        q_h = q16.astype(F32)
        k_h = k16.astype(F32)
        beta_b = bcast(sig, rsl, lb)
        egc_b = bcast(e_gc, rsl, lg)
        kb_h = k_h * beta_b
        kk[u] = _bdot_nt(kb_h, k16)
        qk[u] = _bdot_nt(q16, k16)
        rhs[u] = jnp.concatenate([v_ref[rsl, csl].astype(F32) * beta_b, kb_h * egc_b], axis=1)
        qd[u] = q_h * egc_b
        kd[u] = k_h * bcast(e_rest, rsl, lg)
        egl_b[u] = bcast(e_gl, rsl, lg)
    p, a16, attn = {}, {}, {}
    for u in units:
        gi, h = u
        rsl = slice(gi * DN_GROUP, (gi + 1) * DN_GROUP)
        lg = 2 * DN_HEADS + d * DN_HEADS + h
        diff = bcast(gc, rsl, lg) - gct[lg:lg + 1, rsl]
        decay = jnp.exp(jnp.where(incl, diff, NEG_BIG))
        m = jnp.where(strict, kk[u] * decay, 0.0)
        attn[u] = qk[u] * decay
        p[u] = -m
        a16[u] = m.astype(BF16)
    npow = 1
    while npow < DN_CHUNK // 2:
        a = {u: jnp.dot(a16[u], a16[u], preferred_element_type=F32) for u in units}
        a16 = {u: a[u].astype(BF16) for u in units}
        p = {u: p[u] + a[u] + jnp.dot(p[u].astype(BF16), a16[u], preferred_element_type=F32) for u in units}
        npow *= 2
    uw = {u: rhs[u] + _bdot(p[u], rhs[u]) for u in units}
    s = [s_ref[h] for h in heads]
    for gi in groups:
        for cc in ((1, 0) if rev else (0, 1)):
            c0 = cc * DN_CHUNK
            cs = slice(c0, c0 + DN_CHUNK)
            r2 = [_bdot(jnp.concatenate([uw[gi, h][cs, DN_DK:], qd[gi, h][cs]], axis=0), s[h]) for h in heads]
            v_new = [uw[gi, h][cs, :DN_DK] - r2[h][:DN_CHUNK] for h in heads]
            for h in heads:
                v_pad = (jnp.concatenate([v_new[h], zeros_c], axis=0) if cc == 0
                         else jnp.concatenate([zeros_c, v_new[h]], axis=0))
                o_c = r2[h][DN_CHUNK:] + _bdot(attn[gi, h][cs], v_pad)
                o_ref[gi * DN_GROUP + c0:gi * DN_GROUP + c0 + DN_CHUNK,
                      h * DN_DK:(h + 1) * DN_DK] = o_c.astype(o_ref.dtype)
            for h in heads:
                egl = jnp.concatenate([egl_b[gi, h][cs], egl_b[gi, h][cs]], axis=0)
                s[h] = s[h] * egl + _bdot_tn(kd[gi, h][cs], v_new[h])
    for h in heads:
        s_ref[h] = s[h]

    @pl.when(step == pl.num_programs(1) - 1)
    def _():
        st_ref[...] = s_ref[...]


def _delta_consts(rows, rev):
    t = jnp.arange(rows)
    same = (t[:, None] // DN_CHUNK) == (t[None, :] // DN_CHUNK)
    cum = same & ((t[:, None] <= t[None, :]) if rev else (t[:, None] >= t[None, :]))
    return cum.astype(BF16), cum.T.astype(BF16), same.astype(BF16)


def _delta_call(qkv, gates, gates_t, cvec, ccol, s0, rev, rows):
    bsz, seq, _ = qkv.shape
    width = DN_HEADS * DN_DK
    nsteps = seq // rows
    cum, cumt, ones = _delta_consts(rows, rev)

    def blk(i):
        return nsteps - 1 - i if rev else i

    full = lambda shape: pl.BlockSpec(shape, lambda b, i: (0,) * len(shape))
    state = pl.BlockSpec((None, DN_HEADS, DN_DK, DN_DK), lambda b, i: (b, 0, 0, 0))
    return pl.pallas_call(
        functools.partial(_delta_kernel, rev=rev, rows=rows),
        grid=(bsz, nsteps),
        in_specs=[pl.BlockSpec((None, rows, width), lambda b, i: (b, blk(i), 0)),
                  pl.BlockSpec((None, rows, width), lambda b, i: (b, blk(i), 1)),
                  pl.BlockSpec((None, rows, width), lambda b, i: (b, blk(i), 2)),
                  pl.BlockSpec((None, rows, LANES), lambda b, i: (b, blk(i), 0)),
                  pl.BlockSpec((None, 4 * DN_HEADS, rows), lambda b, i: (b, 0, blk(i))),
                  full((SUBLANES, LANES)),
                  full((2, 4 * DN_HEADS, rows)),
                  full((rows, rows)), full((rows, rows)), full((rows, rows)),
                  state],
        out_specs=[pl.BlockSpec((None, rows, width), lambda b, i: (b, blk(i), 0)), state],
        out_shape=[jax.ShapeDtypeStruct((bsz, seq, width), BF16),
                   jax.ShapeDtypeStruct((bsz, DN_HEADS, DN_DK, DN_DK), F32)],
        scratch_shapes=[pltpu.VMEM((DN_HEADS, DN_DK, DN_DK), F32)],
        compiler_params=_cparams(("parallel", "arbitrary")),
        name="delta_bwd" if rev else "delta_fwd",
    )(qkv, qkv, qkv, gates, gates_t, cvec, ccol, cum, cumt, ones, s0)


def _evenout_kernel(lru_ref, of_ref, ob_ref, z_ref, g_ref, w_ref, res_ref, gate_ref, o_ref, a_ref):
    width = lru_ref.shape[1]
    a_ref[:, 0:width] = lru_ref[...].astype(BF16)
    for h in range(DN_HEADS):
        csl = slice(h * DN_DK, (h + 1) * DN_DK)
        o = of_ref[:, csl].astype(F32) + ob_ref[:, csl].astype(F32)
        z = z_ref[:, csl].astype(F32)
        y = (_rms(o) * g_ref[:, csl]) * (z * _sigmoid(z))
        a_ref[:, width + h * DN_DK:width + (h + 1) * DN_DK] = y.astype(BF16)
    o_ref[...] = res_ref[...] + gate_ref[...] * jnp.dot(a_ref[...], w_ref[...], preferred_element_type=F32)


def _evenout_call(lru, o_f, o_b, proj, z_block, dn_g, w_out, res, gate, tm):
    bsz, seq, width = lru.shape
    dnw = o_f.shape[2]
    d = res.shape[2]
    row = lambda c: pl.BlockSpec((None, tm, c), lambda b, i: (b, i, 0))
    return pl.pallas_call(
        _evenout_kernel,
        grid=(bsz, seq // tm),
        in_specs=[row(width), row(dnw), row(dnw),
                  pl.BlockSpec((None, tm, dnw), lambda b, i: (b, i, z_block)),
                  pl.BlockSpec((1, dnw), lambda b, i: (0, 0)),
                  pl.BlockSpec((width + dnw, d), lambda b, i: (0, 0)),
                  row(d),
                  pl.BlockSpec((None, 1, d), lambda b, i: (b, 0, 0))],
        out_specs=row(d),
        out_shape=jax.ShapeDtypeStruct((bsz, seq, d), F32),
        scratch_shapes=[pltpu.VMEM((tm, width + dnw), BF16)],
        compiler_params=_cparams(("parallel", "parallel")),
        name="evenout",
    )(lru, o_f, o_b, proj, dn_g, w_out, res, gate)


def _linres_kernel(a_ref, w_ref, res_ref, gate_ref, o_ref):
    o_ref[...] = res_ref[...] + gate_ref[...] * jnp.dot(a_ref[...].astype(BF16), w_ref[...],
                                                        preferred_element_type=F32)


def _linres_call(a, w, res, gate, tm):
    bsz, seq, kdim = a.shape
    d = res.shape[2]
    return pl.pallas_call(
        _linres_kernel,
        grid=(bsz, seq // tm),
        in_specs=[pl.BlockSpec((None, tm, kdim), lambda b, i: (b, i, 0)),
                  pl.BlockSpec((kdim, d), lambda b, i: (0, 0)),
                  pl.BlockSpec((None, tm, d), lambda b, i: (b, i, 0)),
                  pl.BlockSpec((None, 1, d), lambda b, i: (b, 0, 0))],
        out_specs=pl.BlockSpec((None, tm, d), lambda b, i: (b, i, 0)),
        out_shape=jax.ShapeDtypeStruct((bsz, seq, d), F32),
        compiler_params=_cparams(("parallel", "parallel")),
        name="linres",
    )(a, w, res, gate)


def _aprep_kernel(*refs, rope, want_q):
    if rope:
        x_ref, qg_ref, kg_ref, cos_ref, sin_ref = refs[:5]
        outs = refs[5:]
    else:
        x_ref, qg_ref, kg_ref = refs[:3]
        outs = refs[3:]
    if want_q:
        q_ref, k_ref, v_ref = outs
    else:
        k_ref, v_ref = outs

    if rope:
        cos = cos_ref[...]
        sin = sin_ref[...]

    def norm_rope(x, g):
        y = _rms(x) * g
        if rope:
            y = y * cos + pltpu.roll(y, HEAD_DIM // 2, 1) * sin
        return y

    if want_q:
        for h in range(N_Q_HEADS):
            x = x_ref[:, h * HEAD_DIM:(h + 1) * HEAD_DIM].astype(F32)
            q_ref[h] = (norm_rope(x, qg_ref[...]) * (HEAD_DIM ** -0.5 * LOG2E)).astype(BF16)
    for h in range(N_KV_HEADS):
        c0 = (N_Q_HEADS + h) * HEAD_DIM
        k_ref[h] = norm_rope(x_ref[:, c0:c0 + HEAD_DIM].astype(F32), kg_ref[...]).astype(BF16)
        c1 = (N_Q_HEADS + N_KV_HEADS + h) * HEAD_DIM
        v_ref[h, 0:HEAD_DIM, :] = x_ref[:, c1:c1 + HEAD_DIM].astype(F32).T.astype(BF16)
        pad_row = lax.broadcasted_iota(jnp.int32, (VT_PAD, x_ref.shape[0]), 0)
        v_ref[h, HEAD_DIM:HEAD_DIM + VT_PAD, :] = jnp.where(pad_row == 0, 1.0, 0.0).astype(BF16)


def _aprep_call(qkv, qg, kg, rope_tabs, want_q, tm):
    bsz, seq, cols = qkv.shape
    rope = rope_tabs is not None
    in_specs = [pl.BlockSpec((None, tm, cols), lambda b, i: (b, i, 0)),
                pl.BlockSpec((1, HEAD_DIM), lambda b, i: (0, 0)),
                pl.BlockSpec((1, HEAD_DIM), lambda b, i: (0, 0))]
    args = [qkv, qg.reshape(1, HEAD_DIM), kg.reshape(1, HEAD_DIM)]
    if rope:
        in_specs += [pl.BlockSpec((tm, HEAD_DIM), lambda b, i: (i, 0))] * 2
        args += list(rope_tabs)
    hspec = lambda n: pl.BlockSpec((None, n, tm, HEAD_DIM), lambda b, i: (b, 0, i, 0))
    hshape = lambda n: jax.ShapeDtypeStruct((bsz, n, seq, HEAD_DIM), BF16)
    tspec = lambda n: pl.BlockSpec((None, n, HEAD_DIM + VT_PAD, tm), lambda b, i: (b, 0, 0, i))
    tshape = lambda n: jax.ShapeDtypeStruct((bsz, n, HEAD_DIM + VT_PAD, seq), BF16)
    out_specs = [hspec(N_KV_HEADS), tspec(N_KV_HEADS)]
    out_shape = [hshape(N_KV_HEADS), tshape(N_KV_HEADS)]
    if want_q:
        out_specs = [hspec(N_Q_HEADS)] + out_specs
        out_shape = [hshape(N_Q_HEADS)] + out_shape
    return pl.pallas_call(
        functools.partial(_aprep_kernel, rope=rope, want_q=want_q),
        grid=(bsz, seq // tm),
        in_specs=in_specs,
        out_specs=out_specs,
        out_shape=out_shape,
        compiler_params=_cparams(("parallel", "parallel")),
        name="aprep_x" if rope else "aprep_ctx",
    )(*args)


def _rope_tables(n_tokens):
    rows = n_tokens // GRID_W
    axis = HEAD_DIM // 2
    row_id = jnp.repeat(jnp.arange(rows), GRID_W).astype(F32)
    col_id = jnp.tile(jnp.arange(GRID_W), rows).astype(F32)
    inv_freq = ROPE_THETA ** (-jnp.arange(0, axis, 2, dtype=F32) / axis)
    ang_r = row_id[:, None] * inv_freq[None]
    ang_c = col_id[:, None] * inv_freq[None]
    cr, sr, cc, sc = jnp.cos(ang_r), jnp.sin(ang_r), jnp.cos(ang_c), jnp.sin(ang_c)
    cos = jnp.concatenate([cr, cc, cr, cc], axis=-1)
    sin = jnp.concatenate([-sr, -sc, sr, sc], axis=-1)
    return cos, sin


def _rope_perm():
    quarter = HEAD_DIM // 4
    return jnp.concatenate([jnp.arange(0, quarter), jnp.arange(2 * quarter, 3 * quarter),
                            jnp.arange(quarter, 2 * quarter), jnp.arange(3 * quarter, 4 * quarter)])


def _flash_kernel(q_ref, k_ref, vt_ref, o_ref, s0_ref, sa_ref, sb_ref, m0_ref, ma_ref, mb_ref, *,
                  tq, tk, nk, nt):
    m_rows = Q_PER_KV * tq

    def q_tile(i):
        r0 = pl.multiple_of(i * tq, tq)
        return q_ref[:, pl.ds(r0, tq), :].reshape(m_rows, HEAD_DIM)

    def scores(q, j, s_ref, cmax_ref):
        c0 = pl.multiple_of(j * tk, tk)
        st = lax.dot_general(k_ref[pl.ds(c0, tk), :], q, (((1,), (1,)), ((), ())),
                             preferred_element_type=F32)
        s_ref[...] = st
        cmax_ref[...] = jnp.max(st, axis=0, keepdims=True)

    def update(j, s_ref, cmax_ref, carry):
        m, acc = carry
        m_new = jnp.maximum(m, cmax_ref[...])
        alpha = jnp.exp2(m - m_new)
        p = jnp.exp2(s_ref[...] - m_new)
        vt = vt_ref[:, pl.ds(pl.multiple_of(j * tk, tk), tk)]
        acc = alpha * acc + jnp.dot(vt, p.astype(BF16), preferred_element_type=F32)
        return m_new, acc

    bufs = ((sa_ref, ma_ref), (sb_ref, mb_ref))

    def tile_body(i, _):
        q = q_tile(i)
        q_next = q_tile(jnp.minimum(i + 1, nt - 1))

        def run(j0, count, carry, last):
            for u in range(count):
                j = j0 + u
                if u + 1 < count or not last:
                    scores(q, j + 1, *bufs[(u + 1) % 2])
                else:
                    scores(q_next, 0, s0_ref, m0_ref)
                carry = update(j, *bufs[u % 2], carry)
            return carry

        carry = (jnp.full((1, m_rows), NEG_BIG, F32), jnp.zeros((HEAD_DIM + VT_PAD, m_rows), F32))
        scores(q, 1, sa_ref, ma_ref)
        carry = update(0, s0_ref, m0_ref, carry)
        rest = nk - 1
        niter = (rest - 1) // FLASH_UNROLL
        carry = lax.fori_loop(0, niter, lambda jj, c: run(1 + jj * FLASH_UNROLL, FLASH_UNROLL, c, False),
                              carry)
        _, acc = run(1 + niter * FLASH_UNROLL, rest - niter * FLASH_UNROLL, carry, True)
        o = (acc[:HEAD_DIM] / acc[HEAD_DIM:HEAD_DIM + 1]).T
        r0 = pl.multiple_of(i * tq, tq)
        for r in range(Q_PER_KV):
            o_ref[pl.ds(r0, tq), r * HEAD_DIM:(r + 1) * HEAD_DIM] = o[r * tq:(r + 1) * tq].astype(o_ref.dtype)
        return 0

    scores(q_tile(0), 0, s0_ref, m0_ref)
    lax.fori_loop(0, nt, tile_body, 0)


def _flash_call(q, k, vt, tq, tk):
    bsz, _, seq, _ = q.shape
    lk = k.shape[2]
    nk = lk // tk
    assert nk >= 2, "the score pipeline needs at least two key blocks"
    q5 = q.reshape(bsz, N_KV_HEADS, Q_PER_KV, seq, HEAD_DIM)
    return pl.pallas_call(
        functools.partial(_flash_kernel, tq=tq, tk=tk, nk=nk, nt=seq // tq),
        grid=(bsz, N_KV_HEADS),
        in_specs=[pl.BlockSpec((None, None, Q_PER_KV, seq, HEAD_DIM), lambda b, g: (b, g, 0, 0, 0)),
                  pl.BlockSpec((None, None, lk, HEAD_DIM), lambda b, g: (b, g, 0, 0)),
                  pl.BlockSpec((None, None, HEAD_DIM + VT_PAD, lk), lambda b, g: (b, g, 0, 0))],
        out_specs=pl.BlockSpec((None, seq, Q_PER_KV * HEAD_DIM), lambda b, g: (b, 0, g)),
        out_shape=jax.ShapeDtypeStruct((bsz, seq, N_Q_HEADS * HEAD_DIM), BF16),
        scratch_shapes=([pltpu.VMEM((tk, Q_PER_KV * tq), F32)] * 3
                        + [pltpu.VMEM((1, Q_PER_KV * tq), F32)] * 3),
        compiler_params=_cparams(("parallel", "parallel")),
        name="flash",
    )(q5, k, vt)


def _row_tile(seq, cap):
    t = min(seq, cap)
    while seq % t:
        t //= 2
    return t


def _even_weights(w_in, lru_wa, lru_wi, dn_alog, dn_dtb, dn_g):
    d, cols = w_in.shape
    ngate = 4 * DN_HEADS
    main = cols - ngate
    w_gates = jnp.concatenate([w_in[:, main:], jnp.zeros((d, LANES - ngate), F32)], axis=1)

    def blockdiag(w):
        w = w * (-LOG2E)
        z = jnp.zeros_like(w[:, 0::2])
        top = jnp.concatenate([w[:, 0::2], z], axis=-1)
        bot = jnp.concatenate([z, w[:, 1::2]], axis=-1)
        return jnp.concatenate([top, bot], axis=-2).astype(BF16)

    alog = dn_alog.reshape(-1)
    dtb = dn_dtb.reshape(-1)
    cvec = jnp.zeros((SUBLANES, LANES), F32)
    cvec = cvec.at[0, 2 * DN_HEADS:ngate].set(alog).at[1, 2 * DN_HEADS:ngate].set(dtb)

    ccol = jnp.zeros((2, ngate), F32).at[0, 2 * DN_HEADS:].set(alog).at[1, 2 * DN_HEADS:].set(dtb)

    return dict(w_in=w_in[:, :main].astype(BF16), w_gates=w_gates.astype(BF16),
                w_gates_t=w_in[:, main:].T.astype(BF16),
                wa=blockdiag(lru_wa), wi=blockdiag(lru_wi), cvec=cvec, ccol=ccol,
                dn_g=jnp.tile(dn_g, DN_HEADS).reshape(1, -1))


def _even_stream(h_in, res, gate, norm_g, sc, sh, ew, lru_cw, lru_cb, lru_ba, lru_bi, lru_lam, dn_cw,
                 w_out, h0, s0, tm_in, tm, rows, dn_rows):
    width = lru_cw.shape[1]
    ngate = 4 * DN_HEADS
    proj, gates, gates_t = _normlin_call(h_in, norm_g, sc, sh, ew["w_in"], ew["w_gates"], ew["w_gates_t"],
                                         tm_in)
    lru, h_t = _lru_call(proj, lru_cw, lru_cb, ew["wa"], ew["wi"], lru_ba, lru_bi, lru_lam, h0,
                         _row_tile(proj.shape[1], LRU_ROWS))
    qkv = _dnprep_call(proj, dn_cw, 2 * width // LANES, _row_tile(proj.shape[1], DNPREP_ROWS))
    ccol = jnp.broadcast_to(ew["ccol"][:, :, None], (2, ngate, dn_rows))
    outs, states = [], []
    for rev in (False, True):
        o_d, s_d = _delta_call(qkv, gates, gates_t, ew["cvec"], ccol,
                               s0[:, :, 1 if rev else 0], rev, dn_rows)
        outs.append(o_d)
        states.append(s_d)
    z_block = (2 * width + dn_cw.shape[1]) // (DN_HEADS * DN_DK)
    out = _evenout_call(lru, outs[0], outs[1], proj, z_block, ew["dn_g"], w_out, res, gate, tm)
    return out, h_t, jnp.stack(states, axis=2)


def kernel(x, c, ctx, c_ctx, norm_mix_g, norm_mlp_g, w_mod, b_mod, mlp_w1, mlp_w2, even_w_in, lru_conv_w,
           lru_conv_b, lru_w_a, lru_b_a, lru_w_i, lru_b_i, lru_lambda, dn_conv_w, dn_a_log, dn_dt_bias,
           dn_norm_g, even_w_out, attn_w_qkv, attn_q_norm_g, attn_k_norm_g, attn_w_o, final_norm_g):
    bsz, seq, d = x.shape
    lc = ctx.shape[1]
    depth = w_mod.shape[0]
    tm_x = _row_tile(seq, 512)
    tm_c = _row_tile(lc, 512)
    tm_mlp = _row_tile(seq, 1024)
    rows_x = _row_tile(seq, 256)
    rows_c = _row_tile(lc, 256)

    cc = jnp.zeros((SUBLANES, d), F32).at[:bsz].set(c).at[bsz].set(c_ctx)
    mod = _mod_call(cc, w_mod, b_mod).reshape(depth, SUBLANES, 6, d)

    rope = _rope_tables(seq)
    xs, cs = x, ctx
    for l in range(depth):
        last = l == depth - 1
        j = l // 2
        mx = [mod[l, :bsz, k][:, None, :] for k in range(6)]
        mc = [jnp.broadcast_to(mod[l, bsz, k][None, None, :], (bsz, 1, d)) for k in range(6)]
        sh1, sc1, g1, sh2, sc2, g2 = mx
        csh1, csc1, cg1, csh2, csc2, cg2 = mc
        w1 = mlp_w1[l].astype(BF16)
        w2 = mlp_w2[l].astype(BF16)
        if l % 2 == 0:
            ew = _even_weights(even_w_in[j], lru_w_a[j], lru_w_i[j], dn_a_log[j], dn_dt_bias[j], dn_norm_g[j])
            w_out = even_w_out[j].astype(BF16)
            h0 = jnp.zeros((bsz, 2, lru_conv_w.shape[2]), F32)
            s0 = jnp.zeros((bsz, DN_HEADS, 2, DN_DK, DN_DK), F32)
            common = (ew, lru_conv_w[j], lru_conv_b[j], lru_b_a[j], lru_b_i[j], lru_lambda[j], dn_conv_w[j], w_out)
            cs_mix, h_t, s_t = _even_stream(cs, cs, cg1, norm_mix_g[l], csc1, csh1, *common, h0, s0,
                                            tm_c, tm_c, rows_c, rows_c)
            xs, _, _ = _even_stream(xs, xs, g1, norm_mix_g[l], sc1, sh1, *common, h_t, s_t,
                                    tm_mlp, tm_x, rows_x, _row_tile(seq, DELTA_ROWS))
        else:
            perm = _rope_perm()
            nqk = (N_Q_HEADS + N_KV_HEADS) * HEAD_DIM
            w_qk = attn_w_qkv[j][:, :nqk].reshape(d, N_Q_HEADS + N_KV_HEADS, HEAD_DIM)[:, :, perm]
            w_qkv = jnp.concatenate([w_qk.reshape(d, nqk), attn_w_qkv[j][:, nqk:]], axis=1).astype(BF16)
            q_g = attn_q_norm_g[j][perm]
            k_g = attn_k_norm_g[j][perm]
            w_o = attn_w_o[j].astype(BF16)
            qkv_c = _normlin_call(cs, norm_mix_g[l], csc1, csh1, w_qkv, None, None, tm_c)
            qkv_x = _normlin_call(xs, norm_mix_g[l], sc1, sh1, w_qkv, None, None, tm_mlp)
            if last:
                k_c, vt_c = _aprep_call(qkv_c, q_g, k_g,None, False, rows_c)
            else:
                qt_c, k_c, vt_c = _aprep_call(qkv_c, q_g, k_g,None, True, rows_c)
            qt_x, k_x, vt_x = _aprep_call(qkv_x, q_g, k_g,rope, True, tm_x)
            k_all = jnp.concatenate([k_c, k_x], axis=2)
            vt_all = jnp.concatenate([vt_c, vt_x], axis=3)
            o_x = _flash_call(qt_x, k_all, vt_all, rows_x, _col_tile(lc + seq, FLASH_TK_CAP))
            xs = _linres_call(o_x, w_o, xs, g1, tm_x)
            if not last:
                o_c = _flash_call(qt_c, k_c, vt_c, rows_c, _col_tile(lc, min(FLASH_TK_CAP, lc // 2)))
                cs_mix = _linres_call(o_c, w_o, cs, cg1, tm_c)
        xs = _mlp_call(xs, norm_mlp_g[l], sc2, sh2, g2, w1, w2, final_norm_g, tm_mlp, last)
        if not last:
            cs = _mlp_call(cs_mix, norm_mlp_g[l], csc2, csh2, cg2, w1, w2, final_norm_g, tm_c, False)
    return xs
```

```python
import functools

import jax
import jax.numpy as jnp
from jax import lax
from jax.experimental import pallas as pl
from jax.experimental.pallas import tpu as pltpu

F32 = jnp.float32
BF16 = jnp.bfloat16

EPS = 1e-6
LANES = 128
SUBLANES = 8
VMEM_LIMIT = 56 * 1024 * 1024

LRU_BLOCK = 64
LRU_C = 8.0
CONV_WIDTH = 4
CONV_LEFT = 2
DN_HEADS = 8
DN_DK = 128
DN_CHUNK = 64
N_Q_HEADS = 8
N_KV_HEADS = 4
Q_PER_KV = N_Q_HEADS // N_KV_HEADS
HEAD_DIM = 128
GRID_W = 64
ROPE_THETA = 10000.0
NEG_BIG = -1e30
LOG2E = 1.4426950408889634
DNPREP_ROWS = 1024
LRU_ROWS = 1024
DELTA_ROWS = 256
FLASH_TK_CAP = 768
VT_PAD = 16
FLASH_UNROLL = 4


def _cparams(sem):
    return pltpu.CompilerParams(dimension_semantics=sem, vmem_limit_bytes=VMEM_LIMIT)


def _bdot(a, b):
    return jnp.dot(a.astype(BF16), b.astype(BF16), preferred_element_type=F32)


def _bdot_nt(a, b):
    return lax.dot_general(a.astype(BF16), b.astype(BF16), (((1,), (1,)), ((), ())),
                           preferred_element_type=F32)


def _bdot_tn(a, b):
    return lax.dot_general(a.astype(BF16), b.astype(BF16), (((0,), (0,)), ((), ())),
                           preferred_element_type=F32)


def _sigmoid(x):
    return jax.nn.sigmoid(x)


def _softplus(x):
    return jnp.maximum(x, 0.0) + jnp.log1p(jnp.exp(-jnp.abs(x)))


def _rms(x):
    return x * lax.rsqrt(jnp.mean(x * x, axis=-1, keepdims=True) + EPS)


def _col_tile(n, cap):
    best = LANES
    t = LANES
    while t <= min(n, cap):
        if n % t == 0:
            best = t
        t += LANES
    return best


def _mod_kernel(c_ref, w_ref, b_ref, o_ref):
    s = c_ref[...]
    s = s * _sigmoid(s)
    o_ref[...] = _bdot(s, w_ref[...]) + b_ref[...]


def _mod_call(cc, w_mod, b_mod):
    depth, d, n = w_mod.shape
    tn = _col_tile(n, 1536)
    return pl.pallas_call(
        _mod_kernel,
        grid=(depth, n // tn),
        in_specs=[pl.BlockSpec((SUBLANES, d), lambda l, j: (0, 0)),
                  pl.BlockSpec((None, d, tn), lambda l, j: (l, 0, j)),
                  pl.BlockSpec((None, 1, tn), lambda l, j: (l, 0, j))],
        out_specs=pl.BlockSpec((None, SUBLANES, tn), lambda l, j: (l, 0, j)),
        out_shape=jax.ShapeDtypeStruct((depth, SUBLANES, n), F32),
        compiler_params=_cparams(("parallel", "parallel")),
        name="mod",
    )(cc, w_mod, b_mod.reshape(depth, 1, n))


def _normlin_kernel(*refs, has_gates):
    if has_gates:
        x_ref, g_ref, sc_ref, sh_ref, w_ref, wg_ref, wgt_ref, o_ref, og_ref, ogt_ref, h_ref = refs
    else:
        x_ref, g_ref, sc_ref, sh_ref, w_ref, o_ref, h_ref = refs

    @pl.when(pl.program_id(2) == 0)
    def _():
        y = _rms(x_ref[...]) * g_ref[...]
        h_ref[...] = (y * (1.0 + sc_ref[...]) + sh_ref[...]).astype(BF16)
        if has_gates:
            og_ref[...] = jnp.dot(h_ref[...], wg_ref[...], preferred_element_type=F32)
            ogt_ref[...] = lax.dot_general(wgt_ref[...], h_ref[...], (((1,), (1,)), ((), ())),
                                           preferred_element_type=F32)

    o_ref[...] = jnp.dot(h_ref[...], w_ref[...], preferred_element_type=F32).astype(o_ref.dtype)


def _normlin_call(x, g, sc, sh, w, w_gates, w_gates_t, tm):
    bsz, seq, d = x.shape
    n = w.shape[1]
    tn = _col_tile(n, 2048)
    has_gates = w_gates is not None
    in_specs = [pl.BlockSpec((None, tm, d), lambda b, i, j: (b, i, 0)),
                pl.BlockSpec((1, d), lambda b, i, j: (0, 0)),
                pl.BlockSpec((None, 1, d), lambda b, i, j: (b, 0, 0)),
                pl.BlockSpec((None, 1, d), lambda b, i, j: (b, 0, 0)),
                pl.BlockSpec((d, tn), lambda b, i, j: (0, j))]
    out_specs = [pl.BlockSpec((None, tm, tn), lambda b, i, j: (b, i, j))]
    out_shape = [jax.ShapeDtypeStruct((bsz, seq, n), BF16)]
    args = [x, g.reshape(1, d), sc, sh, w]
    if has_gates:
        ng = w_gates.shape[1]
        in_specs.append(pl.BlockSpec((d, ng), lambda b, i, j: (0, 0)))
        out_specs.append(pl.BlockSpec((None, tm, ng), lambda b, i, j: (b, i, 0)))
        out_shape.append(jax.ShapeDtypeStruct((bsz, seq, ng), F32))
        ngt = w_gates_t.shape[0]
        in_specs.append(pl.BlockSpec((ngt, d), lambda b, i, j: (0, 0)))
        out_specs.append(pl.BlockSpec((None, ngt, tm), lambda b, i, j: (b, 0, i)))
        out_shape.append(jax.ShapeDtypeStruct((bsz, ngt, seq), F32))
        args += [w_gates, w_gates_t]
    out = pl.pallas_call(
        functools.partial(_normlin_kernel, has_gates=has_gates),
        grid=(bsz, seq // tm, n // tn),
        in_specs=in_specs,
        out_specs=out_specs,
        out_shape=out_shape,
        scratch_shapes=[pltpu.VMEM((tm, d), BF16)],
        compiler_params=_cparams(("parallel", "parallel", "arbitrary")),
        name="normlin",
    )(*args)
    return out if has_gates else out[0]


def _mlp_kernel(x_ref, g_ref, sc_ref, sh_ref, gate_ref, w1_ref, w2_ref, fg_ref, o_ref,
                h_ref, acc_ref, *, final_norm):
    k = pl.program_id(2)

    @pl.when(k == 0)
    def _():
        y = _rms(x_ref[...]) * g_ref[...]
        h_ref[...] = (y * (1.0 + sc_ref[...]) + sh_ref[...]).astype(BF16)
        acc_ref[...] = jnp.zeros_like(acc_ref)

    a = jnp.dot(h_ref[...], w1_ref[...], preferred_element_type=F32)
    a = jnp.square(jnp.maximum(a, 0.0)).astype(BF16)
    acc_ref[...] += jnp.dot(a, w2_ref[...], preferred_element_type=F32)

    @pl.when(k == pl.num_programs(2) - 1)
    def _():
        y = x_ref[...] + gate_ref[...] * acc_ref[...]
        if final_norm:
            y = _rms(y) * fg_ref[...]
        o_ref[...] = y


def _mlp_call(x, g, sc, sh, gate, w1, w2, fg, tm, final_norm):
    bsz, seq, d = x.shape
    f = w1.shape[1]
    tf = _col_tile(f, 2048)
    vec = pl.BlockSpec((None, 1, d), lambda b, i, k: (b, 0, 0))
    return pl.pallas_call(
        functools.partial(_mlp_kernel, final_norm=final_norm),
        grid=(bsz, seq // tm, f // tf),
        in_specs=[pl.BlockSpec((None, tm, d), lambda b, i, k: (b, i, 0)),
                  pl.BlockSpec((1, d), lambda b, i, k: (0, 0)),
                  vec, vec, vec,
                  pl.BlockSpec((d, tf), lambda b, i, k: (0, k)),
                  pl.BlockSpec((tf, d), lambda b, i, k: (k, 0)),
                  pl.BlockSpec((1, d), lambda b, i, k: (0, 0))],
        out_specs=pl.BlockSpec((None, tm, d), lambda b, i, k: (b, i, 0)),
        out_shape=jax.ShapeDtypeStruct((bsz, seq, d), F32),
        scratch_shapes=[pltpu.VMEM((tm, d), BF16), pltpu.VMEM((tm, d), F32)],
        compiler_params=_cparams(("parallel", "parallel", "arbitrary")),
        name="mlp",
    )(x, g.reshape(1, d), sc, sh, gate, w1, w2, fg.reshape(1, d))


CONV_PAD = SUBLANES


def _fill_padded(xpad_ref, x_ref, seq, rows):
    zeros = jnp.zeros((CONV_PAD, xpad_ref.shape[1]), F32)
    xpad_ref[0:CONV_PAD, :] = zeros
    xpad_ref[CONV_PAD + seq:CONV_PAD + seq + CONV_PAD, :] = zeros

    def body(t, carry):
        r0 = pl.multiple_of(t * rows, rows)
        xpad_ref[pl.ds(r0 + CONV_PAD, rows), :] = x_ref[pl.ds(r0, rows), :].astype(F32)
        return carry

    lax.fori_loop(0, seq // rows, body, 0)


def _conv_tile(xpad_ref, r0, rows, w):
    acc = None
    for j in range(CONV_WIDTH):
        off = j - CONV_LEFT
        term = xpad_ref[pl.ds(r0 + (CONV_PAD + off), rows), :] * w[j:j + 1]
        acc = term if acc is None else acc + term
    return acc


def _dnprep_kernel(x_ref, w_ref, o_ref, xpad_ref, *, seq, rows):
    j = pl.program_id(1)
    is_qk = j < 2 * DN_HEADS
    scale = jnp.where(j < DN_HEADS, DN_DK ** -0.5, 1.0).astype(F32)
    _fill_padded(xpad_ref, x_ref, seq, rows)
    w = w_ref[...]

    def body(t, carry):
        r0 = pl.multiple_of(t * rows, rows)
        y = _conv_tile(xpad_ref, r0, rows, w)
        y = y * _sigmoid(y)
        nrm = (y * lax.rsqrt(jnp.sum(y * y, axis=-1, keepdims=True) + EPS)) * scale
        o_ref[pl.ds(r0, rows), :] = jnp.where(is_qk, nrm, y).astype(o_ref.dtype)
        return carry

    lax.fori_loop(0, seq // rows, body, 0)


def _dnprep_call(proj, conv_w, col0_blocks, rows):
    bsz, seq, _ = proj.shape
    ncol = conv_w.shape[1] // LANES
    return pl.pallas_call(
        functools.partial(_dnprep_kernel, seq=seq, rows=rows),
        grid=(bsz, ncol),
        in_specs=[pl.BlockSpec((None, seq, LANES), lambda b, j: (b, 0, col0_blocks + j)),
                  pl.BlockSpec((CONV_WIDTH, LANES), lambda b, j: (0, j))],
        out_specs=pl.BlockSpec((None, seq, LANES), lambda b, j: (b, 0, j)),
        out_shape=jax.ShapeDtypeStruct((bsz, seq, ncol * LANES), BF16),
        scratch_shapes=[pltpu.VMEM((seq + 2 * CONV_PAD, LANES), F32)],
        compiler_params=_cparams(("parallel", "parallel")),
        name="dnprep",
    )(proj, conv_w)


def _scan_tile(a, b, carry, rev, rows):
    nv = rows // SUBLANES
    lanes = a.shape[1]
    a = a.reshape(nv, SUBLANES, lanes)
    b = b.reshape(nv, SUBLANES, lanes)
    row = lax.broadcasted_iota(jnp.int32, a.shape, 1)
    s = 1
    while s < SUBLANES:
        shift = SUBLANES - s if rev else s
        a_s = pltpu.roll(a, shift, 1)
        b_s = pltpu.roll(b, shift, 1)
        valid = (row < SUBLANES - s) if rev else (row >= s)
        b = b + a * jnp.where(valid, b_s, 0.0)
        a = a * jnp.where(valid, a_s, 1.0)
        s *= 2
    hs = [None] * nv
    for v in (range(nv - 1, -1, -1) if rev else range(nv)):
        h = a[v] * carry + b[v]
        carry = h[0:1] if rev else h[SUBLANES - 1:SUBLANES]
        hs[v] = h
    return jnp.concatenate(hs, axis=0), carry


def _gelu_tanh(x):
    return 0.5 * x * (1.0 + jnp.tanh(0.7978845608028654 * (x + 0.044715 * (x * x * x))))


def _lru_kernel(x_ref, y_ref, cw_ref, cb_ref, wa_ref, wi_ref, ba_ref, bi_ref, lam_ref, h0_ref,
                o_ref, ht_ref, xpad_ref, xc_ref, hf_ref, *, seq, rows):
    _fill_padded(xpad_ref, x_ref, seq, rows)
    cw = cw_ref[...]
    cb = cb_ref[...]
    nt = seq // rows

    def gates(xc, d):
        r = 1.0 / (1.0 + jnp.exp2(_bdot(xc, wa_ref[d]) - LOG2E * ba_ref[d:d + 1, :]))
        i = 1.0 / (1.0 + jnp.exp2(_bdot(xc, wi_ref[d]) - LOG2E * bi_ref[d:d + 1, :]))
        neg_rate = LRU_C * _softplus(-lam_ref[d:d + 1, :])
        a = jnp.exp2(r * (-LOG2E * neg_rate))
        b = jnp.sqrt(jnp.tanh(r * neg_rate) * (a * a + 1.0)) * (i * xc)
        return a, b

    def fwd_body(t, carry):
        r0 = pl.multiple_of(t * rows, rows)
        xc = _conv_tile(xpad_ref, r0, rows, cw) + cb
        xc_ref[pl.ds(r0, rows), :] = xc
        a, b = gates(xc, 0)
        h, carry = _scan_tile(a, b, carry, False, rows)
        hf_ref[pl.ds(r0, rows), :] = h
        return carry

    carry_f = lax.fori_loop(0, nt, fwd_body, h0_ref[0:1, :])

    def bwd_body(t, carry):
        r0 = pl.multiple_of((nt - 1 - t) * rows, rows)
        xc = xc_ref[pl.ds(r0, rows), :]
        a, b = gates(xc, 1)
        h, carry = _scan_tile(a, b, carry, True, rows)
        y = y_ref[pl.ds(r0, rows), :].astype(F32)
        o_ref[pl.ds(r0, rows), :] = ((hf_ref[pl.ds(r0, rows), :] + h) * _gelu_tanh(y)).astype(o_ref.dtype)
        return carry

    carry_b = lax.fori_loop(0, nt, bwd_body, h0_ref[1:2, :])
    ht_ref[0:1, :] = carry_f
    ht_ref[1:2, :] = carry_b


def _lru_call(proj, cw, cb, wa, wi, ba, bi, lam, h0, rows):
    bsz, seq, _ = proj.shape
    width = cw.shape[1]
    ncol = width // LANES
    vec2 = pl.BlockSpec((2, LANES), lambda b, j: (0, j))
    wspec = pl.BlockSpec((2, None, LANES, LANES), lambda b, j: (0, j, 0, 0))
    return pl.pallas_call(
        functools.partial(_lru_kernel, seq=seq, rows=rows),
        grid=(bsz, ncol),
        in_specs=[pl.BlockSpec((None, seq, LANES), lambda b, j: (b, 0, j)),
                  pl.BlockSpec((None, seq, LANES), lambda b, j: (b, 0, ncol + j)),
                  pl.BlockSpec((CONV_WIDTH, LANES), lambda b, j: (0, j)),
                  pl.BlockSpec((1, LANES), lambda b, j: (0, j)),
                  wspec, wspec, vec2, vec2, vec2,
                  pl.BlockSpec((None, 2, LANES), lambda b, j: (b, 0, j))],
        out_specs=[pl.BlockSpec((None, seq, LANES), lambda b, j: (b, 0, j)),
                   pl.BlockSpec((None, 2, LANES), lambda b, j: (b, 0, j))],
        out_shape=[jax.ShapeDtypeStruct((bsz, seq, width), BF16),
                   jax.ShapeDtypeStruct((bsz, 2, width), F32)],
        scratch_shapes=[pltpu.VMEM((seq + 2 * CONV_PAD, LANES), F32),
                        pltpu.VMEM((seq, LANES), F32),
                        pltpu.VMEM((seq, LANES), F32)],
        compiler_params=_cparams(("parallel", "parallel")),
        name="lru",
    )(proj, proj, cw, cb.reshape(1, width), wa, wi, ba, bi, lam, h0)


DN_GROUP = 2 * DN_CHUNK


def _split3(x):
    x1 = x.astype(BF16)
    r1 = x - x1.astype(F32)
    x2 = r1.astype(BF16)
    x3 = (r1 - x2.astype(F32)).astype(BF16)
    return x1, x2, x3


def _delta_kernel(q_ref, k_ref, v_ref, gr_ref, grt_ref, cvec_ref, ccol_ref, cum_ref, cumt_ref,
                  ones_ref, s0_ref, o_ref, st_ref, s_ref, *, rev, rows):
    step = pl.program_id(1)
    d = 1 if rev else 0

    @pl.when(step == 0)
    def _():
        s_ref[...] = s0_ref[...]

    raw = gr_ref[...]
    sig = _sigmoid(raw)
    g = -jnp.exp(cvec_ref[0:1, :]) * _softplus(raw + cvec_ref[1:2, :])
    g1, g2, g3 = _split3(g)
    cum = cum_ref[...]
    ones = ones_ref[...]

    def mm(m, parts):
        return sum(jnp.dot(m, p, preferred_element_type=F32) for p in parts)

    gc = mm(cum, (g1, g2, g3))
    gl = mm(ones, (g1, g2, g3))
    e_gc = jnp.exp(gc)
    e_rest = jnp.exp(gl - gc)
    e_gl = jnp.exp(gl)
    gt = -jnp.exp(ccol_ref[0]) * _softplus(grt_ref[...] + ccol_ref[1])
    t1, t2, t3 = _split3(gt)
    cumt = cumt_ref[...]
    gct = sum(jnp.dot(p, cumt, preferred_element_type=F32) for p in (t1, t2, t3))

    ri = lax.broadcasted_iota(jnp.int32, (DN_GROUP, DN_GROUP), 0)
    ci = lax.broadcasted_iota(jnp.int32, (DN_GROUP, DN_GROUP), 1)
    same = (ri // DN_CHUNK) == (ci // DN_CHUNK)
    if rev:
        incl = same & (ri <= ci)
        strict = same & (ri < ci)
    else:
        incl = same & (ri >= ci)
        strict = same & (ri > ci)

    ngroups = rows // DN_GROUP
    zeros_c = jnp.zeros((DN_CHUNK, DN_DK), F32)

    def bcast(x, rsl, lane):
        return jnp.broadcast_to(x[rsl, lane:lane + 1], (DN_GROUP, LANES))

    heads = range(DN_HEADS)
    groups = list(range(ngroups - 1, -1, -1) if rev else range(ngroups))
    units = [(gi, h) for gi in groups for h in heads]
    qd, kd, egl_b, rhs, kk, qk = {}, {}, {}, {}, {}, {}
    for u in units:
        gi, h = u
        rsl = slice(gi * DN_GROUP, (gi + 1) * DN_GROUP)
        lb = d * DN_HEADS + h
        lg = 2 * DN_HEADS + d * DN_HEADS + h
        csl = slice(h * DN_DK, (h + 1) * DN_DK)
        q16 = q_ref[rsl, csl]
        k16 = k_ref[rsl, csl]
        q_h = q16.astype(F32)
        k_h = k16.astype(F32)
        beta_b = bcast(sig, rsl, lb)
        egc_b = bcast(e_gc, rsl, lg)
        kb_h = k_h * beta_b
        kk[u] = _bdot_nt(kb_h, k16)
        qk[u] = _bdot_nt(q16, k16)
        rhs[u] = jnp.concatenate([v_ref[rsl, csl].astype(F32) * beta_b, kb_h * egc_b], axis=1)
        qd[u] = q_h * egc_b
        kd[u] = k_h * bcast(e_rest, rsl, lg)
        egl_b[u] = bcast(e_gl, rsl, lg)
    p, a16, attn = {}, {}, {}
    for u in units:
        gi, h = u
        rsl = slice(gi * DN_GROUP, (gi + 1) * DN_GROUP)
        lg = 2 * DN_HEADS + d * DN_HEADS + h
        diff = bcast(gc, rsl, lg) - gct[lg:lg + 1, rsl]
        decay = jnp.exp(jnp.where(incl, diff, NEG_BIG))
        m = jnp.where(strict, kk[u] * decay, 0.0)
        attn[u] = qk[u] * decay
        p[u] = -m
        a16[u] = m.astype(BF16)
    npow = 1
    while npow < DN_CHUNK // 2:
        a = {u: jnp.dot(a16[u], a16[u], preferred_element_type=F32) for u in units}
        a16 = {u: a[u].astype(BF16) for u in units}
        p = {u: p[u] + a[u] + jnp.dot(p[u].astype(BF16), a16[u], preferred_element_type=F32) for u in units}
        npow *= 2
    uw = {u: rhs[u] + _bdot(p[u], rhs[u]) for u in units}
    s = [s_ref[h] for h in heads]
    for gi in groups:
        for cc in ((1, 0) if rev else (0, 1)):
            c0 = cc * DN_CHUNK
            cs = slice(c0, c0 + DN_CHUNK)
            r2 = [_bdot(jnp.concatenate([uw[gi, h][cs, DN_DK:], qd[gi, h][cs]], axis=0), s[h]) for h in heads]
            v_new = [uw[gi, h][cs, :DN_DK] - r2[h][:DN_CHUNK] for h in heads]
            for h in heads:
                v_pad = (jnp.concatenate([v_new[h], zeros_c], axis=0) if cc == 0
                         else jnp.concatenate([zeros_c, v_new[h]], axis=0))
                o_c = r2[h][DN_CHUNK:] + _bdot(attn[gi, h][cs], v_pad)
                o_ref[gi * DN_GROUP + c0:gi * DN_GROUP + c0 + DN_CHUNK,
                      h * DN_DK:(h + 1) * DN_DK] = o_c.astype(o_ref.dtype)
            for h in heads:
                egl = jnp.concatenate([egl_b[gi, h][cs], egl_b[gi, h][cs]], axis=0)
                s[h] = s[h] * egl + _bdot_tn(kd[gi, h][cs], v_new[h])
    for h in heads:
        s_ref[h] = s[h]

    @pl.when(step == pl.num_programs(1) - 1)
    def _():
        st_ref[...] = s_ref[...]


def _delta_consts(rows, rev):
    t = jnp.arange(rows)
    same = (t[:, None] // DN_CHUNK) == (t[None, :] // DN_CHUNK)
    cum = same & ((t[:, None] <= t[None, :]) if rev else (t[:, None] >= t[None, :]))
    return cum.astype(BF16), cum.T.astype(BF16), same.astype(BF16)


def _delta_call(qkv, gates, gates_t, cvec, ccol, s0, rev, rows):
    bsz, seq, _ = qkv.shape
    width = DN_HEADS * DN_DK
    nsteps = seq // rows
    cum, cumt, ones = _delta_consts(rows, rev)

    def blk(i):
        return nsteps - 1 - i if rev else i

    full = lambda shape: pl.BlockSpec(shape, lambda b, i: (0,) * len(shape))
    state = pl.BlockSpec((None, DN_HEADS, DN_DK, DN_DK), lambda b, i: (b, 0, 0, 0))
    return pl.pallas_call(
        functools.partial(_delta_kernel, rev=rev, rows=rows),
        grid=(bsz, nsteps),
        in_specs=[pl.BlockSpec((None, rows, width), lambda b, i: (b, blk(i), 0)),
                  pl.BlockSpec((None, rows, width), lambda b, i: (b, blk(i), 1)),
                  pl.BlockSpec((None, rows, width), lambda b, i: (b, blk(i), 2)),
                  pl.BlockSpec((None, rows, LANES), lambda b, i: (b, blk(i), 0)),
                  pl.BlockSpec((None, 4 * DN_HEADS, rows), lambda b, i: (b, 0, blk(i))),
                  full((SUBLANES, LANES)),
                  full((2, 4 * DN_HEADS, rows)),
                  full((rows, rows)), full((rows, rows)), full((rows, rows)),
                  state],
        out_specs=[pl.BlockSpec((None, rows, width), lambda b, i: (b, blk(i), 0)), state],
        out_shape=[jax.ShapeDtypeStruct((bsz, seq, width), BF16),
                   jax.ShapeDtypeStruct((bsz, DN_HEADS, DN_DK, DN_DK), F32)],
        scratch_shapes=[pltpu.VMEM((DN_HEADS, DN_DK, DN_DK), F32)],
        compiler_params=_cparams(("parallel", "arbitrary")),
        name="delta_bwd" if rev else "delta_fwd",
    )(qkv, qkv, qkv, gates, gates_t, cvec, ccol, cum, cumt, ones, s0)


def _evenout_kernel(lru_ref, of_ref, ob_ref, z_ref, g_ref, w_ref, res_ref, gate_ref, o_ref, a_ref):
    width = lru_ref.shape[1]
    a_ref[:, 0:width] = lru_ref[...].astype(BF16)
    for h in range(DN_HEADS):
        csl = slice(h * DN_DK, (h + 1) * DN_DK)
        o = of_ref[:, csl].astype(F32) + ob_ref[:, csl].astype(F32)
        z = z_ref[:, csl].astype(F32)
        y = (_rms(o) * g_ref[:, csl]) * (z * _sigmoid(z))
        a_ref[:, width + h * DN_DK:width + (h + 1) * DN_DK] = y.astype(BF16)
    o_ref[...] = res_ref[...] + gate_ref[...] * jnp.dot(a_ref[...], w_ref[...], preferred_element_type=F32)


def _evenout_call(lru, o_f, o_b, proj, z_block, dn_g, w_out, res, gate, tm):
    bsz, seq, width = lru.shape
    dnw = o_f.shape[2]
    d = res.shape[2]
    row = lambda c: pl.BlockSpec((None, tm, c), lambda b, i: (b, i, 0))
    return pl.pallas_call(
        _evenout_kernel,
        grid=(bsz, seq // tm),
        in_specs=[row(width), row(dnw), row(dnw),
                  pl.BlockSpec((None, tm, dnw), lambda b, i: (b, i, z_block)),
                  pl.BlockSpec((1, dnw), lambda b, i: (0, 0)),
                  pl.BlockSpec((width + dnw, d), lambda b, i: (0, 0)),
                  row(d),
                  pl.BlockSpec((None, 1, d), lambda b, i: (b, 0, 0))],
        out_specs=row(d),
        out_shape=jax.ShapeDtypeStruct((bsz, seq, d), F32),
        scratch_shapes=[pltpu.VMEM((tm, width + dnw), BF16)],
        compiler_params=_cparams(("parallel", "parallel")),
        name="evenout",
    )(lru, o_f, o_b, proj, dn_g, w_out, res, gate)


def _linres_kernel(a_ref, w_ref, res_ref, gate_ref, o_ref):
    o_ref[...] = res_ref[...] + gate_ref[...] * jnp.dot(a_ref[...].astype(BF16), w_ref[...],
                                                        preferred_element_type=F32)


def _linres_call(a, w, res, gate, tm):
    bsz, seq, kdim = a.shape
    d = res.shape[2]
    return pl.pallas_call(
        _linres_kernel,
        grid=(bsz, seq // tm),
        in_specs=[pl.BlockSpec((None, tm, kdim), lambda b, i: (b, i, 0)),
                  pl.BlockSpec((kdim, d), lambda b, i: (0, 0)),
                  pl.BlockSpec((None, tm, d), lambda b, i: (b, i, 0)),
                  pl.BlockSpec((None, 1, d), lambda b, i: (b, 0, 0))],
        out_specs=pl.BlockSpec((None, tm, d), lambda b, i: (b, i, 0)),
        out_shape=jax.ShapeDtypeStruct((bsz, seq, d), F32),
        compiler_params=_cparams(("parallel", "parallel")),
        name="linres",
    )(a, w, res, gate)


def _aprep_kernel(*refs, rope, want_q):
    if rope:
        x_ref, qg_ref, kg_ref, cos_ref, sin_ref = refs[:5]
        outs = refs[5:]
    else:
        x_ref, qg_ref, kg_ref = refs[:3]
        outs = refs[3:]
    if want_q:
        q_ref, k_ref, v_ref = outs
    else:
        k_ref, v_ref = outs

    if rope:
        cos = cos_ref[...]
        sin = sin_ref[...]

    def norm_rope(x, g):
        y = _rms(x) * g
        if rope:
            y = y * cos + pltpu.roll(y, HEAD_DIM // 2, 1) * sin
        return y

    if want_q:
        for h in range(N_Q_HEADS):
            x = x_ref[:, h * HEAD_DIM:(h + 1) * HEAD_DIM].astype(F32)
            q_ref[h] = (norm_rope(x, qg_ref[...]) * (HEAD_DIM ** -0.5 * LOG2E)).astype(BF16)
    for h in range(N_KV_HEADS):
        c0 = (N_Q_HEADS + h) * HEAD_DIM
        k_ref[h] = norm_rope(x_ref[:, c0:c0 + HEAD_DIM].astype(F32), kg_ref[...]).astype(BF16)
        c1 = (N_Q_HEADS + N_KV_HEADS + h) * HEAD_DIM
        v_ref[h, 0:HEAD_DIM, :] = x_ref[:, c1:c1 + HEAD_DIM].astype(F32).T.astype(BF16)
        pad_row = lax.broadcasted_iota(jnp.int32, (VT_PAD, x_ref.shape[0]), 0)
        v_ref[h, HEAD_DIM:HEAD_DIM + VT_PAD, :] = jnp.where(pad_row == 0, 1.0, 0.0).astype(BF16)


def _aprep_call(qkv, qg, kg, rope_tabs, want_q, tm):
    bsz, seq, cols = qkv.shape
    rope = rope_tabs is not None
    in_specs = [pl.BlockSpec((None, tm, cols), lambda b, i: (b, i, 0)),
                pl.BlockSpec((1, HEAD_DIM), lambda b, i: (0, 0)),
                pl.BlockSpec((1, HEAD_DIM), lambda b, i: (0, 0))]
    args = [qkv, qg.reshape(1, HEAD_DIM), kg.reshape(1, HEAD_DIM)]
    if rope:
        in_specs += [pl.BlockSpec((tm, HEAD_DIM), lambda b, i: (i, 0))] * 2
        args += list(rope_tabs)
    hspec = lambda n: pl.BlockSpec((None, n, tm, HEAD_DIM), lambda b, i: (b, 0, i, 0))
    hshape = lambda n: jax.ShapeDtypeStruct((bsz, n, seq, HEAD_DIM), BF16)
    tspec = lambda n: pl.BlockSpec((None, n, HEAD_DIM + VT_PAD, tm), lambda b, i: (b, 0, 0, i))
    tshape = lambda n: jax.ShapeDtypeStruct((bsz, n, HEAD_DIM + VT_PAD, seq), BF16)
    out_specs = [hspec(N_KV_HEADS), tspec(N_KV_HEADS)]
    out_shape = [hshape(N_KV_HEADS), tshape(N_KV_HEADS)]
    if want_q:
        out_specs = [hspec(N_Q_HEADS)] + out_specs
        out_shape = [hshape(N_Q_HEADS)] + out_shape
    return pl.pallas_call(
        functools.partial(_aprep_kernel, rope=rope, want_q=want_q),
        grid=(bsz, seq // tm),
        in_specs=in_specs,
        out_specs=out_specs,
        out_shape=out_shape,
        compiler_params=_cparams(("parallel", "parallel")),
        name="aprep_x" if rope else "aprep_ctx",
    )(*args)


def _rope_tables(n_tokens):
    rows = n_tokens // GRID_W
    axis = HEAD_DIM // 2
    row_id = jnp.repeat(jnp.arange(rows), GRID_W).astype(F32)
    col_id = jnp.tile(jnp.arange(GRID_W), rows).astype(F32)
    inv_freq = ROPE_THETA ** (-jnp.arange(0, axis, 2, dtype=F32) / axis)
    ang_r = row_id[:, None] * inv_freq[None]
    ang_c = col_id[:, None] * inv_freq[None]
    cr, sr, cc, sc = jnp.cos(ang_r), jnp.sin(ang_r), jnp.cos(ang_c), jnp.sin(ang_c)
    cos = jnp.concatenate([cr, cc, cr, cc], axis=-1)
    sin = jnp.concatenate([-sr, -sc, sr, sc], axis=-1)
    return cos, sin


def _rope_perm():
    quarter = HEAD_DIM // 4
    return jnp.concatenate([jnp.arange(0, quarter), jnp.arange(2 * quarter, 3 * quarter),
                            jnp.arange(quarter, 2 * quarter), jnp.arange(3 * quarter, 4 * quarter)])


def _flash_kernel(q_ref, k_ref, vt_ref, o_ref, s0_ref, sa_ref, sb_ref, m0_ref, ma_ref, mb_ref, *,
                  tq, tk, nk, nt):
    m_rows = Q_PER_KV * tq

    def q_tile(i):
        r0 = pl.multiple_of(i * tq, tq)
        return q_ref[:, pl.ds(r0, tq), :].reshape(m_rows, HEAD_DIM)

    def scores(q, j, s_ref, cmax_ref):
        c0 = pl.multiple_of(j * tk, tk)
        st = lax.dot_general(k_ref[pl.ds(c0, tk), :], q, (((1,), (1,)), ((), ())),
                             preferred_element_type=F32)
        s_ref[...] = st
        cmax_ref[...] = jnp.max(st, axis=0, keepdims=True)

    def update(j, s_ref, cmax_ref, carry):
        m, acc = carry
        m_new = jnp.maximum(m, cmax_ref[...])
        alpha = jnp.exp2(m - m_new)
        p = jnp.exp2(s_ref[...] - m_new)
        vt = vt_ref[:, pl.ds(pl.multiple_of(j * tk, tk), tk)]
        acc = alpha * acc + jnp.dot(vt, p.astype(BF16), preferred_element_type=F32)
        return m_new, acc

    bufs = ((sa_ref, ma_ref), (sb_ref, mb_ref))

    def tile_body(i, _):
        q = q_tile(i)
        q_next = q_tile(jnp.minimum(i + 1, nt - 1))

        def run(j0, count, carry, last):
            for u in range(count):
                j = j0 + u
                if u + 1 < count or not last:
                    scores(q, j + 1, *bufs[(u + 1) % 2])
                else:
                    scores(q_next, 0, s0_ref, m0_ref)
                carry = update(j, *bufs[u % 2], carry)
            return carry

        carry = (jnp.full((1, m_rows), NEG_BIG, F32), jnp.zeros((HEAD_DIM + VT_PAD, m_rows), F32))
        scores(q, 1, sa_ref, ma_ref)
        carry = update(0, s0_ref, m0_ref, carry)
        rest = nk - 1
        niter = (rest - 1) // FLASH_UNROLL
        carry = lax.fori_loop(0, niter, lambda jj, c: run(1 + jj * FLASH_UNROLL, FLASH_UNROLL, c, False),
                              carry)
        _, acc = run(1 + niter * FLASH_UNROLL, rest - niter * FLASH_UNROLL, carry, True)
        o = (acc[:HEAD_DIM] / acc[HEAD_DIM:HEAD_DIM + 1]).T
        r0 = pl.multiple_of(i * tq, tq)
        for r in range(Q_PER_KV):
            o_ref[pl.ds(r0, tq), r * HEAD_DIM:(r + 1) * HEAD_DIM] = o[r * tq:(r + 1) * tq].astype(o_ref.dtype)
        return 0

    scores(q_tile(0), 0, s0_ref, m0_ref)
    lax.fori_loop(0, nt, tile_body, 0)


def _flash_call(q, k, vt, tq, tk):
    bsz, _, seq, _ = q.shape
    lk = k.shape[2]
    nk = lk // tk
    assert nk >= 2, "the score pipeline needs at least two key blocks"
    q5 = q.reshape(bsz, N_KV_HEADS, Q_PER_KV, seq, HEAD_DIM)
    return pl.pallas_call(
        functools.partial(_flash_kernel, tq=tq, tk=tk, nk=nk, nt=seq // tq),
        grid=(bsz, N_KV_HEADS),
        in_specs=[pl.BlockSpec((None, None, Q_PER_KV, seq, HEAD_DIM), lambda b, g: (b, g, 0, 0, 0)),
                  pl.BlockSpec((None, None, lk, HEAD_DIM), lambda b, g: (b, g, 0, 0)),
                  pl.BlockSpec((None, None, HEAD_DIM + VT_PAD, lk), lambda b, g: (b, g, 0, 0))],
        out_specs=pl.BlockSpec((None, seq, Q_PER_KV * HEAD_DIM), lambda b, g: (b, 0, g)),
        out_shape=jax.ShapeDtypeStruct((bsz, seq, N_Q_HEADS * HEAD_DIM), BF16),
        scratch_shapes=([pltpu.VMEM((tk, Q_PER_KV * tq), F32)] * 3
                        + [pltpu.VMEM((1, Q_PER_KV * tq), F32)] * 3),
        compiler_params=_cparams(("parallel", "parallel")),
        name="flash",
    )(q5, k, vt)


def _row_tile(seq, cap):
    t = min(seq, cap)
    while seq % t:
        t //= 2
    return t


def _even_weights(w_in, lru_wa, lru_wi, dn_alog, dn_dtb, dn_g):
    d, cols = w_in.shape
    ngate = 4 * DN_HEADS
    main = cols - ngate
    w_gates = jnp.concatenate([w_in[:, main:], jnp.zeros((d, LANES - ngate), F32)], axis=1)

    def blockdiag(w):
        w = w * (-LOG2E)
        z = jnp.zeros_like(w[:, 0::2])
        top = jnp.concatenate([w[:, 0::2], z], axis=-1)
        bot = jnp.concatenate([z, w[:, 1::2]], axis=-1)
        return jnp.concatenate([top, bot], axis=-2).astype(BF16)

    alog = dn_alog.reshape(-1)
    dtb = dn_dtb.reshape(-1)
    cvec = jnp.zeros((SUBLANES, LANES), F32)
    cvec = cvec.at[0, 2 * DN_HEADS:ngate].set(alog).at[1, 2 * DN_HEADS:ngate].set(dtb)

    ccol = jnp.zeros((2, ngate), F32).at[0, 2 * DN_HEADS:].set(alog).at[1, 2 * DN_HEADS:].set(dtb)

    return dict(w_in=w_in[:, :main].astype(BF16), w_gates=w_gates.astype(BF16),
                w_gates_t=w_in[:, main:].T.astype(BF16),
                wa=blockdiag(lru_wa), wi=blockdiag(lru_wi), cvec=cvec, ccol=ccol,
                dn_g=jnp.tile(dn_g, DN_HEADS).reshape(1, -1))


def _even_stream(h_in, res, gate, norm_g, sc, sh, ew, lru_cw, lru_cb, lru_ba, lru_bi, lru_lam, dn_cw,
                 w_out, h0, s0, tm_in, tm, rows, dn_rows):
    width = lru_cw.shape[1]
    ngate = 4 * DN_HEADS
    proj, gates, gates_t = _normlin_call(h_in, norm_g, sc, sh, ew["w_in"], ew["w_gates"], ew["w_gates_t"],
                                         tm_in)
    lru, h_t = _lru_call(proj, lru_cw, lru_cb, ew["wa"], ew["wi"], lru_ba, lru_bi, lru_lam, h0,
                         _row_tile(proj.shape[1], LRU_ROWS))
    qkv = _dnprep_call(proj, dn_cw, 2 * width // LANES, _row_tile(proj.shape[1], DNPREP_ROWS))
    ccol = jnp.broadcast_to(ew["ccol"][:, :, None], (2, ngate, dn_rows))
    outs, states = [], []
    for rev in (False, True):
        o_d, s_d = _delta_call(qkv, gates, gates_t, ew["cvec"], ccol,
                               s0[:, :, 1 if rev else 0], rev, dn_rows)
        outs.append(o_d)
        states.append(s_d)
    z_block = (2 * width + dn_cw.shape[1]) // (DN_HEADS * DN_DK)
    out = _evenout_call(lru, outs[0], outs[1], proj, z_block, ew["dn_g"], w_out, res, gate, tm)
    return out, h_t, jnp.stack(states, axis=2)


def kernel(x, c, ctx, c_ctx, norm_mix_g, norm_mlp_g, w_mod, b_mod, mlp_w1, mlp_w2, even_w_in, lru_conv_w,
           lru_conv_b, lru_w_a, lru_b_a, lru_w_i, lru_b_i, lru_lambda, dn_conv_w, dn_a_log, dn_dt_bias,
           dn_norm_g, even_w_out, attn_w_qkv, attn_q_norm_g, attn_k_norm_g, attn_w_o, final_norm_g):
    bsz, seq, d = x.shape
    lc = ctx.shape[1]
    depth = w_mod.shape[0]
    tm_x = _row_tile(seq, 512)
    tm_c = _row_tile(lc, 512)
    tm_mlp = _row_tile(seq, 1024)
    rows_x = _row_tile(seq, 256)
    rows_c = _row_tile(lc, 256)

    cc = jnp.zeros((SUBLANES, d), F32).at[:bsz].set(c).at[bsz].set(c_ctx)
    mod = _mod_call(cc, w_mod, b_mod).reshape(depth, SUBLANES, 6, d)

    rope = _rope_tables(seq)
    xs, cs = x, ctx
    for l in range(depth):
        last = l == depth - 1
        j = l // 2
        mx = [mod[l, :bsz, k][:, None, :] for k in range(6)]
        mc = [jnp.broadcast_to(mod[l, bsz, k][None, None, :], (bsz, 1, d)) for k in range(6)]
        sh1, sc1, g1, sh2, sc2, g2 = mx
        csh1, csc1, cg1, csh2, csc2, cg2 = mc
        w1 = mlp_w1[l].astype(BF16)
        w2 = mlp_w2[l].astype(BF16)
        if l % 2 == 0:
            ew = _even_weights(even_w_in[j], lru_w_a[j], lru_w_i[j], dn_a_log[j], dn_dt_bias[j], dn_norm_g[j])
            w_out = even_w_out[j].astype(BF16)
            h0 = jnp.zeros((bsz, 2, lru_conv_w.shape[2]), F32)
            s0 = jnp.zeros((bsz, DN_HEADS, 2, DN_DK, DN_DK), F32)
            common = (ew, lru_conv_w[j], lru_conv_b[j], lru_b_a[j], lru_b_i[j], lru_lambda[j], dn_conv_w[j], w_out)
            cs_mix, h_t, s_t = _even_stream(cs, cs, cg1, norm_mix_g[l], csc1, csh1, *common, h0, s0,
                                            tm_c, tm_c, rows_c, rows_c)
            xs, _, _ = _even_stream(xs, xs, g1, norm_mix_g[l], sc1, sh1, *common, h_t, s_t,
                                    tm_mlp, tm_mlp, rows_x, _row_tile(seq, DELTA_ROWS))
        else:
            perm = _rope_perm()
            nqk = (N_Q_HEADS + N_KV_HEADS) * HEAD_DIM
            w_qk = attn_w_qkv[j][:, :nqk].reshape(d, N_Q_HEADS + N_KV_HEADS, HEAD_DIM)[:, :, perm]
            w_qkv = jnp.concatenate([w_qk.reshape(d, nqk), attn_w_qkv[j][:, nqk:]], axis=1).astype(BF16)
            q_g = attn_q_norm_g[j][perm]
            k_g = attn_k_norm_g[j][perm]
            w_o = attn_w_o[j].astype(BF16)
            qkv_c = _normlin_call(cs, norm_mix_g[l], csc1, csh1, w_qkv, None, None, tm_c)
            qkv_x = _normlin_call(xs, norm_mix_g[l], sc1, sh1, w_qkv, None, None, tm_mlp)
            if last:
                k_c, vt_c = _aprep_call(qkv_c, q_g, k_g,None, False, rows_c)
            else:
                qt_c, k_c, vt_c = _aprep_call(qkv_c, q_g, k_g,None, True, rows_c)
            qt_x, k_x, vt_x = _aprep_call(qkv_x, q_g, k_g,rope, True, tm_x)
            k_all = jnp.concatenate([k_c, k_x], axis=2)
            vt_all = jnp.concatenate([vt_c, vt_x], axis=3)
            o_x = _flash_call(qt_x, k_all, vt_all, rows_x, _col_tile(lc + seq, FLASH_TK_CAP))
            xs = _linres_call(o_x, w_o, xs, g1, tm_mlp)
            if not last:
                o_c = _flash_call(qt_c, k_c, vt_c, rows_c, _col_tile(lc, min(FLASH_TK_CAP, lc // 2)))
                cs_mix = _linres_call(o_c, w_o, cs, cg1, tm_c)
        xs = _mlp_call(xs, norm_mlp_g[l], sc2, sh2, g2, w1, w2, final_norm_g, tm_mlp, last)
        if not last:
            cs = _mlp_call(cs_mix, norm_mlp_g[l], csc2, csh2, cg2, w1, w2, final_norm_g, tm_c, False)
    return xs
```

```python
import functools

import jax
import jax.numpy as jnp
from jax import lax
from jax.experimental import pallas as pl
from jax.experimental.pallas import tpu as pltpu

F32 = jnp.float32
BF16 = jnp.bfloat16

EPS = 1e-6
LANES = 128
SUBLANES = 8
VMEM_LIMIT = 56 * 1024 * 1024

LRU_BLOCK = 64
LRU_C = 8.0
CONV_WIDTH = 4
CONV_LEFT = 2
DN_HEADS = 8
DN_DK = 128
DN_CHUNK = 64
N_Q_HEADS = 8
N_KV_HEADS = 4
Q_PER_KV = N_Q_HEADS // N_KV_HEADS
HEAD_DIM = 128
GRID_W = 64
ROPE_THETA = 10000.0
NEG_BIG = -1e30
LOG2E = 1.4426950408889634
DNPREP_ROWS = 1024
LRU_ROWS = 1024
DELTA_ROWS = 256
FLASH_TQ = 256
FLASH_TK_CAP = 768
VT_PAD = 16
FLASH_UNROLL = 4


def _cparams(sem):
    return pltpu.CompilerParams(dimension_semantics=sem, vmem_limit_bytes=VMEM_LIMIT)


def _bdot(a, b):
    return jnp.dot(a.astype(BF16), b.astype(BF16), preferred_element_type=F32)


def _bdot_nt(a, b):
    return lax.dot_general(a.astype(BF16), b.astype(BF16), (((1,), (1,)), ((), ())),
                           preferred_element_type=F32)


def _bdot_tn(a, b):
    return lax.dot_general(a.astype(BF16), b.astype(BF16), (((0,), (0,)), ((), ())),
                           preferred_element_type=F32)


def _sigmoid(x):
    return jax.nn.sigmoid(x)


def _softplus(x):
    return jnp.maximum(x, 0.0) + jnp.log1p(jnp.exp(-jnp.abs(x)))


def _rms(x):
    return x * lax.rsqrt(jnp.mean(x * x, axis=-1, keepdims=True) + EPS)


def _col_tile(n, cap):
    best = LANES
    t = LANES
    while t <= min(n, cap):
        if n % t == 0:
            best = t
        t += LANES
    return best


def _mod_kernel(c_ref, w_ref, b_ref, o_ref):
    s = c_ref[...]
    s = s * _sigmoid(s)
    o_ref[...] = _bdot(s, w_ref[...]) + b_ref[...]


def _mod_call(cc, w_mod, b_mod):
    depth, d, n = w_mod.shape
    tn = _col_tile(n, 1536)
    return pl.pallas_call(
        _mod_kernel,
        grid=(depth, n // tn),
        in_specs=[pl.BlockSpec((SUBLANES, d), lambda l, j: (0, 0)),
                  pl.BlockSpec((None, d, tn), lambda l, j: (l, 0, j)),
                  pl.BlockSpec((None, 1, tn), lambda l, j: (l, 0, j))],
        out_specs=pl.BlockSpec((None, SUBLANES, tn), lambda l, j: (l, 0, j)),
        out_shape=jax.ShapeDtypeStruct((depth, SUBLANES, n), F32),
        compiler_params=_cparams(("parallel", "parallel")),
        name="mod",
    )(cc, w_mod, b_mod.reshape(depth, 1, n))


def _normlin_kernel(*refs, has_gates):
    if has_gates:
        x_ref, g_ref, sc_ref, sh_ref, w_ref, wg_ref, wgt_ref, o_ref, og_ref, ogt_ref, h_ref = refs
    else:
        x_ref, g_ref, sc_ref, sh_ref, w_ref, o_ref, h_ref = refs

    @pl.when(pl.program_id(2) == 0)
    def _():
        y = _rms(x_ref[...]) * g_ref[...]
        h_ref[...] = (y * (1.0 + sc_ref[...]) + sh_ref[...]).astype(BF16)
        if has_gates:
            og_ref[...] = jnp.dot(h_ref[...], wg_ref[...], preferred_element_type=F32)
            ogt_ref[...] = lax.dot_general(wgt_ref[...], h_ref[...], (((1,), (1,)), ((), ())),
                                           preferred_element_type=F32)

    o_ref[...] = jnp.dot(h_ref[...], w_ref[...], preferred_element_type=F32).astype(o_ref.dtype)


def _normlin_call(x, g, sc, sh, w, w_gates, w_gates_t, tm):
    bsz, seq, d = x.shape
    n = w.shape[1]
    tn = _col_tile(n, 2048)
    has_gates = w_gates is not None
    in_specs = [pl.BlockSpec((None, tm, d), lambda b, i, j: (b, i, 0)),
                pl.BlockSpec((1, d), lambda b, i, j: (0, 0)),
                pl.BlockSpec((None, 1, d), lambda b, i, j: (b, 0, 0)),
                pl.BlockSpec((None, 1, d), lambda b, i, j: (b, 0, 0)),
                pl.BlockSpec((d, tn), lambda b, i, j: (0, j))]
    out_specs = [pl.BlockSpec((None, tm, tn), lambda b, i, j: (b, i, j))]
    out_shape = [jax.ShapeDtypeStruct((bsz, seq, n), BF16)]
    args = [x, g.reshape(1, d), sc, sh, w]
    if has_gates:
        ng = w_gates.shape[1]
        in_specs.append(pl.BlockSpec((d, ng), lambda b, i, j: (0, 0)))
        out_specs.append(pl.BlockSpec((None, tm, ng), lambda b, i, j: (b, i, 0)))
        out_shape.append(jax.ShapeDtypeStruct((bsz, seq, ng), F32))
        ngt = w_gates_t.shape[0]
        in_specs.append(pl.BlockSpec((ngt, d), lambda b, i, j: (0, 0)))
        out_specs.append(pl.BlockSpec((None, ngt, tm), lambda b, i, j: (b, 0, i)))
        out_shape.append(jax.ShapeDtypeStruct((bsz, ngt, seq), F32))
        args += [w_gates, w_gates_t]
    out = pl.pallas_call(
        functools.partial(_normlin_kernel, has_gates=has_gates),
        grid=(bsz, seq // tm, n // tn),
        in_specs=in_specs,
        out_specs=out_specs,
        out_shape=out_shape,
        scratch_shapes=[pltpu.VMEM((tm, d), BF16)],
        compiler_params=_cparams(("parallel", "parallel", "arbitrary")),
        name="normlin",
    )(*args)
    return out if has_gates else out[0]


def _mlp_kernel(x_ref, g_ref, sc_ref, sh_ref, gate_ref, w1_ref, w2_ref, fg_ref, o_ref,
                h_ref, acc_ref, *, final_norm):
    k = pl.program_id(2)

    @pl.when(k == 0)
    def _():
        y = _rms(x_ref[...]) * g_ref[...]
        h_ref[...] = (y * (1.0 + sc_ref[...]) + sh_ref[...]).astype(BF16)
        acc_ref[...] = jnp.zeros_like(acc_ref)

    a = jnp.dot(h_ref[...], w1_ref[...], preferred_element_type=F32)
    a = jnp.square(jnp.maximum(a, 0.0)).astype(BF16)
    acc_ref[...] += jnp.dot(a, w2_ref[...], preferred_element_type=F32)

    @pl.when(k == pl.num_programs(2) - 1)
    def _():
        y = x_ref[...] + gate_ref[...] * acc_ref[...]
        if final_norm:
            y = _rms(y) * fg_ref[...]
        o_ref[...] = y


def _mlp_call(x, g, sc, sh, gate, w1, w2, fg, tm, final_norm):
    bsz, seq, d = x.shape
    f = w1.shape[1]
    tf = _col_tile(f, 2048)
    vec = pl.BlockSpec((None, 1, d), lambda b, i, k: (b, 0, 0))
    return pl.pallas_call(
        functools.partial(_mlp_kernel, final_norm=final_norm),
        grid=(bsz, seq // tm, f // tf),
        in_specs=[pl.BlockSpec((None, tm, d), lambda b, i, k: (b, i, 0)),
                  pl.BlockSpec((1, d), lambda b, i, k: (0, 0)),
                  vec, vec, vec,
                  pl.BlockSpec((d, tf), lambda b, i, k: (0, k)),
                  pl.BlockSpec((tf, d), lambda b, i, k: (k, 0)),
                  pl.BlockSpec((1, d), lambda b, i, k: (0, 0))],
        out_specs=pl.BlockSpec((None, tm, d), lambda b, i, k: (b, i, 0)),
        out_shape=jax.ShapeDtypeStruct((bsz, seq, d), F32),
        scratch_shapes=[pltpu.VMEM((tm, d), BF16), pltpu.VMEM((tm, d), F32)],
        compiler_params=_cparams(("parallel", "parallel", "arbitrary")),
        name="mlp",
    )(x, g.reshape(1, d), sc, sh, gate, w1, w2, fg.reshape(1, d))


CONV_PAD = SUBLANES


def _fill_padded(xpad_ref, x_ref, seq, rows):
    zeros = jnp.zeros((CONV_PAD, xpad_ref.shape[1]), F32)
    xpad_ref[0:CONV_PAD, :] = zeros
    xpad_ref[CONV_PAD + seq:CONV_PAD + seq + CONV_PAD, :] = zeros

    def body(t, carry):
        r0 = pl.multiple_of(t * rows, rows)
        xpad_ref[pl.ds(r0 + CONV_PAD, rows), :] = x_ref[pl.ds(r0, rows), :].astype(F32)
        return carry

    lax.fori_loop(0, seq // rows, body, 0)


def _conv_tile(xpad_ref, r0, rows, w):
    acc = None
    for j in range(CONV_WIDTH):
        off = j - CONV_LEFT
        term = xpad_ref[pl.ds(r0 + (CONV_PAD + off), rows), :] * w[j:j + 1]
        acc = term if acc is None else acc + term
    return acc


def _dnprep_kernel(x_ref, w_ref, o_ref, xpad_ref, *, seq, rows):
    j = pl.program_id(1)
    scale = jnp.where(j < DN_HEADS, DN_DK ** -0.5, 1.0).astype(F32)
    _fill_padded(xpad_ref, x_ref, seq, rows)
    w = w_ref[...]

    def conv_silu(t):
        r0 = pl.multiple_of(t * rows, rows)
        y = _conv_tile(xpad_ref, r0, rows, w)
        return r0, y * _sigmoid(y)

    def qk_body(t, carry):
        r0, y = conv_silu(t)
        nrm = (y * lax.rsqrt(jnp.sum(y * y, axis=-1, keepdims=True) + EPS)) * scale
        o_ref[pl.ds(r0, rows), :] = nrm.astype(o_ref.dtype)
        return carry

    def v_body(t, carry):
        r0, y = conv_silu(t)
        o_ref[pl.ds(r0, rows), :] = y.astype(o_ref.dtype)
        return carry

    @pl.when(j < 2 * DN_HEADS)
    def _():
        lax.fori_loop(0, seq // rows, qk_body, 0)

    @pl.when(j >= 2 * DN_HEADS)
    def _():
        lax.fori_loop(0, seq // rows, v_body, 0)


def _dnprep_call(proj, conv_w, col0_blocks, rows):
    bsz, seq, _ = proj.shape
    ncol = conv_w.shape[1] // LANES
    return pl.pallas_call(
        functools.partial(_dnprep_kernel, seq=seq, rows=rows),
        grid=(bsz, ncol),
        in_specs=[pl.BlockSpec((None, seq, LANES), lambda b, j: (b, 0, col0_blocks + j)),
                  pl.BlockSpec((CONV_WIDTH, LANES), lambda b, j: (0, j))],
        out_specs=pl.BlockSpec((None, seq, LANES), lambda b, j: (b, 0, j)),
        out_shape=jax.ShapeDtypeStruct((bsz, seq, ncol * LANES), BF16),
        scratch_shapes=[pltpu.VMEM((seq + 2 * CONV_PAD, LANES), F32)],
        compiler_params=_cparams(("parallel", "parallel")),
        name="dnprep",
    )(proj, conv_w)


def _scan_tile(a, b, carry, rev, rows):
    nv = rows // SUBLANES
    lanes = a.shape[1]
    a = a.reshape(nv, SUBLANES, lanes)
    b = b.reshape(nv, SUBLANES, lanes)
    row = lax.broadcasted_iota(jnp.int32, a.shape, 1)
    s = 1
    while s < SUBLANES:
        shift = SUBLANES - s if rev else s
        a_s = pltpu.roll(a, shift, 1)
        b_s = pltpu.roll(b, shift, 1)
        valid = (row < SUBLANES - s) if rev else (row >= s)
        b = b + a * jnp.where(valid, b_s, 0.0)
        a = a * jnp.where(valid, a_s, 1.0)
        s *= 2
    hs = [None] * nv
    for v in (range(nv - 1, -1, -1) if rev else range(nv)):
        h = a[v] * carry + b[v]
        carry = h[0:1] if rev else h[SUBLANES - 1:SUBLANES]
        hs[v] = h
    return jnp.concatenate(hs, axis=0), carry


def _gelu_tanh(x):
    return 0.5 * x * (1.0 + jnp.tanh(0.7978845608028654 * (x + 0.044715 * (x * x * x))))


def _lru_kernel(x_ref, y_ref, cw_ref, cb_ref, wa_ref, wi_ref, ba_ref, bi_ref, lam_ref, h0_ref,
                o_ref, ht_ref, xpad_ref, xc_ref, hf_ref, *, seq, rows):
    _fill_padded(xpad_ref, x_ref, seq, rows)
    cw = cw_ref[...]
    cb = cb_ref[...]
    nt = seq // rows

    def gates(xc, d):
        r = 1.0 / (1.0 + jnp.exp2(_bdot(xc, wa_ref[d]) - LOG2E * ba_ref[d:d + 1, :]))
        i = 1.0 / (1.0 + jnp.exp2(_bdot(xc, wi_ref[d]) - LOG2E * bi_ref[d:d + 1, :]))
        neg_rate = LRU_C * _softplus(-lam_ref[d:d + 1, :])
        a = jnp.exp2(r * (-LOG2E * neg_rate))
        b = jnp.sqrt(jnp.tanh(r * neg_rate) * (a * a + 1.0)) * (i * xc)
        return a, b

    def fwd_body(t, carry):
        r0 = pl.multiple_of(t * rows, rows)
        xc = _conv_tile(xpad_ref, r0, rows, cw) + cb
        xc_ref[pl.ds(r0, rows), :] = xc
        a, b = gates(xc, 0)
        h, carry = _scan_tile(a, b, carry, False, rows)
        hf_ref[pl.ds(r0, rows), :] = h
        return carry

    carry_f = lax.fori_loop(0, nt, fwd_body, h0_ref[0:1, :])

    def bwd_body(t, carry):
        r0 = pl.multiple_of((nt - 1 - t) * rows, rows)
        xc = xc_ref[pl.ds(r0, rows), :]
        a, b = gates(xc, 1)
        h, carry = _scan_tile(a, b, carry, True, rows)
        y = y_ref[pl.ds(r0, rows), :].astype(F32)
        o_ref[pl.ds(r0, rows), :] = ((hf_ref[pl.ds(r0, rows), :] + h) * _gelu_tanh(y)).astype(o_ref.dtype)
        return carry

    carry_b = lax.fori_loop(0, nt, bwd_body, h0_ref[1:2, :])
    ht_ref[0:1, :] = carry_f
    ht_ref[1:2, :] = carry_b


def _lru_call(proj, cw, cb, wa, wi, ba, bi, lam, h0, rows):
    bsz, seq, _ = proj.shape
    width = cw.shape[1]
    ncol = width // LANES
    vec2 = pl.BlockSpec((2, LANES), lambda b, j: (0, j))
    wspec = pl.BlockSpec((2, None, LANES, LANES), lambda b, j: (0, j, 0, 0))
    return pl.pallas_call(
        functools.partial(_lru_kernel, seq=seq, rows=rows),
        grid=(bsz, ncol),
        in_specs=[pl.BlockSpec((None, seq, LANES), lambda b, j: (b, 0, j)),
                  pl.BlockSpec((None, seq, LANES), lambda b, j: (b, 0, ncol + j)),
                  pl.BlockSpec((CONV_WIDTH, LANES), lambda b, j: (0, j)),
                  pl.BlockSpec((1, LANES), lambda b, j: (0, j)),
                  wspec, wspec, vec2, vec2, vec2,
                  pl.BlockSpec((None, 2, LANES), lambda b, j: (b, 0, j))],
        out_specs=[pl.BlockSpec((None, seq, LANES), lambda b, j: (b, 0, j)),
                   pl.BlockSpec((None, 2, LANES), lambda b, j: (b, 0, j))],
        out_shape=[jax.ShapeDtypeStruct((bsz, seq, width), BF16),
                   jax.ShapeDtypeStruct((bsz, 2, width), F32)],
        scratch_shapes=[pltpu.VMEM((seq + 2 * CONV_PAD, LANES), F32),
                        pltpu.VMEM((seq, LANES), F32),
                        pltpu.VMEM((seq, LANES), F32)],
        compiler_params=_cparams(("parallel", "parallel")),
        name="lru",
    )(proj, proj, cw, cb.reshape(1, width), wa, wi, ba, bi, lam, h0)


DN_GROUP = 2 * DN_CHUNK


def _split3(x):
    x1 = x.astype(BF16)
    r1 = x - x1.astype(F32)
    x2 = r1.astype(BF16)
    x3 = (r1 - x2.astype(F32)).astype(BF16)
    return x1, x2, x3


def _delta_kernel(q_ref, k_ref, v_ref, gr_ref, grt_ref, cvec_ref, ccol_ref, cum_ref, cumt_ref,
                  ones_ref, s0_ref, o_ref, st_ref, s_ref, *, rev, rows):
    step = pl.program_id(1)
    d = 1 if rev else 0

    @pl.when(step == 0)
    def _():
        s_ref[...] = s0_ref[...]

    raw = gr_ref[...]
    sig = _sigmoid(raw)
    g = -jnp.exp(cvec_ref[0:1, :]) * _softplus(raw + cvec_ref[1:2, :])
    g1, g2, g3 = _split3(g)
    cum = cum_ref[...]
    ones = ones_ref[...]

    def mm(m, parts):
        return sum(jnp.dot(m, p, preferred_element_type=F32) for p in parts)

    gc = mm(cum, (g1, g2, g3))
    gl = mm(ones, (g1, g2, g3))
    e_gc = jnp.exp(gc)
    e_rest = jnp.exp(gl - gc)
    e_gl = jnp.exp(gl)
    gt = -jnp.exp(ccol_ref[0]) * _softplus(grt_ref[...] + ccol_ref[1])
    t1, t2, t3 = _split3(gt)
    cumt = cumt_ref[...]
    gct = sum(jnp.dot(p, cumt, preferred_element_type=F32) for p in (t1, t2, t3))

    ri = lax.broadcasted_iota(jnp.int32, (DN_GROUP, DN_GROUP), 0)
    ci = lax.broadcasted_iota(jnp.int32, (DN_GROUP, DN_GROUP), 1)
    same = (ri // DN_CHUNK) == (ci // DN_CHUNK)
    if rev:
        incl = same & (ri <= ci)
        strict = same & (ri < ci)
    else:
        incl = same & (ri >= ci)
        strict = same & (ri > ci)

    ngroups = rows // DN_GROUP
    zeros_c = jnp.zeros((DN_CHUNK, DN_DK), F32)

    def bcast(x, rsl, lane):
        return jnp.broadcast_to(x[rsl, lane:lane + 1], (DN_GROUP, LANES))

    heads = range(DN_HEADS)
    groups = list(range(ngroups - 1, -1, -1) if rev else range(ngroups))
    units = [(gi, h) for gi in groups for h in heads]
    qd, kd, egl_b, rhs, kk, qk = {}, {}, {}, {}, {}, {}
    for u in units:
        gi, h = u
        rsl = slice(gi * DN_GROUP, (gi + 1) * DN_GROUP)
        lb = d * DN_HEADS + h
        lg = 2 * DN_HEADS + d * DN_HEADS + h
        csl = slice(h * DN_DK, (h + 1) * DN_DK)
        q16 = q_ref[rsl, csl]
        k16 = k_ref[rsl, csl]
        q_h = q16.astype(F32)
        k_h = k16.astype(F32)
        beta_b = bcast(sig, rsl, lb)
        egc_b = bcast(e_gc, rsl, lg)
        kb_h = k_h * beta_b
        kk[u] = _bdot_nt(kb_h, k16)
        qk[u] = _bdot_nt(q16, k16)
        rhs[u] = jnp.concatenate([v_ref[rsl, csl].astype(F32) * beta_b, kb_h * egc_b], axis=1)
        qd[u] = q_h * egc_b
        kd[u] = k_h * bcast(e_rest, rsl, lg)
        egl_b[u] = bcast(e_gl, rsl, lg)
    p, a16, attn = {}, {}, {}
    for u in units:
        gi, h = u
        rsl = slice(gi * DN_GROUP, (gi + 1) * DN_GROUP)
        lg = 2 * DN_HEADS + d * DN_HEADS + h
        diff = bcast(gc, rsl, lg) - gct[lg:lg + 1, rsl]
        decay = jnp.exp(jnp.where(incl, diff, NEG_BIG))
        m = jnp.where(strict, kk[u] * decay, 0.0)
        attn[u] = qk[u] * decay
        p[u] = -m
        a16[u] = m.astype(BF16)
    npow = 1
    while npow < DN_CHUNK // 2:
        a = {u: jnp.dot(a16[u], a16[u], preferred_element_type=F32) for u in units}
        a16 = {u: a[u].astype(BF16) for u in units}
        p = {u: p[u] + a[u] + jnp.dot(p[u].astype(BF16), a16[u], preferred_element_type=F32) for u in units}
        npow *= 2
    uw = {u: rhs[u] + _bdot(p[u], rhs[u]) for u in units}
    s = [s_ref[h] for h in heads]
    for gi in groups:
        for cc in ((1, 0) if rev else (0, 1)):
            c0 = cc * DN_CHUNK
            cs = slice(c0, c0 + DN_CHUNK)
            r2 = [_bdot(jnp.concatenate([uw[gi, h][cs, DN_DK:], qd[gi, h][cs]], axis=0), s[h]) for h in heads]
            v_new = [uw[gi, h][cs, :DN_DK] - r2[h][:DN_CHUNK] for h in heads]
            for h in heads:
                v_pad = (jnp.concatenate([v_new[h], zeros_c], axis=0) if cc == 0
                         else jnp.concatenate([zeros_c, v_new[h]], axis=0))
                o_c = r2[h][DN_CHUNK:] + _bdot(attn[gi, h][cs], v_pad)
                o_ref[gi * DN_GROUP + c0:gi * DN_GROUP + c0 + DN_CHUNK,
                      h * DN_DK:(h + 1) * DN_DK] = o_c.astype(o_ref.dtype)
            for h in heads:
                egl = jnp.concatenate([egl_b[gi, h][cs], egl_b[gi, h][cs]], axis=0)
                s[h] = s[h] * egl + _bdot_tn(kd[gi, h][cs], v_new[h])
    for h in heads:
        s_ref[h] = s[h]

    @pl.when(step == pl.num_programs(1) - 1)
    def _():
        st_ref[...] = s_ref[...]


def _delta_consts(rows, rev):
    t = jnp.arange(rows)
    same = (t[:, None] // DN_CHUNK) == (t[None, :] // DN_CHUNK)
    cum = same & ((t[:, None] <= t[None, :]) if rev else (t[:, None] >= t[None, :]))
    return cum.astype(BF16), cum.T.astype(BF16), same.astype(BF16)


def _delta_call(qkv, gates, gates_t, cvec, ccol, s0, rev, rows):
    bsz, seq, _ = qkv.shape
    width = DN_HEADS * DN_DK
    nsteps = seq // rows
    cum, cumt, ones = _delta_consts(rows, rev)

    def blk(i):
        return nsteps - 1 - i if rev else i

    full = lambda shape: pl.BlockSpec(shape, lambda b, i: (0,) * len(shape))
    state = pl.BlockSpec((None, DN_HEADS, DN_DK, DN_DK), lambda b, i: (b, 0, 0, 0))
    return pl.pallas_call(
        functools.partial(_delta_kernel, rev=rev, rows=rows),
        grid=(bsz, nsteps),
        in_specs=[pl.BlockSpec((None, rows, width), lambda b, i: (b, blk(i), 0)),
                  pl.BlockSpec((None, rows, width), lambda b, i: (b, blk(i), 1)),
                  pl.BlockSpec((None, rows, width), lambda b, i: (b, blk(i), 2)),
                  pl.BlockSpec((None, rows, LANES), lambda b, i: (b, blk(i), 0)),
                  pl.BlockSpec((None, 4 * DN_HEADS, rows), lambda b, i: (b, 0, blk(i))),
                  full((SUBLANES, LANES)),
                  full((2, 4 * DN_HEADS, rows)),
                  full((rows, rows)), full((rows, rows)), full((rows, rows)),
                  state],
        out_specs=[pl.BlockSpec((None, rows, width), lambda b, i: (b, blk(i), 0)), state],
        out_shape=[jax.ShapeDtypeStruct((bsz, seq, width), BF16),
                   jax.ShapeDtypeStruct((bsz, DN_HEADS, DN_DK, DN_DK), F32)],
        scratch_shapes=[pltpu.VMEM((DN_HEADS, DN_DK, DN_DK), F32)],
        compiler_params=_cparams(("parallel", "arbitrary")),
        name="delta_bwd" if rev else "delta_fwd",
    )(qkv, qkv, qkv, gates, gates_t, cvec, ccol, cum, cumt, ones, s0)


def _evenout_kernel(lru_ref, of_ref, ob_ref, z_ref, g_ref, w_ref, res_ref, gate_ref, o_ref, a_ref):
    width = lru_ref.shape[1]
    a_ref[:, 0:width] = lru_ref[...].astype(BF16)
    for h in range(DN_HEADS):
        csl = slice(h * DN_DK, (h + 1) * DN_DK)
        o = of_ref[:, csl].astype(F32) + ob_ref[:, csl].astype(F32)
        z = z_ref[:, csl].astype(F32)
        y = (_rms(o) * g_ref[:, csl]) * (z * _sigmoid(z))
        a_ref[:, width + h * DN_DK:width + (h + 1) * DN_DK] = y.astype(BF16)
    o_ref[...] = res_ref[...] + gate_ref[...] * jnp.dot(a_ref[...], w_ref[...], preferred_element_type=F32)


def _evenout_call(lru, o_f, o_b, proj, z_block, dn_g, w_out, res, gate, tm):
    bsz, seq, width = lru.shape
    dnw = o_f.shape[2]
    d = res.shape[2]
    row = lambda c: pl.BlockSpec((None, tm, c), lambda b, i: (b, i, 0))
    return pl.pallas_call(
        _evenout_kernel,
        grid=(bsz, seq // tm),
        in_specs=[row(width), row(dnw), row(dnw),
                  pl.BlockSpec((None, tm, dnw), lambda b, i: (b, i, z_block)),
                  pl.BlockSpec((1, dnw), lambda b, i: (0, 0)),
                  pl.BlockSpec((width + dnw, d), lambda b, i: (0, 0)),
                  row(d),
                  pl.BlockSpec((None, 1, d), lambda b, i: (b, 0, 0))],
        out_specs=row(d),
        out_shape=jax.ShapeDtypeStruct((bsz, seq, d), F32),
        scratch_shapes=[pltpu.VMEM((tm, width + dnw), BF16)],
        compiler_params=_cparams(("parallel", "parallel")),
        name="evenout",
    )(lru, o_f, o_b, proj, dn_g, w_out, res, gate)


def _linres_kernel(a_ref, w_ref, res_ref, gate_ref, o_ref):
    o_ref[...] = res_ref[...] + gate_ref[...] * jnp.dot(a_ref[...].astype(BF16), w_ref[...],
                                                        preferred_element_type=F32)


def _linres_call(a, w, res, gate, tm):
    bsz, seq, kdim = a.shape
    d = res.shape[2]
    return pl.pallas_call(
        _linres_kernel,
        grid=(bsz, seq // tm),
        in_specs=[pl.BlockSpec((None, tm, kdim), lambda b, i: (b, i, 0)),
                  pl.BlockSpec((kdim, d), lambda b, i: (0, 0)),
                  pl.BlockSpec((None, tm, d), lambda b, i: (b, i, 0)),
                  pl.BlockSpec((None, 1, d), lambda b, i: (b, 0, 0))],
        out_specs=pl.BlockSpec((None, tm, d), lambda b, i: (b, i, 0)),
        out_shape=jax.ShapeDtypeStruct((bsz, seq, d), F32),
        compiler_params=_cparams(("parallel", "parallel")),
        name="linres",
    )(a, w, res, gate)


def _aprep_kernel(*refs, rope, want_q):
    if rope:
        x_ref, qg_ref, kg_ref, cos_ref, sin_ref = refs[:5]
        outs = refs[5:]
    else:
        x_ref, qg_ref, kg_ref = refs[:3]
        outs = refs[3:]
    if want_q:
        q_ref, k_ref, v_ref = outs
    else:
        k_ref, v_ref = outs

    if rope:
        cos = cos_ref[...]
        sin = sin_ref[...]

    def norm_rope(x, g):
        y = _rms(x) * g
        if rope:
            y = y * cos + pltpu.roll(y, HEAD_DIM // 2, 1) * sin
        return y

    if want_q:
        for h in range(N_Q_HEADS):
            x = x_ref[:, h * HEAD_DIM:(h + 1) * HEAD_DIM].astype(F32)
            q_ref[h] = (norm_rope(x, qg_ref[...]) * (HEAD_DIM ** -0.5 * LOG2E)).astype(BF16)
    for h in range(N_KV_HEADS):
        c0 = (N_Q_HEADS + h) * HEAD_DIM
        k_ref[h] = norm_rope(x_ref[:, c0:c0 + HEAD_DIM].astype(F32), kg_ref[...]).astype(BF16)
        c1 = (N_Q_HEADS + N_KV_HEADS + h) * HEAD_DIM
        v_ref[h, 0:HEAD_DIM, :] = x_ref[:, c1:c1 + HEAD_DIM].astype(F32).T.astype(BF16)
        pad_row = lax.broadcasted_iota(jnp.int32, (VT_PAD, x_ref.shape[0]), 0)
        v_ref[h, HEAD_DIM:HEAD_DIM + VT_PAD, :] = jnp.where(pad_row == 0, 1.0, 0.0).astype(BF16)


def _aprep_call(qkv, qg, kg, rope_tabs, want_q, tm):
    bsz, seq, cols = qkv.shape
    rope = rope_tabs is not None
    in_specs = [pl.BlockSpec((None, tm, cols), lambda b, i: (b, i, 0)),
                pl.BlockSpec((1, HEAD_DIM), lambda b, i: (0, 0)),
                pl.BlockSpec((1, HEAD_DIM), lambda b, i: (0, 0))]
    args = [qkv, qg.reshape(1, HEAD_DIM), kg.reshape(1, HEAD_DIM)]
    if rope:
        in_specs += [pl.BlockSpec((tm, HEAD_DIM), lambda b, i: (i, 0))] * 2
        args += list(rope_tabs)
    hspec = lambda n: pl.BlockSpec((None, n, tm, HEAD_DIM), lambda b, i: (b, 0, i, 0))
    hshape = lambda n: jax.ShapeDtypeStruct((bsz, n, seq, HEAD_DIM), BF16)
    tspec = lambda n: pl.BlockSpec((None, n, HEAD_DIM + VT_PAD, tm), lambda b, i: (b, 0, 0, i))
    tshape = lambda n: jax.ShapeDtypeStruct((bsz, n, HEAD_DIM + VT_PAD, seq), BF16)
    out_specs = [hspec(N_KV_HEADS), tspec(N_KV_HEADS)]
    out_shape = [hshape(N_KV_HEADS), tshape(N_KV_HEADS)]
    if want_q:
        out_specs = [hspec(N_Q_HEADS)] + out_specs
        out_shape = [hshape(N_Q_HEADS)] + out_shape
    return pl.pallas_call(
        functools.partial(_aprep_kernel, rope=rope, want_q=want_q),
        grid=(bsz, seq // tm),
        in_specs=in_specs,
        out_specs=out_specs,
        out_shape=out_shape,
        compiler_params=_cparams(("parallel", "parallel")),
        name="aprep_x" if rope else "aprep_ctx",
    )(*args)


def _rope_tables(n_tokens):
    rows = n_tokens // GRID_W
    axis = HEAD_DIM // 2
    inv_freq = ROPE_THETA ** (-jnp.arange(0, axis, 2, dtype=F32) / axis)
    ang_r = jnp.arange(rows).astype(F32)[:, None] * inv_freq[None]
    ang_c = jnp.arange(GRID_W).astype(F32)[:, None] * inv_freq[None]
    cr, sr = (jnp.repeat(t, GRID_W, axis=0) for t in (jnp.cos(ang_r), jnp.sin(ang_r)))
    cc, sc = (jnp.tile(t, (rows, 1)) for t in (jnp.cos(ang_c), jnp.sin(ang_c)))
    cos = jnp.concatenate([cr, cc, cr, cc], axis=-1)
    sin = jnp.concatenate([-sr, -sc, sr, sc], axis=-1)
    return cos, sin


def _rope_perm():
    quarter = HEAD_DIM // 4
    return jnp.concatenate([jnp.arange(0, quarter), jnp.arange(2 * quarter, 3 * quarter),
                            jnp.arange(quarter, 2 * quarter), jnp.arange(3 * quarter, 4 * quarter)])


def _flash_kernel(q_ref, k_ref, vt_ref, o_ref, s0_ref, sa_ref, sb_ref, m0_ref, ma_ref, mb_ref, *,
                  tq, tk, nk, nt):
    m_rows = Q_PER_KV * tq

    def q_tile(i):
        r0 = pl.multiple_of(i * tq, tq)
        return q_ref[:, pl.ds(r0, tq), :].reshape(m_rows, HEAD_DIM)

    def scores(q, j, s_ref, cmax_ref):
        c0 = pl.multiple_of(j * tk, tk)
        st = lax.dot_general(k_ref[pl.ds(c0, tk), :], q, (((1,), (1,)), ((), ())),
                             preferred_element_type=F32)
        s_ref[...] = st
        cmax_ref[...] = jnp.max(st, axis=0, keepdims=True)

    def update(j, s_ref, cmax_ref, carry):
        m, acc = carry
        m_new = jnp.maximum(m, cmax_ref[...])
        alpha = jnp.exp2(m - m_new)
        p = jnp.exp2(s_ref[...] - m_new)
        vt = vt_ref[:, pl.ds(pl.multiple_of(j * tk, tk), tk)]
        acc = alpha * acc + jnp.dot(vt, p.astype(BF16), preferred_element_type=F32)
        return m_new, acc

    bufs = ((sa_ref, ma_ref), (sb_ref, mb_ref))

    def tile_body(i, _):
        q = q_tile(i)
        q_next = q_tile(jnp.minimum(i + 1, nt - 1))

        def run(j0, count, carry, last):
            for u in range(count):
                j = j0 + u
                if u + 1 < count or not last:
                    scores(q, j + 1, *bufs[(u + 1) % 2])
                else:
                    scores(q_next, 0, s0_ref, m0_ref)
                carry = update(j, *bufs[u % 2], carry)
            return carry

        carry = (jnp.full((1, m_rows), NEG_BIG, F32), jnp.zeros((HEAD_DIM + VT_PAD, m_rows), F32))
        scores(q, 1, sa_ref, ma_ref)
        carry = update(0, s0_ref, m0_ref, carry)
        rest = nk - 1
        niter = (rest - 1) // FLASH_UNROLL
        carry = lax.fori_loop(0, niter, lambda jj, c: run(1 + jj * FLASH_UNROLL, FLASH_UNROLL, c, False),
                              carry)
        _, acc = run(1 + niter * FLASH_UNROLL, rest - niter * FLASH_UNROLL, carry, True)
        o = (acc[:HEAD_DIM] / acc[HEAD_DIM:HEAD_DIM + 1]).T
        r0 = pl.multiple_of(i * tq, tq)
        for r in range(Q_PER_KV):
            o_ref[pl.ds(r0, tq), r * HEAD_DIM:(r + 1) * HEAD_DIM] = o[r * tq:(r + 1) * tq].astype(o_ref.dtype)
        return 0

    scores(q_tile(0), 0, s0_ref, m0_ref)
    lax.fori_loop(0, nt, tile_body, 0)


def _flash_call(q, k, vt, tq, tk):
    bsz, _, seq, _ = q.shape
    lk = k.shape[2]
    nk = lk // tk
    assert nk >= 2, "the score pipeline needs at least two key blocks"
    q5 = q.reshape(bsz, N_KV_HEADS, Q_PER_KV, seq, HEAD_DIM)
    return pl.pallas_call(
        functools.partial(_flash_kernel, tq=tq, tk=tk, nk=nk, nt=seq // tq),
        grid=(bsz, N_KV_HEADS),
        in_specs=[pl.BlockSpec((None, None, Q_PER_KV, seq, HEAD_DIM), lambda b, g: (b, g, 0, 0, 0)),
                  pl.BlockSpec((None, None, lk, HEAD_DIM), lambda b, g: (b, g, 0, 0)),
                  pl.BlockSpec((None, None, HEAD_DIM + VT_PAD, lk), lambda b, g: (b, g, 0, 0))],
        out_specs=pl.BlockSpec((None, seq, Q_PER_KV * HEAD_DIM), lambda b, g: (b, 0, g)),
        out_shape=jax.ShapeDtypeStruct((bsz, seq, N_Q_HEADS * HEAD_DIM), BF16),
        scratch_shapes=([pltpu.VMEM((tk, Q_PER_KV * tq), F32)] * 3
                        + [pltpu.VMEM((1, Q_PER_KV * tq), F32)] * 3),
        compiler_params=_cparams(("parallel", "parallel")),
        name="flash",
    )(q5, k, vt)


def _row_tile(seq, cap):
    t = min(seq, cap)
    while seq % t:
        t //= 2
    return t


def _even_weights(w_in, lru_wa, lru_wi, dn_alog, dn_dtb, dn_g):
    d, cols = w_in.shape
    ngate = 4 * DN_HEADS
    main = cols - ngate
    w_gates = jnp.concatenate([w_in[:, main:], jnp.zeros((d, LANES - ngate), F32)], axis=1)

    def blockdiag(w):
        w = w * (-LOG2E)
        z = jnp.zeros_like(w[:, 0::2])
        top = jnp.concatenate([w[:, 0::2], z], axis=-1)
        bot = jnp.concatenate([z, w[:, 1::2]], axis=-1)
        return jnp.concatenate([top, bot], axis=-2).astype(BF16)

    alog = dn_alog.reshape(-1)
    dtb = dn_dtb.reshape(-1)
    cvec = jnp.zeros((SUBLANES, LANES), F32)
    cvec = cvec.at[0, 2 * DN_HEADS:ngate].set(alog).at[1, 2 * DN_HEADS:ngate].set(dtb)

    ccol = jnp.zeros((2, ngate), F32).at[0, 2 * DN_HEADS:].set(alog).at[1, 2 * DN_HEADS:].set(dtb)

    return dict(w_in=w_in.astype(BF16)[:, :main], w_gates=w_gates.astype(BF16),
                w_gates_t=w_in[:, main:].T.astype(BF16),
                wa=blockdiag(lru_wa), wi=blockdiag(lru_wi), cvec=cvec, ccol=ccol,
                dn_g=jnp.tile(dn_g, DN_HEADS).reshape(1, -1))


def _even_stream(h_in, res, gate, norm_g, sc, sh, ew, lru_cw, lru_cb, lru_ba, lru_bi, lru_lam, dn_cw,
                 w_out, h0, s0, tm_in, tm, rows, dn_rows):
    width = lru_cw.shape[1]
    ngate = 4 * DN_HEADS
    proj, gates, gates_t = _normlin_call(h_in, norm_g, sc, sh, ew["w_in"], ew["w_gates"], ew["w_gates_t"],
                                         tm_in)
    lru, h_t = _lru_call(proj, lru_cw, lru_cb, ew["wa"], ew["wi"], lru_ba, lru_bi, lru_lam, h0,
                         _row_tile(proj.shape[1], LRU_ROWS))
    qkv = _dnprep_call(proj, dn_cw, 2 * width // LANES, _row_tile(proj.shape[1], DNPREP_ROWS))
    ccol = jnp.broadcast_to(ew["ccol"][:, :, None], (2, ngate, dn_rows))
    outs, states = [], []
    for rev in (False, True):
        o_d, s_d = _delta_call(qkv, gates, gates_t, ew["cvec"], ccol,
                               s0[:, :, 1 if rev else 0], rev, dn_rows)
        outs.append(o_d)
        states.append(s_d)
    z_block = (2 * width + dn_cw.shape[1]) // (DN_HEADS * DN_DK)
    out = _evenout_call(lru, outs[0], outs[1], proj, z_block, ew["dn_g"], w_out, res, gate, tm)
    return out, h_t, jnp.stack(states, axis=2)


def kernel(x, c, ctx, c_ctx, norm_mix_g, norm_mlp_g, w_mod, b_mod, mlp_w1, mlp_w2, even_w_in, lru_conv_w,
           lru_conv_b, lru_w_a, lru_b_a, lru_w_i, lru_b_i, lru_lambda, dn_conv_w, dn_a_log, dn_dt_bias,
           dn_norm_g, even_w_out, attn_w_qkv, attn_q_norm_g, attn_k_norm_g, attn_w_o, final_norm_g):
    bsz, seq, d = x.shape
    lc = ctx.shape[1]
    depth = w_mod.shape[0]
    tm_x = _row_tile(seq, 512)
    tm_c = _row_tile(lc, 512)
    tm_mlp = _row_tile(seq, 1024)
    rows_x = _row_tile(seq, 256)
    rows_c = _row_tile(lc, 256)

    cc = jnp.zeros((SUBLANES, d), F32).at[:bsz].set(c).at[bsz].set(c_ctx)
    mod = _mod_call(cc, w_mod, b_mod).reshape(depth, SUBLANES, 6, d)

    rope = _rope_tables(seq)
    xs, cs = x, ctx
    for l in range(depth):
        last = l == depth - 1
        j = l // 2
        mx = [mod[l, :bsz, k][:, None, :] for k in range(6)]
        mc = [jnp.broadcast_to(mod[l, bsz, k][None, None, :], (bsz, 1, d)) for k in range(6)]
        sh1, sc1, g1, sh2, sc2, g2 = mx
        csh1, csc1, cg1, csh2, csc2, cg2 = mc
        w1 = mlp_w1[l].astype(BF16)
        w2 = mlp_w2[l].astype(BF16)
        if l % 2 == 0:
            ew = _even_weights(even_w_in[j], lru_w_a[j], lru_w_i[j], dn_a_log[j], dn_dt_bias[j], dn_norm_g[j])
            w_out = even_w_out[j].astype(BF16)
            h0 = jnp.zeros((bsz, 2, lru_conv_w.shape[2]), F32)
            s0 = jnp.zeros((bsz, DN_HEADS, 2, DN_DK, DN_DK), F32)
            common = (ew, lru_conv_w[j], lru_conv_b[j], lru_b_a[j], lru_b_i[j], lru_lambda[j], dn_conv_w[j], w_out)
            cs_mix, h_t, s_t = _even_stream(cs, cs, cg1, norm_mix_g[l], csc1, csh1, *common, h0, s0,
                                            tm_c, tm_c, rows_c, rows_c)
            xs, _, _ = _even_stream(xs, xs, g1, norm_mix_g[l], sc1, sh1, *common, h_t, s_t,
                                    tm_mlp, tm_mlp, rows_x, _row_tile(seq, DELTA_ROWS))
        else:
            perm = _rope_perm()
            nqk = (N_Q_HEADS + N_KV_HEADS) * HEAD_DIM
            w_qk = attn_w_qkv[j][:, :nqk].reshape(d, N_Q_HEADS + N_KV_HEADS, HEAD_DIM)[:, :, perm]
            w_qkv = jnp.concatenate([w_qk.reshape(d, nqk), attn_w_qkv[j][:, nqk:]], axis=1).astype(BF16)
            q_g = attn_q_norm_g[j][perm]
            k_g = attn_k_norm_g[j][perm]
            w_o = attn_w_o[j].astype(BF16)
            qkv_c = _normlin_call(cs, norm_mix_g[l], csc1, csh1, w_qkv, None, None, tm_c)
            qkv_x = _normlin_call(xs, norm_mix_g[l], sc1, sh1, w_qkv, None, None, tm_mlp)
            if last:
                k_c, vt_c = _aprep_call(qkv_c, q_g, k_g, None, False, rows_c)
            else:
                qt_c, k_c, vt_c = _aprep_call(qkv_c, q_g, k_g, None, True, rows_c)
            qt_x, k_x, vt_x = _aprep_call(qkv_x, q_g, k_g, rope, True, tm_x)
            k_all = jnp.concatenate([k_c, k_x], axis=2)
            vt_all = jnp.concatenate([vt_c, vt_x], axis=3)
            o_x = _flash_call(qt_x, k_all, vt_all, _row_tile(seq, FLASH_TQ), _col_tile(lc + seq, FLASH_TK_CAP))
            xs = _linres_call(o_x, w_o, xs, g1, tm_mlp)
            if not last:
                o_c = _flash_call(qt_c, k_c, vt_c, rows_c, _col_tile(lc, min(FLASH_TK_CAP, lc // 2)))
                cs_mix = _linres_call(o_c, w_o, cs, cg1, tm_c)
        xs = _mlp_call(xs, norm_mlp_g[l], sc2, sh2, g2, w1, w2, final_norm_g, tm_mlp, last)
        if not last:
            cs = _mlp_call(cs_mix, norm_mlp_g[l], csc2, csh2, cg2, w1, w2, final_norm_g, tm_c, False)
    return xs
```

```python
import functools

import jax
import jax.numpy as jnp
from jax import lax
from jax.experimental import pallas as pl
from jax.experimental.pallas import tpu as pltpu

F32 = jnp.float32
BF16 = jnp.bfloat16

EPS = 1e-6
LANES = 128
SUBLANES = 8
VMEM_LIMIT = 56 * 1024 * 1024

LRU_BLOCK = 64
LRU_C = 8.0
CONV_WIDTH = 4
CONV_LEFT = 2
DN_HEADS = 8
DN_DK = 128
DN_CHUNK = 64
N_Q_HEADS = 8
N_KV_HEADS = 4
Q_PER_KV = N_Q_HEADS // N_KV_HEADS
HEAD_DIM = 128
GRID_W = 64
ROPE_THETA = 10000.0
NEG_BIG = -1e30
LOG2E = 1.4426950408889634
DNPREP_ROWS = 1024
LRU_ROWS = 1024
DELTA_ROWS = 256
FLASH_TQ = 256
FLASH_TK_CAP = 768
VT_PAD = 16
FLASH_UNROLL = 4


def _cparams(sem):
    return pltpu.CompilerParams(dimension_semantics=sem, vmem_limit_bytes=VMEM_LIMIT)


def _bdot(a, b):
    return jnp.dot(a.astype(BF16), b.astype(BF16), preferred_element_type=F32)


def _bdot_nt(a, b):
    return lax.dot_general(a.astype(BF16), b.astype(BF16), (((1,), (1,)), ((), ())),
                           preferred_element_type=F32)


def _bdot_tn(a, b):
    return lax.dot_general(a.astype(BF16), b.astype(BF16), (((0,), (0,)), ((), ())),
                           preferred_element_type=F32)


def _sigmoid(x):
    return jax.nn.sigmoid(x)


def _softplus(x):
    return jnp.maximum(x, 0.0) + jnp.log1p(jnp.exp(-jnp.abs(x)))


def _rms(x):
    return x * lax.rsqrt(jnp.mean(x * x, axis=-1, keepdims=True) + EPS)


def _col_tile(n, cap):
    best = LANES
    t = LANES
    while t <= min(n, cap):
        if n % t == 0:
            best = t
        t += LANES
    return best


def _mod_kernel(c_ref, w_ref, b_ref, o_ref):
    s = c_ref[...]
    s = s * _sigmoid(s)
    o_ref[...] = _bdot(s, w_ref[...]) + b_ref[...]


def _mod_call(cc, w_mod, b_mod):
    depth, d, n = w_mod.shape
    tn = _col_tile(n, 1536)
    return pl.pallas_call(
        _mod_kernel,
        grid=(depth, n // tn),
        in_specs=[pl.BlockSpec((SUBLANES, d), lambda l, j: (0, 0)),
                  pl.BlockSpec((None, d, tn), lambda l, j: (l, 0, j)),
                  pl.BlockSpec((None, 1, tn), lambda l, j: (l, 0, j))],
        out_specs=pl.BlockSpec((None, SUBLANES, tn), lambda l, j: (l, 0, j)),
        out_shape=jax.ShapeDtypeStruct((depth, SUBLANES, n), F32),
        compiler_params=_cparams(("parallel", "parallel")),
        name="mod",
    )(cc, w_mod, b_mod.reshape(depth, 1, n))


def _normlin_kernel(*refs, has_gates):
    if has_gates:
        x_ref, g_ref, sc_ref, sh_ref, w_ref, wg_ref, wgt_ref, o_ref, og_ref, ogt_ref, h_ref = refs
    else:
        x_ref, g_ref, sc_ref, sh_ref, w_ref, o_ref, h_ref = refs

    @pl.when(pl.program_id(2) == 0)
    def _():
        y = _rms(x_ref[...]) * g_ref[...]
        h_ref[...] = (y * (1.0 + sc_ref[...]) + sh_ref[...]).astype(BF16)
        if has_gates:
            og_ref[...] = jnp.dot(h_ref[...], wg_ref[...], preferred_element_type=F32)
            ogt_ref[...] = lax.dot_general(wgt_ref[...], h_ref[...], (((1,), (1,)), ((), ())),
                                           preferred_element_type=F32)

    o_ref[...] = jnp.dot(h_ref[...], w_ref[...], preferred_element_type=F32).astype(o_ref.dtype)


def _normlin_call(x, g, sc, sh, w, w_gates, w_gates_t, tm):
    bsz, seq, d = x.shape
    n = w.shape[1]
    tn = _col_tile(n, 2048)
    has_gates = w_gates is not None
    in_specs = [pl.BlockSpec((None, tm, d), lambda b, i, j: (b, i, 0)),
                pl.BlockSpec((1, d), lambda b, i, j: (0, 0)),
                pl.BlockSpec((None, 1, d), lambda b, i, j: (b, 0, 0)),
                pl.BlockSpec((None, 1, d), lambda b, i, j: (b, 0, 0)),
                pl.BlockSpec((d, tn), lambda b, i, j: (0, j))]
    out_specs = [pl.BlockSpec((None, tm, tn), lambda b, i, j: (b, i, j))]
    out_shape = [jax.ShapeDtypeStruct((bsz, seq, n), BF16)]
    args = [x, g.reshape(1, d), sc, sh, w]
    if has_gates:
        ng = w_gates.shape[1]
        in_specs.append(pl.BlockSpec((d, ng), lambda b, i, j: (0, 0)))
        out_specs.append(pl.BlockSpec((None, tm, ng), lambda b, i, j: (b, i, 0)))
        out_shape.append(jax.ShapeDtypeStruct((bsz, seq, ng), F32))
        ngt = w_gates_t.shape[0]
        in_specs.append(pl.BlockSpec((ngt, d), lambda b, i, j: (0, 0)))
        out_specs.append(pl.BlockSpec((None, ngt, tm), lambda b, i, j: (b, 0, i)))
        out_shape.append(jax.ShapeDtypeStruct((bsz, ngt, seq), F32))
        args += [w_gates, w_gates_t]
    out = pl.pallas_call(
        functools.partial(_normlin_kernel, has_gates=has_gates),
        grid=(bsz, seq // tm, n // tn),
        in_specs=in_specs,
        out_specs=out_specs,
        out_shape=out_shape,
        scratch_shapes=[pltpu.VMEM((tm, d), BF16)],
        compiler_params=_cparams(("parallel", "parallel", "arbitrary")),
        name="normlin",
    )(*args)
    return out if has_gates else out[0]


def _mlp_kernel(x_ref, g_ref, sc_ref, sh_ref, gate_ref, w1_ref, w2_ref, fg_ref, o_ref,
                h_ref, acc_ref, *, final_norm):
    k = pl.program_id(2)

    @pl.when(k == 0)
    def _():
        y = _rms(x_ref[...]) * g_ref[...]
        h_ref[...] = (y * (1.0 + sc_ref[...]) + sh_ref[...]).astype(BF16)
        acc_ref[...] = jnp.zeros_like(acc_ref)

    a = jnp.dot(h_ref[...], w1_ref[...], preferred_element_type=F32)
    a = jnp.square(jnp.maximum(a, 0.0)).astype(BF16)
    acc_ref[...] += jnp.dot(a, w2_ref[...], preferred_element_type=F32)

    @pl.when(k == pl.num_programs(2) - 1)
    def _():
        y = x_ref[...] + gate_ref[...] * acc_ref[...]
        if final_norm:
            y = _rms(y) * fg_ref[...]
        o_ref[...] = y


def _mlp_call(x, g, sc, sh, gate, w1, w2, fg, tm, final_norm):
    bsz, seq, d = x.shape
    f = w1.shape[1]
    tf = _col_tile(f, 2048)
    vec = pl.BlockSpec((None, 1, d), lambda b, i, k: (b, 0, 0))
    return pl.pallas_call(
        functools.partial(_mlp_kernel, final_norm=final_norm),
        grid=(bsz, seq // tm, f // tf),
        in_specs=[pl.BlockSpec((None, tm, d), lambda b, i, k: (b, i, 0)),
                  pl.BlockSpec((1, d), lambda b, i, k: (0, 0)),
                  vec, vec, vec,
                  pl.BlockSpec((d, tf), lambda b, i, k: (0, k)),
                  pl.BlockSpec((tf, d), lambda b, i, k: (k, 0)),
                  pl.BlockSpec((1, d), lambda b, i, k: (0, 0))],
        out_specs=pl.BlockSpec((None, tm, d), lambda b, i, k: (b, i, 0)),
        out_shape=jax.ShapeDtypeStruct((bsz, seq, d), F32),
        scratch_shapes=[pltpu.VMEM((tm, d), BF16), pltpu.VMEM((tm, d), F32)],
        compiler_params=_cparams(("parallel", "parallel", "arbitrary")),
        name="mlp",
    )(x, g.reshape(1, d), sc, sh, gate, w1, w2, fg.reshape(1, d))


CONV_PAD = SUBLANES


def _fill_padded(xpad_ref, x_ref, seq, rows):
    zeros = jnp.zeros((CONV_PAD, xpad_ref.shape[1]), F32)
    xpad_ref[0:CONV_PAD, :] = zeros
    xpad_ref[CONV_PAD + seq:CONV_PAD + seq + CONV_PAD, :] = zeros

    def body(t, carry):
        r0 = pl.multiple_of(t * rows, rows)
        xpad_ref[pl.ds(r0 + CONV_PAD, rows), :] = x_ref[pl.ds(r0, rows), :].astype(F32)
        return carry

    lax.fori_loop(0, seq // rows, body, 0)


def _conv_tile(xpad_ref, r0, rows, w):
    acc = None
    for j in range(CONV_WIDTH):
        off = j - CONV_LEFT
        term = xpad_ref[pl.ds(r0 + (CONV_PAD + off), rows), :] * w[j:j + 1]
        acc = term if acc is None else acc + term
    return acc


def _dnprep_kernel(x_ref, w_ref, o_ref, xpad_ref, *, seq, rows):
    j = pl.program_id(1)
    scale = jnp.where(j < DN_HEADS, DN_DK ** -0.5, 1.0).astype(F32)
    _fill_padded(xpad_ref, x_ref, seq, rows)
    w = w_ref[...]

    def conv_silu(t):
        r0 = pl.multiple_of(t * rows, rows)
        y = _conv_tile(xpad_ref, r0, rows, w)
        return r0, y * _sigmoid(y)

    def qk_body(t, carry):
        r0, y = conv_silu(t)
        nrm = (y * lax.rsqrt(jnp.sum(y * y, axis=-1, keepdims=True) + EPS)) * scale
        o_ref[pl.ds(r0, rows), :] = nrm.astype(o_ref.dtype)
        return carry

    def v_body(t, carry):
        r0, y = conv_silu(t)
        o_ref[pl.ds(r0, rows), :] = y.astype(o_ref.dtype)
        return carry

    @pl.when(j < 2 * DN_HEADS)
    def _():
        lax.fori_loop(0, seq // rows, qk_body, 0)

    @pl.when(j >= 2 * DN_HEADS)
    def _():
        lax.fori_loop(0, seq // rows, v_body, 0)


def _dnprep_call(proj, conv_w, col0_blocks, rows):
    bsz, seq, _ = proj.shape
    ncol = conv_w.shape[1] // LANES
    return pl.pallas_call(
        functools.partial(_dnprep_kernel, seq=seq, rows=rows),
        grid=(bsz, ncol),
        in_specs=[pl.BlockSpec((None, seq, LANES), lambda b, j: (b, 0, col0_blocks + j)),
                  pl.BlockSpec((CONV_WIDTH, LANES), lambda b, j: (0, j))],
        out_specs=pl.BlockSpec((None, seq, LANES), lambda b, j: (b, 0, j)),
        out_shape=jax.ShapeDtypeStruct((bsz, seq, ncol * LANES), BF16),
        scratch_shapes=[pltpu.VMEM((seq + 2 * CONV_PAD, LANES), F32)],
        compiler_params=_cparams(("parallel", "parallel")),
        name="dnprep",
    )(proj, conv_w)


def _scan_tile(a, b, carry, rev, rows):
    nv = rows // SUBLANES
    lanes = a.shape[1]
    a = a.reshape(nv, SUBLANES, lanes)
    b = b.reshape(nv, SUBLANES, lanes)
    row = lax.broadcasted_iota(jnp.int32, a.shape, 1)
    s = 1
    while s < SUBLANES:
        shift = SUBLANES - s if rev else s
        a_s = pltpu.roll(a, shift, 1)
        b_s = pltpu.roll(b, shift, 1)
        valid = (row < SUBLANES - s) if rev else (row >= s)
        b = b + a * jnp.where(valid, b_s, 0.0)
        a = a * jnp.where(valid, a_s, 1.0)
        s *= 2
    hs = [None] * nv
    for v in (range(nv - 1, -1, -1) if rev else range(nv)):
        h = a[v] * carry + b[v]
        carry = h[0:1] if rev else h[SUBLANES - 1:SUBLANES]
        hs[v] = h
    return jnp.concatenate(hs, axis=0), carry


def _gelu_tanh(x):
    return 0.5 * x * (1.0 + jnp.tanh(0.7978845608028654 * (x + 0.044715 * (x * x * x))))


def _lru_kernel(x_ref, y_ref, cw_ref, cb_ref, wa_ref, wi_ref, ba_ref, bi_ref, lam_ref, h0_ref,
                o_ref, ht_ref, xpad_ref, xc_ref, hf_ref, *, seq, rows):
    _fill_padded(xpad_ref, x_ref, seq, rows)
    cw = cw_ref[...]
    cb = cb_ref[...]
    nt = seq // rows

    def gates(xc, d):
        r = 1.0 / (1.0 + jnp.exp2(_bdot(xc, wa_ref[d]) - LOG2E * ba_ref[d:d + 1, :]))
        i = 1.0 / (1.0 + jnp.exp2(_bdot(xc, wi_ref[d]) - LOG2E * bi_ref[d:d + 1, :]))
        neg_rate = LRU_C * _softplus(-lam_ref[d:d + 1, :])
        a = jnp.exp2(r * (-LOG2E * neg_rate))
        b = jnp.sqrt(jnp.tanh(r * neg_rate) * (a * a + 1.0)) * (i * xc)
        return a, b

    def fwd_body(t, carry):
        r0 = pl.multiple_of(t * rows, rows)
        xc = _conv_tile(xpad_ref, r0, rows, cw) + cb
        xc_ref[pl.ds(r0, rows), :] = xc
        a, b = gates(xc, 0)
        h, carry = _scan_tile(a, b, carry, False, rows)
        hf_ref[pl.ds(r0, rows), :] = h
        return carry

    carry_f = lax.fori_loop(0, nt, fwd_body, h0_ref[0:1, :])

    def bwd_body(t, carry):
        r0 = pl.multiple_of((nt - 1 - t) * rows, rows)
        xc = xc_ref[pl.ds(r0, rows), :]
        a, b = gates(xc, 1)
        h, carry = _scan_tile(a, b, carry, True, rows)
        y = y_ref[pl.ds(r0, rows), :].astype(F32)
        o_ref[pl.ds(r0, rows), :] = ((hf_ref[pl.ds(r0, rows), :] + h) * _gelu_tanh(y)).astype(o_ref.dtype)
        return carry

    carry_b = lax.fori_loop(0, nt, bwd_body, h0_ref[1:2, :])
    ht_ref[0:1, :] = carry_f
    ht_ref[1:2, :] = carry_b


def _lru_call(proj, cw, cb, wa, wi, ba, bi, lam, h0, rows):
    bsz, seq, _ = proj.shape
    width = cw.shape[1]
    ncol = width // LANES
    vec2 = pl.BlockSpec((2, LANES), lambda b, j: (0, j))
    wspec = pl.BlockSpec((2, None, LANES, LANES), lambda b, j: (0, j, 0, 0))
    return pl.pallas_call(
        functools.partial(_lru_kernel, seq=seq, rows=rows),
        grid=(bsz, ncol),
        in_specs=[pl.BlockSpec((None, seq, LANES), lambda b, j: (b, 0, j)),
                  pl.BlockSpec((None, seq, LANES), lambda b, j: (b, 0, ncol + j)),
                  pl.BlockSpec((CONV_WIDTH, LANES), lambda b, j: (0, j)),
                  pl.BlockSpec((1, LANES), lambda b, j: (0, j)),
                  wspec, wspec, vec2, vec2, vec2,
                  pl.BlockSpec((None, 2, LANES), lambda b, j: (b, 0, j))],
        out_specs=[pl.BlockSpec((None, seq, LANES), lambda b, j: (b, 0, j)),
                   pl.BlockSpec((None, 2, LANES), lambda b, j: (b, 0, j))],
        out_shape=[jax.ShapeDtypeStruct((bsz, seq, width), BF16),
                   jax.ShapeDtypeStruct((bsz, 2, width), F32)],
        scratch_shapes=[pltpu.VMEM((seq + 2 * CONV_PAD, LANES), F32),
                        pltpu.VMEM((seq, LANES), F32),
                        pltpu.VMEM((seq, LANES), F32)],
        compiler_params=_cparams(("parallel", "parallel")),
        name="lru",
    )(proj, proj, cw, cb.reshape(1, width), wa, wi, ba, bi, lam, h0)


DN_GROUP = 2 * DN_CHUNK


def _split3(x):
    x1 = x.astype(BF16)
    r1 = x - x1.astype(F32)
    x2 = r1.astype(BF16)
    x3 = (r1 - x2.astype(F32)).astype(BF16)
    return x1, x2, x3


def _delta_kernel(*refs, rows):
    (qf_ref, kf_ref, vf_ref, grf_ref, grtf_ref, qb_ref, kb_ref, vb_ref, grb_ref, grtb_ref,
     cvec_ref, ccol_ref, cum_ref, cumt_ref, ones_ref, s0_ref, of_ref, ob_ref, st_ref, s_ref) = refs
    step = pl.program_id(1)
    q_refs, k_refs, v_refs = (qf_ref, qb_ref), (kf_ref, kb_ref), (vf_ref, vb_ref)
    o_refs = (of_ref, ob_ref)
    dirs = (0, 1)

    @pl.when(step == 0)
    def _():
        s_ref[...] = s0_ref[...]

    ones = ones_ref[...]

    def mm(m, parts):
        return sum(jnp.dot(m, p, preferred_element_type=F32) for p in parts)

    sig, gc, e_gc, e_rest, e_gl, gct = [], [], [], [], [], []
    for d, (gr_ref, grt_ref) in enumerate(((grf_ref, grtf_ref), (grb_ref, grtb_ref))):
        raw = gr_ref[...]
        g = -jnp.exp(cvec_ref[0:1, :]) * _softplus(raw + cvec_ref[1:2, :])
        parts = _split3(g)
        gc_d = mm(cum_ref[d], parts)
        gl_d = mm(ones, parts)
        sig.append(_sigmoid(raw))
        gc.append(gc_d)
        e_gc.append(jnp.exp(gc_d))
        e_rest.append(jnp.exp(gl_d - gc_d))
        e_gl.append(jnp.exp(gl_d))
        gt = -jnp.exp(ccol_ref[0]) * _softplus(grt_ref[...] + ccol_ref[1])
        cumt = cumt_ref[d]
        gct.append(sum(jnp.dot(p, cumt, preferred_element_type=F32) for p in _split3(gt)))

    ri = lax.broadcasted_iota(jnp.int32, (DN_GROUP, DN_GROUP), 0)
    ci = lax.broadcasted_iota(jnp.int32, (DN_GROUP, DN_GROUP), 1)
    same = (ri // DN_CHUNK) == (ci // DN_CHUNK)
    incl = (same & (ri >= ci), same & (ri <= ci))
    strict = (same & (ri > ci), same & (ri < ci))

    ngroups = rows // DN_GROUP
    zeros_c = jnp.zeros((DN_CHUNK, DN_DK), F32)

    def bcast(x, rsl, lane):
        return jnp.broadcast_to(x[rsl, lane:lane + 1], (DN_GROUP, LANES))

    heads = range(DN_HEADS)
    groups = (list(range(ngroups)), list(range(ngroups - 1, -1, -1)))
    units = [(d, gi, h) for d in dirs for gi in groups[d] for h in heads]
    qd, kd, egl_b, rhs, kk, qk = {}, {}, {}, {}, {}, {}
    for u in units:
        d, gi, h = u
        rsl = slice(gi * DN_GROUP, (gi + 1) * DN_GROUP)
        lb = d * DN_HEADS + h
        lg = 2 * DN_HEADS + d * DN_HEADS + h
        csl = slice(h * DN_DK, (h + 1) * DN_DK)
        q16 = q_refs[d][rsl, csl]
        k16 = k_refs[d][rsl, csl]
        q_h = q16.astype(F32)
        k_h = k16.astype(F32)
        beta_b = bcast(sig[d], rsl, lb)
        egc_b = bcast(e_gc[d], rsl, lg)
        kb_h = k_h * beta_b
        kk[u] = _bdot_nt(kb_h, k16)
        qk[u] = _bdot_nt(q16, k16)
        rhs[u] = jnp.concatenate([v_refs[d][rsl, csl].astype(F32) * beta_b, kb_h * egc_b], axis=1)
        qd[u] = q_h * egc_b
        kd[u] = k_h * bcast(e_rest[d], rsl, lg)
        egl_b[u] = bcast(e_gl[d], rsl, lg)
    p, a16, attn = {}, {}, {}
    for u in units:
        d, gi, h = u
        rsl = slice(gi * DN_GROUP, (gi + 1) * DN_GROUP)
        lg = 2 * DN_HEADS + d * DN_HEADS + h
        diff = bcast(gc[d], rsl, lg) - gct[d][lg:lg + 1, rsl]
        decay = jnp.exp(jnp.where(incl[d], diff, NEG_BIG))
        m = jnp.where(strict[d], kk[u] * decay, 0.0)
        attn[u] = qk[u] * decay
        p[u] = -m
        a16[u] = m.astype(BF16)
    npow = 1
    while npow < DN_CHUNK // 2:
        a = {u: jnp.dot(a16[u], a16[u], preferred_element_type=F32) for u in units}
        a16 = {u: a[u].astype(BF16) for u in units}
        p = {u: p[u] + a[u] + jnp.dot(p[u].astype(BF16), a16[u], preferred_element_type=F32) for u in units}
        npow *= 2
    uw = {u: rhs[u] + _bdot(p[u], rhs[u]) for u in units}
    chains = [(d, h) for d in dirs for h in heads]
    s = {c: s_ref[c[1], c[0]] for c in chains}
    slots = tuple([(gi, cc) for gi in groups[d] for cc in ((1, 0) if d else (0, 1))] for d in dirs)
    for slot in range(2 * ngroups):
        r2, v_new = {}, {}
        for c in chains:
            d, h = c
            gi, cc = slots[d][slot]
            cs = slice(cc * DN_CHUNK, (cc + 1) * DN_CHUNK)
            r2[c] = _bdot(jnp.concatenate([uw[d, gi, h][cs, DN_DK:], qd[d, gi, h][cs]], axis=0), s[c])
        for c in chains:
            d, h = c
            gi, cc = slots[d][slot]
            cs = slice(cc * DN_CHUNK, (cc + 1) * DN_CHUNK)
            v_new[c] = uw[d, gi, h][cs, :DN_DK] - r2[c][:DN_CHUNK]
        for c in chains:
            d, h = c
            gi, cc = slots[d][slot]
            c0 = cc * DN_CHUNK
            cs = slice(c0, c0 + DN_CHUNK)
            v_pad = (jnp.concatenate([v_new[c], zeros_c], axis=0) if cc == 0
                     else jnp.concatenate([zeros_c, v_new[c]], axis=0))
            o_c = r2[c][DN_CHUNK:] + _bdot(attn[d, gi, h][cs], v_pad)
            o_refs[d][gi * DN_GROUP + c0:gi * DN_GROUP + c0 + DN_CHUNK,
                      h * DN_DK:(h + 1) * DN_DK] = o_c.astype(o_refs[d].dtype)
        for c in chains:
            d, h = c
            gi, cc = slots[d][slot]
            cs = slice(cc * DN_CHUNK, (cc + 1) * DN_CHUNK)
            egl = jnp.concatenate([egl_b[d, gi, h][cs], egl_b[d, gi, h][cs]], axis=0)
            s[c] = s[c] * egl + _bdot_tn(kd[d, gi, h][cs], v_new[c])
    for c in chains:
        s_ref[c[1], c[0]] = s[c]

    @pl.when(step == pl.num_programs(1) - 1)
    def _():
        st_ref[...] = s_ref[...]


def _delta_consts(rows):
    t = jnp.arange(rows)
    same = (t[:, None] // DN_CHUNK) == (t[None, :] // DN_CHUNK)
    cum = jnp.stack([same & (t[:, None] >= t[None, :]), same & (t[:, None] <= t[None, :])])
    return cum.astype(BF16), jnp.swapaxes(cum, 1, 2).astype(BF16), same.astype(BF16)


def _delta_call(qkv, gates, gates_t, cvec, ccol, s0, rows):
    bsz, seq, _ = qkv.shape
    width = DN_HEADS * DN_DK
    nsteps = seq // rows
    cum, cumt, ones = _delta_consts(rows)

    def per_dir(rev):
        blk = (lambda i: nsteps - 1 - i) if rev else (lambda i: i)
        return [pl.BlockSpec((None, rows, width), lambda b, i: (b, blk(i), 0)),
                pl.BlockSpec((None, rows, width), lambda b, i: (b, blk(i), 1)),
                pl.BlockSpec((None, rows, width), lambda b, i: (b, blk(i), 2)),
                pl.BlockSpec((None, rows, LANES), lambda b, i: (b, blk(i), 0)),
                pl.BlockSpec((None, 4 * DN_HEADS, rows), lambda b, i: (b, 0, blk(i)))]

    full = lambda shape: pl.BlockSpec(shape, lambda b, i: (0,) * len(shape))
    state = pl.BlockSpec((None, DN_HEADS, 2, DN_DK, DN_DK), lambda b, i: (b, 0, 0, 0, 0))
    o_fwd = pl.BlockSpec((None, rows, width), lambda b, i: (b, i, 0))
    o_bwd = pl.BlockSpec((None, rows, width), lambda b, i: (b, nsteps - 1 - i, 0))
    data = [qkv, qkv, qkv, gates, gates_t]
    return pl.pallas_call(
        functools.partial(_delta_kernel, rows=rows),
        grid=(bsz, nsteps),
        in_specs=per_dir(False) + per_dir(True) + [
            full((SUBLANES, LANES)), full((2, 4 * DN_HEADS, rows)),
            full((2, rows, rows)), full((2, rows, rows)), full((rows, rows)), state],
        out_specs=[o_fwd, o_bwd, state],
        out_shape=[jax.ShapeDtypeStruct((bsz, seq, width), BF16),
                   jax.ShapeDtypeStruct((bsz, seq, width), BF16),
                   jax.ShapeDtypeStruct((bsz, DN_HEADS, 2, DN_DK, DN_DK), F32)],
        scratch_shapes=[pltpu.VMEM((DN_HEADS, 2, DN_DK, DN_DK), F32)],
        compiler_params=_cparams(("parallel", "arbitrary")),
        name="delta",
    )(*data, *data, cvec, ccol, cum, cumt, ones, s0)


def _evenout_kernel(lru_ref, of_ref, ob_ref, z_ref, g_ref, w_ref, res_ref, gate_ref, o_ref, a_ref):
    width = lru_ref.shape[1]
    a_ref[:, 0:width] = lru_ref[...].astype(BF16)
    for h in range(DN_HEADS):
        csl = slice(h * DN_DK, (h + 1) * DN_DK)
        o = of_ref[:, csl].astype(F32) + ob_ref[:, csl].astype(F32)
        z = z_ref[:, csl].astype(F32)
        y = (_rms(o) * g_ref[:, csl]) * (z * _sigmoid(z))
        a_ref[:, width + h * DN_DK:width + (h + 1) * DN_DK] = y.astype(BF16)
    o_ref[...] = res_ref[...] + gate_ref[...] * jnp.dot(a_ref[...], w_ref[...], preferred_element_type=F32)


def _evenout_call(lru, o_f, o_b, proj, z_block, dn_g, w_out, res, gate, tm):
    bsz, seq, width = lru.shape
    dnw = o_f.shape[2]
    d = res.shape[2]
    row = lambda c: pl.BlockSpec((None, tm, c), lambda b, i: (b, i, 0))
    return pl.pallas_call(
        _evenout_kernel,
        grid=(bsz, seq // tm),
        in_specs=[row(width), row(dnw), row(dnw),
                  pl.BlockSpec((None, tm, dnw), lambda b, i: (b, i, z_block)),
                  pl.BlockSpec((1, dnw), lambda b, i: (0, 0)),
                  pl.BlockSpec((width + dnw, d), lambda b, i: (0, 0)),
                  row(d),
                  pl.BlockSpec((None, 1, d), lambda b, i: (b, 0, 0))],
        out_specs=row(d),
        out_shape=jax.ShapeDtypeStruct((bsz, seq, d), F32),
        scratch_shapes=[pltpu.VMEM((tm, width + dnw), BF16)],
        compiler_params=_cparams(("parallel", "parallel")),
        name="evenout",
    )(lru, o_f, o_b, proj, dn_g, w_out, res, gate)


def _linres_kernel(a_ref, w_ref, res_ref, gate_ref, o_ref):
    o_ref[...] = res_ref[...] + gate_ref[...] * jnp.dot(a_ref[...].astype(BF16), w_ref[...],
                                                        preferred_element_type=F32)


def _linres_call(a, w, res, gate, tm):
    bsz, seq, kdim = a.shape
    d = res.shape[2]
    return pl.pallas_call(
        _linres_kernel,
        grid=(bsz, seq // tm),
        in_specs=[pl.BlockSpec((None, tm, kdim), lambda b, i: (b, i, 0)),
                  pl.BlockSpec((kdim, d), lambda b, i: (0, 0)),
                  pl.BlockSpec((None, tm, d), lambda b, i: (b, i, 0)),
                  pl.BlockSpec((None, 1, d), lambda b, i: (b, 0, 0))],
        out_specs=pl.BlockSpec((None, tm, d), lambda b, i: (b, i, 0)),
        out_shape=jax.ShapeDtypeStruct((bsz, seq, d), F32),
        compiler_params=_cparams(("parallel", "parallel")),
        name="linres",
    )(a, w, res, gate)


def _aprep_kernel(*refs, rope, want_q):
    if rope:
        x_ref, qg_ref, kg_ref, cos_ref, sin_ref = refs[:5]
        outs = refs[5:]
    else:
        x_ref, qg_ref, kg_ref = refs[:3]
        outs = refs[3:]
    if want_q:
        q_ref, k_ref, v_ref = outs
    else:
        k_ref, v_ref = outs

    if rope:
        cos = cos_ref[...]
        sin = sin_ref[...]

    def norm_rope(x, g):
        y = _rms(x) * g
        if rope:
            y = y * cos + pltpu.roll(y, HEAD_DIM // 2, 1) * sin
        return y

    if want_q:
        for h in range(N_Q_HEADS):
            x = x_ref[:, h * HEAD_DIM:(h + 1) * HEAD_DIM].astype(F32)
            q_ref[h] = (norm_rope(x, qg_ref[...]) * (HEAD_DIM ** -0.5 * LOG2E)).astype(BF16)
    for h in range(N_KV_HEADS):
        c0 = (N_Q_HEADS + h) * HEAD_DIM
        k_ref[h] = norm_rope(x_ref[:, c0:c0 + HEAD_DIM].astype(F32), kg_ref[...]).astype(BF16)
        c1 = (N_Q_HEADS + N_KV_HEADS + h) * HEAD_DIM
        v_ref[h, 0:HEAD_DIM, :] = x_ref[:, c1:c1 + HEAD_DIM].astype(F32).T.astype(BF16)
        pad_row = lax.broadcasted_iota(jnp.int32, (VT_PAD, x_ref.shape[0]), 0)
        v_ref[h, HEAD_DIM:HEAD_DIM + VT_PAD, :] = jnp.where(pad_row == 0, 1.0, 0.0).astype(BF16)


def _aprep_call(qkv, qg, kg, rope_tabs, want_q, tm):
    bsz, seq, cols = qkv.shape
    rope = rope_tabs is not None
    in_specs = [pl.BlockSpec((None, tm, cols), lambda b, i: (b, i, 0)),
                pl.BlockSpec((1, HEAD_DIM), lambda b, i: (0, 0)),
                pl.BlockSpec((1, HEAD_DIM), lambda b, i: (0, 0))]
    args = [qkv, qg.reshape(1, HEAD_DIM), kg.reshape(1, HEAD_DIM)]
    if rope:
        in_specs += [pl.BlockSpec((tm, HEAD_DIM), lambda b, i: (i, 0))] * 2
        args += list(rope_tabs)
    hspec = lambda n: pl.BlockSpec((None, n, tm, HEAD_DIM), lambda b, i: (b, 0, i, 0))
    hshape = lambda n: jax.ShapeDtypeStruct((bsz, n, seq, HEAD_DIM), BF16)
    tspec = lambda n: pl.BlockSpec((None, n, HEAD_DIM + VT_PAD, tm), lambda b, i: (b, 0, 0, i))
    tshape = lambda n: jax.ShapeDtypeStruct((bsz, n, HEAD_DIM + VT_PAD, seq), BF16)
    out_specs = [hspec(N_KV_HEADS), tspec(N_KV_HEADS)]
    out_shape = [hshape(N_KV_HEADS), tshape(N_KV_HEADS)]
    if want_q:
        out_specs = [hspec(N_Q_HEADS)] + out_specs
        out_shape = [hshape(N_Q_HEADS)] + out_shape
    return pl.pallas_call(
        functools.partial(_aprep_kernel, rope=rope, want_q=want_q),
        grid=(bsz, seq // tm),
        in_specs=in_specs,
        out_specs=out_specs,
        out_shape=out_shape,
        compiler_params=_cparams(("parallel", "parallel")),
        name="aprep_x" if rope else "aprep_ctx",
    )(*args)


def _rope_tables(n_tokens):
    rows = n_tokens // GRID_W
    axis = HEAD_DIM // 2
    inv_freq = ROPE_THETA ** (-jnp.arange(0, axis, 2, dtype=F32) / axis)
    ang_r = jnp.arange(rows).astype(F32)[:, None] * inv_freq[None]
    ang_c = jnp.arange(GRID_W).astype(F32)[:, None] * inv_freq[None]
    cr, sr = (jnp.repeat(t, GRID_W, axis=0) for t in (jnp.cos(ang_r), jnp.sin(ang_r)))
    cc, sc = (jnp.tile(t, (rows, 1)) for t in (jnp.cos(ang_c), jnp.sin(ang_c)))
    cos = jnp.concatenate([cr, cc, cr, cc], axis=-1)
    sin = jnp.concatenate([-sr, -sc, sr, sc], axis=-1)
    return cos, sin


def _rope_perm():
    quarter = HEAD_DIM // 4
    return jnp.concatenate([jnp.arange(0, quarter), jnp.arange(2 * quarter, 3 * quarter),
                            jnp.arange(quarter, 2 * quarter), jnp.arange(3 * quarter, 4 * quarter)])


def _flash_kernel(q_ref, k_ref, vt_ref, o_ref, s0_ref, sa_ref, sb_ref, m0_ref, ma_ref, mb_ref, *,
                  tq, tk, nk, nt):
    m_rows = Q_PER_KV * tq

    def q_tile(i):
        r0 = pl.multiple_of(i * tq, tq)
        return q_ref[:, pl.ds(r0, tq), :].reshape(m_rows, HEAD_DIM)

    def scores(q, j, s_ref, cmax_ref):
        c0 = pl.multiple_of(j * tk, tk)
        st = lax.dot_general(k_ref[pl.ds(c0, tk), :], q, (((1,), (1,)), ((), ())),
                             preferred_element_type=F32)
        s_ref[...] = st
        cmax_ref[...] = jnp.max(st, axis=0, keepdims=True)

    def update(j, s_ref, cmax_ref, carry):
        m, acc = carry
        m_new = jnp.maximum(m, cmax_ref[...])
        alpha = jnp.exp2(m - m_new)
        p = jnp.exp2(s_ref[...] - m_new)
        vt = vt_ref[:, pl.ds(pl.multiple_of(j * tk, tk), tk)]
        acc = alpha * acc + jnp.dot(vt, p.astype(BF16), preferred_element_type=F32)
        return m_new, acc

    bufs = ((sa_ref, ma_ref), (sb_ref, mb_ref))

    def tile_body(i, _):
        q = q_tile(i)
        q_next = q_tile(jnp.minimum(i + 1, nt - 1))

        def run(j0, count, carry, last):
            for u in range(count):
                j = j0 + u
                if u + 1 < count or not last:
                    scores(q, j + 1, *bufs[(u + 1) % 2])
                else:
                    scores(q_next, 0, s0_ref, m0_ref)
                carry = update(j, *bufs[u % 2], carry)
            return carry

        carry = (jnp.full((1, m_rows), NEG_BIG, F32), jnp.zeros((HEAD_DIM + VT_PAD, m_rows), F32))
        scores(q, 1, sa_ref, ma_ref)
        carry = update(0, s0_ref, m0_ref, carry)
        rest = nk - 1
        niter = (rest - 1) // FLASH_UNROLL
        carry = lax.fori_loop(0, niter, lambda jj, c: run(1 + jj * FLASH_UNROLL, FLASH_UNROLL, c, False),
                              carry)
        _, acc = run(1 + niter * FLASH_UNROLL, rest - niter * FLASH_UNROLL, carry, True)
        o = (acc[:HEAD_DIM] / acc[HEAD_DIM:HEAD_DIM + 1]).T
        r0 = pl.multiple_of(i * tq, tq)
        for r in range(Q_PER_KV):
            o_ref[pl.ds(r0, tq), r * HEAD_DIM:(r + 1) * HEAD_DIM] = o[r * tq:(r + 1) * tq].astype(o_ref.dtype)
        return 0

    scores(q_tile(0), 0, s0_ref, m0_ref)
    lax.fori_loop(0, nt, tile_body, 0)


def _flash_call(q, k, vt, tq, tk):
    bsz, _, seq, _ = q.shape
    lk = k.shape[2]
    nk = lk // tk
    assert nk >= 2, "the score pipeline needs at least two key blocks"
    q5 = q.reshape(bsz, N_KV_HEADS, Q_PER_KV, seq, HEAD_DIM)
    return pl.pallas_call(
        functools.partial(_flash_kernel, tq=tq, tk=tk, nk=nk, nt=seq // tq),
        grid=(bsz, N_KV_HEADS),
        in_specs=[pl.BlockSpec((None, None, Q_PER_KV, seq, HEAD_DIM), lambda b, g: (b, g, 0, 0, 0)),
                  pl.BlockSpec((None, None, lk, HEAD_DIM), lambda b, g: (b, g, 0, 0)),
                  pl.BlockSpec((None, None, HEAD_DIM + VT_PAD, lk), lambda b, g: (b, g, 0, 0))],
        out_specs=pl.BlockSpec((None, seq, Q_PER_KV * HEAD_DIM), lambda b, g: (b, 0, g)),
        out_shape=jax.ShapeDtypeStruct((bsz, seq, N_Q_HEADS * HEAD_DIM), BF16),
        scratch_shapes=([pltpu.VMEM((tk, Q_PER_KV * tq), F32)] * 3
                        + [pltpu.VMEM((1, Q_PER_KV * tq), F32)] * 3),
        compiler_params=_cparams(("parallel", "parallel")),
        name="flash",
    )(q5, k, vt)


def _row_tile(seq, cap):
    t = min(seq, cap)
    while seq % t:
        t //= 2
    return t


def _even_weights(w_in, lru_wa, lru_wi, dn_alog, dn_dtb, dn_g):
    d, cols = w_in.shape
    ngate = 4 * DN_HEADS
    main = cols - ngate
    w_gates = jnp.concatenate([w_in[:, main:], jnp.zeros((d, LANES - ngate), F32)], axis=1)

    def blockdiag(w):
        w = w * (-LOG2E)
        z = jnp.zeros_like(w[:, 0::2])
        top = jnp.concatenate([w[:, 0::2], z], axis=-1)
        bot = jnp.concatenate([z, w[:, 1::2]], axis=-1)
        return jnp.concatenate([top, bot], axis=-2).astype(BF16)

    alog = dn_alog.reshape(-1)
    dtb = dn_dtb.reshape(-1)
    cvec = jnp.zeros((SUBLANES, LANES), F32)
    cvec = cvec.at[0, 2 * DN_HEADS:ngate].set(alog).at[1, 2 * DN_HEADS:ngate].set(dtb)

    ccol = jnp.zeros((2, ngate), F32).at[0, 2 * DN_HEADS:].set(alog).at[1, 2 * DN_HEADS:].set(dtb)

    return dict(w_in=w_in.astype(BF16)[:, :main], w_gates=w_gates.astype(BF16),
                w_gates_t=w_in[:, main:].T.astype(BF16),
                wa=blockdiag(lru_wa), wi=blockdiag(lru_wi), cvec=cvec, ccol=ccol,
                dn_g=jnp.tile(dn_g, DN_HEADS).reshape(1, -1))


def _even_stream(h_in, res, gate, norm_g, sc, sh, ew, lru_cw, lru_cb, lru_ba, lru_bi, lru_lam, dn_cw,
                 w_out, h0, s0, tm_in, tm, rows, dn_rows):
    width = lru_cw.shape[1]
    ngate = 4 * DN_HEADS
    proj, gates, gates_t = _normlin_call(h_in, norm_g, sc, sh, ew["w_in"], ew["w_gates"], ew["w_gates_t"],
                                         tm_in)
    lru, h_t = _lru_call(proj, lru_cw, lru_cb, ew["wa"], ew["wi"], lru_ba, lru_bi, lru_lam, h0,
                         _row_tile(proj.shape[1], LRU_ROWS))
    qkv = _dnprep_call(proj, dn_cw, 2 * width // LANES, _row_tile(proj.shape[1], DNPREP_ROWS))
    ccol = jnp.broadcast_to(ew["ccol"][:, :, None], (2, ngate, dn_rows))
    o_f, o_b, s_t = _delta_call(qkv, gates, gates_t, ew["cvec"], ccol, s0, dn_rows)
    z_block = (2 * width + dn_cw.shape[1]) // (DN_HEADS * DN_DK)
    out = _evenout_call(lru, o_f, o_b, proj, z_block, ew["dn_g"], w_out, res, gate, tm)
    return out, h_t, s_t


def kernel(x, c, ctx, c_ctx, norm_mix_g, norm_mlp_g, w_mod, b_mod, mlp_w1, mlp_w2, even_w_in, lru_conv_w,
           lru_conv_b, lru_w_a, lru_b_a, lru_w_i, lru_b_i, lru_lambda, dn_conv_w, dn_a_log, dn_dt_bias,
           dn_norm_g, even_w_out, attn_w_qkv, attn_q_norm_g, attn_k_norm_g, attn_w_o, final_norm_g):
    bsz, seq, d = x.shape
    lc = ctx.shape[1]
    depth = w_mod.shape[0]
    tm_x = _row_tile(seq, 512)
    tm_c = _row_tile(lc, 512)
    tm_mlp = _row_tile(seq, 1024)
    rows_x = _row_tile(seq, 256)
    rows_c = _row_tile(lc, 256)

    cc = jnp.zeros((SUBLANES, d), F32).at[:bsz].set(c).at[bsz].set(c_ctx)
    mod = _mod_call(cc, w_mod, b_mod).reshape(depth, SUBLANES, 6, d)

    rope = _rope_tables(seq)
    xs, cs = x, ctx
    for l in range(depth):
        last = l == depth - 1
        j = l // 2
        mx = [mod[l, :bsz, k][:, None, :] for k in range(6)]
        mc = [jnp.broadcast_to(mod[l, bsz, k][None, None, :], (bsz, 1, d)) for k in range(6)]
        sh1, sc1, g1, sh2, sc2, g2 = mx
        csh1, csc1, cg1, csh2, csc2, cg2 = mc
        w1 = mlp_w1[l].astype(BF16)
        w2 = mlp_w2[l].astype(BF16)
        if l % 2 == 0:
            ew = _even_weights(even_w_in[j], lru_w_a[j], lru_w_i[j], dn_a_log[j], dn_dt_bias[j], dn_norm_g[j])
            w_out = even_w_out[j].astype(BF16)
            h0 = jnp.zeros((bsz, 2, lru_conv_w.shape[2]), F32)
            s0 = jnp.zeros((bsz, DN_HEADS, 2, DN_DK, DN_DK), F32)
            common = (ew, lru_conv_w[j], lru_conv_b[j], lru_b_a[j], lru_b_i[j], lru_lambda[j], dn_conv_w[j], w_out)
            cs_mix, h_t, s_t = _even_stream(cs, cs, cg1, norm_mix_g[l], csc1, csh1, *common, h0, s0,
                                            tm_c, tm_c, rows_c, rows_c)
            xs, _, _ = _even_stream(xs, xs, g1, norm_mix_g[l], sc1, sh1, *common, h_t, s_t,
                                    tm_mlp, tm_mlp, rows_x, _row_tile(seq, DELTA_ROWS))
        else:
            perm = _rope_perm()
            nqk = (N_Q_HEADS + N_KV_HEADS) * HEAD_DIM
            w_qk = attn_w_qkv[j][:, :nqk].reshape(d, N_Q_HEADS + N_KV_HEADS, HEAD_DIM)[:, :, perm]
            w_qkv = jnp.concatenate([w_qk.reshape(d, nqk), attn_w_qkv[j][:, nqk:]], axis=1).astype(BF16)
            q_g = attn_q_norm_g[j][perm]
            k_g = attn_k_norm_g[j][perm]
            w_o = attn_w_o[j].astype(BF16)
            qkv_c = _normlin_call(cs, norm_mix_g[l], csc1, csh1, w_qkv, None, None, tm_c)
            qkv_x = _normlin_call(xs, norm_mix_g[l], sc1, sh1, w_qkv, None, None, tm_mlp)
            if last:
                k_c, vt_c = _aprep_call(qkv_c, q_g, k_g, None, False, rows_c)
            else:
                qt_c, k_c, vt_c = _aprep_call(qkv_c, q_g, k_g, None, True, rows_c)
            qt_x, k_x, vt_x = _aprep_call(qkv_x, q_g, k_g, rope, True, tm_x)
            k_all = jnp.concatenate([k_c, k_x], axis=2)
            vt_all = jnp.concatenate([vt_c, vt_x], axis=3)
            o_x = _flash_call(qt_x, k_all, vt_all, _row_tile(seq, FLASH_TQ), _col_tile(lc + seq, FLASH_TK_CAP))
            xs = _linres_call(o_x, w_o, xs, g1, tm_mlp)
            if not last:
                o_c = _flash_call(qt_c, k_c, vt_c, rows_c, _col_tile(lc, min(FLASH_TK_CAP, lc // 2)))
                cs_mix = _linres_call(o_c, w_o, cs, cg1, tm_c)
        xs = _mlp_call(xs, norm_mlp_g[l], sc2, sh2, g2, w1, w2, final_norm_g, tm_mlp, last)
        if not last:
            cs = _mlp_call(cs_mix, norm_mlp_g[l], csc2, csh2, cg2, w1, w2, final_norm_g, tm_c, False)
    return xs
```

```python
import functools

import jax
import jax.numpy as jnp
from jax import lax
from jax.experimental import pallas as pl
from jax.experimental.pallas import tpu as pltpu

F32 = jnp.float32
BF16 = jnp.bfloat16

EPS = 1e-6
LANES = 128
SUBLANES = 8
VMEM_LIMIT = 56 * 1024 * 1024

LRU_BLOCK = 64
LRU_C = 8.0
CONV_WIDTH = 4
CONV_LEFT = 2
DN_HEADS = 8
DN_DK = 128
DN_CHUNK = 64
N_Q_HEADS = 8
N_KV_HEADS = 4
Q_PER_KV = N_Q_HEADS // N_KV_HEADS
HEAD_DIM = 128
GRID_W = 64
ROPE_THETA = 10000.0
NEG_BIG = -1e30
LOG2E = 1.4426950408889634
DNPREP_ROWS = 1024
LRU_ROWS = 1024
DELTA_ROWS = 256
FLASH_TQ = 256
FLASH_TK_CAP = 768
VT_PAD = 16
FLASH_UNROLL = 4


def _cparams(sem):
    return pltpu.CompilerParams(dimension_semantics=sem, vmem_limit_bytes=VMEM_LIMIT)


def _bdot(a, b):
    return jnp.dot(a.astype(BF16), b.astype(BF16), preferred_element_type=F32)


def _bdot_nt(a, b):
    return lax.dot_general(a.astype(BF16), b.astype(BF16), (((1,), (1,)), ((), ())),
                           preferred_element_type=F32)


def _bdot_tn(a, b):
    return lax.dot_general(a.astype(BF16), b.astype(BF16), (((0,), (0,)), ((), ())),
                           preferred_element_type=F32)


def _sigmoid(x):
    return jax.nn.sigmoid(x)


def _softplus(x):
    return jnp.maximum(x, 0.0) + jnp.log1p(jnp.exp(-jnp.abs(x)))


def _rms(x):
    return x * lax.rsqrt(jnp.mean(x * x, axis=-1, keepdims=True) + EPS)


def _col_tile(n, cap):
    best = LANES
    t = LANES
    while t <= min(n, cap):
        if n % t == 0:
            best = t
        t += LANES
    return best


def _mod_kernel(c_ref, w_ref, b_ref, o_ref):
    s = c_ref[...]
    s = s * _sigmoid(s)
    o_ref[...] = _bdot(s, w_ref[...]) + b_ref[...]


def _mod_call(cc, w_mod, b_mod):
    depth, d, n = w_mod.shape
    tn = _col_tile(n, 1536)
    return pl.pallas_call(
        _mod_kernel,
        grid=(depth, n // tn),
        in_specs=[pl.BlockSpec((SUBLANES, d), lambda l, j: (0, 0)),
                  pl.BlockSpec((None, d, tn), lambda l, j: (l, 0, j)),
                  pl.BlockSpec((None, 1, tn), lambda l, j: (l, 0, j))],
        out_specs=pl.BlockSpec((None, SUBLANES, tn), lambda l, j: (l, 0, j)),
        out_shape=jax.ShapeDtypeStruct((depth, SUBLANES, n), F32),
        compiler_params=_cparams(("parallel", "parallel")),
        name="mod",
    )(cc, w_mod, b_mod.reshape(depth, 1, n))


def _normlin_kernel(*refs, has_gates):
    if has_gates:
        x_ref, g_ref, sc_ref, sh_ref, w_ref, wg_ref, wgt_ref, o_ref, og_ref, ogt_ref, h_ref = refs
    else:
        x_ref, g_ref, sc_ref, sh_ref, w_ref, o_ref, h_ref = refs

    @pl.when(pl.program_id(2) == 0)
    def _():
        y = _rms(x_ref[...]) * g_ref[...]
        h_ref[...] = (y * (1.0 + sc_ref[...]) + sh_ref[...]).astype(BF16)
        if has_gates:
            og_ref[...] = jnp.dot(h_ref[...], wg_ref[...], preferred_element_type=F32)
            ogt_ref[...] = lax.dot_general(wgt_ref[...], h_ref[...], (((1,), (1,)), ((), ())),
                                           preferred_element_type=F32)

    o_ref[...] = jnp.dot(h_ref[...], w_ref[...], preferred_element_type=F32).astype(o_ref.dtype)


def _normlin_call(x, g, sc, sh, w, w_gates, w_gates_t, tm):
    bsz, seq, d = x.shape
    n = w.shape[1]
    tn = _col_tile(n, 2048)
    has_gates = w_gates is not None
    in_specs = [pl.BlockSpec((None, tm, d), lambda b, i, j: (b, i, 0)),
                pl.BlockSpec((1, d), lambda b, i, j: (0, 0)),
                pl.BlockSpec((None, 1, d), lambda b, i, j: (b, 0, 0)),
                pl.BlockSpec((None, 1, d), lambda b, i, j: (b, 0, 0)),
                pl.BlockSpec((d, tn), lambda b, i, j: (0, j))]
    out_specs = [pl.BlockSpec((None, tm, tn), lambda b, i, j: (b, i, j))]
    out_shape = [jax.ShapeDtypeStruct((bsz, seq, n), BF16)]
    args = [x, g.reshape(1, d), sc, sh, w]
    if has_gates:
        ng = w_gates.shape[1]
        in_specs.append(pl.BlockSpec((d, ng), lambda b, i, j: (0, 0)))
        out_specs.append(pl.BlockSpec((None, tm, ng), lambda b, i, j: (b, i, 0)))
        out_shape.append(jax.ShapeDtypeStruct((bsz, seq, ng), F32))
        ngt = w_gates_t.shape[0]
        in_specs.append(pl.BlockSpec((ngt, d), lambda b, i, j: (0, 0)))
        out_specs.append(pl.BlockSpec((None, ngt, tm), lambda b, i, j: (b, 0, i)))
        out_shape.append(jax.ShapeDtypeStruct((bsz, ngt, seq), F32))
        args += [w_gates, w_gates_t]
    out = pl.pallas_call(
        functools.partial(_normlin_kernel, has_gates=has_gates),
        grid=(bsz, seq // tm, n // tn),
        in_specs=in_specs,
        out_specs=out_specs,
        out_shape=out_shape,
        scratch_shapes=[pltpu.VMEM((tm, d), BF16)],
        compiler_params=_cparams(("parallel", "parallel", "arbitrary")),
        name="normlin",
    )(*args)
    return out if has_gates else out[0]


def _mlp_kernel(x_ref, g_ref, sc_ref, sh_ref, gate_ref, w1_ref, w2_ref, fg_ref, o_ref,
                h_ref, acc_ref, *, final_norm):
    k = pl.program_id(2)

    @pl.when(k == 0)
    def _():
        y = _rms(x_ref[...]) * g_ref[...]
        h_ref[...] = (y * (1.0 + sc_ref[...]) + sh_ref[...]).astype(BF16)
        acc_ref[...] = jnp.zeros_like(acc_ref)

    a = jnp.dot(h_ref[...], w1_ref[...], preferred_element_type=F32)
    a = jnp.square(jnp.maximum(a, 0.0)).astype(BF16)
    acc_ref[...] += jnp.dot(a, w2_ref[...], preferred_element_type=F32)

    @pl.when(k == pl.num_programs(2) - 1)
    def _():
        y = x_ref[...] + gate_ref[...] * acc_ref[...]
        if final_norm:
            y = _rms(y) * fg_ref[...]
        o_ref[...] = y


def _mlp_call(x, g, sc, sh, gate, w1, w2, fg, tm, final_norm):
    bsz, seq, d = x.shape
    f = w1.shape[1]
    tf = _col_tile(f, 2048)
    vec = pl.BlockSpec((None, 1, d), lambda b, i, k: (b, 0, 0))
    return pl.pallas_call(
        functools.partial(_mlp_kernel, final_norm=final_norm),
        grid=(bsz, seq // tm, f // tf),
        in_specs=[pl.BlockSpec((None, tm, d), lambda b, i, k: (b, i, 0)),
                  pl.BlockSpec((1, d), lambda b, i, k: (0, 0)),
                  vec, vec, vec,
                  pl.BlockSpec((d, tf), lambda b, i, k: (0, k)),
                  pl.BlockSpec((tf, d), lambda b, i, k: (k, 0)),
                  pl.BlockSpec((1, d), lambda b, i, k: (0, 0))],
        out_specs=pl.BlockSpec((None, tm, d), lambda b, i, k: (b, i, 0)),
        out_shape=jax.ShapeDtypeStruct((bsz, seq, d), F32),
        scratch_shapes=[pltpu.VMEM((tm, d), BF16), pltpu.VMEM((tm, d), F32)],
        compiler_params=_cparams(("parallel", "parallel", "arbitrary")),
        name="mlp",
    )(x, g.reshape(1, d), sc, sh, gate, w1, w2, fg.reshape(1, d))


CONV_PAD = SUBLANES


def _fill_padded(xpad_ref, x_ref, seq, rows):
    zeros = jnp.zeros((CONV_PAD, xpad_ref.shape[1]), F32)
    xpad_ref[0:CONV_PAD, :] = zeros
    xpad_ref[CONV_PAD + seq:CONV_PAD + seq + CONV_PAD, :] = zeros

    def body(t, carry):
        r0 = pl.multiple_of(t * rows, rows)
        xpad_ref[pl.ds(r0 + CONV_PAD, rows), :] = x_ref[pl.ds(r0, rows), :].astype(F32)
        return carry

    lax.fori_loop(0, seq // rows, body, 0)


def _conv_tile(xpad_ref, r0, rows, w):
    acc = None
    for j in range(CONV_WIDTH):
        off = j - CONV_LEFT
        term = xpad_ref[pl.ds(r0 + (CONV_PAD + off), rows), :] * w[j:j + 1]
        acc = term if acc is None else acc + term
    return acc


def _dnprep_kernel(x_ref, w_ref, o_ref, xpad_ref, *, seq, rows):
    j = pl.program_id(1)
    scale = jnp.where(j < DN_HEADS, DN_DK ** -0.5, 1.0).astype(F32)
    _fill_padded(xpad_ref, x_ref, seq, rows)
    w = w_ref[...]

    def conv_silu(t):
        r0 = pl.multiple_of(t * rows, rows)
        y = _conv_tile(xpad_ref, r0, rows, w)
        return r0, y * _sigmoid(y)

    def qk_body(t, carry):
        r0, y = conv_silu(t)
        nrm = (y * lax.rsqrt(jnp.sum(y * y, axis=-1, keepdims=True) + EPS)) * scale
        o_ref[pl.ds(r0, rows), :] = nrm.astype(o_ref.dtype)
        return carry

    def v_body(t, carry):
        r0, y = conv_silu(t)
        o_ref[pl.ds(r0, rows), :] = y.astype(o_ref.dtype)
        return carry

    @pl.when(j < 2 * DN_HEADS)
    def _():
        lax.fori_loop(0, seq // rows, qk_body, 0)

    @pl.when(j >= 2 * DN_HEADS)
    def _():
        lax.fori_loop(0, seq // rows, v_body, 0)


def _dnprep_call(proj, conv_w, col0_blocks, rows):
    bsz, seq, _ = proj.shape
    ncol = conv_w.shape[1] // LANES
    return pl.pallas_call(
        functools.partial(_dnprep_kernel, seq=seq, rows=rows),
        grid=(bsz, ncol),
        in_specs=[pl.BlockSpec((None, seq, LANES), lambda b, j: (b, 0, col0_blocks + j)),
                  pl.BlockSpec((CONV_WIDTH, LANES), lambda b, j: (0, j))],
        out_specs=pl.BlockSpec((None, seq, LANES), lambda b, j: (b, 0, j)),
        out_shape=jax.ShapeDtypeStruct((bsz, seq, ncol * LANES), BF16),
        scratch_shapes=[pltpu.VMEM((seq + 2 * CONV_PAD, LANES), F32)],
        compiler_params=_cparams(("parallel", "parallel")),
        name="dnprep",
    )(proj, conv_w)


def _scan_tile(a, b, carry, rev, rows):
    nv = rows // SUBLANES
    lanes = a.shape[1]
    a = a.reshape(nv, SUBLANES, lanes)
    b = b.reshape(nv, SUBLANES, lanes)
    row = lax.broadcasted_iota(jnp.int32, a.shape, 1)
    s = 1
    while s < SUBLANES:
        shift = SUBLANES - s if rev else s
        a_s = pltpu.roll(a, shift, 1)
        b_s = pltpu.roll(b, shift, 1)
        valid = (row < SUBLANES - s) if rev else (row >= s)
        b = b + a * jnp.where(valid, b_s, 0.0)
        a = a * jnp.where(valid, a_s, 1.0)
        s *= 2
    hs = [None] * nv
    for v in (range(nv - 1, -1, -1) if rev else range(nv)):
        h = a[v] * carry + b[v]
        carry = h[0:1] if rev else h[SUBLANES - 1:SUBLANES]
        hs[v] = h
    return jnp.concatenate(hs, axis=0), carry


def _gelu_tanh(x):
    return 0.5 * x * (1.0 + jnp.tanh(0.7978845608028654 * (x + 0.044715 * (x * x * x))))


def _lru_kernel(x_ref, y_ref, cw_ref, cb_ref, wa_ref, wi_ref, ba_ref, bi_ref, lam_ref, h0_ref,
                o_ref, ht_ref, xpad_ref, xc_ref, hf_ref, *, seq, rows):
    _fill_padded(xpad_ref, x_ref, seq, rows)
    cw = cw_ref[...]
    cb = cb_ref[...]
    nt = seq // rows

    def gates(xc, d):
        r = 1.0 / (1.0 + jnp.exp2(_bdot(xc, wa_ref[d]) - LOG2E * ba_ref[d:d + 1, :]))
        i = 1.0 / (1.0 + jnp.exp2(_bdot(xc, wi_ref[d]) - LOG2E * bi_ref[d:d + 1, :]))
        neg_rate = LRU_C * _softplus(-lam_ref[d:d + 1, :])
        a = jnp.exp2(r * (-LOG2E * neg_rate))
        b = jnp.sqrt(jnp.tanh(r * neg_rate) * (a * a + 1.0)) * (i * xc)
        return a, b

    def fwd_body(t, carry):
        r0 = pl.multiple_of(t * rows, rows)
        xc = _conv_tile(xpad_ref, r0, rows, cw) + cb
        xc_ref[pl.ds(r0, rows), :] = xc
        a, b = gates(xc, 0)
        h, carry = _scan_tile(a, b, carry, False, rows)
        hf_ref[pl.ds(r0, rows), :] = h
        return carry

    carry_f = lax.fori_loop(0, nt, fwd_body, h0_ref[0:1, :])

    def bwd_body(t, carry):
        r0 = pl.multiple_of((nt - 1 - t) * rows, rows)
        xc = xc_ref[pl.ds(r0, rows), :]
        a, b = gates(xc, 1)
        h, carry = _scan_tile(a, b, carry, True, rows)
        y = y_ref[pl.ds(r0, rows), :].astype(F32)
        o_ref[pl.ds(r0, rows), :] = ((hf_ref[pl.ds(r0, rows), :] + h) * _gelu_tanh(y)).astype(o_ref.dtype)
        return carry

    carry_b = lax.fori_loop(0, nt, bwd_body, h0_ref[1:2, :])
    ht_ref[0:1, :] = carry_f
    ht_ref[1:2, :] = carry_b


def _lru_call(proj, cw, cb, wa, wi, ba, bi, lam, h0, rows):
    bsz, seq, _ = proj.shape
    width = cw.shape[1]
    ncol = width // LANES
    vec2 = pl.BlockSpec((2, LANES), lambda b, j: (0, j))
    wspec = pl.BlockSpec((2, None, LANES, LANES), lambda b, j: (0, j, 0, 0))
    return pl.pallas_call(
        functools.partial(_lru_kernel, seq=seq, rows=rows),
        grid=(bsz, ncol),
        in_specs=[pl.BlockSpec((None, seq, LANES), lambda b, j: (b, 0, j)),
                  pl.BlockSpec((None, seq, LANES), lambda b, j: (b, 0, ncol + j)),
                  pl.BlockSpec((CONV_WIDTH, LANES), lambda b, j: (0, j)),
                  pl.BlockSpec((1, LANES), lambda b, j: (0, j)),
                  wspec, wspec, vec2, vec2, vec2,
                  pl.BlockSpec((None, 2, LANES), lambda b, j: (b, 0, j))],
        out_specs=[pl.BlockSpec((None, seq, LANES), lambda b, j: (b, 0, j)),
                   pl.BlockSpec((None, 2, LANES), lambda b, j: (b, 0, j))],
        out_shape=[jax.ShapeDtypeStruct((bsz, seq, width), BF16),
                   jax.ShapeDtypeStruct((bsz, 2, width), F32)],
        scratch_shapes=[pltpu.VMEM((seq + 2 * CONV_PAD, LANES), F32),
                        pltpu.VMEM((seq, LANES), F32),
                        pltpu.VMEM((seq, LANES), F32)],
        compiler_params=_cparams(("parallel", "parallel")),
        name="lru",
    )(proj, proj, cw, cb.reshape(1, width), wa, wi, ba, bi, lam, h0)


DN_GROUP = 2 * DN_CHUNK


def _split3(x):
    x1 = x.astype(BF16)
    r1 = x - x1.astype(F32)
    x2 = r1.astype(BF16)
    x3 = (r1 - x2.astype(F32)).astype(BF16)
    return x1, x2, x3


def _delta_kernel(*refs, rows):
    (qf_ref, kf_ref, vf_ref, grf_ref, grtf_ref, qb_ref, kb_ref, vb_ref, grb_ref, grtb_ref,
     cvec_ref, ccol_ref, cum_ref, cumt_ref, ones_ref, s0_ref, of_ref, ob_ref, st_ref, s_ref) = refs
    step = pl.program_id(1)
    q_refs, k_refs, v_refs = (qf_ref, qb_ref), (kf_ref, kb_ref), (vf_ref, vb_ref)
    o_refs = (of_ref, ob_ref)
    dirs = (0, 1)

    @pl.when(step == 0)
    def _():
        s_ref[...] = s0_ref[...]

    ones = ones_ref[...]

    def mm(m, parts):
        return sum(jnp.dot(m, p, preferred_element_type=F32) for p in parts)

    sig, gc, e_gc, e_rest, e_gl, gct = [], [], [], [], [], []
    for d, (gr_ref, grt_ref) in enumerate(((grf_ref, grtf_ref), (grb_ref, grtb_ref))):
        raw = gr_ref[...]
        g = -jnp.exp(cvec_ref[0:1, :]) * _softplus(raw + cvec_ref[1:2, :])
        parts = _split3(g)
        gc_d = mm(cum_ref[d], parts)
        gl_d = mm(ones, parts)
        sig.append(_sigmoid(raw))
        gc.append(gc_d)
        e_gc.append(jnp.exp(gc_d))
        e_rest.append(jnp.exp(gl_d - gc_d))
        e_gl.append(jnp.exp(gl_d))
        gt = -jnp.exp(ccol_ref[0]) * _softplus(grt_ref[...] + ccol_ref[1])
        cumt = cumt_ref[d]
        gct.append(sum(jnp.dot(p, cumt, preferred_element_type=F32) for p in _split3(gt)))

    ri = lax.broadcasted_iota(jnp.int32, (DN_GROUP, DN_GROUP), 0)
    ci = lax.broadcasted_iota(jnp.int32, (DN_GROUP, DN_GROUP), 1)
    same = (ri // DN_CHUNK) == (ci // DN_CHUNK)
    incl = (same & (ri >= ci), same & (ri <= ci))
    strict = (same & (ri > ci), same & (ri < ci))

    ngroups = rows // DN_GROUP
    zeros_c = jnp.zeros((DN_CHUNK, DN_DK), F32)

    def bcast(x, rsl, lane):
        return jnp.broadcast_to(x[rsl, lane:lane + 1], (DN_GROUP, LANES))

    heads = range(DN_HEADS)
    groups = (list(range(ngroups)), list(range(ngroups - 1, -1, -1)))
    units = [(d, gi, h) for d in dirs for gi in groups[d] for h in heads]
    qd, kd, egl_b, rhs, kk, qk = {}, {}, {}, {}, {}, {}
    for u in units:
        d, gi, h = u
        rsl = slice(gi * DN_GROUP, (gi + 1) * DN_GROUP)
        lb = d * DN_HEADS + h
        lg = 2 * DN_HEADS + d * DN_HEADS + h
        csl = slice(h * DN_DK, (h + 1) * DN_DK)
        q16 = q_refs[d][rsl, csl]
        k16 = k_refs[d][rsl, csl]
        q_h = q16.astype(F32)
        k_h = k16.astype(F32)
        beta_b = bcast(sig[d], rsl, lb)
        egc_b = bcast(e_gc[d], rsl, lg)
        kb_h = k_h * beta_b
        kk[u] = _bdot_nt(kb_h, k16)
        qk[u] = _bdot_nt(q16, k16)
        rhs[u] = jnp.concatenate([v_refs[d][rsl, csl].astype(F32) * beta_b, kb_h * egc_b], axis=1)
        qd[u] = q_h * egc_b
        kd[u] = k_h * bcast(e_rest[d], rsl, lg)
        egl_b[u] = bcast(e_gl[d], rsl, lg)
    p, a16, attn = {}, {}, {}
    for u in units:
        d, gi, h = u
        rsl = slice(gi * DN_GROUP, (gi + 1) * DN_GROUP)
        lg = 2 * DN_HEADS + d * DN_HEADS + h
        diff = bcast(gc[d], rsl, lg) - gct[d][lg:lg + 1, rsl]
        decay = jnp.exp(jnp.where(incl[d], diff, NEG_BIG))
        m = jnp.where(strict[d], kk[u] * decay, 0.0)
        attn[u] = qk[u] * decay
        p[u] = -m
        a16[u] = m.astype(BF16)
    npow = 1
    while npow < DN_CHUNK // 2:
        a = {u: jnp.dot(a16[u], a16[u], preferred_element_type=F32) for u in units}
        a16 = {u: a[u].astype(BF16) for u in units}
        p = {u: p[u] + a[u] + jnp.dot(p[u].astype(BF16), a16[u], preferred_element_type=F32) for u in units}
        npow *= 2
    uw = {u: rhs[u] + _bdot(p[u], rhs[u]) for u in units}
    chains = [(d, h) for d in dirs for h in heads]
    s = {c: s_ref[c[1], c[0]] for c in chains}
    slots = tuple([(gi, cc) for gi in groups[d] for cc in ((1, 0) if d else (0, 1))] for d in dirs)
    for slot in range(2 * ngroups):
        r2, v_new = {}, {}
        for c in chains:
            d, h = c
            gi, cc = slots[d][slot]
            cs = slice(cc * DN_CHUNK, (cc + 1) * DN_CHUNK)
            r2[c] = _bdot(jnp.concatenate([uw[d, gi, h][cs, DN_DK:], qd[d, gi, h][cs]], axis=0), s[c])
        for c in chains:
            d, h = c
            gi, cc = slots[d][slot]
            cs = slice(cc * DN_CHUNK, (cc + 1) * DN_CHUNK)
            v_new[c] = uw[d, gi, h][cs, :DN_DK] - r2[c][:DN_CHUNK]
        for c in chains:
            d, h = c
            gi, cc = slots[d][slot]
            c0 = cc * DN_CHUNK
            cs = slice(c0, c0 + DN_CHUNK)
            v_pad = (jnp.concatenate([v_new[c], zeros_c], axis=0) if cc == 0
                     else jnp.concatenate([zeros_c, v_new[c]], axis=0))
            o_c = r2[c][DN_CHUNK:] + _bdot(attn[d, gi, h][cs], v_pad)
            o_refs[d][gi * DN_GROUP + c0:gi * DN_GROUP + c0 + DN_CHUNK,
                      h * DN_DK:(h + 1) * DN_DK] = o_c.astype(o_refs[d].dtype)
        for c in chains:
            d, h = c
            gi, cc = slots[d][slot]
            cs = slice(cc * DN_CHUNK, (cc + 1) * DN_CHUNK)
            egl = jnp.concatenate([egl_b[d, gi, h][cs], egl_b[d, gi, h][cs]], axis=0)
            s[c] = s[c] * egl + _bdot_tn(kd[d, gi, h][cs], v_new[c])
    for c in chains:
        s_ref[c[1], c[0]] = s[c]

    @pl.when(step == pl.num_programs(1) - 1)
    def _():
        st_ref[...] = s_ref[...]


def _delta_consts(rows):
    t = jnp.arange(rows)
    same = (t[:, None] // DN_CHUNK) == (t[None, :] // DN_CHUNK)
    cum = jnp.stack([same & (t[:, None] >= t[None, :]), same & (t[:, None] <= t[None, :])])
    return cum.astype(BF16), jnp.swapaxes(cum, 1, 2).astype(BF16), same.astype(BF16)


def _delta_call(qkv, gates, gates_t, cvec, ccol, s0, rows):
    bsz, seq, _ = qkv.shape
    width = DN_HEADS * DN_DK
    nsteps = seq // rows
    cum, cumt, ones = _delta_consts(rows)

    def per_dir(rev):
        blk = (lambda i: nsteps - 1 - i) if rev else (lambda i: i)
        return [pl.BlockSpec((None, rows, width), lambda b, i: (b, blk(i), 0)),
                pl.BlockSpec((None, rows, width), lambda b, i: (b, blk(i), 1)),
                pl.BlockSpec((None, rows, width), lambda b, i: (b, blk(i), 2)),
                pl.BlockSpec((None, rows, LANES), lambda b, i: (b, blk(i), 0)),
                pl.BlockSpec((None, 4 * DN_HEADS, rows), lambda b, i: (b, 0, blk(i)))]

    full = lambda shape: pl.BlockSpec(shape, lambda b, i: (0,) * len(shape))
    state = pl.BlockSpec((None, DN_HEADS, 2, DN_DK, DN_DK), lambda b, i: (b, 0, 0, 0, 0))
    o_fwd = pl.BlockSpec((None, rows, width), lambda b, i: (b, i, 0))
    o_bwd = pl.BlockSpec((None, rows, width), lambda b, i: (b, nsteps - 1 - i, 0))
    data = [qkv, qkv, qkv, gates, gates_t]
    return pl.pallas_call(
        functools.partial(_delta_kernel, rows=rows),
        grid=(bsz, nsteps),
        in_specs=per_dir(False) + per_dir(True) + [
            full((SUBLANES, LANES)), full((2, 4 * DN_HEADS, rows)),
            full((2, rows, rows)), full((2, rows, rows)), full((rows, rows)), state],
        out_specs=[o_fwd, o_bwd, state],
        out_shape=[jax.ShapeDtypeStruct((bsz, seq, width), BF16),
                   jax.ShapeDtypeStruct((bsz, seq, width), BF16),
                   jax.ShapeDtypeStruct((bsz, DN_HEADS, 2, DN_DK, DN_DK), F32)],
        scratch_shapes=[pltpu.VMEM((DN_HEADS, 2, DN_DK, DN_DK), F32)],
        compiler_params=_cparams(("parallel", "arbitrary")),
        name="delta",
    )(*data, *data, cvec, ccol, cum, cumt, ones, s0)


def _evenout_kernel(lru_ref, of_ref, ob_ref, z_ref, g_ref, w_ref, res_ref, gate_ref, o_ref, a_ref):
    width = lru_ref.shape[1]
    a_ref[:, 0:width] = lru_ref[...].astype(BF16)
    for h in range(DN_HEADS):
        csl = slice(h * DN_DK, (h + 1) * DN_DK)
        o = of_ref[:, csl].astype(F32) + ob_ref[:, csl].astype(F32)
        z = z_ref[:, csl].astype(F32)
        y = (_rms(o) * g_ref[:, csl]) * (z * _sigmoid(z))
        a_ref[:, width + h * DN_DK:width + (h + 1) * DN_DK] = y.astype(BF16)
    o_ref[...] = res_ref[...] + gate_ref[...] * jnp.dot(a_ref[...], w_ref[...], preferred_element_type=F32)


def _evenout_call(lru, o_f, o_b, proj, z_block, dn_g, w_out, res, gate, tm):
    bsz, seq, width = lru.shape
    dnw = o_f.shape[2]
    d = res.shape[2]
    row = lambda c: pl.BlockSpec((None, tm, c), lambda b, i: (b, i, 0))
    return pl.pallas_call(
        _evenout_kernel,
        grid=(bsz, seq // tm),
        in_specs=[row(width), row(dnw), row(dnw),
                  pl.BlockSpec((None, tm, dnw), lambda b, i: (b, i, z_block)),
                  pl.BlockSpec((1, dnw), lambda b, i: (0, 0)),
                  pl.BlockSpec((width + dnw, d), lambda b, i: (0, 0)),
                  row(d),
                  pl.BlockSpec((None, 1, d), lambda b, i: (b, 0, 0))],
        out_specs=row(d),
        out_shape=jax.ShapeDtypeStruct((bsz, seq, d), F32),
        scratch_shapes=[pltpu.VMEM((tm, width + dnw), BF16)],
        compiler_params=_cparams(("parallel", "parallel")),
        name="evenout",
    )(lru, o_f, o_b, proj, dn_g, w_out, res, gate)


def _linres_kernel(a_ref, w_ref, res_ref, gate_ref, o_ref):
    o_ref[...] = res_ref[...] + gate_ref[...] * jnp.dot(a_ref[...].astype(BF16), w_ref[...],
                                                        preferred_element_type=F32)


def _linres_call(a, w, res, gate, tm):
    bsz, seq, kdim = a.shape
    d = res.shape[2]
    return pl.pallas_call(
        _linres_kernel,
        grid=(bsz, seq // tm),
        in_specs=[pl.BlockSpec((None, tm, kdim), lambda b, i: (b, i, 0)),
                  pl.BlockSpec((kdim, d), lambda b, i: (0, 0)),
                  pl.BlockSpec((None, tm, d), lambda b, i: (b, i, 0)),
                  pl.BlockSpec((None, 1, d), lambda b, i: (b, 0, 0))],
        out_specs=pl.BlockSpec((None, tm, d), lambda b, i: (b, i, 0)),
        out_shape=jax.ShapeDtypeStruct((bsz, seq, d), F32),
        compiler_params=_cparams(("parallel", "parallel")),
        name="linres",
    )(a, w, res, gate)


def _aprep_kernel(*refs, rope, want_q):
    if rope:
        x_ref, qg_ref, kg_ref, cos_ref, sin_ref = refs[:5]
        outs = refs[5:]
    else:
        x_ref, qg_ref, kg_ref = refs[:3]
        outs = refs[3:]
    if want_q:
        q_ref, k_ref, v_ref = outs
    else:
        k_ref, v_ref = outs

    if rope:
        cos = cos_ref[...]
        sin = sin_ref[...]

    def norm_rope(x, g):
        y = _rms(x) * g
        if rope:
            y = y * cos + pltpu.roll(y, HEAD_DIM // 2, 1) * sin
        return y

    if want_q:
        for h in range(N_Q_HEADS):
            x = x_ref[:, h * HEAD_DIM:(h + 1) * HEAD_DIM].astype(F32)
            q_ref[h] = (norm_rope(x, qg_ref[...]) * (HEAD_DIM ** -0.5 * LOG2E)).astype(BF16)
    for h in range(N_KV_HEADS):
        c0 = (N_Q_HEADS + h) * HEAD_DIM
        k_ref[h] = norm_rope(x_ref[:, c0:c0 + HEAD_DIM].astype(F32), kg_ref[...]).astype(BF16)
        c1 = (N_Q_HEADS + N_KV_HEADS + h) * HEAD_DIM
        v_ref[h, 0:HEAD_DIM, :] = x_ref[:, c1:c1 + HEAD_DIM].astype(F32).T.astype(BF16)
        pad_row = lax.broadcasted_iota(jnp.int32, (VT_PAD, x_ref.shape[0]), 0)
        v_ref[h, HEAD_DIM:HEAD_DIM + VT_PAD, :] = jnp.where(pad_row == 0, 1.0, 0.0).astype(BF16)


def _aprep_call(qkv, qg, kg, rope_tabs, want_q, tm):
    bsz, seq, cols = qkv.shape
    rope = rope_tabs is not None
    in_specs = [pl.BlockSpec((None, tm, cols), lambda b, i: (b, i, 0)),
                pl.BlockSpec((1, HEAD_DIM), lambda b, i: (0, 0)),
                pl.BlockSpec((1, HEAD_DIM), lambda b, i: (0, 0))]
    args = [qkv, qg.reshape(1, HEAD_DIM), kg.reshape(1, HEAD_DIM)]
    if rope:
        in_specs += [pl.BlockSpec((tm, HEAD_DIM), lambda b, i: (i, 0))] * 2
        args += list(rope_tabs)
    hspec = lambda n: pl.BlockSpec((None, n, tm, HEAD_DIM), lambda b, i: (b, 0, i, 0))
    hshape = lambda n: jax.ShapeDtypeStruct((bsz, n, seq, HEAD_DIM), BF16)
    tspec = lambda n: pl.BlockSpec((None, n, HEAD_DIM + VT_PAD, tm), lambda b, i: (b, 0, 0, i))
    tshape = lambda n: jax.ShapeDtypeStruct((bsz, n, HEAD_DIM + VT_PAD, seq), BF16)
    out_specs = [hspec(N_KV_HEADS), tspec(N_KV_HEADS)]
    out_shape = [hshape(N_KV_HEADS), tshape(N_KV_HEADS)]
    if want_q:
        out_specs = [hspec(N_Q_HEADS)] + out_specs
        out_shape = [hshape(N_Q_HEADS)] + out_shape
    return pl.pallas_call(
        functools.partial(_aprep_kernel, rope=rope, want_q=want_q),
        grid=(bsz, seq // tm),
        in_specs=in_specs,
        out_specs=out_specs,
        out_shape=out_shape,
        compiler_params=_cparams(("parallel", "parallel")),
        name="aprep_x" if rope else "aprep_ctx",
    )(*args)


def _rope_tables(n_tokens):
    rows = n_tokens // GRID_W
    axis = HEAD_DIM // 2
    inv_freq = ROPE_THETA ** (-jnp.arange(0, axis, 2, dtype=F32) / axis)
    ang_r = jnp.arange(rows).astype(F32)[:, None] * inv_freq[None]
    ang_c = jnp.arange(GRID_W).astype(F32)[:, None] * inv_freq[None]
    cr, sr = (jnp.repeat(t, GRID_W, axis=0) for t in (jnp.cos(ang_r), jnp.sin(ang_r)))
    cc, sc = (jnp.tile(t, (rows, 1)) for t in (jnp.cos(ang_c), jnp.sin(ang_c)))
    cos = jnp.concatenate([cr, cc, cr, cc], axis=-1)
    sin = jnp.concatenate([-sr, -sc, sr, sc], axis=-1)
    return cos, sin


def _rope_perm():
    quarter = HEAD_DIM // 4
    return jnp.concatenate([jnp.arange(0, quarter), jnp.arange(2 * quarter, 3 * quarter),
                            jnp.arange(quarter, 2 * quarter), jnp.arange(3 * quarter, 4 * quarter)])


def _flash_kernel(q_ref, k_ref, vt_ref, o_ref, s0_ref, sa_ref, sb_ref, m0_ref, ma_ref, mb_ref, *,
                  tq, tk, nk, nt):
    m_rows = Q_PER_KV * tq

    def q_tile(i):
        r0 = pl.multiple_of(i * tq, tq)
        return q_ref[:, pl.ds(r0, tq), :].reshape(m_rows, HEAD_DIM)

    def scores(q, j, s_ref, cmax_ref):
        c0 = pl.multiple_of(j * tk, tk)
        st = lax.dot_general(k_ref[pl.ds(c0, tk), :], q, (((1,), (1,)), ((), ())),
                             preferred_element_type=F32)
        s_ref[...] = st
        cmax_ref[...] = jnp.max(st, axis=0, keepdims=True)

    def update(j, s_ref, cmax_ref, carry):
        m, acc = carry
        m_new = jnp.maximum(m, cmax_ref[...])
        alpha = jnp.exp2(m - m_new)
        p = jnp.exp2(s_ref[...] - m_new)
        vt = vt_ref[:, pl.ds(pl.multiple_of(j * tk, tk), tk)]
        acc = alpha * acc + jnp.dot(vt, p.astype(BF16), preferred_element_type=F32)
        return m_new, acc

    bufs = ((sa_ref, ma_ref), (sb_ref, mb_ref))

    def tile_body(i, _):
        q = q_tile(i)
        q_next = q_tile(jnp.minimum(i + 1, nt - 1))

        def run(j0, count, carry, last):
            for u in range(count):
                j = j0 + u
                if u + 1 < count or not last:
                    scores(q, j + 1, *bufs[(u + 1) % 2])
                else:
                    scores(q_next, 0, s0_ref, m0_ref)
                carry = update(j, *bufs[u % 2], carry)
            return carry

        carry = (jnp.full((1, m_rows), NEG_BIG, F32), jnp.zeros((HEAD_DIM + VT_PAD, m_rows), F32))
        scores(q, 1, sa_ref, ma_ref)
        carry = update(0, s0_ref, m0_ref, carry)
        rest = nk - 1
        niter = (rest - 1) // FLASH_UNROLL
        carry = lax.fori_loop(0, niter, lambda jj, c: run(1 + jj * FLASH_UNROLL, FLASH_UNROLL, c, False),
                              carry)
        _, acc = run(1 + niter * FLASH_UNROLL, rest - niter * FLASH_UNROLL, carry, True)
        o = (acc[:HEAD_DIM] / acc[HEAD_DIM:HEAD_DIM + 1]).T
        r0 = pl.multiple_of(i * tq, tq)
        for r in range(Q_PER_KV):
            o_ref[pl.ds(r0, tq), r * HEAD_DIM:(r + 1) * HEAD_DIM] = o[r * tq:(r + 1) * tq].astype(o_ref.dtype)
        return 0

    scores(q_tile(0), 0, s0_ref, m0_ref)
    lax.fori_loop(0, nt, tile_body, 0)


def _flash_call(q, k, vt, tq, tk):
    bsz, _, seq, _ = q.shape
    lk = k.shape[2]
    nk = lk // tk
    assert nk >= 2, "the score pipeline needs at least two key blocks"
    q5 = q.reshape(bsz, N_KV_HEADS, Q_PER_KV, seq, HEAD_DIM)
    return pl.pallas_call(
        functools.partial(_flash_kernel, tq=tq, tk=tk, nk=nk, nt=seq // tq),
        grid=(bsz, N_KV_HEADS),
        in_specs=[pl.BlockSpec((None, None, Q_PER_KV, seq, HEAD_DIM), lambda b, g: (b, g, 0, 0, 0)),
                  pl.BlockSpec((None, None, lk, HEAD_DIM), lambda b, g: (b, g, 0, 0)),
                  pl.BlockSpec((None, None, HEAD_DIM + VT_PAD, lk), lambda b, g: (b, g, 0, 0))],
        out_specs=pl.BlockSpec((None, seq, Q_PER_KV * HEAD_DIM), lambda b, g: (b, 0, g)),
        out_shape=jax.ShapeDtypeStruct((bsz, seq, N_Q_HEADS * HEAD_DIM), BF16),
        scratch_shapes=([pltpu.VMEM((tk, Q_PER_KV * tq), F32)] * 3
                        + [pltpu.VMEM((1, Q_PER_KV * tq), F32)] * 3),
        compiler_params=_cparams(("parallel", "parallel")),
        name="flash",
    )(q5, k, vt)


def _row_tile(seq, cap):
    t = min(seq, cap)
    while seq % t:
        t //= 2
    return t


def _even_weights(w_in, lru_wa, lru_wi, dn_alog, dn_dtb, dn_g):
    d, cols = w_in.shape
    ngate = 4 * DN_HEADS
    main = cols - ngate
    w_gates = jnp.concatenate([w_in[:, main:], jnp.zeros((d, LANES - ngate), F32)], axis=1)

    def blockdiag(w):
        w = w * (-LOG2E)
        z = jnp.zeros_like(w[:, 0::2])
        top = jnp.concatenate([w[:, 0::2], z], axis=-1)
        bot = jnp.concatenate([z, w[:, 1::2]], axis=-1)
        return jnp.concatenate([top, bot], axis=-2).astype(BF16)

    alog = dn_alog.reshape(-1)
    dtb = dn_dtb.reshape(-1)
    cvec = jnp.zeros((SUBLANES, LANES), F32)
    cvec = cvec.at[0, 2 * DN_HEADS:ngate].set(alog).at[1, 2 * DN_HEADS:ngate].set(dtb)

    ccol = jnp.zeros((2, ngate), F32).at[0, 2 * DN_HEADS:].set(alog).at[1, 2 * DN_HEADS:].set(dtb)

    return dict(w_in=w_in.astype(BF16)[:, :main], w_gates=w_gates.astype(BF16),
                w_gates_t=w_in[:, main:].T.astype(BF16),
                wa=blockdiag(lru_wa), wi=blockdiag(lru_wi), cvec=cvec, ccol=ccol,
                dn_g=jnp.tile(dn_g, DN_HEADS).reshape(1, -1))


def _even_stream(h_in, res, gate, norm_g, sc, sh, ew, lru_cw, lru_cb, lru_ba, lru_bi, lru_lam, dn_cw,
                 w_out, h0, s0, tm_in, tm, rows, dn_rows, fold_batch):
    width = lru_cw.shape[1]
    ngate = 4 * DN_HEADS
    b0, l0 = h_in.shape[:2]
    fold = (lambda a: a.reshape(1, b0 * l0, a.shape[-1])) if fold_batch else (lambda a: a)
    unfold = (lambda a: a.reshape(b0, l0, a.shape[-1])) if fold_batch else (lambda a: a)
    proj, gates, gates_t = _normlin_call(fold(h_in), norm_g, sc, sh, ew["w_in"], ew["w_gates"], ew["w_gates_t"],
                                         tm_in)
    proj, gates = unfold(proj), unfold(gates)
    if fold_batch:
        gates_t = jnp.transpose(gates_t.reshape(gates_t.shape[1], b0, l0), (1, 0, 2))
    lru, h_t = _lru_call(proj, lru_cw, lru_cb, ew["wa"], ew["wi"], lru_ba, lru_bi, lru_lam, h0,
                         _row_tile(proj.shape[1], LRU_ROWS))
    qkv = _dnprep_call(proj, dn_cw, 2 * width // LANES, _row_tile(proj.shape[1], DNPREP_ROWS))
    ccol = jnp.broadcast_to(ew["ccol"][:, :, None], (2, ngate, dn_rows))
    o_f, o_b, s_t = _delta_call(qkv, gates, gates_t, ew["cvec"], ccol, s0, dn_rows)
    z_block = (2 * width + dn_cw.shape[1]) // (DN_HEADS * DN_DK)
    out = _evenout_call(fold(lru), fold(o_f), fold(o_b), fold(proj), z_block, ew["dn_g"], w_out, fold(res), gate, tm)
    return unfold(out), h_t, s_t


def kernel(x, c, ctx, c_ctx, norm_mix_g, norm_mlp_g, w_mod, b_mod, mlp_w1, mlp_w2, even_w_in, lru_conv_w,
           lru_conv_b, lru_w_a, lru_b_a, lru_w_i, lru_b_i, lru_lambda, dn_conv_w, dn_a_log, dn_dt_bias,
           dn_norm_g, even_w_out, attn_w_qkv, attn_q_norm_g, attn_k_norm_g, attn_w_o, final_norm_g):
    bsz, seq, d = x.shape
    lc = ctx.shape[1]
    depth = w_mod.shape[0]
    tm_x = _row_tile(seq, 512)
    tm_cf = _row_tile(bsz * lc, 1024)
    fold_c = lambda a: a.reshape(1, bsz * lc, a.shape[-1])
    unfold_c = lambda a: a.reshape(bsz, lc, a.shape[-1])
    tm_mlp = _row_tile(seq, 1024)
    rows_x = _row_tile(seq, 256)
    rows_c = _row_tile(lc, 256)

    cc = jnp.zeros((SUBLANES, d), F32).at[:bsz].set(c).at[bsz].set(c_ctx)
    mod = _mod_call(cc, w_mod, b_mod).reshape(depth, SUBLANES, 6, d)

    rope = _rope_tables(seq)
    xs, cs = x, ctx
    for l in range(depth):
        last = l == depth - 1
        j = l // 2
        mx = [mod[l, :bsz, k][:, None, :] for k in range(6)]
        mc = [mod[l, bsz, k][None, None, :] for k in range(6)]
        sh1, sc1, g1, sh2, sc2, g2 = mx
        csh1, csc1, cg1, csh2, csc2, cg2 = mc
        w1 = mlp_w1[l].astype(BF16)
        w2 = mlp_w2[l].astype(BF16)
        if l % 2 == 0:
            ew = _even_weights(even_w_in[j], lru_w_a[j], lru_w_i[j], dn_a_log[j], dn_dt_bias[j], dn_norm_g[j])
            w_out = even_w_out[j].astype(BF16)
            h0 = jnp.zeros((bsz, 2, lru_conv_w.shape[2]), F32)
            s0 = jnp.zeros((bsz, DN_HEADS, 2, DN_DK, DN_DK), F32)
            common = (ew, lru_conv_w[j], lru_conv_b[j], lru_b_a[j], lru_b_i[j], lru_lambda[j], dn_conv_w[j], w_out)
            cs_mix, h_t, s_t = _even_stream(cs, cs, cg1, norm_mix_g[l], csc1, csh1, *common, h0, s0,
                                            tm_cf, tm_cf, rows_c, rows_c, True)
            xs, _, _ = _even_stream(xs, xs, g1, norm_mix_g[l], sc1, sh1, *common, h_t, s_t,
                                    tm_mlp, tm_mlp, rows_x, _row_tile(seq, DELTA_ROWS), False)
        else:
            perm = _rope_perm()
            nqk = (N_Q_HEADS + N_KV_HEADS) * HEAD_DIM
            w_qk = attn_w_qkv[j][:, :nqk].reshape(d, N_Q_HEADS + N_KV_HEADS, HEAD_DIM)[:, :, perm]
            w_qkv = jnp.concatenate([w_qk.reshape(d, nqk), attn_w_qkv[j][:, nqk:]], axis=1).astype(BF16)
            q_g = attn_q_norm_g[j][perm]
            k_g = attn_k_norm_g[j][perm]
            w_o = attn_w_o[j].astype(BF16)
            qkv_c = unfold_c(_normlin_call(fold_c(cs), norm_mix_g[l], csc1, csh1, w_qkv, None, None, tm_cf))
            qkv_x = _normlin_call(xs, norm_mix_g[l], sc1, sh1, w_qkv, None, None, tm_mlp)
            if last:
                k_c, vt_c = _aprep_call(qkv_c, q_g, k_g, None, False, rows_c)
            else:
                qt_c, k_c, vt_c = _aprep_call(qkv_c, q_g, k_g, None, True, rows_c)
            qt_x, k_x, vt_x = _aprep_call(qkv_x, q_g, k_g, rope, True, tm_x)
            k_all = jnp.concatenate([k_c, k_x], axis=2)
            vt_all = jnp.concatenate([vt_c, vt_x], axis=3)
            o_x = _flash_call(qt_x, k_all, vt_all, _row_tile(seq, FLASH_TQ), _col_tile(lc + seq, FLASH_TK_CAP))
            xs = _linres_call(o_x, w_o, xs, g1, tm_mlp)
            if not last:
                o_c = _flash_call(qt_c, k_c, vt_c, rows_c, _col_tile(lc, min(FLASH_TK_CAP, lc // 2)))
                cs_mix = unfold_c(_linres_call(fold_c(o_c), w_o, fold_c(cs), cg1, tm_cf))
        xs = _mlp_call(xs, norm_mlp_g[l], sc2, sh2, g2, w1, w2, final_norm_g, tm_mlp, last)
        if not last:
            cs = unfold_c(_mlp_call(fold_c(cs_mix), norm_mlp_g[l], csc2, csh2, cg2, w1, w2, final_norm_g,
                                    tm_cf, False))
    return xs
```

```python
import functools

import jax
import jax.numpy as jnp
from jax import lax
from jax.experimental import pallas as pl
from jax.experimental.pallas import tpu as pltpu

F32 = jnp.float32
BF16 = jnp.bfloat16

EPS = 1e-6
LANES = 128
SUBLANES = 8
VMEM_LIMIT = 56 * 1024 * 1024

LRU_BLOCK = 64
LRU_C = 8.0
CONV_WIDTH = 4
CONV_LEFT = 2
DN_HEADS = 8
DN_DK = 128
DN_CHUNK = 64
N_Q_HEADS = 8
N_KV_HEADS = 4
Q_PER_KV = N_Q_HEADS // N_KV_HEADS
HEAD_DIM = 128
GRID_W = 64
ROPE_THETA = 10000.0
NEG_BIG = -1e30
LOG2E = 1.4426950408889634
DNPREP_ROWS = 1024
LRU_ROWS = 1024
SHORT_SEQ_ROWS = 4096
DELTA_ROWS = 256
FLASH_TQ = 256
FLASH_TK_CAP = 768
VT_PAD = 16
FLASH_UNROLL = 4


def _cparams(sem):
    return pltpu.CompilerParams(dimension_semantics=sem, vmem_limit_bytes=VMEM_LIMIT)


def _bdot(a, b):
    return jnp.dot(a.astype(BF16), b.astype(BF16), preferred_element_type=F32)


def _bdot_nt(a, b):
    return lax.dot_general(a.astype(BF16), b.astype(BF16), (((1,), (1,)), ((), ())),
                           preferred_element_type=F32)


def _bdot_tn(a, b):
    return lax.dot_general(a.astype(BF16), b.astype(BF16), (((0,), (0,)), ((), ())),
                           preferred_element_type=F32)


def _sigmoid(x):
    return jax.nn.sigmoid(x)


def _softplus(x):
    return jnp.maximum(x, 0.0) + jnp.log1p(jnp.exp(-jnp.abs(x)))


def _rms(x):
    return x * lax.rsqrt(jnp.mean(x * x, axis=-1, keepdims=True) + EPS)


def _col_tile(n, cap):
    best = LANES
    t = LANES
    while t <= min(n, cap):
        if n % t == 0:
            best = t
        t += LANES
    return best


def _mod_kernel(c_ref, w_ref, b_ref, o_ref):
    s = c_ref[...]
    s = s * _sigmoid(s)
    o_ref[...] = _bdot(s, w_ref[...]) + b_ref[...]


def _mod_call(cc, w_mod, b_mod):
    depth, d, n = w_mod.shape
    tn = _col_tile(n, 1536)
    return pl.pallas_call(
        _mod_kernel,
        grid=(depth, n // tn),
        in_specs=[pl.BlockSpec((SUBLANES, d), lambda l, j: (0, 0)),
                  pl.BlockSpec((None, d, tn), lambda l, j: (l, 0, j)),
                  pl.BlockSpec((None, 1, tn), lambda l, j: (l, 0, j))],
        out_specs=pl.BlockSpec((None, SUBLANES, tn), lambda l, j: (l, 0, j)),
        out_shape=jax.ShapeDtypeStruct((depth, SUBLANES, n), F32),
        compiler_params=_cparams(("parallel", "parallel")),
        name="mod",
    )(cc, w_mod, b_mod.reshape(depth, 1, n))


def _normlin_kernel(*refs, has_gates):
    if has_gates:
        x_ref, g_ref, sc_ref, sh_ref, w_ref, wg_ref, wgt_ref, o_ref, og_ref, ogt_ref, h_ref = refs
    else:
        x_ref, g_ref, sc_ref, sh_ref, w_ref, o_ref, h_ref = refs

    @pl.when(pl.program_id(2) == 0)
    def _():
        y = _rms(x_ref[...]) * g_ref[...]
        h_ref[...] = (y * (1.0 + sc_ref[...]) + sh_ref[...]).astype(BF16)
        if has_gates:
            og_ref[...] = jnp.dot(h_ref[...], wg_ref[...], preferred_element_type=F32)
            ogt_ref[...] = lax.dot_general(wgt_ref[...], h_ref[...], (((1,), (1,)), ((), ())),
                                           preferred_element_type=F32)

    o_ref[...] = jnp.dot(h_ref[...], w_ref[...], preferred_element_type=F32).astype(o_ref.dtype)


def _normlin_call(x, g, sc, sh, w, w_gates, w_gates_t, tm):
    bsz, seq, d = x.shape
    has_gates = w_gates is not None
    n = w.shape[1] - w_gates_t.shape[0] if has_gates else w.shape[1]
    tn = _col_tile(n, 2048)
    in_specs = [pl.BlockSpec((None, tm, d), lambda b, i, j: (b, i, 0)),
                pl.BlockSpec((1, d), lambda b, i, j: (0, 0)),
                pl.BlockSpec((None, 1, d), lambda b, i, j: (b, 0, 0)),
                pl.BlockSpec((None, 1, d), lambda b, i, j: (b, 0, 0)),
                pl.BlockSpec((d, tn), lambda b, i, j: (0, j))]
    out_specs = [pl.BlockSpec((None, tm, tn), lambda b, i, j: (b, i, j))]
    out_shape = [jax.ShapeDtypeStruct((bsz, seq, n), BF16)]
    args = [x, g.reshape(1, d), sc, sh, w]
    if has_gates:
        ng = w_gates.shape[1]
        in_specs.append(pl.BlockSpec((d, ng), lambda b, i, j: (0, 0)))
        out_specs.append(pl.BlockSpec((None, tm, ng), lambda b, i, j: (b, i, 0)))
        out_shape.append(jax.ShapeDtypeStruct((bsz, seq, ng), F32))
        ngt = w_gates_t.shape[0]
        in_specs.append(pl.BlockSpec((ngt, d), lambda b, i, j: (0, 0)))
        out_specs.append(pl.BlockSpec((None, ngt, tm), lambda b, i, j: (b, 0, i)))
        out_shape.append(jax.ShapeDtypeStruct((bsz, ngt, seq), F32))
        args += [w_gates, w_gates_t]
    out = pl.pallas_call(
        functools.partial(_normlin_kernel, has_gates=has_gates),
        grid=(bsz, seq // tm, n // tn),
        in_specs=in_specs,
        out_specs=out_specs,
        out_shape=out_shape,
        scratch_shapes=[pltpu.VMEM((tm, d), BF16)],
        compiler_params=_cparams(("parallel", "parallel", "arbitrary")),
        name="normlin",
    )(*args)
    return out if has_gates else out[0]


def _mlp_kernel(x_ref, g_ref, sc_ref, sh_ref, gate_ref, w1_ref, w2_ref, fg_ref, o_ref,
                h_ref, acc_ref, *, final_norm):
    k = pl.program_id(2)

    @pl.when(k == 0)
    def _():
        y = _rms(x_ref[...]) * g_ref[...]
        h_ref[...] = (y * (1.0 + sc_ref[...]) + sh_ref[...]).astype(BF16)
        acc_ref[...] = jnp.zeros_like(acc_ref)

    a = jnp.dot(h_ref[...], w1_ref[...], preferred_element_type=F32)
    a = jnp.square(jnp.maximum(a, 0.0)).astype(BF16)
    acc_ref[...] += jnp.dot(a, w2_ref[...], preferred_element_type=F32)

    @pl.when(k == pl.num_programs(2) - 1)
    def _():
        y = x_ref[...] + gate_ref[...] * acc_ref[...]
        if final_norm:
            y = _rms(y) * fg_ref[...]
        o_ref[...] = y


def _mlp_call(x, g, sc, sh, gate, w1, w2, fg, tm, final_norm):
    bsz, seq, d = x.shape
    f = w1.shape[1]
    tf = _col_tile(f, 2048)
    vec = pl.BlockSpec((None, 1, d), lambda b, i, k: (b, 0, 0))
    return pl.pallas_call(
        functools.partial(_mlp_kernel, final_norm=final_norm),
        grid=(bsz, seq // tm, f // tf),
        in_specs=[pl.BlockSpec((None, tm, d), lambda b, i, k: (b, i, 0)),
                  pl.BlockSpec((1, d), lambda b, i, k: (0, 0)),
                  vec, vec, vec,
                  pl.BlockSpec((d, tf), lambda b, i, k: (0, k)),
                  pl.BlockSpec((tf, d), lambda b, i, k: (k, 0)),
                  pl.BlockSpec((1, d), lambda b, i, k: (0, 0))],
        out_specs=pl.BlockSpec((None, tm, d), lambda b, i, k: (b, i, 0)),
        out_shape=jax.ShapeDtypeStruct((bsz, seq, d), F32),
        scratch_shapes=[pltpu.VMEM((tm, d), BF16), pltpu.VMEM((tm, d), F32)],
        compiler_params=_cparams(("parallel", "parallel", "arbitrary")),
        name="mlp",
    )(x, g.reshape(1, d), sc, sh, gate, w1, w2, fg.reshape(1, d))


def _batch_block(bsz, seq):
    return bsz if bsz * seq <= SHORT_SEQ_ROWS else 1


CONV_PAD = SUBLANES


def _fill_padded(xpad_ref, x_ref, seq, rows):
    zeros = jnp.zeros((CONV_PAD, xpad_ref.shape[1]), F32)
    xpad_ref[0:CONV_PAD, :] = zeros
    xpad_ref[CONV_PAD + seq:CONV_PAD + seq + CONV_PAD, :] = zeros

    def body(t, carry):
        r0 = pl.multiple_of(t * rows, rows)
        xpad_ref[pl.ds(r0 + CONV_PAD, rows), :] = x_ref[pl.ds(r0, rows), :].astype(F32)
        return carry

    lax.fori_loop(0, seq // rows, body, 0)


def _conv_tile(xpad_ref, r0, rows, w):
    acc = None
    for j in range(CONV_WIDTH):
        off = j - CONV_LEFT
        term = xpad_ref[pl.ds(r0 + (CONV_PAD + off), rows), :] * w[j:j + 1]
        acc = term if acc is None else acc + term
    return acc


def _dnprep_kernel(xs_ref, w_ref, os_ref, xpad_ref, *, seq, rows):
    for bi in range(xs_ref.shape[0]):
        _dnprep_one(xs_ref.at[bi], w_ref, os_ref.at[bi], xpad_ref, seq, rows)


def _dnprep_one(x_ref, w_ref, o_ref, xpad_ref, seq, rows):
    j = pl.program_id(1)
    scale = jnp.where(j < DN_HEADS, DN_DK ** -0.5, 1.0).astype(F32)
    _fill_padded(xpad_ref, x_ref, seq, rows)
    w = w_ref[...]

    def conv_silu(t):
        r0 = pl.multiple_of(t * rows, rows)
        y = _conv_tile(xpad_ref, r0, rows, w)
        return r0, y * _sigmoid(y)

    def qk_body(t, carry):
        r0, y = conv_silu(t)
        nrm = (y * lax.rsqrt(jnp.sum(y * y, axis=-1, keepdims=True) + EPS)) * scale
        o_ref[pl.ds(r0, rows), :] = nrm.astype(o_ref.dtype)
        return carry

    def v_body(t, carry):
        r0, y = conv_silu(t)
        o_ref[pl.ds(r0, rows), :] = y.astype(o_ref.dtype)
        return carry

    @pl.when(j < 2 * DN_HEADS)
    def _():
        lax.fori_loop(0, seq // rows, qk_body, 0)

    @pl.when(j >= 2 * DN_HEADS)
    def _():
        lax.fori_loop(0, seq // rows, v_body, 0)


def _dnprep_call(proj, conv_w, col0_blocks, rows):
    bsz, seq, _ = proj.shape
    ncol = conv_w.shape[1] // LANES
    nb = _batch_block(bsz, seq)
    return pl.pallas_call(
        functools.partial(_dnprep_kernel, seq=seq, rows=rows),
        grid=(bsz // nb, ncol),
        in_specs=[pl.BlockSpec((nb, seq, LANES), lambda b, j: (b, 0, col0_blocks + j)),
                  pl.BlockSpec((CONV_WIDTH, LANES), lambda b, j: (0, j))],
        out_specs=pl.BlockSpec((nb, seq, LANES), lambda b, j: (b, 0, j)),
        out_shape=jax.ShapeDtypeStruct((bsz, seq, ncol * LANES), BF16),
        scratch_shapes=[pltpu.VMEM((seq + 2 * CONV_PAD, LANES), F32)],
        compiler_params=_cparams(("parallel", "parallel")),
        name="dnprep",
    )(proj, conv_w)


def _scan_tile(a, b, carry, rev, rows):
    nv = rows // SUBLANES
    lanes = a.shape[1]
    a = a.reshape(nv, SUBLANES, lanes)
    b = b.reshape(nv, SUBLANES, lanes)
    row = lax.broadcasted_iota(jnp.int32, a.shape, 1)
    s = 1
    while s < SUBLANES:
        shift = SUBLANES - s if rev else s
        a_s = pltpu.roll(a, shift, 1)
        b_s = pltpu.roll(b, shift, 1)
        valid = (row < SUBLANES - s) if rev else (row >= s)
        b = b + a * jnp.where(valid, b_s, 0.0)
        a = a * jnp.where(valid, a_s, 1.0)
        s *= 2
    hs = [None] * nv
    for v in (range(nv - 1, -1, -1) if rev else range(nv)):
        h = a[v] * carry + b[v]
        carry = h[0:1] if rev else h[SUBLANES - 1:SUBLANES]
        hs[v] = h
    return jnp.concatenate(hs, axis=0), carry


def _gelu_tanh(x):
    return 0.5 * x * (1.0 + jnp.tanh(0.7978845608028654 * (x + 0.044715 * (x * x * x))))


def _lru_kernel(xs_ref, ys_ref, cw_ref, cb_ref, wa_ref, wi_ref, ba_ref, bi_ref, lam_ref, h0s_ref,
                os_ref, hts_ref, xpad_ref, xc_ref, hf_ref, *, seq, rows):
    for bi in range(xs_ref.shape[0]):
        _lru_one(xs_ref.at[bi], ys_ref.at[bi], cw_ref, cb_ref, wa_ref, wi_ref, ba_ref, bi_ref, lam_ref,
                 h0s_ref.at[bi], os_ref.at[bi], hts_ref.at[bi], xpad_ref, xc_ref, hf_ref, seq, rows)


def _lru_one(x_ref, y_ref, cw_ref, cb_ref, wa_ref, wi_ref, ba_ref, bi_ref, lam_ref, h0_ref,
             o_ref, ht_ref, xpad_ref, xc_ref, hf_ref, seq, rows):
    _fill_padded(xpad_ref, x_ref, seq, rows)
    cw = cw_ref[...]
    cb = cb_ref[...]
    nt = seq // rows

    def gates(xc, d):
        r = 1.0 / (1.0 + jnp.exp2(_bdot(xc, wa_ref[d]) - LOG2E * ba_ref[d:d + 1, :]))
        i = 1.0 / (1.0 + jnp.exp2(_bdot(xc, wi_ref[d]) - LOG2E * bi_ref[d:d + 1, :]))
        neg_rate = LRU_C * _softplus(-lam_ref[d:d + 1, :])
        a = jnp.exp2(r * (-LOG2E * neg_rate))
        b = jnp.sqrt(jnp.tanh(r * neg_rate) * (a * a + 1.0)) * (i * xc)
        return a, b

    def fwd_body(t, carry):
        r0 = pl.multiple_of(t * rows, rows)
        xc = _conv_tile(xpad_ref, r0, rows, cw) + cb
        xc_ref[pl.ds(r0, rows), :] = xc
        a, b = gates(xc, 0)
        h, carry = _scan_tile(a, b, carry, False, rows)
        hf_ref[pl.ds(r0, rows), :] = h
        return carry

    carry_f = lax.fori_loop(0, nt, fwd_body, h0_ref[0:1, :])

    def bwd_body(t, carry):
        r0 = pl.multiple_of((nt - 1 - t) * rows, rows)
        xc = xc_ref[pl.ds(r0, rows), :]
        a, b = gates(xc, 1)
        h, carry = _scan_tile(a, b, carry, True, rows)
        y = y_ref[pl.ds(r0, rows), :].astype(F32)
        o_ref[pl.ds(r0, rows), :] = ((hf_ref[pl.ds(r0, rows), :] + h) * _gelu_tanh(y)).astype(o_ref.dtype)
        return carry

    carry_b = lax.fori_loop(0, nt, bwd_body, h0_ref[1:2, :])
    ht_ref[0:1, :] = carry_f
    ht_ref[1:2, :] = carry_b


def _lru_call(proj, cw, cb, wa, wi, ba, bi, lam, h0, rows):
    bsz, seq, _ = proj.shape
    width = cw.shape[1]
    ncol = width // LANES
    vec2 = pl.BlockSpec((2, LANES), lambda b, j: (0, j))
    wspec = pl.BlockSpec((2, None, LANES, LANES), lambda b, j: (0, j, 0, 0))
    nb = _batch_block(bsz, seq)
    return pl.pallas_call(
        functools.partial(_lru_kernel, seq=seq, rows=rows),
        grid=(bsz // nb, ncol),
        in_specs=[pl.BlockSpec((nb, seq, LANES), lambda b, j: (b, 0, j)),
                  pl.BlockSpec((nb, seq, LANES), lambda b, j: (b, 0, ncol + j)),
                  pl.BlockSpec((CONV_WIDTH, LANES), lambda b, j: (0, j)),
                  pl.BlockSpec((1, LANES), lambda b, j: (0, j)),
                  wspec, wspec, vec2, vec2, vec2,
                  pl.BlockSpec((nb, 2, LANES), lambda b, j: (b, 0, j))],
        out_specs=[pl.BlockSpec((nb, seq, LANES), lambda b, j: (b, 0, j)),
                   pl.BlockSpec((nb, 2, LANES), lambda b, j: (b, 0, j))],
        out_shape=[jax.ShapeDtypeStruct((bsz, seq, width), BF16),
                   jax.ShapeDtypeStruct((bsz, 2, width), F32)],
        scratch_shapes=[pltpu.VMEM((seq + 2 * CONV_PAD, LANES), F32),
                        pltpu.VMEM((seq, LANES), F32),
                        pltpu.VMEM((seq, LANES), F32)],
        compiler_params=_cparams(("parallel", "parallel")),
        name="lru",
    )(proj, proj, cw, cb.reshape(1, width), wa, wi, ba, bi, lam, h0)


DN_GROUP = 2 * DN_CHUNK


def _split3(x):
    x1 = x.astype(BF16)
    r1 = x - x1.astype(F32)
    x2 = r1.astype(BF16)
    x3 = (r1 - x2.astype(F32)).astype(BF16)
    return x1, x2, x3


def _delta_kernel(*refs, rows):
    (qf_ref, kf_ref, vf_ref, grf_ref, grtf_ref, qb_ref, kb_ref, vb_ref, grb_ref, grtb_ref,
     cvec_ref, ccol_ref, cum_ref, cumt_ref, ones_ref, s0_ref, of_ref, ob_ref, st_ref, s_ref) = refs
    step = pl.program_id(1)
    q_refs, k_refs, v_refs = (qf_ref, qb_ref), (kf_ref, kb_ref), (vf_ref, vb_ref)
    o_refs = (of_ref, ob_ref)
    dirs = (0, 1)

    @pl.when(step == 0)
    def _():
        s_ref[...] = s0_ref[...]

    ones = ones_ref[...]

    def mm(m, parts):
        return sum(jnp.dot(m, p, preferred_element_type=F32) for p in parts)

    sig, gc, e_gc, e_rest, e_gl, gct = [], [], [], [], [], []
    for d, (gr_ref, grt_ref) in enumerate(((grf_ref, grtf_ref), (grb_ref, grtb_ref))):
        raw = gr_ref[...]
        g = -jnp.exp(cvec_ref[0:1, :]) * _softplus(raw + cvec_ref[1:2, :])
        parts = _split3(g)
        gc_d = mm(cum_ref[d], parts)
        gl_d = mm(ones, parts)
        sig.append(_sigmoid(raw))
        gc.append(gc_d)
        e_gc.append(jnp.exp(gc_d))
        e_rest.append(jnp.exp(gl_d - gc_d))
        e_gl.append(jnp.exp(gl_d))
        gt = -jnp.exp(ccol_ref[0]) * _softplus(grt_ref[...] + ccol_ref[1])
        cumt = cumt_ref[d]
        gct.append(sum(jnp.dot(p, cumt, preferred_element_type=F32) for p in _split3(gt)))

    ri = lax.broadcasted_iota(jnp.int32, (DN_GROUP, DN_GROUP), 0)
    ci = lax.broadcasted_iota(jnp.int32, (DN_GROUP, DN_GROUP), 1)
    same = (ri // DN_CHUNK) == (ci // DN_CHUNK)
    incl = (same & (ri >= ci), same & (ri <= ci))
    strict = (same & (ri > ci), same & (ri < ci))

    ngroups = rows // DN_GROUP
    zeros_c = jnp.zeros((DN_CHUNK, DN_DK), F32)

    def bcast(x, rsl, lane):
        return jnp.broadcast_to(x[rsl, lane:lane + 1], (DN_GROUP, LANES))

    heads = range(DN_HEADS)
    groups = (list(range(ngroups)), list(range(ngroups - 1, -1, -1)))
    units = [(d, gi, h) for d in dirs for gi in groups[d] for h in heads]
    qd, kd, egl_b, rhs, kk, qk = {}, {}, {}, {}, {}, {}
    for u in units:
        d, gi, h = u
        rsl = slice(gi * DN_GROUP, (gi + 1) * DN_GROUP)
        lb = d * DN_HEADS + h
        lg = 2 * DN_HEADS + d * DN_HEADS + h
        csl = slice(h * DN_DK, (h + 1) * DN_DK)
        q16 = q_refs[d][rsl, csl]
        k16 = k_refs[d][rsl, csl]
        q_h = q16.astype(F32)
        k_h = k16.astype(F32)
        beta_b = bcast(sig[d], rsl, lb)
        egc_b = bcast(e_gc[d], rsl, lg)
        kb_h = k_h * beta_b
        kk[u] = _bdot_nt(kb_h, k16)
        qk[u] = _bdot_nt(q16, k16)
        rhs[u] = jnp.concatenate([v_refs[d][rsl, csl].astype(F32) * beta_b, kb_h * egc_b], axis=1)
        qd[u] = q_h * egc_b
        kd[u] = k_h * bcast(e_rest[d], rsl, lg)
        egl_b[u] = bcast(e_gl[d], rsl, lg)
    p, a16, attn = {}, {}, {}
    for u in units:
        d, gi, h = u
        rsl = slice(gi * DN_GROUP, (gi + 1) * DN_GROUP)
        lg = 2 * DN_HEADS + d * DN_HEADS + h
        diff = bcast(gc[d], rsl, lg) - gct[d][lg:lg + 1, rsl]
        decay = jnp.exp(jnp.where(incl[d], diff, NEG_BIG))
        m = jnp.where(strict[d], kk[u] * decay, 0.0)
        attn[u] = qk[u] * decay
        p[u] = -m
        a16[u] = m.astype(BF16)
    npow = 1
    while npow < DN_CHUNK // 2:
        a = {u: jnp.dot(a16[u], a16[u], preferred_element_type=F32) for u in units}
        a16 = {u: a[u].astype(BF16) for u in units}
        p = {u: p[u] + a[u] + jnp.dot(p[u].astype(BF16), a16[u], preferred_element_type=F32) for u in units}
        npow *= 2
    uw = {u: rhs[u] + _bdot(p[u], rhs[u]) for u in units}
    chains = [(d, h) for d in dirs for h in heads]
    s = {c: s_ref[c[1], c[0]] for c in chains}
    slots = tuple([(gi, cc) for gi in groups[d] for cc in ((1, 0) if d else (0, 1))] for d in dirs)
    for slot in range(2 * ngroups):
        r2, v_new = {}, {}
        for c in chains:
            d, h = c
            gi, cc = slots[d][slot]
            cs = slice(cc * DN_CHUNK, (cc + 1) * DN_CHUNK)
            r2[c] = _bdot(jnp.concatenate([uw[d, gi, h][cs, DN_DK:], qd[d, gi, h][cs]], axis=0), s[c])
        for c in chains:
            d, h = c
            gi, cc = slots[d][slot]
            cs = slice(cc * DN_CHUNK, (cc + 1) * DN_CHUNK)
            v_new[c] = uw[d, gi, h][cs, :DN_DK] - r2[c][:DN_CHUNK]
        for c in chains:
            d, h = c
            gi, cc = slots[d][slot]
            c0 = cc * DN_CHUNK
            cs = slice(c0, c0 + DN_CHUNK)
            v_pad = (jnp.concatenate([v_new[c], zeros_c], axis=0) if cc == 0
                     else jnp.concatenate([zeros_c, v_new[c]], axis=0))
            o_c = r2[c][DN_CHUNK:] + _bdot(attn[d, gi, h][cs], v_pad)
            o_refs[d][gi * DN_GROUP + c0:gi * DN_GROUP + c0 + DN_CHUNK,
                      h * DN_DK:(h + 1) * DN_DK] = o_c.astype(o_refs[d].dtype)
        for c in chains:
            d, h = c
            gi, cc = slots[d][slot]
            cs = slice(cc * DN_CHUNK, (cc + 1) * DN_CHUNK)
            egl = jnp.concatenate([egl_b[d, gi, h][cs], egl_b[d, gi, h][cs]], axis=0)
            s[c] = s[c] * egl + _bdot_tn(kd[d, gi, h][cs], v_new[c])
    for c in chains:
        s_ref[c[1], c[0]] = s[c]

    @pl.when(step == pl.num_programs(1) - 1)
    def _():
        st_ref[...] = s_ref[...]


def _delta_consts(rows):
    t = jnp.arange(rows)
    same = (t[:, None] // DN_CHUNK) == (t[None, :] // DN_CHUNK)
    cum = jnp.stack([same & (t[:, None] >= t[None, :]), same & (t[:, None] <= t[None, :])])
    return cum.astype(BF16), jnp.swapaxes(cum, 1, 2).astype(BF16), same.astype(BF16)


def _delta_call(qkv, gates, gates_t, cvec, ccol, s0, rows):
    bsz, seq, _ = qkv.shape
    width = DN_HEADS * DN_DK
    nsteps = seq // rows
    cum, cumt, ones = _delta_consts(rows)

    def per_dir(rev):
        blk = (lambda i: nsteps - 1 - i) if rev else (lambda i: i)
        return [pl.BlockSpec((None, rows, width), lambda b, i: (b, blk(i), 0)),
                pl.BlockSpec((None, rows, width), lambda b, i: (b, blk(i), 1)),
                pl.BlockSpec((None, rows, width), lambda b, i: (b, blk(i), 2)),
                pl.BlockSpec((None, rows, LANES), lambda b, i: (b, blk(i), 0)),
                pl.BlockSpec((None, 4 * DN_HEADS, rows), lambda b, i: (b, 0, blk(i)))]

    full = lambda shape: pl.BlockSpec(shape, lambda b, i: (0,) * len(shape))
    state = pl.BlockSpec((None, DN_HEADS, 2, DN_DK, DN_DK), lambda b, i: (b, 0, 0, 0, 0))
    o_fwd = pl.BlockSpec((None, rows, width), lambda b, i: (b, i, 0))
    o_bwd = pl.BlockSpec((None, rows, width), lambda b, i: (b, nsteps - 1 - i, 0))
    data = [qkv, qkv, qkv, gates, gates_t]
    return pl.pallas_call(
        functools.partial(_delta_kernel, rows=rows),
        grid=(bsz, nsteps),
        in_specs=per_dir(False) + per_dir(True) + [
            full((SUBLANES, LANES)), full((2, 4 * DN_HEADS, rows)),
            full((2, rows, rows)), full((2, rows, rows)), full((rows, rows)), state],
        out_specs=[o_fwd, o_bwd, state],
        out_shape=[jax.ShapeDtypeStruct((bsz, seq, width), BF16),
                   jax.ShapeDtypeStruct((bsz, seq, width), BF16),
                   jax.ShapeDtypeStruct((bsz, DN_HEADS, 2, DN_DK, DN_DK), F32)],
        scratch_shapes=[pltpu.VMEM((DN_HEADS, 2, DN_DK, DN_DK), F32)],
        compiler_params=_cparams(("parallel", "arbitrary")),
        name="delta",
    )(*data, *data, cvec, ccol, cum, cumt, ones, s0)


def _evenout_kernel(lru_ref, of_ref, ob_ref, z_ref, g_ref, w_ref, res_ref, gate_ref, o_ref, a_ref):
    width = lru_ref.shape[1]
    a_ref[:, 0:width] = lru_ref[...].astype(BF16)
    for h in range(DN_HEADS):
        csl = slice(h * DN_DK, (h + 1) * DN_DK)
        o = of_ref[:, csl].astype(F32) + ob_ref[:, csl].astype(F32)
        z = z_ref[:, csl].astype(F32)
        y = (_rms(o) * g_ref[:, csl]) * (z * _sigmoid(z))
        a_ref[:, width + h * DN_DK:width + (h + 1) * DN_DK] = y.astype(BF16)
    o_ref[...] = res_ref[...] + gate_ref[...] * jnp.dot(a_ref[...], w_ref[...], preferred_element_type=F32)


def _evenout_call(lru, o_f, o_b, proj, z_block, dn_g, w_out, res, gate, tm):
    bsz, seq, width = lru.shape
    dnw = o_f.shape[2]
    d = res.shape[2]
    row = lambda c: pl.BlockSpec((None, tm, c), lambda b, i: (b, i, 0))
    return pl.pallas_call(
        _evenout_kernel,
        grid=(bsz, seq // tm),
        in_specs=[row(width), row(dnw), row(dnw),
                  pl.BlockSpec((None, tm, dnw), lambda b, i: (b, i, z_block)),
                  pl.BlockSpec((1, dnw), lambda b, i: (0, 0)),
                  pl.BlockSpec((width + dnw, d), lambda b, i: (0, 0)),
                  row(d),
                  pl.BlockSpec((None, 1, d), lambda b, i: (b, 0, 0))],
        out_specs=row(d),
        out_shape=jax.ShapeDtypeStruct((bsz, seq, d), F32),
        scratch_shapes=[pltpu.VMEM((tm, width + dnw), BF16)],
        compiler_params=_cparams(("parallel", "parallel")),
        name="evenout",
    )(lru, o_f, o_b, proj, dn_g, w_out, res, gate)


def _linres_kernel(a_ref, w_ref, res_ref, gate_ref, o_ref):
    o_ref[...] = res_ref[...] + gate_ref[...] * jnp.dot(a_ref[...].astype(BF16), w_ref[...],
                                                        preferred_element_type=F32)


def _linres_call(a, w, res, gate, tm):
    bsz, seq, kdim = a.shape
    d = res.shape[2]
    return pl.pallas_call(
        _linres_kernel,
        grid=(bsz, seq // tm),
        in_specs=[pl.BlockSpec((None, tm, kdim), lambda b, i: (b, i, 0)),
                  pl.BlockSpec((kdim, d), lambda b, i: (0, 0)),
                  pl.BlockSpec((None, tm, d), lambda b, i: (b, i, 0)),
                  pl.BlockSpec((None, 1, d), lambda b, i: (b, 0, 0))],
        out_specs=pl.BlockSpec((None, tm, d), lambda b, i: (b, i, 0)),
        out_shape=jax.ShapeDtypeStruct((bsz, seq, d), F32),
        compiler_params=_cparams(("parallel", "parallel")),
        name="linres",
    )(a, w, res, gate)


def _aprep_kernel(*refs, rope, want_q):
    if rope:
        x_ref, qg_ref, kg_ref, cos_ref, sin_ref = refs[:5]
        outs = refs[5:]
    else:
        x_ref, qg_ref, kg_ref = refs[:3]
        outs = refs[3:]
    if want_q:
        q_ref, k_ref, v_ref = outs
    else:
        k_ref, v_ref = outs

    if rope:
        cos = cos_ref[...]
        sin = sin_ref[...]

    def norm_rope(x, g):
        y = _rms(x) * g
        if rope:
            y = y * cos + pltpu.roll(y, HEAD_DIM // 2, 1) * sin
        return y

    if want_q:
        for h in range(N_Q_HEADS):
            x = x_ref[:, h * HEAD_DIM:(h + 1) * HEAD_DIM].astype(F32)
            q_ref[h] = (norm_rope(x, qg_ref[...]) * (HEAD_DIM ** -0.5 * LOG2E)).astype(BF16)
    for h in range(N_KV_HEADS):
        c0 = (N_Q_HEADS + h) * HEAD_DIM
        k_ref[h] = norm_rope(x_ref[:, c0:c0 + HEAD_DIM].astype(F32), kg_ref[...]).astype(BF16)
        c1 = (N_Q_HEADS + N_KV_HEADS + h) * HEAD_DIM
        v_ref[h, 0:HEAD_DIM, :] = x_ref[:, c1:c1 + HEAD_DIM].astype(F32).T.astype(BF16)
        pad_row = lax.broadcasted_iota(jnp.int32, (VT_PAD, x_ref.shape[0]), 0)
        v_ref[h, HEAD_DIM:HEAD_DIM + VT_PAD, :] = jnp.where(pad_row == 0, 1.0, 0.0).astype(BF16)


def _aprep_call(qkv, qg, kg, rope_tabs, want_q, tm):
    bsz, seq, cols = qkv.shape
    rope = rope_tabs is not None
    in_specs = [pl.BlockSpec((None, tm, cols), lambda b, i: (b, i, 0)),
                pl.BlockSpec((1, HEAD_DIM), lambda b, i: (0, 0)),
                pl.BlockSpec((1, HEAD_DIM), lambda b, i: (0, 0))]
    args = [qkv, qg.reshape(1, HEAD_DIM), kg.reshape(1, HEAD_DIM)]
    if rope:
        in_specs += [pl.BlockSpec((tm, HEAD_DIM), lambda b, i: (i, 0))] * 2
        args += list(rope_tabs)
    hspec = lambda n: pl.BlockSpec((None, n, tm, HEAD_DIM), lambda b, i: (b, 0, i, 0))
    hshape = lambda n: jax.ShapeDtypeStruct((bsz, n, seq, HEAD_DIM), BF16)
    tspec = lambda n: pl.BlockSpec((None, n, HEAD_DIM + VT_PAD, tm), lambda b, i: (b, 0, 0, i))
    tshape = lambda n: jax.ShapeDtypeStruct((bsz, n, HEAD_DIM + VT_PAD, seq), BF16)
    out_specs = [hspec(N_KV_HEADS), tspec(N_KV_HEADS)]
    out_shape = [hshape(N_KV_HEADS), tshape(N_KV_HEADS)]
    if want_q:
        out_specs = [hspec(N_Q_HEADS)] + out_specs
        out_shape = [hshape(N_Q_HEADS)] + out_shape
    return pl.pallas_call(
        functools.partial(_aprep_kernel, rope=rope, want_q=want_q),
        grid=(bsz, seq // tm),
        in_specs=in_specs,
        out_specs=out_specs,
        out_shape=out_shape,
        compiler_params=_cparams(("parallel", "parallel")),
        name="aprep_x" if rope else "aprep_ctx",
    )(*args)


def _rope_tables(n_tokens):
    rows = n_tokens // GRID_W
    axis = HEAD_DIM // 2
    inv_freq = ROPE_THETA ** (-jnp.arange(0, axis, 2, dtype=F32) / axis)
    ang_r = jnp.arange(rows).astype(F32)[:, None] * inv_freq[None]
    ang_c = jnp.arange(GRID_W).astype(F32)[:, None] * inv_freq[None]
    cr, sr = (jnp.repeat(t, GRID_W, axis=0) for t in (jnp.cos(ang_r), jnp.sin(ang_r)))
    cc, sc = (jnp.tile(t, (rows, 1)) for t in (jnp.cos(ang_c), jnp.sin(ang_c)))
    cos = jnp.concatenate([cr, cc, cr, cc], axis=-1)
    sin = jnp.concatenate([-sr, -sc, sr, sc], axis=-1)
    return cos, sin


def _rope_perm():
    quarter = HEAD_DIM // 4
    return jnp.concatenate([jnp.arange(0, quarter), jnp.arange(2 * quarter, 3 * quarter),
                            jnp.arange(quarter, 2 * quarter), jnp.arange(3 * quarter, 4 * quarter)])


def _flash_kernel(q_ref, k_ref, vt_ref, o_ref, s0_ref, sa_ref, sb_ref, m0_ref, ma_ref, mb_ref, *,
                  tq, tk, nk, nt):
    m_rows = Q_PER_KV * tq

    def q_tile(i):
        r0 = pl.multiple_of(i * tq, tq)
        return q_ref[:, pl.ds(r0, tq), :].reshape(m_rows, HEAD_DIM)

    def scores(q, j, s_ref, cmax_ref):
        c0 = pl.multiple_of(j * tk, tk)
        st = lax.dot_general(k_ref[pl.ds(c0, tk), :], q, (((1,), (1,)), ((), ())),
                             preferred_element_type=F32)
        s_ref[...] = st
        cmax_ref[...] = jnp.max(st, axis=0, keepdims=True)

    def update(j, s_ref, cmax_ref, carry):
        m, acc = carry
        m_new = jnp.maximum(m, cmax_ref[...])
        alpha = jnp.exp2(m - m_new)
        p = jnp.exp2(s_ref[...] - m_new)
        vt = vt_ref[:, pl.ds(pl.multiple_of(j * tk, tk), tk)]
        acc = alpha * acc + jnp.dot(vt, p.astype(BF16), preferred_element_type=F32)
        return m_new, acc

    bufs = ((sa_ref, ma_ref), (sb_ref, mb_ref))

    def tile_body(i, _):
        q = q_tile(i)
        q_next = q_tile(jnp.minimum(i + 1, nt - 1))

        def run(j0, count, carry, last):
            for u in range(count):
                j = j0 + u
                if u + 1 < count or not last:
                    scores(q, j + 1, *bufs[(u + 1) % 2])
                else:
                    scores(q_next, 0, s0_ref, m0_ref)
                carry = update(j, *bufs[u % 2], carry)
            return carry

        carry = (jnp.full((1, m_rows), NEG_BIG, F32), jnp.zeros((HEAD_DIM + VT_PAD, m_rows), F32))
        scores(q, 1, sa_ref, ma_ref)
        carry = update(0, s0_ref, m0_ref, carry)
        rest = nk - 1
        niter = (rest - 1) // FLASH_UNROLL
        carry = lax.fori_loop(0, niter, lambda jj, c: run(1 + jj * FLASH_UNROLL, FLASH_UNROLL, c, False),
                              carry)
        _, acc = run(1 + niter * FLASH_UNROLL, rest - niter * FLASH_UNROLL, carry, True)
        o = (acc[:HEAD_DIM] / acc[HEAD_DIM:HEAD_DIM + 1]).T
        r0 = pl.multiple_of(i * tq, tq)
        for r in range(Q_PER_KV):
            o_ref[pl.ds(r0, tq), r * HEAD_DIM:(r + 1) * HEAD_DIM] = o[r * tq:(r + 1) * tq].astype(o_ref.dtype)
        return 0

    scores(q_tile(0), 0, s0_ref, m0_ref)
    lax.fori_loop(0, nt, tile_body, 0)


def _flash_call(q, k, vt, tq, tk):
    bsz, _, seq, _ = q.shape
    lk = k.shape[2]
    nk = lk // tk
    assert nk >= 2, "the score pipeline needs at least two key blocks"
    q5 = q.reshape(bsz, N_KV_HEADS, Q_PER_KV, seq, HEAD_DIM)
    return pl.pallas_call(
        functools.partial(_flash_kernel, tq=tq, tk=tk, nk=nk, nt=seq // tq),
        grid=(bsz, N_KV_HEADS),
        in_specs=[pl.BlockSpec((None, None, Q_PER_KV, seq, HEAD_DIM), lambda b, g: (b, g, 0, 0, 0)),
                  pl.BlockSpec((None, None, lk, HEAD_DIM), lambda b, g: (b, g, 0, 0)),
                  pl.BlockSpec((None, None, HEAD_DIM + VT_PAD, lk), lambda b, g: (b, g, 0, 0))],
        out_specs=pl.BlockSpec((None, seq, Q_PER_KV * HEAD_DIM), lambda b, g: (b, 0, g)),
        out_shape=jax.ShapeDtypeStruct((bsz, seq, N_Q_HEADS * HEAD_DIM), BF16),
        scratch_shapes=([pltpu.VMEM((tk, Q_PER_KV * tq), F32)] * 3
                        + [pltpu.VMEM((1, Q_PER_KV * tq), F32)] * 3),
        compiler_params=_cparams(("parallel", "parallel")),
        name="flash",
    )(q5, k, vt)


def _row_tile(seq, cap):
    t = min(seq, cap)
    while seq % t:
        t //= 2
    return t


def _even_weights(w_in, lru_wa, lru_wi, dn_alog, dn_dtb, dn_g):
    d, cols = w_in.shape
    ngate = 4 * DN_HEADS
    main = cols - ngate
    w_gates = jnp.concatenate([w_in[:, main:], jnp.zeros((d, LANES - ngate), F32)], axis=1)

    def blockdiag(w):
        w = w * (-LOG2E)
        z = jnp.zeros_like(w[:, 0::2])
        top = jnp.concatenate([w[:, 0::2], z], axis=-1)
        bot = jnp.concatenate([z, w[:, 1::2]], axis=-1)
        return jnp.concatenate([top, bot], axis=-2).astype(BF16)

    alog = dn_alog.reshape(-1)
    dtb = dn_dtb.reshape(-1)
    cvec = jnp.zeros((SUBLANES, LANES), F32)
    cvec = cvec.at[0, 2 * DN_HEADS:ngate].set(alog).at[1, 2 * DN_HEADS:ngate].set(dtb)

    ccol = jnp.zeros((2, ngate), F32).at[0, 2 * DN_HEADS:].set(alog).at[1, 2 * DN_HEADS:].set(dtb)

    return dict(w_in=w_in.astype(BF16), w_gates=w_gates.astype(BF16),
                w_gates_t=w_in[:, main:].T.astype(BF16),
                wa=blockdiag(lru_wa), wi=blockdiag(lru_wi), cvec=cvec, ccol=ccol,
                dn_g=jnp.tile(dn_g, DN_HEADS).reshape(1, -1))


def _even_stream(h_in, res, gate, norm_g, sc, sh, ew, lru_cw, lru_cb, lru_ba, lru_bi, lru_lam, dn_cw,
                 w_out, h0, s0, tm_in, tm, rows, dn_rows, fold_batch):
    width = lru_cw.shape[1]
    ngate = 4 * DN_HEADS
    b0, l0 = h_in.shape[:2]
    fold = (lambda a: a.reshape(1, b0 * l0, a.shape[-1])) if fold_batch else (lambda a: a)
    unfold = (lambda a: a.reshape(b0, l0, a.shape[-1])) if fold_batch else (lambda a: a)
    proj, gates, gates_t = _normlin_call(fold(h_in), norm_g, sc, sh, ew["w_in"], ew["w_gates"], ew["w_gates_t"],
                                         tm_in)
    proj, gates = unfold(proj), unfold(gates)
    if fold_batch:
        gates_t = jnp.transpose(gates_t.reshape(gates_t.shape[1], b0, l0), (1, 0, 2))
    lru, h_t = _lru_call(proj, lru_cw, lru_cb, ew["wa"], ew["wi"], lru_ba, lru_bi, lru_lam, h0,
                         _row_tile(proj.shape[1], LRU_ROWS))
    qkv = _dnprep_call(proj, dn_cw, 2 * width // LANES, _row_tile(proj.shape[1], DNPREP_ROWS))
    ccol = jnp.broadcast_to(ew["ccol"][:, :, None], (2, ngate, dn_rows))
    o_f, o_b, s_t = _delta_call(qkv, gates, gates_t, ew["cvec"], ccol, s0, dn_rows)
    z_block = (2 * width + dn_cw.shape[1]) // (DN_HEADS * DN_DK)
    out = _evenout_call(fold(lru), fold(o_f), fold(o_b), fold(proj), z_block, ew["dn_g"], w_out, fold(res), gate, tm)
    return unfold(out), h_t, s_t


def kernel(x, c, ctx, c_ctx, norm_mix_g, norm_mlp_g, w_mod, b_mod, mlp_w1, mlp_w2, even_w_in, lru_conv_w,
           lru_conv_b, lru_w_a, lru_b_a, lru_w_i, lru_b_i, lru_lambda, dn_conv_w, dn_a_log, dn_dt_bias,
           dn_norm_g, even_w_out, attn_w_qkv, attn_q_norm_g, attn_k_norm_g, attn_w_o, final_norm_g):
    bsz, seq, d = x.shape
    lc = ctx.shape[1]
    depth = w_mod.shape[0]
    tm_x = _row_tile(seq, 512)
    tm_cf = _row_tile(bsz * lc, 1024)
    fold_c = lambda a: a.reshape(1, bsz * lc, a.shape[-1])
    unfold_c = lambda a: a.reshape(bsz, lc, a.shape[-1])
    tm_mlp = _row_tile(seq, 1024)
    rows_x = _row_tile(seq, 256)
    rows_c = _row_tile(lc, 256)

    cc = jnp.zeros((SUBLANES, d), F32).at[:bsz].set(c).at[bsz].set(c_ctx)
    mod = _mod_call(cc, w_mod, b_mod).reshape(depth, SUBLANES, 6, d)

    rope = _rope_tables(seq)
    xs, cs = x, ctx
    for l in range(depth):
        last = l == depth - 1
        j = l // 2
        mx = [mod[l, :bsz, k][:, None, :] for k in range(6)]
        mc = [mod[l, bsz, k][None, None, :] for k in range(6)]
        sh1, sc1, g1, sh2, sc2, g2 = mx
        csh1, csc1, cg1, csh2, csc2, cg2 = mc
        w1 = mlp_w1[l].astype(BF16)
        w2 = mlp_w2[l].astype(BF16)
        if l % 2 == 0:
            ew = _even_weights(even_w_in[j], lru_w_a[j], lru_w_i[j], dn_a_log[j], dn_dt_bias[j], dn_norm_g[j])
            w_out = even_w_out[j].astype(BF16)
            h0 = jnp.zeros((bsz, 2, lru_conv_w.shape[2]), F32)
            s0 = jnp.zeros((bsz, DN_HEADS, 2, DN_DK, DN_DK), F32)
            common = (ew, lru_conv_w[j], lru_conv_b[j], lru_b_a[j], lru_b_i[j], lru_lambda[j], dn_conv_w[j], w_out)
            cs_mix, h_t, s_t = _even_stream(cs, cs, cg1, norm_mix_g[l], csc1, csh1, *common, h0, s0,
                                            tm_cf, tm_cf, rows_c, rows_c, True)
            xs, _, _ = _even_stream(xs, xs, g1, norm_mix_g[l], sc1, sh1, *common, h_t, s_t,
                                    tm_mlp, tm_mlp, rows_x, _row_tile(seq, DELTA_ROWS), False)
        else:
            perm = _rope_perm()
            nqk = (N_Q_HEADS + N_KV_HEADS) * HEAD_DIM
            w_qk = attn_w_qkv[j][:, :nqk].reshape(d, N_Q_HEADS + N_KV_HEADS, HEAD_DIM)[:, :, perm]
            w_qkv = jnp.concatenate([w_qk.reshape(d, nqk), attn_w_qkv[j][:, nqk:]], axis=1).astype(BF16)
            q_g = attn_q_norm_g[j][perm]
            k_g = attn_k_norm_g[j][perm]
            w_o = attn_w_o[j].astype(BF16)
            qkv_c = unfold_c(_normlin_call(fold_c(cs), norm_mix_g[l], csc1, csh1, w_qkv, None, None, tm_cf))
            qkv_x = _normlin_call(xs, norm_mix_g[l], sc1, sh1, w_qkv, None, None, tm_mlp)
            if last:
                k_c, vt_c = _aprep_call(qkv_c, q_g, k_g, None, False, rows_c)
            else:
                qt_c, k_c, vt_c = _aprep_call(qkv_c, q_g, k_g, None, True, rows_c)
            qt_x, k_x, vt_x = _aprep_call(qkv_x, q_g, k_g, rope, True, tm_x)
            k_all = jnp.concatenate([k_c, k_x], axis=2)
            vt_all = jnp.concatenate([vt_c, vt_x], axis=3)
            o_x = _flash_call(qt_x, k_all, vt_all, _row_tile(seq, FLASH_TQ), _col_tile(lc + seq, FLASH_TK_CAP))
            xs = _linres_call(o_x, w_o, xs, g1, tm_mlp)
            if not last:
                o_c = _flash_call(qt_c, k_c, vt_c, rows_c, _col_tile(lc, min(FLASH_TK_CAP, lc // 2)))
                cs_mix = unfold_c(_linres_call(fold_c(o_c), w_o, fold_c(cs), cg1, tm_cf))
        xs = _mlp_call(xs, norm_mlp_g[l], sc2, sh2, g2, w1, w2, final_norm_g, tm_mlp, last)
        if not last:
            cs = unfold_c(_mlp_call(fold_c(cs_mix), norm_mlp_g[l], csc2, csh2, cg2, w1, w2, final_norm_g,
                                    tm_cf, False))
    return xs
```

```python
import functools

import jax
import jax.numpy as jnp
from jax import lax
from jax.experimental import pallas as pl
from jax.experimental.pallas import tpu as pltpu

F32 = jnp.float32
BF16 = jnp.bfloat16

EPS = 1e-6
LANES = 128
SUBLANES = 8
VMEM_LIMIT = 56 * 1024 * 1024

LRU_BLOCK = 64
LRU_C = 8.0
CONV_WIDTH = 4
CONV_LEFT = 2
DN_HEADS = 8
DN_DK = 128
DN_CHUNK = 64
N_Q_HEADS = 8
N_KV_HEADS = 4
Q_PER_KV = N_Q_HEADS // N_KV_HEADS
HEAD_DIM = 128
GRID_W = 64
ROPE_THETA = 10000.0
NEG_BIG = -1e30
LOG2E = 1.4426950408889634
DNPREP_ROWS = 2048
LRU_ROWS = 2048
SHORT_SEQ_ROWS = 4096
DELTA_ROWS = 256
FLASH_TQ = 256
FLASH_TK_CAP = 768
VT_PAD = 16
FLASH_UNROLL = 4


def _cparams(sem):
    return pltpu.CompilerParams(dimension_semantics=sem, vmem_limit_bytes=VMEM_LIMIT)


def _bdot(a, b):
    return jnp.dot(a.astype(BF16), b.astype(BF16), preferred_element_type=F32)


def _bdot_nt(a, b):
    return lax.dot_general(a.astype(BF16), b.astype(BF16), (((1,), (1,)), ((), ())),
                           preferred_element_type=F32)


def _bdot_tn(a, b):
    return lax.dot_general(a.astype(BF16), b.astype(BF16), (((0,), (0,)), ((), ())),
                           preferred_element_type=F32)


def _sigmoid(x):
    return jax.nn.sigmoid(x)


def _softplus(x):
    return jnp.maximum(x, 0.0) + jnp.log1p(jnp.exp(-jnp.abs(x)))


def _rms(x):
    return x * lax.rsqrt(jnp.mean(x * x, axis=-1, keepdims=True) + EPS)


def _col_tile(n, cap):
    best = LANES
    t = LANES
    while t <= min(n, cap):
        if n % t == 0:
            best = t
        t += LANES
    return best


def _mod_kernel(c_ref, w_ref, b_ref, o_ref):
    s = c_ref[...]
    s = s * _sigmoid(s)
    o_ref[...] = _bdot(s, w_ref[...]) + b_ref[...]


def _mod_call(cc, w_mod, b_mod):
    depth, d, n = w_mod.shape
    tn = _col_tile(n, 1536)
    return pl.pallas_call(
        _mod_kernel,
        grid=(depth, n // tn),
        in_specs=[pl.BlockSpec((SUBLANES, d), lambda l, j: (0, 0)),
                  pl.BlockSpec((None, d, tn), lambda l, j: (l, 0, j)),
                  pl.BlockSpec((None, 1, tn), lambda l, j: (l, 0, j))],
        out_specs=pl.BlockSpec((None, SUBLANES, tn), lambda l, j: (l, 0, j)),
        out_shape=jax.ShapeDtypeStruct((depth, SUBLANES, n), F32),
        compiler_params=_cparams(("parallel", "parallel")),
        name="mod",
    )(cc, w_mod, b_mod.reshape(depth, 1, n))


def _normlin_kernel(*refs, has_gates):
    if has_gates:
        x_ref, g_ref, sc_ref, sh_ref, w_ref, wg_ref, wgt_ref, o_ref, og_ref, ogt_ref, h_ref = refs
    else:
        x_ref, g_ref, sc_ref, sh_ref, w_ref, o_ref, h_ref = refs

    @pl.when(pl.program_id(2) == 0)
    def _():
        y = _rms(x_ref[...]) * g_ref[...]
        h_ref[...] = (y * (1.0 + sc_ref[...]) + sh_ref[...]).astype(BF16)
        if has_gates:
            og_ref[...] = jnp.dot(h_ref[...], wg_ref[...], preferred_element_type=F32)
            ogt_ref[...] = lax.dot_general(wgt_ref[...], h_ref[...], (((1,), (1,)), ((), ())),
                                           preferred_element_type=F32)

    o_ref[...] = jnp.dot(h_ref[...], w_ref[...], preferred_element_type=F32).astype(o_ref.dtype)


def _normlin_call(x, g, sc, sh, w, w_gates, w_gates_t, tm):
    bsz, seq, d = x.shape
    has_gates = w_gates is not None
    n = w.shape[1] - w_gates_t.shape[0] if has_gates else w.shape[1]
    tn = _col_tile(n, 2048)
    in_specs = [pl.BlockSpec((None, tm, d), lambda b, i, j: (b, i, 0)),
                pl.BlockSpec((1, d), lambda b, i, j: (0, 0)),
                pl.BlockSpec((None, 1, d), lambda b, i, j: (b, 0, 0)),
                pl.BlockSpec((None, 1, d), lambda b, i, j: (b, 0, 0)),
                pl.BlockSpec((d, tn), lambda b, i, j: (0, j))]
    out_specs = [pl.BlockSpec((None, tm, tn), lambda b, i, j: (b, i, j))]
    out_shape = [jax.ShapeDtypeStruct((bsz, seq, n), BF16)]
    args = [x, g.reshape(1, d), sc, sh, w]
    if has_gates:
        ng = w_gates.shape[1]
        in_specs.append(pl.BlockSpec((d, ng), lambda b, i, j: (0, 0)))
        out_specs.append(pl.BlockSpec((None, tm, ng), lambda b, i, j: (b, i, 0)))
        out_shape.append(jax.ShapeDtypeStruct((bsz, seq, ng), F32))
        ngt = w_gates_t.shape[0]
        in_specs.append(pl.BlockSpec((ngt, d), lambda b, i, j: (0, 0)))
        out_specs.append(pl.BlockSpec((None, ngt, tm), lambda b, i, j: (b, 0, i)))
        out_shape.append(jax.ShapeDtypeStruct((bsz, ngt, seq), F32))
        args += [w_gates, w_gates_t]
    out = pl.pallas_call(
        functools.partial(_normlin_kernel, has_gates=has_gates),
        grid=(bsz, seq // tm, n // tn),
        in_specs=in_specs,
        out_specs=out_specs,
        out_shape=out_shape,
        scratch_shapes=[pltpu.VMEM((tm, d), BF16)],
        compiler_params=_cparams(("parallel", "parallel", "arbitrary")),
        name="normlin",
    )(*args)
    return out if has_gates else out[0]


def _mlp_kernel(x_ref, g_ref, sc_ref, sh_ref, gate_ref, w1_ref, w2_ref, fg_ref, o_ref,
                h_ref, acc_ref, *, final_norm):
    k = pl.program_id(2)

    @pl.when(k == 0)
    def _():
        y = _rms(x_ref[...]) * g_ref[...]
        h_ref[...] = (y * (1.0 + sc_ref[...]) + sh_ref[...]).astype(BF16)
        acc_ref[...] = jnp.zeros_like(acc_ref)

    a = jnp.dot(h_ref[...], w1_ref[...], preferred_element_type=F32)
    a = jnp.square(jnp.maximum(a, 0.0)).astype(BF16)
    acc_ref[...] += jnp.dot(a, w2_ref[...], preferred_element_type=F32)

    @pl.when(k == pl.num_programs(2) - 1)
    def _():
        y = x_ref[...] + gate_ref[...] * acc_ref[...]
        if final_norm:
            y = _rms(y) * fg_ref[...]
        o_ref[...] = y


def _mlp_call(x, g, sc, sh, gate, w1, w2, fg, tm, final_norm):
    bsz, seq, d = x.shape
    f = w1.shape[1]
    tf = _col_tile(f, 2048)
    vec = pl.BlockSpec((None, 1, d), lambda b, i, k: (b, 0, 0))
    return pl.pallas_call(
        functools.partial(_mlp_kernel, final_norm=final_norm),
        grid=(bsz, seq // tm, f // tf),
        in_specs=[pl.BlockSpec((None, tm, d), lambda b, i, k: (b, i, 0)),
                  pl.BlockSpec((1, d), lambda b, i, k: (0, 0)),
                  vec, vec, vec,
                  pl.BlockSpec((d, tf), lambda b, i, k: (0, k)),
                  pl.BlockSpec((tf, d), lambda b, i, k: (k, 0)),
                  pl.BlockSpec((1, d), lambda b, i, k: (0, 0))],
        out_specs=pl.BlockSpec((None, tm, d), lambda b, i, k: (b, i, 0)),
        out_shape=jax.ShapeDtypeStruct((bsz, seq, d), F32),
        scratch_shapes=[pltpu.VMEM((tm, d), BF16), pltpu.VMEM((tm, d), F32)],
        compiler_params=_cparams(("parallel", "parallel", "arbitrary")),
        name="mlp",
    )(x, g.reshape(1, d), sc, sh, gate, w1, w2, fg.reshape(1, d))


def _batch_block(bsz, seq):
    return bsz if bsz * seq <= SHORT_SEQ_ROWS else 1


CONV_PAD = SUBLANES


def _fill_padded(xpad_ref, x_ref, seq, rows):
    zeros = jnp.zeros((CONV_PAD, xpad_ref.shape[1]), F32)
    xpad_ref[0:CONV_PAD, :] = zeros
    xpad_ref[CONV_PAD + seq:CONV_PAD + seq + CONV_PAD, :] = zeros

    def body(t, carry):
        r0 = pl.multiple_of(t * rows, rows)
        xpad_ref[pl.ds(r0 + CONV_PAD, rows), :] = x_ref[pl.ds(r0, rows), :].astype(F32)
        return carry

    lax.fori_loop(0, seq // rows, body, 0)


def _conv_tile(xpad_ref, r0, rows, w):
    acc = None
    for j in range(CONV_WIDTH):
        off = j - CONV_LEFT
        term = xpad_ref[pl.ds(r0 + (CONV_PAD + off), rows), :] * w[j:j + 1]
        acc = term if acc is None else acc + term
    return acc


def _dnprep_kernel(xs_ref, w_ref, os_ref, xpad_ref, *, seq, rows):
    for bi in range(xs_ref.shape[0]):
        _dnprep_one(xs_ref.at[bi], w_ref, os_ref.at[bi], xpad_ref, seq, rows)


def _dnprep_one(x_ref, w_ref, o_ref, xpad_ref, seq, rows):
    j = pl.program_id(1)
    scale = jnp.where(j < DN_HEADS, DN_DK ** -0.5, 1.0).astype(F32)
    _fill_padded(xpad_ref, x_ref, seq, rows)
    w = w_ref[...]

    def conv_silu(t):
        r0 = pl.multiple_of(t * rows, rows)
        y = _conv_tile(xpad_ref, r0, rows, w)
        return r0, y * _sigmoid(y)

    def qk_body(t, carry):
        r0, y = conv_silu(t)
        nrm = (y * lax.rsqrt(jnp.sum(y * y, axis=-1, keepdims=True) + EPS)) * scale
        o_ref[pl.ds(r0, rows), :] = nrm.astype(o_ref.dtype)
        return carry

    def v_body(t, carry):
        r0, y = conv_silu(t)
        o_ref[pl.ds(r0, rows), :] = y.astype(o_ref.dtype)
        return carry

    @pl.when(j < 2 * DN_HEADS)
    def _():
        lax.fori_loop(0, seq // rows, qk_body, 0)

    @pl.when(j >= 2 * DN_HEADS)
    def _():
        lax.fori_loop(0, seq // rows, v_body, 0)


def _dnprep_call(proj, conv_w, col0_blocks, rows):
    bsz, seq, _ = proj.shape
    ncol = conv_w.shape[1] // LANES
    nb = _batch_block(bsz, seq)
    return pl.pallas_call(
        functools.partial(_dnprep_kernel, seq=seq, rows=rows),
        grid=(bsz // nb, ncol),
        in_specs=[pl.BlockSpec((nb, seq, LANES), lambda b, j: (b, 0, col0_blocks + j)),
                  pl.BlockSpec((CONV_WIDTH, LANES), lambda b, j: (0, j))],
        out_specs=pl.BlockSpec((nb, seq, LANES), lambda b, j: (b, 0, j)),
        out_shape=jax.ShapeDtypeStruct((bsz, seq, ncol * LANES), BF16),
        scratch_shapes=[pltpu.VMEM((seq + 2 * CONV_PAD, LANES), F32)],
        compiler_params=_cparams(("parallel", "parallel")),
        name="dnprep",
    )(proj, conv_w)


def _scan_tile(a, b, carry, rev, rows):
    nv = rows // SUBLANES
    lanes = a.shape[1]
    a = a.reshape(nv, SUBLANES, lanes)
    b = b.reshape(nv, SUBLANES, lanes)
    row = lax.broadcasted_iota(jnp.int32, a.shape, 1)
    s = 1
    while s < SUBLANES:
        shift = SUBLANES - s if rev else s
        a_s = pltpu.roll(a, shift, 1)
        b_s = pltpu.roll(b, shift, 1)
        valid = (row < SUBLANES - s) if rev else (row >= s)
        b = b + a * jnp.where(valid, b_s, 0.0)
        a = a * jnp.where(valid, a_s, 1.0)
        s *= 2
    hs = [None] * nv
    for v in (range(nv - 1, -1, -1) if rev else range(nv)):
        h = a[v] * carry + b[v]
        carry = h[0:1] if rev else h[SUBLANES - 1:SUBLANES]
        hs[v] = h
    return jnp.concatenate(hs, axis=0), carry


def _gelu_tanh(x):
    return 0.5 * x * (1.0 + jnp.tanh(0.7978845608028654 * (x + 0.044715 * (x * x * x))))


def _lru_kernel(xs_ref, ys_ref, cw_ref, cb_ref, wa_ref, wi_ref, ba_ref, bi_ref, lam_ref, h0s_ref,
                os_ref, hts_ref, xpad_ref, xc_ref, hf_ref, *, seq, rows):
    for bi in range(xs_ref.shape[0]):
        _lru_one(xs_ref.at[bi], ys_ref.at[bi], cw_ref, cb_ref, wa_ref, wi_ref, ba_ref, bi_ref, lam_ref,
                 h0s_ref.at[bi], os_ref.at[bi], hts_ref.at[bi], xpad_ref, xc_ref, hf_ref, seq, rows)


def _lru_one(x_ref, y_ref, cw_ref, cb_ref, wa_ref, wi_ref, ba_ref, bi_ref, lam_ref, h0_ref,
             o_ref, ht_ref, xpad_ref, xc_ref, hf_ref, seq, rows):
    _fill_padded(xpad_ref, x_ref, seq, rows)
    cw = cw_ref[...]
    cb = cb_ref[...]
    nt = seq // rows

    def gates(xc, d):
        r = 1.0 / (1.0 + jnp.exp2(_bdot(xc, wa_ref[d]) - LOG2E * ba_ref[d:d + 1, :]))
        i = 1.0 / (1.0 + jnp.exp2(_bdot(xc, wi_ref[d]) - LOG2E * bi_ref[d:d + 1, :]))
        neg_rate = LRU_C * _softplus(-lam_ref[d:d + 1, :])
        a = jnp.exp2(r * (-LOG2E * neg_rate))
        b = jnp.sqrt(jnp.tanh(r * neg_rate) * (a * a + 1.0)) * (i * xc)
        return a, b

    def fwd_body(t, carry):
        r0 = pl.multiple_of(t * rows, rows)
        xc = _conv_tile(xpad_ref, r0, rows, cw) + cb
        xc_ref[pl.ds(r0, rows), :] = xc
        a, b = gates(xc, 0)
        h, carry = _scan_tile(a, b, carry, False, rows)
        hf_ref[pl.ds(r0, rows), :] = h
        return carry

    carry_f = lax.fori_loop(0, nt, fwd_body, h0_ref[0:1, :])

    def bwd_body(t, carry):
        r0 = pl.multiple_of((nt - 1 - t) * rows, rows)
        xc = xc_ref[pl.ds(r0, rows), :]
        a, b = gates(xc, 1)
        h, carry = _scan_tile(a, b, carry, True, rows)
        y = y_ref[pl.ds(r0, rows), :].astype(F32)
        o_ref[pl.ds(r0, rows), :] = ((hf_ref[pl.ds(r0, rows), :] + h) * _gelu_tanh(y)).astype(o_ref.dtype)
        return carry

    carry_b = lax.fori_loop(0, nt, bwd_body, h0_ref[1:2, :])
    ht_ref[0:1, :] = carry_f
    ht_ref[1:2, :] = carry_b


def _lru_call(proj, cw, cb, wa, wi, ba, bi, lam, h0, rows):
    bsz, seq, _ = proj.shape
    width = cw.shape[1]
    ncol = width // LANES
    vec2 = pl.BlockSpec((2, LANES), lambda b, j: (0, j))
    wspec = pl.BlockSpec((2, None, LANES, LANES), lambda b, j: (0, j, 0, 0))
    nb = _batch_block(bsz, seq)
    return pl.pallas_call(
        functools.partial(_lru_kernel, seq=seq, rows=rows),
        grid=(bsz // nb, ncol),
        in_specs=[pl.BlockSpec((nb, seq, LANES), lambda b, j: (b, 0, j)),
                  pl.BlockSpec((nb, seq, LANES), lambda b, j: (b, 0, ncol + j)),
                  pl.BlockSpec((CONV_WIDTH, LANES), lambda b, j: (0, j)),
                  pl.BlockSpec((1, LANES), lambda b, j: (0, j)),
                  wspec, wspec, vec2, vec2, vec2,
                  pl.BlockSpec((nb, 2, LANES), lambda b, j: (b, 0, j))],
        out_specs=[pl.BlockSpec((nb, seq, LANES), lambda b, j: (b, 0, j)),
                   pl.BlockSpec((nb, 2, LANES), lambda b, j: (b, 0, j))],
        out_shape=[jax.ShapeDtypeStruct((bsz, seq, width), BF16),
                   jax.ShapeDtypeStruct((bsz, 2, width), F32)],
        scratch_shapes=[pltpu.VMEM((seq + 2 * CONV_PAD, LANES), F32),
                        pltpu.VMEM((seq, LANES), F32),
                        pltpu.VMEM((seq, LANES), F32)],
        compiler_params=_cparams(("parallel", "parallel")),
        name="lru",
    )(proj, proj, cw, cb.reshape(1, width), wa, wi, ba, bi, lam, h0)


DN_GROUP = 2 * DN_CHUNK


def _split3(x):
    x1 = x.astype(BF16)
    r1 = x - x1.astype(F32)
    x2 = r1.astype(BF16)
    x3 = (r1 - x2.astype(F32)).astype(BF16)
    return x1, x2, x3


def _delta_kernel(*refs, rows):
    (qf_ref, kf_ref, vf_ref, grf_ref, grtf_ref, qb_ref, kb_ref, vb_ref, grb_ref, grtb_ref,
     cvec_ref, ccol_ref, cum_ref, cumt_ref, ones_ref, s0_ref, of_ref, ob_ref, st_ref, s_ref) = refs
    step = pl.program_id(1)
    q_refs, k_refs, v_refs = (qf_ref, qb_ref), (kf_ref, kb_ref), (vf_ref, vb_ref)
    o_refs = (of_ref, ob_ref)
    dirs = (0, 1)

    @pl.when(step == 0)
    def _():
        s_ref[...] = s0_ref[...]

    ones = ones_ref[...]

    def mm(m, parts):
        return sum(jnp.dot(m, p, preferred_element_type=F32) for p in parts)

    sig, gc, e_gc, e_rest, e_gl, gct = [], [], [], [], [], []
    for d, (gr_ref, grt_ref) in enumerate(((grf_ref, grtf_ref), (grb_ref, grtb_ref))):
        raw = gr_ref[...]
        g = -jnp.exp(cvec_ref[0:1, :]) * _softplus(raw + cvec_ref[1:2, :])
        parts = _split3(g)
        gc_d = mm(cum_ref[d], parts)
        gl_d = mm(ones, parts)
        sig.append(_sigmoid(raw))
        gc.append(gc_d)
        e_gc.append(jnp.exp(gc_d))
        e_rest.append(jnp.exp(gl_d - gc_d))
        e_gl.append(jnp.exp(gl_d))
        gt = -jnp.exp(ccol_ref[0]) * _softplus(grt_ref[...] + ccol_ref[1])
        cumt = cumt_ref[d]
        gct.append(sum(jnp.dot(p, cumt, preferred_element_type=F32) for p in _split3(gt)))

    ri = lax.broadcasted_iota(jnp.int32, (DN_GROUP, DN_GROUP), 0)
    ci = lax.broadcasted_iota(jnp.int32, (DN_GROUP, DN_GROUP), 1)
    same = (ri // DN_CHUNK) == (ci // DN_CHUNK)
    incl = (same & (ri >= ci), same & (ri <= ci))
    strict = (same & (ri > ci), same & (ri < ci))

    ngroups = rows // DN_GROUP
    zeros_c = jnp.zeros((DN_CHUNK, DN_DK), F32)

    def bcast(x, rsl, lane):
        return jnp.broadcast_to(x[rsl, lane:lane + 1], (DN_GROUP, LANES))

    heads = range(DN_HEADS)
    groups = (list(range(ngroups)), list(range(ngroups - 1, -1, -1)))
    units = [(d, gi, h) for d in dirs for gi in groups[d] for h in heads]
    qd, kd, egl_b, rhs, kk, qk = {}, {}, {}, {}, {}, {}
    for u in units:
        d, gi, h = u
        rsl = slice(gi * DN_GROUP, (gi + 1) * DN_GROUP)
        lb = d * DN_HEADS + h
        lg = 2 * DN_HEADS + d * DN_HEADS + h
        csl = slice(h * DN_DK, (h + 1) * DN_DK)
        q16 = q_refs[d][rsl, csl]
        k16 = k_refs[d][rsl, csl]
        q_h = q16.astype(F32)
        k_h = k16.astype(F32)
        beta_b = bcast(sig[d], rsl, lb)
        egc_b = bcast(e_gc[d], rsl, lg)
        kb_h = k_h * beta_b
        kk[u] = _bdot_nt(kb_h, k16)
        qk[u] = _bdot_nt(q16, k16)
        rhs[u] = jnp.concatenate([v_refs[d][rsl, csl].astype(F32) * beta_b, kb_h * egc_b], axis=1)
        qd[u] = q_h * egc_b
        kd[u] = k_h * bcast(e_rest[d], rsl, lg)
        egl_b[u] = bcast(e_gl[d], rsl, lg)
    p, a16, attn = {}, {}, {}
    for u in units:
        d, gi, h = u
        rsl = slice(gi * DN_GROUP, (gi + 1) * DN_GROUP)
        lg = 2 * DN_HEADS + d * DN_HEADS + h
        diff = bcast(gc[d], rsl, lg) - gct[d][lg:lg + 1, rsl]
        decay = jnp.exp(jnp.where(incl[d], diff, NEG_BIG))
        m = jnp.where(strict[d], kk[u] * decay, 0.0)
        attn[u] = qk[u] * decay
        p[u] = -m
        a16[u] = m.astype(BF16)
    npow = 1
    while npow < DN_CHUNK // 2:
        a = {u: jnp.dot(a16[u], a16[u], preferred_element_type=F32) for u in units}
        a16 = {u: a[u].astype(BF16) for u in units}
        p = {u: p[u] + a[u] + jnp.dot(p[u].astype(BF16), a16[u], preferred_element_type=F32) for u in units}
        npow *= 2
    uw = {u: rhs[u] + _bdot(p[u], rhs[u]) for u in units}
    chains = [(d, h) for d in dirs for h in heads]
    s = {c: s_ref[c[1], c[0]] for c in chains}
    slots = tuple([(gi, cc) for gi in groups[d] for cc in ((1, 0) if d else (0, 1))] for d in dirs)
    for slot in range(2 * ngroups):
        r2, v_new = {}, {}
        for c in chains:
            d, h = c
            gi, cc = slots[d][slot]
            cs = slice(cc * DN_CHUNK, (cc + 1) * DN_CHUNK)
            r2[c] = _bdot(jnp.concatenate([uw[d, gi, h][cs, DN_DK:], qd[d, gi, h][cs]], axis=0), s[c])
        for c in chains:
            d, h = c
            gi, cc = slots[d][slot]
            cs = slice(cc * DN_CHUNK, (cc + 1) * DN_CHUNK)
            v_new[c] = uw[d, gi, h][cs, :DN_DK] - r2[c][:DN_CHUNK]
        for c in chains:
            d, h = c
            gi, cc = slots[d][slot]
            c0 = cc * DN_CHUNK
            cs = slice(c0, c0 + DN_CHUNK)
            v_pad = (jnp.concatenate([v_new[c], zeros_c], axis=0) if cc == 0
                     else jnp.concatenate([zeros_c, v_new[c]], axis=0))
            o_c = r2[c][DN_CHUNK:] + _bdot(attn[d, gi, h][cs], v_pad)
            o_refs[d][gi * DN_GROUP + c0:gi * DN_GROUP + c0 + DN_CHUNK,
                      h * DN_DK:(h + 1) * DN_DK] = o_c.astype(o_refs[d].dtype)
        for c in chains:
            d, h = c
            gi, cc = slots[d][slot]
            cs = slice(cc * DN_CHUNK, (cc + 1) * DN_CHUNK)
            egl = jnp.concatenate([egl_b[d, gi, h][cs], egl_b[d, gi, h][cs]], axis=0)
            s[c] = s[c] * egl + _bdot_tn(kd[d, gi, h][cs], v_new[c])
    for c in chains:
        s_ref[c[1], c[0]] = s[c]

    @pl.when(step == pl.num_programs(1) - 1)
    def _():
        st_ref[...] = s_ref[...]


def _delta_consts(rows):
    t = jnp.arange(rows)
    same = (t[:, None] // DN_CHUNK) == (t[None, :] // DN_CHUNK)
    cum = jnp.stack([same & (t[:, None] >= t[None, :]), same & (t[:, None] <= t[None, :])])
    return cum.astype(BF16), jnp.swapaxes(cum, 1, 2).astype(BF16), same.astype(BF16)


def _delta_call(qkv, gates, gates_t, cvec, ccol, s0, rows):
    bsz, seq, _ = qkv.shape
    width = DN_HEADS * DN_DK
    nsteps = seq // rows
    cum, cumt, ones = _delta_consts(rows)

    def per_dir(rev):
        blk = (lambda i: nsteps - 1 - i) if rev else (lambda i: i)
        return [pl.BlockSpec((None, rows, width), lambda b, i: (b, blk(i), 0)),
                pl.BlockSpec((None, rows, width), lambda b, i: (b, blk(i), 1)),
                pl.BlockSpec((None, rows, width), lambda b, i: (b, blk(i), 2)),
                pl.BlockSpec((None, rows, LANES), lambda b, i: (b, blk(i), 0)),
                pl.BlockSpec((None, 4 * DN_HEADS, rows), lambda b, i: (b, 0, blk(i)))]

    full = lambda shape: pl.BlockSpec(shape, lambda b, i: (0,) * len(shape))
    state = pl.BlockSpec((None, DN_HEADS, 2, DN_DK, DN_DK), lambda b, i: (b, 0, 0, 0, 0))
    o_fwd = pl.BlockSpec((None, rows, width), lambda b, i: (b, i, 0))
    o_bwd = pl.BlockSpec((None, rows, width), lambda b, i: (b, nsteps - 1 - i, 0))
    data = [qkv, qkv, qkv, gates, gates_t]
    return pl.pallas_call(
        functools.partial(_delta_kernel, rows=rows),
        grid=(bsz, nsteps),
        in_specs=per_dir(False) + per_dir(True) + [
            full((SUBLANES, LANES)), full((2, 4 * DN_HEADS, rows)),
            full((2, rows, rows)), full((2, rows, rows)), full((rows, rows)), state],
        out_specs=[o_fwd, o_bwd, state],
        out_shape=[jax.ShapeDtypeStruct((bsz, seq, width), BF16),
                   jax.ShapeDtypeStruct((bsz, seq, width), BF16),
                   jax.ShapeDtypeStruct((bsz, DN_HEADS, 2, DN_DK, DN_DK), F32)],
        scratch_shapes=[pltpu.VMEM((DN_HEADS, 2, DN_DK, DN_DK), F32)],
        compiler_params=_cparams(("parallel", "arbitrary")),
        name="delta",
    )(*data, *data, cvec, ccol, cum, cumt, ones, s0)


def _evenout_kernel(lru_ref, of_ref, ob_ref, z_ref, g_ref, w_ref, res_ref, gate_ref, o_ref, a_ref):
    width = lru_ref.shape[1]
    a_ref[:, 0:width] = lru_ref[...].astype(BF16)
    for h in range(DN_HEADS):
        csl = slice(h * DN_DK, (h + 1) * DN_DK)
        o = of_ref[:, csl].astype(F32) + ob_ref[:, csl].astype(F32)
        z = z_ref[:, csl].astype(F32)
        y = (_rms(o) * g_ref[:, csl]) * (z * _sigmoid(z))
        a_ref[:, width + h * DN_DK:width + (h + 1) * DN_DK] = y.astype(BF16)
    o_ref[...] = res_ref[...] + gate_ref[...] * jnp.dot(a_ref[...], w_ref[...], preferred_element_type=F32)


def _evenout_call(lru, o_f, o_b, proj, z_block, dn_g, w_out, res, gate, tm):
    bsz, seq, width = lru.shape
    dnw = o_f.shape[2]
    d = res.shape[2]
    row = lambda c: pl.BlockSpec((None, tm, c), lambda b, i: (b, i, 0))
    return pl.pallas_call(
        _evenout_kernel,
        grid=(bsz, seq // tm),
        in_specs=[row(width), row(dnw), row(dnw),
                  pl.BlockSpec((None, tm, dnw), lambda b, i: (b, i, z_block)),
                  pl.BlockSpec((1, dnw), lambda b, i: (0, 0)),
                  pl.BlockSpec((width + dnw, d), lambda b, i: (0, 0)),
                  row(d),
                  pl.BlockSpec((None, 1, d), lambda b, i: (b, 0, 0))],
        out_specs=row(d),
        out_shape=jax.ShapeDtypeStruct((bsz, seq, d), F32),
        scratch_shapes=[pltpu.VMEM((tm, width + dnw), BF16)],
        compiler_params=_cparams(("parallel", "parallel")),
        name="evenout",
    )(lru, o_f, o_b, proj, dn_g, w_out, res, gate)


def _linres_kernel(a_ref, w_ref, res_ref, gate_ref, o_ref):
    o_ref[...] = res_ref[...] + gate_ref[...] * jnp.dot(a_ref[...].astype(BF16), w_ref[...],
                                                        preferred_element_type=F32)


def _linres_call(a, w, res, gate, tm):
    bsz, seq, kdim = a.shape
    d = res.shape[2]
    return pl.pallas_call(
        _linres_kernel,
        grid=(bsz, seq // tm),
        in_specs=[pl.BlockSpec((None, tm, kdim), lambda b, i: (b, i, 0)),
                  pl.BlockSpec((kdim, d), lambda b, i: (0, 0)),
                  pl.BlockSpec((None, tm, d), lambda b, i: (b, i, 0)),
                  pl.BlockSpec((None, 1, d), lambda b, i: (b, 0, 0))],
        out_specs=pl.BlockSpec((None, tm, d), lambda b, i: (b, i, 0)),
        out_shape=jax.ShapeDtypeStruct((bsz, seq, d), F32),
        compiler_params=_cparams(("parallel", "parallel")),
        name="linres",
    )(a, w, res, gate)


def _aprep_kernel(*refs, rope, want_q):
    if rope:
        x_ref, qg_ref, kg_ref, cos_ref, sin_ref = refs[:5]
        outs = refs[5:]
    else:
        x_ref, qg_ref, kg_ref = refs[:3]
        outs = refs[3:]
    if want_q:
        q_ref, k_ref, v_ref = outs
    else:
        k_ref, v_ref = outs

    if rope:
        cos = cos_ref[...]
        sin = sin_ref[...]

    def norm_rope(x, g):
        y = _rms(x) * g
        if rope:
            y = y * cos + pltpu.roll(y, HEAD_DIM // 2, 1) * sin
        return y

    if want_q:
        for h in range(N_Q_HEADS):
            x = x_ref[:, h * HEAD_DIM:(h + 1) * HEAD_DIM].astype(F32)
            q_ref[h] = (norm_rope(x, qg_ref[...]) * (HEAD_DIM ** -0.5 * LOG2E)).astype(BF16)
    for h in range(N_KV_HEADS):
        c0 = (N_Q_HEADS + h) * HEAD_DIM
        k_ref[h] = norm_rope(x_ref[:, c0:c0 + HEAD_DIM].astype(F32), kg_ref[...]).astype(BF16)
        c1 = (N_Q_HEADS + N_KV_HEADS + h) * HEAD_DIM
        v_ref[h, 0:HEAD_DIM, :] = x_ref[:, c1:c1 + HEAD_DIM].astype(F32).T.astype(BF16)
        pad_row = lax.broadcasted_iota(jnp.int32, (VT_PAD, x_ref.shape[0]), 0)
        v_ref[h, HEAD_DIM:HEAD_DIM + VT_PAD, :] = jnp.where(pad_row == 0, 1.0, 0.0).astype(BF16)


def _aprep_call(qkv, qg, kg, rope_tabs, want_q, tm):
    bsz, seq, cols = qkv.shape
    rope = rope_tabs is not None
    in_specs = [pl.BlockSpec((None, tm, cols), lambda b, i: (b, i, 0)),
                pl.BlockSpec((1, HEAD_DIM), lambda b, i: (0, 0)),
                pl.BlockSpec((1, HEAD_DIM), lambda b, i: (0, 0))]
    args = [qkv, qg.reshape(1, HEAD_DIM), kg.reshape(1, HEAD_DIM)]
    if rope:
        in_specs += [pl.BlockSpec((tm, HEAD_DIM), lambda b, i: (i, 0))] * 2
        args += list(rope_tabs)
    hspec = lambda n: pl.BlockSpec((None, n, tm, HEAD_DIM), lambda b, i: (b, 0, i, 0))
    hshape = lambda n: jax.ShapeDtypeStruct((bsz, n, seq, HEAD_DIM), BF16)
    tspec = lambda n: pl.BlockSpec((None, n, HEAD_DIM + VT_PAD, tm), lambda b, i: (b, 0, 0, i))
    tshape = lambda n: jax.ShapeDtypeStruct((bsz, n, HEAD_DIM + VT_PAD, seq), BF16)
    out_specs = [hspec(N_KV_HEADS), tspec(N_KV_HEADS)]
    out_shape = [hshape(N_KV_HEADS), tshape(N_KV_HEADS)]
    if want_q:
        out_specs = [hspec(N_Q_HEADS)] + out_specs
        out_shape = [hshape(N_Q_HEADS)] + out_shape
    return pl.pallas_call(
        functools.partial(_aprep_kernel, rope=rope, want_q=want_q),
        grid=(bsz, seq // tm),
        in_specs=in_specs,
        out_specs=out_specs,
        out_shape=out_shape,
        compiler_params=_cparams(("parallel", "parallel")),
        name="aprep_x" if rope else "aprep_ctx",
    )(*args)


def _rope_tables(n_tokens):
    rows = n_tokens // GRID_W
    axis = HEAD_DIM // 2
    inv_freq = ROPE_THETA ** (-jnp.arange(0, axis, 2, dtype=F32) / axis)
    ang_r = jnp.arange(rows).astype(F32)[:, None] * inv_freq[None]
    ang_c = jnp.arange(GRID_W).astype(F32)[:, None] * inv_freq[None]
    cr, sr = (jnp.repeat(t, GRID_W, axis=0) for t in (jnp.cos(ang_r), jnp.sin(ang_r)))
    cc, sc = (jnp.tile(t, (rows, 1)) for t in (jnp.cos(ang_c), jnp.sin(ang_c)))
    cos = jnp.concatenate([cr, cc, cr, cc], axis=-1)
    sin = jnp.concatenate([-sr, -sc, sr, sc], axis=-1)
    return cos, sin


def _rope_perm():
    quarter = HEAD_DIM // 4
    return jnp.concatenate([jnp.arange(0, quarter), jnp.arange(2 * quarter, 3 * quarter),
                            jnp.arange(quarter, 2 * quarter), jnp.arange(3 * quarter, 4 * quarter)])


def _flash_kernel(q_ref, k_ref, vt_ref, o_ref, s0_ref, sa_ref, sb_ref, m0_ref, ma_ref, mb_ref, *,
                  tq, tk, nk, nt):
    m_rows = Q_PER_KV * tq

    def q_tile(i):
        r0 = pl.multiple_of(i * tq, tq)
        return q_ref[:, pl.ds(r0, tq), :].reshape(m_rows, HEAD_DIM)

    def scores(q, j, s_ref, cmax_ref):
        c0 = pl.multiple_of(j * tk, tk)
        st = lax.dot_general(k_ref[pl.ds(c0, tk), :], q, (((1,), (1,)), ((), ())),
                             preferred_element_type=F32)
        s_ref[...] = st
        cmax_ref[...] = jnp.max(st, axis=0, keepdims=True)

    def update(j, s_ref, cmax_ref, carry):
        m, acc = carry
        m_new = jnp.maximum(m, cmax_ref[...])
        alpha = jnp.exp2(m - m_new)
        p = jnp.exp2(s_ref[...] - m_new)
        vt = vt_ref[:, pl.ds(pl.multiple_of(j * tk, tk), tk)]
        acc = alpha * acc + jnp.dot(vt, p.astype(BF16), preferred_element_type=F32)
        return m_new, acc

    bufs = ((sa_ref, ma_ref), (sb_ref, mb_ref))

    def tile_body(i, _):
        q = q_tile(i)
        q_next = q_tile(jnp.minimum(i + 1, nt - 1))

        def run(j0, count, carry, last):
            for u in range(count):
                j = j0 + u
                if u + 1 < count or not last:
                    scores(q, j + 1, *bufs[(u + 1) % 2])
                else:
                    scores(q_next, 0, s0_ref, m0_ref)
                carry = update(j, *bufs[u % 2], carry)
            return carry

        carry = (jnp.full((1, m_rows), NEG_BIG, F32), jnp.zeros((HEAD_DIM + VT_PAD, m_rows), F32))
        scores(q, 1, sa_ref, ma_ref)
        carry = update(0, s0_ref, m0_ref, carry)
        rest = nk - 1
        niter = (rest - 1) // FLASH_UNROLL
        carry = lax.fori_loop(0, niter, lambda jj, c: run(1 + jj * FLASH_UNROLL, FLASH_UNROLL, c, False),
                              carry)
        _, acc = run(1 + niter * FLASH_UNROLL, rest - niter * FLASH_UNROLL, carry, True)
        o = (acc[:HEAD_DIM] / acc[HEAD_DIM:HEAD_DIM + 1]).T
        r0 = pl.multiple_of(i * tq, tq)
        for r in range(Q_PER_KV):
            o_ref[pl.ds(r0, tq), r * HEAD_DIM:(r + 1) * HEAD_DIM] = o[r * tq:(r + 1) * tq].astype(o_ref.dtype)
        return 0

    scores(q_tile(0), 0, s0_ref, m0_ref)
    lax.fori_loop(0, nt, tile_body, 0)


def _flash_call(q, k, vt, tq, tk):
    bsz, _, seq, _ = q.shape
    lk = k.shape[2]
    nk = lk // tk
    assert nk >= 2, "the score pipeline needs at least two key blocks"
    q5 = q.reshape(bsz, N_KV_HEADS, Q_PER_KV, seq, HEAD_DIM)
    return pl.pallas_call(
        functools.partial(_flash_kernel, tq=tq, tk=tk, nk=nk, nt=seq // tq),
        grid=(bsz, N_KV_HEADS),
        in_specs=[pl.BlockSpec((None, None, Q_PER_KV, seq, HEAD_DIM), lambda b, g: (b, g, 0, 0, 0)),
                  pl.BlockSpec((None, None, lk, HEAD_DIM), lambda b, g: (b, g, 0, 0)),
                  pl.BlockSpec((None, None, HEAD_DIM + VT_PAD, lk), lambda b, g: (b, g, 0, 0))],
        out_specs=pl.BlockSpec((None, seq, Q_PER_KV * HEAD_DIM), lambda b, g: (b, 0, g)),
        out_shape=jax.ShapeDtypeStruct((bsz, seq, N_Q_HEADS * HEAD_DIM), BF16),
        scratch_shapes=([pltpu.VMEM((tk, Q_PER_KV * tq), F32)] * 3
                        + [pltpu.VMEM((1, Q_PER_KV * tq), F32)] * 3),
        compiler_params=_cparams(("parallel", "parallel")),
        name="flash",
    )(q5, k, vt)


def _row_tile(seq, cap):
    t = min(seq, cap)
    while seq % t:
        t //= 2
    return t


def _even_weights(w_in, lru_wa, lru_wi, dn_alog, dn_dtb, dn_g):
    d, cols = w_in.shape
    ngate = 4 * DN_HEADS
    main = cols - ngate
    w_gates = jnp.concatenate([w_in[:, main:], jnp.zeros((d, LANES - ngate), F32)], axis=1)

    def blockdiag(w):
        w = w * (-LOG2E)
        z = jnp.zeros_like(w[:, 0::2])
        top = jnp.concatenate([w[:, 0::2], z], axis=-1)
        bot = jnp.concatenate([z, w[:, 1::2]], axis=-1)
        return jnp.concatenate([top, bot], axis=-2).astype(BF16)

    alog = dn_alog.reshape(-1)
    dtb = dn_dtb.reshape(-1)
    cvec = jnp.zeros((SUBLANES, LANES), F32)
    cvec = cvec.at[0, 2 * DN_HEADS:ngate].set(alog).at[1, 2 * DN_HEADS:ngate].set(dtb)

    ccol = jnp.zeros((2, ngate), F32).at[0, 2 * DN_HEADS:].set(alog).at[1, 2 * DN_HEADS:].set(dtb)

    return dict(w_in=w_in.astype(BF16), w_gates=w_gates.astype(BF16),
                w_gates_t=w_in[:, main:].T.astype(BF16),
                wa=blockdiag(lru_wa), wi=blockdiag(lru_wi), cvec=cvec, ccol=ccol,
                dn_g=jnp.tile(dn_g, DN_HEADS).reshape(1, -1))


def _even_stream(h_in, res, gate, norm_g, sc, sh, ew, lru_cw, lru_cb, lru_ba, lru_bi, lru_lam, dn_cw,
                 w_out, h0, s0, tm_in, tm, rows, dn_rows, fold_batch):
    width = lru_cw.shape[1]
    ngate = 4 * DN_HEADS
    b0, l0 = h_in.shape[:2]
    fold = (lambda a: a.reshape(1, b0 * l0, a.shape[-1])) if fold_batch else (lambda a: a)
    unfold = (lambda a: a.reshape(b0, l0, a.shape[-1])) if fold_batch else (lambda a: a)
    proj, gates, gates_t = _normlin_call(fold(h_in), norm_g, sc, sh, ew["w_in"], ew["w_gates"], ew["w_gates_t"],
                                         tm_in)
    proj, gates = unfold(proj), unfold(gates)
    if fold_batch:
        gates_t = jnp.transpose(gates_t.reshape(gates_t.shape[1], b0, l0), (1, 0, 2))
    lru, h_t = _lru_call(proj, lru_cw, lru_cb, ew["wa"], ew["wi"], lru_ba, lru_bi, lru_lam, h0,
                         _row_tile(proj.shape[1], LRU_ROWS))
    qkv = _dnprep_call(proj, dn_cw, 2 * width // LANES, _row_tile(proj.shape[1], DNPREP_ROWS))
    ccol = jnp.broadcast_to(ew["ccol"][:, :, None], (2, ngate, dn_rows))
    o_f, o_b, s_t = _delta_call(qkv, gates, gates_t, ew["cvec"], ccol, s0, dn_rows)
    z_block = (2 * width + dn_cw.shape[1]) // (DN_HEADS * DN_DK)
    out = _evenout_call(fold(lru), fold(o_f), fold(o_b), fold(proj), z_block, ew["dn_g"], w_out, fold(res), gate, tm)
    return unfold(out), h_t, s_t


def kernel(x, c, ctx, c_ctx, norm_mix_g, norm_mlp_g, w_mod, b_mod, mlp_w1, mlp_w2, even_w_in, lru_conv_w,
           lru_conv_b, lru_w_a, lru_b_a, lru_w_i, lru_b_i, lru_lambda, dn_conv_w, dn_a_log, dn_dt_bias,
           dn_norm_g, even_w_out, attn_w_qkv, attn_q_norm_g, attn_k_norm_g, attn_w_o, final_norm_g):
    bsz, seq, d = x.shape
    lc = ctx.shape[1]
    depth = w_mod.shape[0]
    tm_x = _row_tile(seq, 512)
    tm_cf = _row_tile(bsz * lc, 1024)
    fold_c = lambda a: a.reshape(1, bsz * lc, a.shape[-1])
    unfold_c = lambda a: a.reshape(bsz, lc, a.shape[-1])
    tm_mlp = _row_tile(seq, 1024)
    rows_x = _row_tile(seq, 256)
    rows_c = _row_tile(lc, 256)

    cc = jnp.zeros((SUBLANES, d), F32).at[:bsz].set(c).at[bsz].set(c_ctx)
    mod = _mod_call(cc, w_mod, b_mod).reshape(depth, SUBLANES, 6, d)

    rope = _rope_tables(seq)
    xs, cs = x, ctx
    for l in range(depth):
        last = l == depth - 1
        j = l // 2
        mx = [mod[l, :bsz, k][:, None, :] for k in range(6)]
        mc = [mod[l, bsz, k][None, None, :] for k in range(6)]
        sh1, sc1, g1, sh2, sc2, g2 = mx
        csh1, csc1, cg1, csh2, csc2, cg2 = mc
        w1 = mlp_w1[l].astype(BF16)
        w2 = mlp_w2[l].astype(BF16)
        if l % 2 == 0:
            ew = _even_weights(even_w_in[j], lru_w_a[j], lru_w_i[j], dn_a_log[j], dn_dt_bias[j], dn_norm_g[j])
            w_out = even_w_out[j].astype(BF16)
            h0 = jnp.zeros((bsz, 2, lru_conv_w.shape[2]), F32)
            s0 = jnp.zeros((bsz, DN_HEADS, 2, DN_DK, DN_DK), F32)
            common = (ew, lru_conv_w[j], lru_conv_b[j], lru_b_a[j], lru_b_i[j], lru_lambda[j], dn_conv_w[j], w_out)
            cs_mix, h_t, s_t = _even_stream(cs, cs, cg1, norm_mix_g[l], csc1, csh1, *common, h0, s0,
                                            tm_cf, tm_cf, rows_c, rows_c, True)
            xs, _, _ = _even_stream(xs, xs, g1, norm_mix_g[l], sc1, sh1, *common, h_t, s_t,
                                    tm_mlp, tm_mlp, rows_x, _row_tile(seq, DELTA_ROWS), False)
        else:
            perm = _rope_perm()
            nqk = (N_Q_HEADS + N_KV_HEADS) * HEAD_DIM
            w_qk = attn_w_qkv[j][:, :nqk].reshape(d, N_Q_HEADS + N_KV_HEADS, HEAD_DIM)[:, :, perm]
            w_qkv = jnp.concatenate([w_qk.reshape(d, nqk), attn_w_qkv[j][:, nqk:]], axis=1).astype(BF16)
            q_g = attn_q_norm_g[j][perm]
            k_g = attn_k_norm_g[j][perm]
            w_o = attn_w_o[j].astype(BF16)
            qkv_c = unfold_c(_normlin_call(fold_c(cs), norm_mix_g[l], csc1, csh1, w_qkv, None, None, tm_cf))
            qkv_x = _normlin_call(xs, norm_mix_g[l], sc1, sh1, w_qkv, None, None, tm_mlp)
            if last:
                k_c, vt_c = _aprep_call(qkv_c, q_g, k_g, None, False, rows_c)
            else:
                qt_c, k_c, vt_c = _aprep_call(qkv_c, q_g, k_g, None, True, rows_c)
            qt_x, k_x, vt_x = _aprep_call(qkv_x, q_g, k_g, rope, True, tm_mlp)
            k_all = jnp.concatenate([k_c, k_x], axis=2)
            vt_all = jnp.concatenate([vt_c, vt_x], axis=3)
            o_x = _flash_call(qt_x, k_all, vt_all, _row_tile(seq, FLASH_TQ), _col_tile(lc + seq, FLASH_TK_CAP))
            xs = _linres_call(o_x, w_o, xs, g1, tm_mlp)
            if not last:
                o_c = _flash_call(qt_c, k_c, vt_c, rows_c, _col_tile(lc, min(FLASH_TK_CAP, lc // 2)))
                cs_mix = unfold_c(_linres_call(fold_c(o_c), w_o, fold_c(cs), cg1, tm_cf))
        xs = _mlp_call(xs, norm_mlp_g[l], sc2, sh2, g2, w1, w2, final_norm_g, tm_mlp, last)
        if not last:
            cs = unfold_c(_mlp_call(fold_c(cs_mix), norm_mlp_g[l], csc2, csh2, cg2, w1, w2, final_norm_g,
                                    tm_cf, False))
    return xs
```

```python
import functools

import jax
import jax.numpy as jnp
from jax import lax
from jax.experimental import pallas as pl
from jax.experimental.pallas import tpu as pltpu

F32 = jnp.float32
BF16 = jnp.bfloat16

EPS = 1e-6
LANES = 128
SUBLANES = 8
VMEM_LIMIT = 56 * 1024 * 1024

LRU_BLOCK = 64
LRU_C = 8.0
CONV_WIDTH = 4
CONV_LEFT = 2
DN_HEADS = 8
DN_DK = 128
DN_CHUNK = 64
N_Q_HEADS = 8
N_KV_HEADS = 4
Q_PER_KV = N_Q_HEADS // N_KV_HEADS
HEAD_DIM = 128
GRID_W = 64
ROPE_THETA = 10000.0
NEG_BIG = -1e30
LOG2E = 1.4426950408889634
DNPREP_ROWS = 2048
LRU_ROWS = 2048
SHORT_SEQ_ROWS = 4096
DELTA_ROWS = 256
FLASH_TQ = 256
FLASH_TK_CAP = 768
VT_PAD = 16
FLASH_UNROLL = 4


def _cparams(sem):
    return pltpu.CompilerParams(dimension_semantics=sem, vmem_limit_bytes=VMEM_LIMIT)


def _bdot(a, b):
    return jnp.dot(a.astype(BF16), b.astype(BF16), preferred_element_type=F32)


def _bdot_nt(a, b):
    return lax.dot_general(a.astype(BF16), b.astype(BF16), (((1,), (1,)), ((), ())),
                           preferred_element_type=F32)


def _bdot_tn(a, b):
    return lax.dot_general(a.astype(BF16), b.astype(BF16), (((0,), (0,)), ((), ())),
                           preferred_element_type=F32)


def _sigmoid(x):
    return jax.nn.sigmoid(x)


def _softplus(x):
    return jnp.maximum(x, 0.0) + jnp.log1p(jnp.exp(-jnp.abs(x)))


def _rms(x):
    return x * lax.rsqrt(jnp.mean(x * x, axis=-1, keepdims=True) + EPS)


def _col_tile(n, cap):
    best = LANES
    t = LANES
    while t <= min(n, cap):
        if n % t == 0:
            best = t
        t += LANES
    return best


def _mod_kernel(c_ref, w_ref, b_ref, o_ref):
    s = c_ref[...]
    s = s * _sigmoid(s)
    o_ref[...] = _bdot(s, w_ref[...]) + b_ref[...]


def _mod_call(cc, w_mod, b_mod):
    depth, d, n = w_mod.shape
    tn = _col_tile(n, 1536)
    return pl.pallas_call(
        _mod_kernel,
        grid=(depth, n // tn),
        in_specs=[pl.BlockSpec((SUBLANES, d), lambda l, j: (0, 0)),
                  pl.BlockSpec((None, d, tn), lambda l, j: (l, 0, j)),
                  pl.BlockSpec((None, 1, tn), lambda l, j: (l, 0, j))],
        out_specs=pl.BlockSpec((None, SUBLANES, tn), lambda l, j: (l, 0, j)),
        out_shape=jax.ShapeDtypeStruct((depth, SUBLANES, n), F32),
        compiler_params=_cparams(("parallel", "parallel")),
        name="mod",
    )(cc, w_mod, b_mod.reshape(depth, 1, n))


def _normlin_kernel(*refs, has_gates):
    if has_gates:
        x_ref, g_ref, sc_ref, sh_ref, w_ref, wg_ref, wgt_ref, o_ref, og_ref, ogt_ref, h_ref = refs
    else:
        x_ref, g_ref, sc_ref, sh_ref, w_ref, o_ref, h_ref = refs

    @pl.when(pl.program_id(2) == 0)
    def _():
        y = _rms(x_ref[...]) * g_ref[...]
        h = (y * (1.0 + sc_ref[...]) + sh_ref[...]).astype(BF16)
        h_ref[...] = h
        o_ref[...] = jnp.dot(h, w_ref[...], preferred_element_type=F32).astype(o_ref.dtype)
        if has_gates:
            og_ref[...] = jnp.dot(h, wg_ref[...], preferred_element_type=F32)
            ogt_ref[...] = lax.dot_general(wgt_ref[...], h, (((1,), (1,)), ((), ())),
                                           preferred_element_type=F32)

    @pl.when(pl.program_id(2) > 0)
    def _():
        o_ref[...] = jnp.dot(h_ref[...], w_ref[...], preferred_element_type=F32).astype(o_ref.dtype)


def _normlin_call(x, g, sc, sh, w, w_gates, w_gates_t, tm):
    bsz, seq, d = x.shape
    has_gates = w_gates is not None
    n = w.shape[1] - w_gates_t.shape[0] if has_gates else w.shape[1]
    tn = _col_tile(n, 2048)
    in_specs = [pl.BlockSpec((None, tm, d), lambda b, i, j: (b, i, 0)),
                pl.BlockSpec((1, d), lambda b, i, j: (0, 0)),
                pl.BlockSpec((None, 1, d), lambda b, i, j: (b, 0, 0)),
                pl.BlockSpec((None, 1, d), lambda b, i, j: (b, 0, 0)),
                pl.BlockSpec((d, tn), lambda b, i, j: (0, j))]
    out_specs = [pl.BlockSpec((None, tm, tn), lambda b, i, j: (b, i, j))]
    out_shape = [jax.ShapeDtypeStruct((bsz, seq, n), BF16)]
    args = [x, g.reshape(1, d), sc, sh, w]
    if has_gates:
        ng = w_gates.shape[1]
        in_specs.append(pl.BlockSpec((d, ng), lambda b, i, j: (0, 0)))
        out_specs.append(pl.BlockSpec((None, tm, ng), lambda b, i, j: (b, i, 0)))
        out_shape.append(jax.ShapeDtypeStruct((bsz, seq, ng), F32))
        ngt = w_gates_t.shape[0]
        in_specs.append(pl.BlockSpec((ngt, d), lambda b, i, j: (0, 0)))
        out_specs.append(pl.BlockSpec((None, ngt, tm), lambda b, i, j: (b, 0, i)))
        out_shape.append(jax.ShapeDtypeStruct((bsz, ngt, seq), F32))
        args += [w_gates, w_gates_t]
    out = pl.pallas_call(
        functools.partial(_normlin_kernel, has_gates=has_gates),
        grid=(bsz, seq // tm, n // tn),
        in_specs=in_specs,
        out_specs=out_specs,
        out_shape=out_shape,
        scratch_shapes=[pltpu.VMEM((tm, d), BF16)],
        compiler_params=_cparams(("parallel", "parallel", "arbitrary")),
        name="normlin",
    )(*args)
    return out if has_gates else out[0]


def _mlp_kernel(x_ref, g_ref, sc_ref, sh_ref, gate_ref, w1_ref, w2_ref, fg_ref, o_ref,
                h_ref, acc_ref, *, final_norm):
    k = pl.program_id(2)
    last = pl.num_programs(2) - 1

    def ffn(h):
        a = jnp.dot(h, w1_ref[...], preferred_element_type=F32)
        a = jnp.square(jnp.maximum(a, 0.0)).astype(BF16)
        return jnp.dot(a, w2_ref[...], preferred_element_type=F32)

    def finish(total):
        y = x_ref[...] + gate_ref[...] * total
        if final_norm:
            y = _rms(y) * fg_ref[...]
        o_ref[...] = y

    @pl.when(k == 0)
    def _():
        y = _rms(x_ref[...]) * g_ref[...]
        h = (y * (1.0 + sc_ref[...]) + sh_ref[...]).astype(BF16)
        h_ref[...] = h
        acc_ref[...] = ffn(h)

    @pl.when((k > 0) & (k < last))
    def _():
        acc_ref[...] += ffn(h_ref[...])

    @pl.when((k > 0) & (k == last))
    def _():
        finish(acc_ref[...] + ffn(h_ref[...]))

    @pl.when((k == 0) & (k == last))
    def _():
        finish(acc_ref[...])


def _mlp_call(x, g, sc, sh, gate, w1, w2, fg, tm, final_norm):
    bsz, seq, d = x.shape
    f = w1.shape[1]
    tf = _col_tile(f, 2048)
    vec = pl.BlockSpec((None, 1, d), lambda b, i, k: (b, 0, 0))
    return pl.pallas_call(
        functools.partial(_mlp_kernel, final_norm=final_norm),
        grid=(bsz, seq // tm, f // tf),
        in_specs=[pl.BlockSpec((None, tm, d), lambda b, i, k: (b, i, 0)),
                  pl.BlockSpec((1, d), lambda b, i, k: (0, 0)),
                  vec, vec, vec,
                  pl.BlockSpec((d, tf), lambda b, i, k: (0, k)),
                  pl.BlockSpec((tf, d), lambda b, i, k: (k, 0)),
                  pl.BlockSpec((1, d), lambda b, i, k: (0, 0))],
        out_specs=pl.BlockSpec((None, tm, d), lambda b, i, k: (b, i, 0)),
        out_shape=jax.ShapeDtypeStruct((bsz, seq, d), F32),
        scratch_shapes=[pltpu.VMEM((tm, d), BF16), pltpu.VMEM((tm, d), F32)],
        compiler_params=_cparams(("parallel", "parallel", "arbitrary")),
        name="mlp",
    )(x, g.reshape(1, d), sc, sh, gate, w1, w2, fg.reshape(1, d))


def _batch_block(bsz, seq):
    return bsz if bsz * seq <= SHORT_SEQ_ROWS else 1


CONV_PAD = SUBLANES


def _fill_padded(xpad_ref, x_ref, seq, rows):
    zeros = jnp.zeros((CONV_PAD, xpad_ref.shape[1]), F32)
    xpad_ref[0:CONV_PAD, :] = zeros
    xpad_ref[CONV_PAD + seq:CONV_PAD + seq + CONV_PAD, :] = zeros

    def body(t, carry):
        r0 = pl.multiple_of(t * rows, rows)
        xpad_ref[pl.ds(r0 + CONV_PAD, rows), :] = x_ref[pl.ds(r0, rows), :].astype(F32)
        return carry

    lax.fori_loop(0, seq // rows, body, 0)


def _conv_tile(xpad_ref, r0, rows, w):
    acc = None
    for j in range(CONV_WIDTH):
        off = j - CONV_LEFT
        term = xpad_ref[pl.ds(r0 + (CONV_PAD + off), rows), :] * w[j:j + 1]
        acc = term if acc is None else acc + term
    return acc


def _dnprep_kernel(xs_ref, w_ref, os_ref, xpad_ref, *, seq, rows):
    for bi in range(xs_ref.shape[0]):
        _dnprep_one(xs_ref.at[bi], w_ref, os_ref.at[bi], xpad_ref, seq, rows)


def _dnprep_one(x_ref, w_ref, o_ref, xpad_ref, seq, rows):
    j = pl.program_id(1)
    scale = jnp.where(j < DN_HEADS, DN_DK ** -0.5, 1.0).astype(F32)
    _fill_padded(xpad_ref, x_ref, seq, rows)
    w = w_ref[...]

    def conv_silu(t):
        r0 = pl.multiple_of(t * rows, rows)
        y = _conv_tile(xpad_ref, r0, rows, w)
        return r0, y * _sigmoid(y)

    def qk_body(t, carry):
        r0, y = conv_silu(t)
        nrm = (y * lax.rsqrt(jnp.sum(y * y, axis=-1, keepdims=True) + EPS)) * scale
        o_ref[pl.ds(r0, rows), :] = nrm.astype(o_ref.dtype)
        return carry

    def v_body(t, carry):
        r0, y = conv_silu(t)
        o_ref[pl.ds(r0, rows), :] = y.astype(o_ref.dtype)
        return carry

    @pl.when(j < 2 * DN_HEADS)
    def _():
        lax.fori_loop(0, seq // rows, qk_body, 0)

    @pl.when(j >= 2 * DN_HEADS)
    def _():
        lax.fori_loop(0, seq // rows, v_body, 0)


def _dnprep_call(proj, conv_w, col0_blocks, rows):
    bsz, seq, _ = proj.shape
    ncol = conv_w.shape[1] // LANES
    nb = _batch_block(bsz, seq)
    return pl.pallas_call(
        functools.partial(_dnprep_kernel, seq=seq, rows=rows),
        grid=(bsz // nb, ncol),
        in_specs=[pl.BlockSpec((nb, seq, LANES), lambda b, j: (b, 0, col0_blocks + j)),
                  pl.BlockSpec((CONV_WIDTH, LANES), lambda b, j: (0, j))],
        out_specs=pl.BlockSpec((nb, seq, LANES), lambda b, j: (b, 0, j)),
        out_shape=jax.ShapeDtypeStruct((bsz, seq, ncol * LANES), BF16),
        scratch_shapes=[pltpu.VMEM((seq + 2 * CONV_PAD, LANES), F32)],
        compiler_params=_cparams(("parallel", "parallel")),
        name="dnprep",
    )(proj, conv_w)


def _scan_tile(a, b, carry, rev, rows):
    nv = rows // SUBLANES
    lanes = a.shape[1]
    a = a.reshape(nv, SUBLANES, lanes)
    b = b.reshape(nv, SUBLANES, lanes)
    row = lax.broadcasted_iota(jnp.int32, a.shape, 1)
    s = 1
    while s < SUBLANES:
        shift = SUBLANES - s if rev else s
        a_s = pltpu.roll(a, shift, 1)
        b_s = pltpu.roll(b, shift, 1)
        valid = (row < SUBLANES - s) if rev else (row >= s)
        b = b + a * jnp.where(valid, b_s, 0.0)
        a = a * jnp.where(valid, a_s, 1.0)
        s *= 2
    hs = [None] * nv
    for v in (range(nv - 1, -1, -1) if rev else range(nv)):
        h = a[v] * carry + b[v]
        carry = h[0:1] if rev else h[SUBLANES - 1:SUBLANES]
        hs[v] = h
    return jnp.concatenate(hs, axis=0), carry


def _gelu_tanh(x):
    return 0.5 * x * (1.0 + jnp.tanh(0.7978845608028654 * (x + 0.044715 * (x * x * x))))


def _lru_kernel(xs_ref, ys_ref, cw_ref, cb_ref, wa_ref, wi_ref, ba_ref, bi_ref, lam_ref, h0s_ref,
                os_ref, hts_ref, xpad_ref, xc_ref, hf_ref, *, seq, rows):
    for bi in range(xs_ref.shape[0]):
        _lru_one(xs_ref.at[bi], ys_ref.at[bi], cw_ref, cb_ref, wa_ref, wi_ref, ba_ref, bi_ref, lam_ref,
                 h0s_ref.at[bi], os_ref.at[bi], hts_ref.at[bi], xpad_ref, xc_ref, hf_ref, seq, rows)


def _lru_one(x_ref, y_ref, cw_ref, cb_ref, wa_ref, wi_ref, ba_ref, bi_ref, lam_ref, h0_ref,
             o_ref, ht_ref, xpad_ref, xc_ref, hf_ref, seq, rows):
    _fill_padded(xpad_ref, x_ref, seq, rows)
    cw = cw_ref[...]
    cb = cb_ref[...]
    nt = seq // rows

    def gates(xc, d):
        r = 1.0 / (1.0 + jnp.exp2(_bdot(xc, wa_ref[d]) - LOG2E * ba_ref[d:d + 1, :]))
        i = 1.0 / (1.0 + jnp.exp2(_bdot(xc, wi_ref[d]) - LOG2E * bi_ref[d:d + 1, :]))
        neg_rate = LRU_C * _softplus(-lam_ref[d:d + 1, :])
        a = jnp.exp2(r * (-LOG2E * neg_rate))
        b = jnp.sqrt(jnp.tanh(r * neg_rate) * (a * a + 1.0)) * (i * xc)
        return a, b

    def fwd_body(t, carry):
        r0 = pl.multiple_of(t * rows, rows)
        xc = _conv_tile(xpad_ref, r0, rows, cw) + cb
        xc_ref[pl.ds(r0, rows), :] = xc
        a, b = gates(xc, 0)
        h, carry = _scan_tile(a, b, carry, False, rows)
        hf_ref[pl.ds(r0, rows), :] = h
        return carry

    carry_f = lax.fori_loop(0, nt, fwd_body, h0_ref[0:1, :])

    def bwd_body(t, carry):
        r0 = pl.multiple_of((nt - 1 - t) * rows, rows)
        xc = xc_ref[pl.ds(r0, rows), :]
        a, b = gates(xc, 1)
        h, carry = _scan_tile(a, b, carry, True, rows)
        y = y_ref[pl.ds(r0, rows), :].astype(F32)
        o_ref[pl.ds(r0, rows), :] = ((hf_ref[pl.ds(r0, rows), :] + h) * _gelu_tanh(y)).astype(o_ref.dtype)
        return carry

    carry_b = lax.fori_loop(0, nt, bwd_body, h0_ref[1:2, :])
    ht_ref[0:1, :] = carry_f
    ht_ref[1:2, :] = carry_b


def _lru_call(proj, cw, cb, wa, wi, ba, bi, lam, h0, rows):
    bsz, seq, _ = proj.shape
    width = cw.shape[1]
    ncol = width // LANES
    vec2 = pl.BlockSpec((2, LANES), lambda b, j: (0, j))
    wspec = pl.BlockSpec((2, None, LANES, LANES), lambda b, j: (0, j, 0, 0))
    nb = _batch_block(bsz, seq)
    return pl.pallas_call(
        functools.partial(_lru_kernel, seq=seq, rows=rows),
        grid=(bsz // nb, ncol),
        in_specs=[pl.BlockSpec((nb, seq, LANES), lambda b, j: (b, 0, j)),
                  pl.BlockSpec((nb, seq, LANES), lambda b, j: (b, 0, ncol + j)),
                  pl.BlockSpec((CONV_WIDTH, LANES), lambda b, j: (0, j)),
                  pl.BlockSpec((1, LANES), lambda b, j: (0, j)),
                  wspec, wspec, vec2, vec2, vec2,
                  pl.BlockSpec((nb, 2, LANES), lambda b, j: (b, 0, j))],
        out_specs=[pl.BlockSpec((nb, seq, LANES), lambda b, j: (b, 0, j)),
                   pl.BlockSpec((nb, 2, LANES), lambda b, j: (b, 0, j))],
        out_shape=[jax.ShapeDtypeStruct((bsz, seq, width), BF16),
                   jax.ShapeDtypeStruct((bsz, 2, width), F32)],
        scratch_shapes=[pltpu.VMEM((seq + 2 * CONV_PAD, LANES), F32),
                        pltpu.VMEM((seq, LANES), F32),
                        pltpu.VMEM((seq, LANES), F32)],
        compiler_params=_cparams(("parallel", "parallel")),
        name="lru",
    )(proj, proj, cw, cb.reshape(1, width), wa, wi, ba, bi, lam, h0)


DN_GROUP = 2 * DN_CHUNK


def _split3(x):
    x1 = x.astype(BF16)
    r1 = x - x1.astype(F32)
    x2 = r1.astype(BF16)
    x3 = (r1 - x2.astype(F32)).astype(BF16)
    return x1, x2, x3


def _delta_kernel(*refs, rows):
    (qf_ref, kf_ref, vf_ref, grf_ref, grtf_ref, qb_ref, kb_ref, vb_ref, grb_ref, grtb_ref,
     cvec_ref, ccol_ref, cum_ref, cumt_ref, ones_ref, s0_ref, of_ref, ob_ref, st_ref, s_ref) = refs
    step = pl.program_id(1)
    q_refs, k_refs, v_refs = (qf_ref, qb_ref), (kf_ref, kb_ref), (vf_ref, vb_ref)
    o_refs = (of_ref, ob_ref)
    dirs = (0, 1)

    @pl.when(step == 0)
    def _():
        s_ref[...] = s0_ref[...]

    ones = ones_ref[...]

    def mm(m, parts):
        return sum(jnp.dot(m, p, preferred_element_type=F32) for p in parts)

    sig, gc, e_gc, e_rest, e_gl, gct = [], [], [], [], [], []
    for d, (gr_ref, grt_ref) in enumerate(((grf_ref, grtf_ref), (grb_ref, grtb_ref))):
        raw = gr_ref[...]
        g = -jnp.exp(cvec_ref[0:1, :]) * _softplus(raw + cvec_ref[1:2, :])
        parts = _split3(g)
        gc_d = mm(cum_ref[d], parts)
        gl_d = mm(ones, parts)
        sig.append(_sigmoid(raw))
        gc.append(gc_d)
        e_gc.append(jnp.exp(gc_d))
        e_rest.append(jnp.exp(gl_d - gc_d))
        e_gl.append(jnp.exp(gl_d))
        gt = -jnp.exp(ccol_ref[0]) * _softplus(grt_ref[...] + ccol_ref[1])
        cumt = cumt_ref[d]
        gct.append(sum(jnp.dot(p, cumt, preferred_element_type=F32) for p in _split3(gt)))

    ri = lax.broadcasted_iota(jnp.int32, (DN_GROUP, DN_GROUP), 0)
    ci = lax.broadcasted_iota(jnp.int32, (DN_GROUP, DN_GROUP), 1)
    same = (ri // DN_CHUNK) == (ci // DN_CHUNK)
    incl = (same & (ri >= ci), same & (ri <= ci))
    strict = (same & (ri > ci), same & (ri < ci))

    ngroups = rows // DN_GROUP
    zeros_c = jnp.zeros((DN_CHUNK, DN_DK), F32)

    def bcast(x, rsl, lane):
        return jnp.broadcast_to(x[rsl, lane:lane + 1], (DN_GROUP, LANES))

    heads = range(DN_HEADS)
    groups = (list(range(ngroups)), list(range(ngroups - 1, -1, -1)))
    units = [(d, gi, h) for d in dirs for gi in groups[d] for h in heads]
    qd, kd, egl_b, rhs, kk, qk = {}, {}, {}, {}, {}, {}
    for u in units:
        d, gi, h = u
        rsl = slice(gi * DN_GROUP, (gi + 1) * DN_GROUP)
        lb = d * DN_HEADS + h
        lg = 2 * DN_HEADS + d * DN_HEADS + h
        csl = slice(h * DN_DK, (h + 1) * DN_DK)
        q16 = q_refs[d][rsl, csl]
        k16 = k_refs[d][rsl, csl]
        q_h = q16.astype(F32)
        k_h = k16.astype(F32)
        beta_b = bcast(sig[d], rsl, lb)
        egc_b = bcast(e_gc[d], rsl, lg)
        kb_h = k_h * beta_b
        kk[u] = _bdot_nt(kb_h, k16)
        qk[u] = _bdot_nt(q16, k16)
        rhs[u] = jnp.concatenate([v_refs[d][rsl, csl].astype(F32) * beta_b, kb_h * egc_b], axis=1)
        qd[u] = q_h * egc_b
        kd[u] = k_h * bcast(e_rest[d], rsl, lg)
        egl_b[u] = bcast(e_gl[d], rsl, lg)
    p, a16, attn = {}, {}, {}
    for u in units:
        d, gi, h = u
        rsl = slice(gi * DN_GROUP, (gi + 1) * DN_GROUP)
        lg = 2 * DN_HEADS + d * DN_HEADS + h
        diff = bcast(gc[d], rsl, lg) - gct[d][lg:lg + 1, rsl]
        decay = jnp.exp(jnp.where(incl[d], diff, NEG_BIG))
        m = jnp.where(strict[d], kk[u] * decay, 0.0)
        attn[u] = qk[u] * decay
        p[u] = -m
        a16[u] = m.astype(BF16)
    npow = 1
    while npow < DN_CHUNK // 2:
        a = {u: jnp.dot(a16[u], a16[u], preferred_element_type=F32) for u in units}
        a16 = {u: a[u].astype(BF16) for u in units}
        p = {u: p[u] + a[u] + jnp.dot(p[u].astype(BF16), a16[u], preferred_element_type=F32) for u in units}
        npow *= 2
    uw = {u: rhs[u] + _bdot(p[u], rhs[u]) for u in units}
    chains = [(d, h) for d in dirs for h in heads]
    s = {c: s_ref[c[1], c[0]] for c in chains}
    slots = tuple([(gi, cc) for gi in groups[d] for cc in ((1, 0) if d else (0, 1))] for d in dirs)
    for slot in range(2 * ngroups):
        r2, v_new = {}, {}
        for c in chains:
            d, h = c
            gi, cc = slots[d][slot]
            cs = slice(cc * DN_CHUNK, (cc + 1) * DN_CHUNK)
            r2[c] = _bdot(jnp.concatenate([uw[d, gi, h][cs, DN_DK:], qd[d, gi, h][cs]], axis=0), s[c])
        for c in chains:
            d, h = c
            gi, cc = slots[d][slot]
            cs = slice(cc * DN_CHUNK, (cc + 1) * DN_CHUNK)
            v_new[c] = uw[d, gi, h][cs, :DN_DK] - r2[c][:DN_CHUNK]
        for c in chains:
            d, h = c
            gi, cc = slots[d][slot]
            c0 = cc * DN_CHUNK
            cs = slice(c0, c0 + DN_CHUNK)
            v_pad = (jnp.concatenate([v_new[c], zeros_c], axis=0) if cc == 0
                     else jnp.concatenate([zeros_c, v_new[c]], axis=0))
            o_c = r2[c][DN_CHUNK:] + _bdot(attn[d, gi, h][cs], v_pad)
            o_refs[d][gi * DN_GROUP + c0:gi * DN_GROUP + c0 + DN_CHUNK,
                      h * DN_DK:(h + 1) * DN_DK] = o_c.astype(o_refs[d].dtype)
        for c in chains:
            d, h = c
            gi, cc = slots[d][slot]
            cs = slice(cc * DN_CHUNK, (cc + 1) * DN_CHUNK)
            egl = jnp.concatenate([egl_b[d, gi, h][cs], egl_b[d, gi, h][cs]], axis=0)
            s[c] = s[c] * egl + _bdot_tn(kd[d, gi, h][cs], v_new[c])
    for c in chains:
        s_ref[c[1], c[0]] = s[c]

    @pl.when(step == pl.num_programs(1) - 1)
    def _():
        st_ref[...] = s_ref[...]


def _delta_consts(rows):
    t = jnp.arange(rows)
    same = (t[:, None] // DN_CHUNK) == (t[None, :] // DN_CHUNK)
    cum = jnp.stack([same & (t[:, None] >= t[None, :]), same & (t[:, None] <= t[None, :])])
    return cum.astype(BF16), jnp.swapaxes(cum, 1, 2).astype(BF16), same.astype(BF16)


def _delta_call(qkv, gates, gates_t, cvec, ccol, s0, rows):
    bsz, seq, _ = qkv.shape
    width = DN_HEADS * DN_DK
    nsteps = seq // rows
    cum, cumt, ones = _delta_consts(rows)

    def per_dir(rev):
        blk = (lambda i: nsteps - 1 - i) if rev else (lambda i: i)
        return [pl.BlockSpec((None, rows, width), lambda b, i: (b, blk(i), 0)),
                pl.BlockSpec((None, rows, width), lambda b, i: (b, blk(i), 1)),
                pl.BlockSpec((None, rows, width), lambda b, i: (b, blk(i), 2)),
                pl.BlockSpec((None, rows, LANES), lambda b, i: (b, blk(i), 0)),
                pl.BlockSpec((None, 4 * DN_HEADS, rows), lambda b, i: (b, 0, blk(i)))]

    full = lambda shape: pl.BlockSpec(shape, lambda b, i: (0,) * len(shape))
    state = pl.BlockSpec((None, DN_HEADS, 2, DN_DK, DN_DK), lambda b, i: (b, 0, 0, 0, 0))
    o_fwd = pl.BlockSpec((None, rows, width), lambda b, i: (b, i, 0))
    o_bwd = pl.BlockSpec((None, rows, width), lambda b, i: (b, nsteps - 1 - i, 0))
    data = [qkv, qkv, qkv, gates, gates_t]
    return pl.pallas_call(
        functools.partial(_delta_kernel, rows=rows),
        grid=(bsz, nsteps),
        in_specs=per_dir(False) + per_dir(True) + [
            full((SUBLANES, LANES)), full((2, 4 * DN_HEADS, rows)),
            full((2, rows, rows)), full((2, rows, rows)), full((rows, rows)), state],
        out_specs=[o_fwd, o_bwd, state],
        out_shape=[jax.ShapeDtypeStruct((bsz, seq, width), BF16),
                   jax.ShapeDtypeStruct((bsz, seq, width), BF16),
                   jax.ShapeDtypeStruct((bsz, DN_HEADS, 2, DN_DK, DN_DK), F32)],
        scratch_shapes=[pltpu.VMEM((DN_HEADS, 2, DN_DK, DN_DK), F32)],
        compiler_params=_cparams(("parallel", "arbitrary")),
        name="delta",
    )(*data, *data, cvec, ccol, cum, cumt, ones, s0)


def _evenout_kernel(lru_ref, of_ref, ob_ref, z_ref, g_ref, w_ref, res_ref, gate_ref, o_ref, a_ref):
    width = lru_ref.shape[1]
    a_ref[:, 0:width] = lru_ref[...].astype(BF16)
    for h in range(DN_HEADS):
        csl = slice(h * DN_DK, (h + 1) * DN_DK)
        o = of_ref[:, csl].astype(F32) + ob_ref[:, csl].astype(F32)
        z = z_ref[:, csl].astype(F32)
        y = (_rms(o) * g_ref[:, csl]) * (z * _sigmoid(z))
        a_ref[:, width + h * DN_DK:width + (h + 1) * DN_DK] = y.astype(BF16)
    o_ref[...] = res_ref[...] + gate_ref[...] * jnp.dot(a_ref[...], w_ref[...], preferred_element_type=F32)


def _evenout_call(lru, o_f, o_b, proj, z_block, dn_g, w_out, res, gate, tm):
    bsz, seq, width = lru.shape
    dnw = o_f.shape[2]
    d = res.shape[2]
    row = lambda c: pl.BlockSpec((None, tm, c), lambda b, i: (b, i, 0))
    return pl.pallas_call(
        _evenout_kernel,
        grid=(bsz, seq // tm),
        in_specs=[row(width), row(dnw), row(dnw),
                  pl.BlockSpec((None, tm, dnw), lambda b, i: (b, i, z_block)),
                  pl.BlockSpec((1, dnw), lambda b, i: (0, 0)),
                  pl.BlockSpec((width + dnw, d), lambda b, i: (0, 0)),
                  row(d),
                  pl.BlockSpec((None, 1, d), lambda b, i: (b, 0, 0))],
        out_specs=row(d),
        out_shape=jax.ShapeDtypeStruct((bsz, seq, d), F32),
        scratch_shapes=[pltpu.VMEM((tm, width + dnw), BF16)],
        compiler_params=_cparams(("parallel", "parallel")),
        name="evenout",
    )(lru, o_f, o_b, proj, dn_g, w_out, res, gate)


def _linres_kernel(a_ref, w_ref, res_ref, gate_ref, o_ref):
    o_ref[...] = res_ref[...] + gate_ref[...] * jnp.dot(a_ref[...].astype(BF16), w_ref[...],
                                                        preferred_element_type=F32)


def _linres_call(a, w, res, gate, tm):
    bsz, seq, kdim = a.shape
    d = res.shape[2]
    return pl.pallas_call(
        _linres_kernel,
        grid=(bsz, seq // tm),
        in_specs=[pl.BlockSpec((None, tm, kdim), lambda b, i: (b, i, 0)),
                  pl.BlockSpec((kdim, d), lambda b, i: (0, 0)),
                  pl.BlockSpec((None, tm, d), lambda b, i: (b, i, 0)),
                  pl.BlockSpec((None, 1, d), lambda b, i: (b, 0, 0))],
        out_specs=pl.BlockSpec((None, tm, d), lambda b, i: (b, i, 0)),
        out_shape=jax.ShapeDtypeStruct((bsz, seq, d), F32),
        compiler_params=_cparams(("parallel", "parallel")),
        name="linres",
    )(a, w, res, gate)


def _aprep_kernel(*refs, rope, want_q):
    if rope:
        x_ref, qg_ref, kg_ref, cos_ref, sin_ref = refs[:5]
        outs = refs[5:]
    else:
        x_ref, qg_ref, kg_ref = refs[:3]
        outs = refs[3:]
    if want_q:
        q_ref, k_ref, v_ref = outs
    else:
        k_ref, v_ref = outs

    if rope:
        cos = cos_ref[...]
        sin = sin_ref[...]

    def norm_rope(x, g):
        y = _rms(x) * g
        if rope:
            y = y * cos + pltpu.roll(y, HEAD_DIM // 2, 1) * sin
        return y

    if want_q:
        for h in range(N_Q_HEADS):
            x = x_ref[:, h * HEAD_DIM:(h + 1) * HEAD_DIM].astype(F32)
            q_ref[h] = (norm_rope(x, qg_ref[...]) * (HEAD_DIM ** -0.5 * LOG2E)).astype(BF16)
    for h in range(N_KV_HEADS):
        c0 = (N_Q_HEADS + h) * HEAD_DIM
        k_ref[h] = norm_rope(x_ref[:, c0:c0 + HEAD_DIM].astype(F32), kg_ref[...]).astype(BF16)
        c1 = (N_Q_HEADS + N_KV_HEADS + h) * HEAD_DIM
        v_ref[h, 0:HEAD_DIM, :] = x_ref[:, c1:c1 + HEAD_DIM].astype(F32).T.astype(BF16)
        pad_row = lax.broadcasted_iota(jnp.int32, (VT_PAD, x_ref.shape[0]), 0)
        v_ref[h, HEAD_DIM:HEAD_DIM + VT_PAD, :] = jnp.where(pad_row == 0, 1.0, 0.0).astype(BF16)


def _aprep_call(qkv, qg, kg, rope_tabs, want_q, tm):
    bsz, seq, cols = qkv.shape
    rope = rope_tabs is not None
    in_specs = [pl.BlockSpec((None, tm, cols), lambda b, i: (b, i, 0)),
                pl.BlockSpec((1, HEAD_DIM), lambda b, i: (0, 0)),
                pl.BlockSpec((1, HEAD_DIM), lambda b, i: (0, 0))]
    args = [qkv, qg.reshape(1, HEAD_DIM), kg.reshape(1, HEAD_DIM)]
    if rope:
        in_specs += [pl.BlockSpec((tm, HEAD_DIM), lambda b, i: (i, 0))] * 2
        args += list(rope_tabs)
    hspec = lambda n: pl.BlockSpec((None, n, tm, HEAD_DIM), lambda b, i: (b, 0, i, 0))
    hshape = lambda n: jax.ShapeDtypeStruct((bsz, n, seq, HEAD_DIM), BF16)
    tspec = lambda n: pl.BlockSpec((None, n, HEAD_DIM + VT_PAD, tm), lambda b, i: (b, 0, 0, i))
    tshape = lambda n: jax.ShapeDtypeStruct((bsz, n, HEAD_DIM + VT_PAD, seq), BF16)
    out_specs = [hspec(N_KV_HEADS), tspec(N_KV_HEADS)]
    out_shape = [hshape(N_KV_HEADS), tshape(N_KV_HEADS)]
    if want_q:
        out_specs = [hspec(N_Q_HEADS)] + out_specs
        out_shape = [hshape(N_Q_HEADS)] + out_shape
    return pl.pallas_call(
        functools.partial(_aprep_kernel, rope=rope, want_q=want_q),
        grid=(bsz, seq // tm),
        in_specs=in_specs,
        out_specs=out_specs,
        out_shape=out_shape,
        compiler_params=_cparams(("parallel", "parallel")),
        name="aprep_x" if rope else "aprep_ctx",
    )(*args)


def _rope_tables(n_tokens):
    rows = n_tokens // GRID_W
    axis = HEAD_DIM // 2
    inv_freq = ROPE_THETA ** (-jnp.arange(0, axis, 2, dtype=F32) / axis)
    ang_r = jnp.arange(rows).astype(F32)[:, None] * inv_freq[None]
    ang_c = jnp.arange(GRID_W).astype(F32)[:, None] * inv_freq[None]
    cr, sr = (jnp.repeat(t, GRID_W, axis=0) for t in (jnp.cos(ang_r), jnp.sin(ang_r)))
    cc, sc = (jnp.tile(t, (rows, 1)) for t in (jnp.cos(ang_c), jnp.sin(ang_c)))
    cos = jnp.concatenate([cr, cc, cr, cc], axis=-1)
    sin = jnp.concatenate([-sr, -sc, sr, sc], axis=-1)
    return cos, sin


def _rope_perm():
    quarter = HEAD_DIM // 4
    return jnp.concatenate([jnp.arange(0, quarter), jnp.arange(2 * quarter, 3 * quarter),
                            jnp.arange(quarter, 2 * quarter), jnp.arange(3 * quarter, 4 * quarter)])


def _flash_kernel(q_ref, k_ref, vt_ref, o_ref, s0_ref, sa_ref, sb_ref, m0_ref, ma_ref, mb_ref, *,
                  tq, tk, nk, nt):
    m_rows = Q_PER_KV * tq

    def q_tile(i):
        r0 = pl.multiple_of(i * tq, tq)
        return q_ref[:, pl.ds(r0, tq), :].reshape(m_rows, HEAD_DIM)

    def scores(q, j, s_ref, cmax_ref):
        c0 = pl.multiple_of(j * tk, tk)
        st = lax.dot_general(k_ref[pl.ds(c0, tk), :], q, (((1,), (1,)), ((), ())),
                             preferred_element_type=F32)
        s_ref[...] = st
        cmax_ref[...] = jnp.max(st, axis=0, keepdims=True)

    def update(j, s_ref, cmax_ref, carry):
        m, acc = carry
        m_new = jnp.maximum(m, cmax_ref[...])
        alpha = jnp.exp2(m - m_new)
        p = jnp.exp2(s_ref[...] - m_new)
        vt = vt_ref[:, pl.ds(pl.multiple_of(j * tk, tk), tk)]
        acc = alpha * acc + jnp.dot(vt, p.astype(BF16), preferred_element_type=F32)
        return m_new, acc

    bufs = ((sa_ref, ma_ref), (sb_ref, mb_ref))

    def tile_body(i, _):
        q = q_tile(i)
        q_next = q_tile(jnp.minimum(i + 1, nt - 1))

        def run(j0, count, carry, last):
            for u in range(count):
                j = j0 + u
                if u + 1 < count or not last:
                    scores(q, j + 1, *bufs[(u + 1) % 2])
                else:
                    scores(q_next, 0, s0_ref, m0_ref)
                carry = update(j, *bufs[u % 2], carry)
            return carry

        carry = (jnp.full((1, m_rows), NEG_BIG, F32), jnp.zeros((HEAD_DIM + VT_PAD, m_rows), F32))
        scores(q, 1, sa_ref, ma_ref)
        carry = update(0, s0_ref, m0_ref, carry)
        rest = nk - 1
        niter = (rest - 1) // FLASH_UNROLL
        carry = lax.fori_loop(0, niter, lambda jj, c: run(1 + jj * FLASH_UNROLL, FLASH_UNROLL, c, False),
                              carry)
        _, acc = run(1 + niter * FLASH_UNROLL, rest - niter * FLASH_UNROLL, carry, True)
        o = (acc[:HEAD_DIM] / acc[HEAD_DIM:HEAD_DIM + 1]).T
        r0 = pl.multiple_of(i * tq, tq)
        for r in range(Q_PER_KV):
            o_ref[pl.ds(r0, tq), r * HEAD_DIM:(r + 1) * HEAD_DIM] = o[r * tq:(r + 1) * tq].astype(o_ref.dtype)
        return 0

    scores(q_tile(0), 0, s0_ref, m0_ref)
    lax.fori_loop(0, nt, tile_body, 0)


def _flash_call(q, k, vt, tq, tk):
    bsz, _, seq, _ = q.shape
    lk = k.shape[2]
    nk = lk // tk
    assert nk >= 2, "the score pipeline needs at least two key blocks"
    q5 = q.reshape(bsz, N_KV_HEADS, Q_PER_KV, seq, HEAD_DIM)
    return pl.pallas_call(
        functools.partial(_flash_kernel, tq=tq, tk=tk, nk=nk, nt=seq // tq),
        grid=(bsz, N_KV_HEADS),
        in_specs=[pl.BlockSpec((None, None, Q_PER_KV, seq, HEAD_DIM), lambda b, g: (b, g, 0, 0, 0)),
                  pl.BlockSpec((None, None, lk, HEAD_DIM), lambda b, g: (b, g, 0, 0)),
                  pl.BlockSpec((None, None, HEAD_DIM + VT_PAD, lk), lambda b, g: (b, g, 0, 0))],
        out_specs=pl.BlockSpec((None, seq, Q_PER_KV * HEAD_DIM), lambda b, g: (b, 0, g)),
        out_shape=jax.ShapeDtypeStruct((bsz, seq, N_Q_HEADS * HEAD_DIM), BF16),
        scratch_shapes=([pltpu.VMEM((tk, Q_PER_KV * tq), F32)] * 3
                        + [pltpu.VMEM((1, Q_PER_KV * tq), F32)] * 3),
        compiler_params=_cparams(("parallel", "parallel")),
        name="flash",
    )(q5, k, vt)


def _row_tile(seq, cap):
    t = min(seq, cap)
    while seq % t:
        t //= 2
    return t


def _even_weights(w_in, lru_wa, lru_wi, dn_alog, dn_dtb, dn_g):
    d, cols = w_in.shape
    ngate = 4 * DN_HEADS
    main = cols - ngate
    w_gates = jnp.concatenate([w_in[:, main:], jnp.zeros((d, LANES - ngate), F32)], axis=1)

    def blockdiag(w):
        w = w * (-LOG2E)
        z = jnp.zeros_like(w[:, 0::2])
        top = jnp.concatenate([w[:, 0::2], z], axis=-1)
        bot = jnp.concatenate([z, w[:, 1::2]], axis=-1)
        return jnp.concatenate([top, bot], axis=-2).astype(BF16)

    alog = dn_alog.reshape(-1)
    dtb = dn_dtb.reshape(-1)
    cvec = jnp.zeros((SUBLANES, LANES), F32)
    cvec = cvec.at[0, 2 * DN_HEADS:ngate].set(alog).at[1, 2 * DN_HEADS:ngate].set(dtb)

    ccol = jnp.zeros((2, ngate), F32).at[0, 2 * DN_HEADS:].set(alog).at[1, 2 * DN_HEADS:].set(dtb)

    return dict(w_in=w_in.astype(BF16), w_gates=w_gates.astype(BF16),
                w_gates_t=w_in[:, main:].T.astype(BF16),
                wa=blockdiag(lru_wa), wi=blockdiag(lru_wi), cvec=cvec, ccol=ccol,
                dn_g=jnp.tile(dn_g, DN_HEADS).reshape(1, -1))


def _even_stream(h_in, res, gate, norm_g, sc, sh, ew, lru_cw, lru_cb, lru_ba, lru_bi, lru_lam, dn_cw,
                 w_out, h0, s0, tm_in, tm, rows, dn_rows, fold_batch):
    width = lru_cw.shape[1]
    ngate = 4 * DN_HEADS
    b0, l0 = h_in.shape[:2]
    fold = (lambda a: a.reshape(1, b0 * l0, a.shape[-1])) if fold_batch else (lambda a: a)
    unfold = (lambda a: a.reshape(b0, l0, a.shape[-1])) if fold_batch else (lambda a: a)
    proj, gates, gates_t = _normlin_call(fold(h_in), norm_g, sc, sh, ew["w_in"], ew["w_gates"], ew["w_gates_t"],
                                         tm_in)
    proj, gates = unfold(proj), unfold(gates)
    if fold_batch:
        gates_t = jnp.transpose(gates_t.reshape(gates_t.shape[1], b0, l0), (1, 0, 2))
    lru, h_t = _lru_call(proj, lru_cw, lru_cb, ew["wa"], ew["wi"], lru_ba, lru_bi, lru_lam, h0,
                         _row_tile(proj.shape[1], LRU_ROWS))
    qkv = _dnprep_call(proj, dn_cw, 2 * width // LANES, _row_tile(proj.shape[1], DNPREP_ROWS))
    ccol = jnp.broadcast_to(ew["ccol"][:, :, None], (2, ngate, dn_rows))
    o_f, o_b, s_t = _delta_call(qkv, gates, gates_t, ew["cvec"], ccol, s0, dn_rows)
    z_block = (2 * width + dn_cw.shape[1]) // (DN_HEADS * DN_DK)
    out = _evenout_call(fold(lru), fold(o_f), fold(o_b), fold(proj), z_block, ew["dn_g"], w_out, fold(res), gate, tm)
    return unfold(out), h_t, s_t


def kernel(x, c, ctx, c_ctx, norm_mix_g, norm_mlp_g, w_mod, b_mod, mlp_w1, mlp_w2, even_w_in, lru_conv_w,
           lru_conv_b, lru_w_a, lru_b_a, lru_w_i, lru_b_i, lru_lambda, dn_conv_w, dn_a_log, dn_dt_bias,
           dn_norm_g, even_w_out, attn_w_qkv, attn_q_norm_g, attn_k_norm_g, attn_w_o, final_norm_g):
    bsz, seq, d = x.shape
    lc = ctx.shape[1]
    depth = w_mod.shape[0]
    tm_x = _row_tile(seq, 512)
    tm_cf = _row_tile(bsz * lc, 1024)
    fold_c = lambda a: a.reshape(1, bsz * lc, a.shape[-1])
    unfold_c = lambda a: a.reshape(bsz, lc, a.shape[-1])
    tm_mlp = _row_tile(seq, 1024)
    rows_x = _row_tile(seq, 256)
    rows_c = _row_tile(lc, 256)

    cc = jnp.zeros((SUBLANES, d), F32).at[:bsz].set(c).at[bsz].set(c_ctx)
    mod = _mod_call(cc, w_mod, b_mod).reshape(depth, SUBLANES, 6, d)

    rope = _rope_tables(seq)
    xs, cs = x, ctx
    for l in range(depth):
        last = l == depth - 1
        j = l // 2
        mx = [mod[l, :bsz, k][:, None, :] for k in range(6)]
        mc = [mod[l, bsz, k][None, None, :] for k in range(6)]
        sh1, sc1, g1, sh2, sc2, g2 = mx
        csh1, csc1, cg1, csh2, csc2, cg2 = mc
        w1 = mlp_w1[l].astype(BF16)
        w2 = mlp_w2[l].astype(BF16)
        if l % 2 == 0:
            ew = _even_weights(even_w_in[j], lru_w_a[j], lru_w_i[j], dn_a_log[j], dn_dt_bias[j], dn_norm_g[j])
            w_out = even_w_out[j].astype(BF16)
            h0 = jnp.zeros((bsz, 2, lru_conv_w.shape[2]), F32)
            s0 = jnp.zeros((bsz, DN_HEADS, 2, DN_DK, DN_DK), F32)
            common = (ew, lru_conv_w[j], lru_conv_b[j], lru_b_a[j], lru_b_i[j], lru_lambda[j], dn_conv_w[j], w_out)
            cs_mix, h_t, s_t = _even_stream(cs, cs, cg1, norm_mix_g[l], csc1, csh1, *common, h0, s0,
                                            tm_cf, tm_cf, rows_c, rows_c, True)
            xs, _, _ = _even_stream(xs, xs, g1, norm_mix_g[l], sc1, sh1, *common, h_t, s_t,
                                    tm_mlp, tm_mlp, rows_x, _row_tile(seq, DELTA_ROWS), False)
        else:
            perm = _rope_perm()
            nqk = (N_Q_HEADS + N_KV_HEADS) * HEAD_DIM
            w_qk = attn_w_qkv[j][:, :nqk].reshape(d, N_Q_HEADS + N_KV_HEADS, HEAD_DIM)[:, :, perm]
            w_qkv = jnp.concatenate([w_qk.reshape(d, nqk), attn_w_qkv[j][:, nqk:]], axis=1).astype(BF16)
            q_g = attn_q_norm_g[j][perm]
            k_g = attn_k_norm_g[j][perm]
            w_o = attn_w_o[j].astype(BF16)
            qkv_c = unfold_c(_normlin_call(fold_c(cs), norm_mix_g[l], csc1, csh1, w_qkv, None, None, tm_cf))
            qkv_x = _normlin_call(xs, norm_mix_g[l], sc1, sh1, w_qkv, None, None, tm_mlp)
            if last:
                k_c, vt_c = _aprep_call(qkv_c, q_g, k_g, None, False, rows_c)
            else:
                qt_c, k_c, vt_c = _aprep_call(qkv_c, q_g, k_g, None, True, rows_c)
            qt_x, k_x, vt_x = _aprep_call(qkv_x, q_g, k_g, rope, True, tm_mlp)
            k_all = jnp.concatenate([k_c, k_x], axis=2)
            vt_all = jnp.concatenate([vt_c, vt_x], axis=3)
            o_x = _flash_call(qt_x, k_all, vt_all, _row_tile(seq, FLASH_TQ), _col_tile(lc + seq, FLASH_TK_CAP))
            xs = _linres_call(o_x, w_o, xs, g1, tm_mlp)
            if not last:
                o_c = _flash_call(qt_c, k_c, vt_c, rows_c, _col_tile(lc, min(FLASH_TK_CAP, lc // 2)))
                cs_mix = unfold_c(_linres_call(fold_c(o_c), w_o, fold_c(cs), cg1, tm_cf))
        xs = _mlp_call(xs, norm_mlp_g[l], sc2, sh2, g2, w1, w2, final_norm_g, tm_mlp, last)
        if not last:
            cs = unfold_c(_mlp_call(fold_c(cs_mix), norm_mlp_g[l], csc2, csh2, cg2, w1, w2, final_norm_g,
                                    tm_cf, False))
    return xs
```

```python
import functools

import jax
import jax.numpy as jnp
from jax import lax
from jax.experimental import pallas as pl
from jax.experimental.pallas import tpu as pltpu

F32 = jnp.float32
BF16 = jnp.bfloat16

EPS = 1e-6
LANES = 128
SUBLANES = 8
VMEM_LIMIT = 56 * 1024 * 1024

LRU_C = 8.0
CONV_WIDTH = 4
CONV_LEFT = 2
DN_HEADS = 8
DN_DK = 128
DN_CHUNK = 64
N_Q_HEADS = 8
N_KV_HEADS = 4
Q_PER_KV = N_Q_HEADS // N_KV_HEADS
HEAD_DIM = 128
GRID_W = 64
ROPE_THETA = 10000.0
NEG_BIG = -1e30
LOG2E = 1.4426950408889634
DNPREP_ROWS = 2048
LRU_ROWS = 2048
SHORT_SEQ_ROWS = 4096
DELTA_ROWS = 256
FLASH_TQ = 256
FLASH_TK_CAP = 768
VT_PAD = 16
FLASH_UNROLL = 4


def _cparams(sem):
    return pltpu.CompilerParams(dimension_semantics=sem, vmem_limit_bytes=VMEM_LIMIT)


def _bdot(a, b):
    return jnp.dot(a.astype(BF16), b.astype(BF16), preferred_element_type=F32)


def _bdot_nt(a, b):
    return lax.dot_general(a.astype(BF16), b.astype(BF16), (((1,), (1,)), ((), ())),
                           preferred_element_type=F32)


def _bdot_tn(a, b):
    return lax.dot_general(a.astype(BF16), b.astype(BF16), (((0,), (0,)), ((), ())),
                           preferred_element_type=F32)


def _sigmoid(x):
    return jax.nn.sigmoid(x)


def _softplus(x):
    return jnp.maximum(x, 0.0) + jnp.log1p(jnp.exp(-jnp.abs(x)))


def _rms(x):
    return x * lax.rsqrt(jnp.mean(x * x, axis=-1, keepdims=True) + EPS)


def _col_tile(n, cap):
    best = LANES
    t = LANES
    while t <= min(n, cap):
        if n % t == 0:
            best = t
        t += LANES
    return best


def _mod_kernel(c_ref, w_ref, b_ref, o_ref):
    s = c_ref[...]
    s = s * _sigmoid(s)
    o_ref[...] = _bdot(s, w_ref[...]) + b_ref[...]


def _mod_call(cc, w_mod, b_mod):
    depth, d, n = w_mod.shape
    tn = _col_tile(n, 1536)
    return pl.pallas_call(
        _mod_kernel,
        grid=(depth, n // tn),
        in_specs=[pl.BlockSpec((SUBLANES, d), lambda l, j: (0, 0)),
                  pl.BlockSpec((None, d, tn), lambda l, j: (l, 0, j)),
                  pl.BlockSpec((None, 1, tn), lambda l, j: (l, 0, j))],
        out_specs=pl.BlockSpec((None, SUBLANES, tn), lambda l, j: (l, 0, j)),
        out_shape=jax.ShapeDtypeStruct((depth, SUBLANES, n), F32),
        compiler_params=_cparams(("parallel", "parallel")),
        name="mod",
    )(cc, w_mod, b_mod.reshape(depth, 1, n))


def _normlin_kernel(*refs, has_gates):
    if has_gates:
        x_ref, g_ref, sc_ref, sh_ref, w_ref, wg_ref, wgt_ref, o_ref, og_ref, ogt_ref, h_ref = refs
    else:
        x_ref, g_ref, sc_ref, sh_ref, w_ref, o_ref, h_ref = refs

    @pl.when(pl.program_id(2) == 0)
    def _():
        y = _rms(x_ref[...]) * g_ref[...]
        h = (y * (1.0 + sc_ref[...]) + sh_ref[...]).astype(BF16)
        h_ref[...] = h
        o_ref[...] = jnp.dot(h, w_ref[...], preferred_element_type=F32).astype(o_ref.dtype)
        if has_gates:
            og_ref[...] = jnp.dot(h, wg_ref[...], preferred_element_type=F32)
            ogt_ref[...] = lax.dot_general(wgt_ref[...], h, (((1,), (1,)), ((), ())),
                                           preferred_element_type=F32)

    @pl.when(pl.program_id(2) > 0)
    def _():
        o_ref[...] = jnp.dot(h_ref[...], w_ref[...], preferred_element_type=F32).astype(o_ref.dtype)


def _normlin_call(x, g, sc, sh, w, w_gates, w_gates_t, tm):
    bsz, seq, d = x.shape
    has_gates = w_gates is not None
    n = w.shape[1] - w_gates_t.shape[0] if has_gates else w.shape[1]
    tn = _col_tile(n, 2048)
    in_specs = [pl.BlockSpec((None, tm, d), lambda b, i, j: (b, i, 0)),
                pl.BlockSpec((1, d), lambda b, i, j: (0, 0)),
                pl.BlockSpec((None, 1, d), lambda b, i, j: (b, 0, 0)),
                pl.BlockSpec((None, 1, d), lambda b, i, j: (b, 0, 0)),
                pl.BlockSpec((d, tn), lambda b, i, j: (0, j))]
    out_specs = [pl.BlockSpec((None, tm, tn), lambda b, i, j: (b, i, j))]
    out_shape = [jax.ShapeDtypeStruct((bsz, seq, n), BF16)]
    args = [x, g.reshape(1, d), sc, sh, w]
    if has_gates:
        ng = w_gates.shape[1]
        in_specs.append(pl.BlockSpec((d, ng), lambda b, i, j: (0, 0)))
        out_specs.append(pl.BlockSpec((None, tm, ng), lambda b, i, j: (b, i, 0)))
        out_shape.append(jax.ShapeDtypeStruct((bsz, seq, ng), F32))
        ngt = w_gates_t.shape[0]
        in_specs.append(pl.BlockSpec((ngt, d), lambda b, i, j: (0, 0)))
        out_specs.append(pl.BlockSpec((None, ngt, tm), lambda b, i, j: (b, 0, i)))
        out_shape.append(jax.ShapeDtypeStruct((bsz, ngt, seq), F32))
        args += [w_gates, w_gates_t]
    out = pl.pallas_call(
        functools.partial(_normlin_kernel, has_gates=has_gates),
        grid=(bsz, seq // tm, n // tn),
        in_specs=in_specs,
        out_specs=out_specs,
        out_shape=out_shape,
        scratch_shapes=[pltpu.VMEM((tm, d), BF16)],
        compiler_params=_cparams(("parallel", "parallel", "arbitrary")),
        name="normlin",
    )(*args)
    return out if has_gates else out[0]


def _mlp_kernel(x_ref, g_ref, sc_ref, sh_ref, gate_ref, w1_ref, w2_ref, fg_ref, o_ref,
                h_ref, acc_ref, *, final_norm):
    k = pl.program_id(2)
    last = pl.num_programs(2) - 1

    def ffn(h):
        a = jnp.dot(h, w1_ref[...], preferred_element_type=F32)
        a = jnp.square(jnp.maximum(a, 0.0)).astype(BF16)
        return jnp.dot(a, w2_ref[...], preferred_element_type=F32)

    def finish(total):
        y = x_ref[...] + gate_ref[...] * total
        if final_norm:
            y = _rms(y) * fg_ref[...]
        o_ref[...] = y

    @pl.when(k == 0)
    def _():
        y = _rms(x_ref[...]) * g_ref[...]
        h = (y * (1.0 + sc_ref[...]) + sh_ref[...]).astype(BF16)
        h_ref[...] = h
        acc_ref[...] = ffn(h)

    @pl.when((k > 0) & (k < last))
    def _():
        acc_ref[...] += ffn(h_ref[...])

    @pl.when((k > 0) & (k == last))
    def _():
        finish(acc_ref[...] + ffn(h_ref[...]))

    @pl.when((k == 0) & (k == last))
    def _():
        finish(acc_ref[...])


def _mlp_call(x, g, sc, sh, gate, w1, w2, fg, tm, final_norm):
    bsz, seq, d = x.shape
    f = w1.shape[1]
    tf = _col_tile(f, 2048)
    vec = pl.BlockSpec((None, 1, d), lambda b, i, k: (b, 0, 0))
    return pl.pallas_call(
        functools.partial(_mlp_kernel, final_norm=final_norm),
        grid=(bsz, seq // tm, f // tf),
        in_specs=[pl.BlockSpec((None, tm, d), lambda b, i, k: (b, i, 0)),
                  pl.BlockSpec((1, d), lambda b, i, k: (0, 0)),
                  vec, vec, vec,
                  pl.BlockSpec((d, tf), lambda b, i, k: (0, k)),
                  pl.BlockSpec((tf, d), lambda b, i, k: (k, 0)),
                  pl.BlockSpec((1, d), lambda b, i, k: (0, 0))],
        out_specs=pl.BlockSpec((None, tm, d), lambda b, i, k: (b, i, 0)),
        out_shape=jax.ShapeDtypeStruct((bsz, seq, d), F32),
        scratch_shapes=[pltpu.VMEM((tm, d), BF16), pltpu.VMEM((tm, d), F32)],
        compiler_params=_cparams(("parallel", "parallel", "arbitrary")),
        name="mlp",
    )(x, g.reshape(1, d), sc, sh, gate, w1, w2, fg.reshape(1, d))


def _batch_block(bsz, seq):
    return bsz if bsz * seq <= SHORT_SEQ_ROWS else 1


CONV_PAD = SUBLANES


def _fill_padded(xpad_ref, x_ref, seq, rows):
    zeros = jnp.zeros((CONV_PAD, xpad_ref.shape[1]), F32)
    xpad_ref[0:CONV_PAD, :] = zeros
    xpad_ref[CONV_PAD + seq:CONV_PAD + seq + CONV_PAD, :] = zeros

    def body(t, carry):
        r0 = pl.multiple_of(t * rows, rows)
        xpad_ref[pl.ds(r0 + CONV_PAD, rows), :] = x_ref[pl.ds(r0, rows), :].astype(F32)
        return carry

    lax.fori_loop(0, seq // rows, body, 0)


def _conv_tile(xpad_ref, r0, rows, w):
    acc = None
    for j in range(CONV_WIDTH):
        off = j - CONV_LEFT
        term = xpad_ref[pl.ds(r0 + (CONV_PAD + off), rows), :] * w[j:j + 1]
        acc = term if acc is None else acc + term
    return acc


def _dnprep_kernel(xs_ref, w_ref, os_ref, xpad_ref, *, seq, rows):
    for bi in range(xs_ref.shape[0]):
        _dnprep_one(xs_ref.at[bi], w_ref, os_ref.at[bi], xpad_ref, seq, rows)


def _dnprep_one(x_ref, w_ref, o_ref, xpad_ref, seq, rows):
    j = pl.program_id(1)
    scale = jnp.where(j < DN_HEADS, DN_DK ** -0.5, 1.0).astype(F32)
    _fill_padded(xpad_ref, x_ref, seq, rows)
    w = w_ref[...]

    def conv_silu(t):
        r0 = pl.multiple_of(t * rows, rows)
        y = _conv_tile(xpad_ref, r0, rows, w)
        return r0, y * _sigmoid(y)

    def qk_body(t, carry):
        r0, y = conv_silu(t)
        nrm = (y * lax.rsqrt(jnp.sum(y * y, axis=-1, keepdims=True) + EPS)) * scale
        o_ref[pl.ds(r0, rows), :] = nrm.astype(o_ref.dtype)
        return carry

    def v_body(t, carry):
        r0, y = conv_silu(t)
        o_ref[pl.ds(r0, rows), :] = y.astype(o_ref.dtype)
        return carry

    @pl.when(j < 2 * DN_HEADS)
    def _():
        lax.fori_loop(0, seq // rows, qk_body, 0)

    @pl.when(j >= 2 * DN_HEADS)
    def _():
        lax.fori_loop(0, seq // rows, v_body, 0)


def _dnprep_call(proj, conv_w, col0_blocks, rows):
    bsz, seq, _ = proj.shape
    ncol = conv_w.shape[1] // LANES
    nb = _batch_block(bsz, seq)
    return pl.pallas_call(
        functools.partial(_dnprep_kernel, seq=seq, rows=rows),
        grid=(bsz // nb, ncol),
        in_specs=[pl.BlockSpec((nb, seq, LANES), lambda b, j: (b, 0, col0_blocks + j)),
                  pl.BlockSpec((CONV_WIDTH, LANES), lambda b, j: (0, j))],
        out_specs=pl.BlockSpec((nb, seq, LANES), lambda b, j: (b, 0, j)),
        out_shape=jax.ShapeDtypeStruct((bsz, seq, ncol * LANES), BF16),
        scratch_shapes=[pltpu.VMEM((seq + 2 * CONV_PAD, LANES), F32)],
        compiler_params=_cparams(("parallel", "parallel")),
        name="dnprep",
    )(proj, conv_w)


def _scan_tile(a, b, carry, rev, rows):
    nv = rows // SUBLANES
    lanes = a.shape[1]
    a = a.reshape(nv, SUBLANES, lanes)
    b = b.reshape(nv, SUBLANES, lanes)
    row = lax.broadcasted_iota(jnp.int32, a.shape, 1)
    s = 1
    while s < SUBLANES:
        shift = SUBLANES - s if rev else s
        a_s = pltpu.roll(a, shift, 1)
        b_s = pltpu.roll(b, shift, 1)
        valid = (row < SUBLANES - s) if rev else (row >= s)
        b = b + a * jnp.where(valid, b_s, 0.0)
        a = a * jnp.where(valid, a_s, 1.0)
        s *= 2
    hs = [None] * nv
    for v in (range(nv - 1, -1, -1) if rev else range(nv)):
        h = a[v] * carry + b[v]
        carry = h[0:1] if rev else h[SUBLANES - 1:SUBLANES]
        hs[v] = h
    return jnp.concatenate(hs, axis=0), carry


def _gelu_tanh(x):
    return 0.5 * x * (1.0 + jnp.tanh(0.7978845608028654 * (x + 0.044715 * (x * x * x))))


def _lru_kernel(xs_ref, ys_ref, cw_ref, cb_ref, wa_ref, wi_ref, ba_ref, bi_ref, lam_ref, h0s_ref,
                os_ref, hts_ref, xpad_ref, xc_ref, hf_ref, *, seq, rows):
    for bi in range(xs_ref.shape[0]):
        _lru_one(xs_ref.at[bi], ys_ref.at[bi], cw_ref, cb_ref, wa_ref, wi_ref, ba_ref, bi_ref, lam_ref,
                 h0s_ref.at[bi], os_ref.at[bi], hts_ref.at[bi], xpad_ref, xc_ref, hf_ref, seq, rows)


def _lru_one(x_ref, y_ref, cw_ref, cb_ref, wa_ref, wi_ref, ba_ref, bi_ref, lam_ref, h0_ref,
             o_ref, ht_ref, xpad_ref, xc_ref, hf_ref, seq, rows):
    _fill_padded(xpad_ref, x_ref, seq, rows)
    cw = cw_ref[...]
    cb = cb_ref[...]
    nt = seq // rows

    def gates(xc, d):
        r = 1.0 / (1.0 + jnp.exp2(_bdot(xc, wa_ref[d]) - LOG2E * ba_ref[d:d + 1, :]))
        i = 1.0 / (1.0 + jnp.exp2(_bdot(xc, wi_ref[d]) - LOG2E * bi_ref[d:d + 1, :]))
        neg_rate = LRU_C * _softplus(-lam_ref[d:d + 1, :])
        a = jnp.exp2(r * (-LOG2E * neg_rate))
        b = jnp.sqrt(jnp.tanh(r * neg_rate) * (a * a + 1.0)) * (i * xc)
        return a, b

    def fwd_body(t, carry):
        r0 = pl.multiple_of(t * rows, rows)
        xc = _conv_tile(xpad_ref, r0, rows, cw) + cb
        xc_ref[pl.ds(r0, rows), :] = xc
        a, b = gates(xc, 0)
        h, carry = _scan_tile(a, b, carry, False, rows)
        hf_ref[pl.ds(r0, rows), :] = h
        return carry

    carry_f = lax.fori_loop(0, nt, fwd_body, h0_ref[0:1, :])

    def bwd_body(t, carry):
        r0 = pl.multiple_of((nt - 1 - t) * rows, rows)
        xc = xc_ref[pl.ds(r0, rows), :]
        a, b = gates(xc, 1)
        h, carry = _scan_tile(a, b, carry, True, rows)
        y = y_ref[pl.ds(r0, rows), :].astype(F32)
        o_ref[pl.ds(r0, rows), :] = ((hf_ref[pl.ds(r0, rows), :] + h) * _gelu_tanh(y)).astype(o_ref.dtype)
        return carry

    carry_b = lax.fori_loop(0, nt, bwd_body, h0_ref[1:2, :])
    ht_ref[0:1, :] = carry_f
    ht_ref[1:2, :] = carry_b


def _lru_call(proj, cw, cb, wa, wi, ba, bi, lam, h0, rows):
    bsz, seq, _ = proj.shape
    width = cw.shape[1]
    ncol = width // LANES
    vec2 = pl.BlockSpec((2, LANES), lambda b, j: (0, j))
    wspec = pl.BlockSpec((2, None, LANES, LANES), lambda b, j: (0, j, 0, 0))
    nb = _batch_block(bsz, seq)
    return pl.pallas_call(
        functools.partial(_lru_kernel, seq=seq, rows=rows),
        grid=(bsz // nb, ncol),
        in_specs=[pl.BlockSpec((nb, seq, LANES), lambda b, j: (b, 0, j)),
                  pl.BlockSpec((nb, seq, LANES), lambda b, j: (b, 0, ncol + j)),
                  pl.BlockSpec((CONV_WIDTH, LANES), lambda b, j: (0, j)),
                  pl.BlockSpec((1, LANES), lambda b, j: (0, j)),
                  wspec, wspec, vec2, vec2, vec2,
                  pl.BlockSpec((nb, 2, LANES), lambda b, j: (b, 0, j))],
        out_specs=[pl.BlockSpec((nb, seq, LANES), lambda b, j: (b, 0, j)),
                   pl.BlockSpec((nb, 2, LANES), lambda b, j: (b, 0, j))],
        out_shape=[jax.ShapeDtypeStruct((bsz, seq, width), BF16),
                   jax.ShapeDtypeStruct((bsz, 2, width), F32)],
        scratch_shapes=[pltpu.VMEM((seq + 2 * CONV_PAD, LANES), F32),
                        pltpu.VMEM((seq, LANES), F32),
                        pltpu.VMEM((seq, LANES), F32)],
        compiler_params=_cparams(("parallel", "parallel")),
        name="lru",
    )(proj, proj, cw, cb.reshape(1, width), wa, wi, ba, bi, lam, h0)


DN_GROUP = 2 * DN_CHUNK


def _split3(x):
    x1 = x.astype(BF16)
    r1 = x - x1.astype(F32)
    x2 = r1.astype(BF16)
    x3 = (r1 - x2.astype(F32)).astype(BF16)
    return x1, x2, x3


def _delta_kernel(*refs, rows):
    (qf_ref, kf_ref, vf_ref, grf_ref, grtf_ref, qb_ref, kb_ref, vb_ref, grb_ref, grtb_ref,
     cvec_ref, ccol_ref, cum_ref, cumt_ref, ones_ref, s0_ref, of_ref, ob_ref, st_ref, s_ref) = refs
    step = pl.program_id(1)
    q_refs, k_refs, v_refs = (qf_ref, qb_ref), (kf_ref, kb_ref), (vf_ref, vb_ref)
    o_refs = (of_ref, ob_ref)
    dirs = (0, 1)

    @pl.when(step == 0)
    def _():
        s_ref[...] = s0_ref[...]

    ones = ones_ref[...]

    def mm(m, parts):
        return sum(jnp.dot(m, p, preferred_element_type=F32) for p in parts)

    sig, gc, e_gc, e_rest, e_gl, gct = [], [], [], [], [], []
    for d, (gr_ref, grt_ref) in enumerate(((grf_ref, grtf_ref), (grb_ref, grtb_ref))):
        raw = gr_ref[...]
        g = -jnp.exp(cvec_ref[0:1, :]) * _softplus(raw + cvec_ref[1:2, :])
        parts = _split3(g)
        gc_d = mm(cum_ref[d], parts)
        gl_d = mm(ones, parts)
        sig.append(_sigmoid(raw))
        gc.append(gc_d)
        e_gc.append(jnp.exp(gc_d))
        e_rest.append(jnp.exp(gl_d - gc_d))
        e_gl.append(jnp.exp(gl_d))
        gt = -jnp.exp(ccol_ref[0]) * _softplus(grt_ref[...] + ccol_ref[1])
        cumt = cumt_ref[d]
        gct.append(sum(jnp.dot(p, cumt, preferred_element_type=F32) for p in _split3(gt)))

    ri = lax.broadcasted_iota(jnp.int32, (DN_GROUP, DN_GROUP), 0)
    ci = lax.broadcasted_iota(jnp.int32, (DN_GROUP, DN_GROUP), 1)
    same = (ri // DN_CHUNK) == (ci // DN_CHUNK)
    incl = (same & (ri >= ci), same & (ri <= ci))
    strict = (same & (ri > ci), same & (ri < ci))

    ngroups = rows // DN_GROUP
    zeros_c = jnp.zeros((DN_CHUNK, DN_DK), F32)

    def bcast(x, rsl, lane):
        return jnp.broadcast_to(x[rsl, lane:lane + 1], (DN_GROUP, LANES))

    heads = range(DN_HEADS)
    groups = (list(range(ngroups)), list(range(ngroups - 1, -1, -1)))
    units = [(d, gi, h) for d in dirs for gi in groups[d] for h in heads]
    qd, kd, egl_b, rhs, kk, qk = {}, {}, {}, {}, {}, {}
    for u in units:
        d, gi, h = u
        rsl = slice(gi * DN_GROUP, (gi + 1) * DN_GROUP)
        lb = d * DN_HEADS + h
        lg = 2 * DN_HEADS + d * DN_HEADS + h
        csl = slice(h * DN_DK, (h + 1) * DN_DK)
        q16 = q_refs[d][rsl, csl]
        k16 = k_refs[d][rsl, csl]
        q_h = q16.astype(F32)
        k_h = k16.astype(F32)
        beta_b = bcast(sig[d], rsl, lb)
        egc_b = bcast(e_gc[d], rsl, lg)
        kb_h = k_h * beta_b
        kk[u] = _bdot_nt(kb_h, k16)
        qk[u] = _bdot_nt(q16, k16)
        rhs[u] = jnp.concatenate([v_refs[d][rsl, csl].astype(F32) * beta_b, kb_h * egc_b], axis=1)
        qd[u] = q_h * egc_b
        kd[u] = k_h * bcast(e_rest[d], rsl, lg)
        egl_b[u] = bcast(e_gl[d], rsl, lg)
    p, a16, attn = {}, {}, {}
    for u in units:
        d, gi, h = u
        rsl = slice(gi * DN_GROUP, (gi + 1) * DN_GROUP)
        lg = 2 * DN_HEADS + d * DN_HEADS + h
        diff = bcast(gc[d], rsl, lg) - gct[d][lg:lg + 1, rsl]
        decay = jnp.exp(jnp.where(incl[d], diff, NEG_BIG))
        m = jnp.where(strict[d], kk[u] * decay, 0.0)
        attn[u] = qk[u] * decay
        p[u] = -m
        a16[u] = m.astype(BF16)
    npow = 1
    while npow < DN_CHUNK // 2:
        a = {u: jnp.dot(a16[u], a16[u], preferred_element_type=F32) for u in units}
        a16 = {u: a[u].astype(BF16) for u in units}
        p = {u: p[u] + a[u] + jnp.dot(p[u].astype(BF16), a16[u], preferred_element_type=F32) for u in units}
        npow *= 2
    uw = {u: rhs[u] + _bdot(p[u], rhs[u]) for u in units}
    chains = [(d, h) for d in dirs for h in heads]
    s = {c: s_ref[c[1], c[0]] for c in chains}
    slots = tuple([(gi, cc) for gi in groups[d] for cc in ((1, 0) if d else (0, 1))] for d in dirs)
    for slot in range(2 * ngroups):
        r2, v_new = {}, {}
        for c in chains:
            d, h = c
            gi, cc = slots[d][slot]
            cs = slice(cc * DN_CHUNK, (cc + 1) * DN_CHUNK)
            r2[c] = _bdot(jnp.concatenate([uw[d, gi, h][cs, DN_DK:], qd[d, gi, h][cs]], axis=0), s[c])
        for c in chains:
            d, h = c
            gi, cc = slots[d][slot]
            cs = slice(cc * DN_CHUNK, (cc + 1) * DN_CHUNK)
            v_new[c] = uw[d, gi, h][cs, :DN_DK] - r2[c][:DN_CHUNK]
        for c in chains:
            d, h = c
            gi, cc = slots[d][slot]
            c0 = cc * DN_CHUNK
            cs = slice(c0, c0 + DN_CHUNK)
            v_pad = (jnp.concatenate([v_new[c], zeros_c], axis=0) if cc == 0
                     else jnp.concatenate([zeros_c, v_new[c]], axis=0))
            o_c = r2[c][DN_CHUNK:] + _bdot(attn[d, gi, h][cs], v_pad)
            o_refs[d][gi * DN_GROUP + c0:gi * DN_GROUP + c0 + DN_CHUNK,
                      h * DN_DK:(h + 1) * DN_DK] = o_c.astype(o_refs[d].dtype)
        for c in chains:
            d, h = c
            gi, cc = slots[d][slot]
            cs = slice(cc * DN_CHUNK, (cc + 1) * DN_CHUNK)
            egl = jnp.concatenate([egl_b[d, gi, h][cs], egl_b[d, gi, h][cs]], axis=0)
            s[c] = s[c] * egl + _bdot_tn(kd[d, gi, h][cs], v_new[c])
    for c in chains:
        s_ref[c[1], c[0]] = s[c]

    @pl.when(step == pl.num_programs(1) - 1)
    def _():
        st_ref[...] = s_ref[...]


def _delta_consts(rows):
    t = jnp.arange(rows)
    same = (t[:, None] // DN_CHUNK) == (t[None, :] // DN_CHUNK)
    cum = jnp.stack([same & (t[:, None] >= t[None, :]), same & (t[:, None] <= t[None, :])])
    return cum.astype(BF16), jnp.swapaxes(cum, 1, 2).astype(BF16), same.astype(BF16)


def _delta_call(qkv, gates, gates_t, cvec, ccol, s0, rows):
    bsz, seq, _ = qkv.shape
    width = DN_HEADS * DN_DK
    nsteps = seq // rows
    cum, cumt, ones = _delta_consts(rows)

    def per_dir(rev):
        blk = (lambda i: nsteps - 1 - i) if rev else (lambda i: i)
        return [pl.BlockSpec((None, rows, width), lambda b, i: (b, blk(i), 0)),
                pl.BlockSpec((None, rows, width), lambda b, i: (b, blk(i), 1)),
                pl.BlockSpec((None, rows, width), lambda b, i: (b, blk(i), 2)),
                pl.BlockSpec((None, rows, LANES), lambda b, i: (b, blk(i), 0)),
                pl.BlockSpec((None, 4 * DN_HEADS, rows), lambda b, i: (b, 0, blk(i)))]

    full = lambda shape: pl.BlockSpec(shape, lambda b, i: (0,) * len(shape))
    state = pl.BlockSpec((None, DN_HEADS, 2, DN_DK, DN_DK), lambda b, i: (b, 0, 0, 0, 0))
    o_fwd = pl.BlockSpec((None, rows, width), lambda b, i: (b, i, 0))
    o_bwd = pl.BlockSpec((None, rows, width), lambda b, i: (b, nsteps - 1 - i, 0))
    data = [qkv, qkv, qkv, gates, gates_t]
    return pl.pallas_call(
        functools.partial(_delta_kernel, rows=rows),
        grid=(bsz, nsteps),
        in_specs=per_dir(False) + per_dir(True) + [
            full((SUBLANES, LANES)), full((2, 4 * DN_HEADS, rows)),
            full((2, rows, rows)), full((2, rows, rows)), full((rows, rows)), state],
        out_specs=[o_fwd, o_bwd, state],
        out_shape=[jax.ShapeDtypeStruct((bsz, seq, width), BF16),
                   jax.ShapeDtypeStruct((bsz, seq, width), BF16),
                   jax.ShapeDtypeStruct((bsz, DN_HEADS, 2, DN_DK, DN_DK), F32)],
        scratch_shapes=[pltpu.VMEM((DN_HEADS, 2, DN_DK, DN_DK), F32)],
        compiler_params=_cparams(("parallel", "arbitrary")),
        name="delta",
    )(*data, *data, cvec, ccol, cum, cumt, ones, s0)


def _evenout_kernel(lru_ref, of_ref, ob_ref, z_ref, g_ref, w_ref, res_ref, gate_ref, o_ref, a_ref):
    width = lru_ref.shape[1]
    a_ref[:, 0:width] = lru_ref[...].astype(BF16)
    for h in range(DN_HEADS):
        csl = slice(h * DN_DK, (h + 1) * DN_DK)
        o = of_ref[:, csl].astype(F32) + ob_ref[:, csl].astype(F32)
        z = z_ref[:, csl].astype(F32)
        y = (_rms(o) * g_ref[:, csl]) * (z * _sigmoid(z))
        a_ref[:, width + h * DN_DK:width + (h + 1) * DN_DK] = y.astype(BF16)
    o_ref[...] = res_ref[...] + gate_ref[...] * jnp.dot(a_ref[...], w_ref[...], preferred_element_type=F32)


def _evenout_call(lru, o_f, o_b, proj, z_block, dn_g, w_out, res, gate, tm):
    bsz, seq, width = lru.shape
    dnw = o_f.shape[2]
    d = res.shape[2]
    row = lambda c: pl.BlockSpec((None, tm, c), lambda b, i: (b, i, 0))
    return pl.pallas_call(
        _evenout_kernel,
        grid=(bsz, seq // tm),
        in_specs=[row(width), row(dnw), row(dnw),
                  pl.BlockSpec((None, tm, dnw), lambda b, i: (b, i, z_block)),
                  pl.BlockSpec((1, dnw), lambda b, i: (0, 0)),
                  pl.BlockSpec((width + dnw, d), lambda b, i: (0, 0)),
                  row(d),
                  pl.BlockSpec((None, 1, d), lambda b, i: (b, 0, 0))],
        out_specs=row(d),
        out_shape=jax.ShapeDtypeStruct((bsz, seq, d), F32),
        scratch_shapes=[pltpu.VMEM((tm, width + dnw), BF16)],
        compiler_params=_cparams(("parallel", "parallel")),
        name="evenout",
    )(lru, o_f, o_b, proj, dn_g, w_out, res, gate)


def _linres_kernel(a_ref, w_ref, res_ref, gate_ref, o_ref):
    o_ref[...] = res_ref[...] + gate_ref[...] * jnp.dot(a_ref[...].astype(BF16), w_ref[...],
                                                        preferred_element_type=F32)


def _linres_call(a, w, res, gate, tm):
    bsz, seq, kdim = a.shape
    d = res.shape[2]
    return pl.pallas_call(
        _linres_kernel,
        grid=(bsz, seq // tm),
        in_specs=[pl.BlockSpec((None, tm, kdim), lambda b, i: (b, i, 0)),
                  pl.BlockSpec((kdim, d), lambda b, i: (0, 0)),
                  pl.BlockSpec((None, tm, d), lambda b, i: (b, i, 0)),
                  pl.BlockSpec((None, 1, d), lambda b, i: (b, 0, 0))],
        out_specs=pl.BlockSpec((None, tm, d), lambda b, i: (b, i, 0)),
        out_shape=jax.ShapeDtypeStruct((bsz, seq, d), F32),
        compiler_params=_cparams(("parallel", "parallel")),
        name="linres",
    )(a, w, res, gate)


def _aprep_kernel(*refs, rope, want_q):
    if rope:
        x_ref, qg_ref, kg_ref, cos_ref, sin_ref = refs[:5]
        outs = refs[5:]
    else:
        x_ref, qg_ref, kg_ref = refs[:3]
        outs = refs[3:]
    if want_q:
        q_ref, k_ref, v_ref = outs
    else:
        k_ref, v_ref = outs

    if rope:
        cos = cos_ref[...]
        sin = sin_ref[...]

    def norm_rope(x, g):
        y = _rms(x) * g
        if rope:
            y = y * cos + pltpu.roll(y, HEAD_DIM // 2, 1) * sin
        return y

    if want_q:
        for h in range(N_Q_HEADS):
            x = x_ref[:, h * HEAD_DIM:(h + 1) * HEAD_DIM].astype(F32)
            q_ref[h] = (norm_rope(x, qg_ref[...]) * (HEAD_DIM ** -0.5 * LOG2E)).astype(BF16)
    for h in range(N_KV_HEADS):
        c0 = (N_Q_HEADS + h) * HEAD_DIM
        k_ref[h] = norm_rope(x_ref[:, c0:c0 + HEAD_DIM].astype(F32), kg_ref[...]).astype(BF16)
        c1 = (N_Q_HEADS + N_KV_HEADS + h) * HEAD_DIM
        v_ref[h, 0:HEAD_DIM, :] = x_ref[:, c1:c1 + HEAD_DIM].astype(F32).T.astype(BF16)
        pad_row = lax.broadcasted_iota(jnp.int32, (VT_PAD, x_ref.shape[0]), 0)
        v_ref[h, HEAD_DIM:HEAD_DIM + VT_PAD, :] = jnp.where(pad_row == 0, 1.0, 0.0).astype(BF16)


def _aprep_call(qkv, qg, kg, rope_tabs, want_q, tm):
    bsz, seq, cols = qkv.shape
    rope = rope_tabs is not None
    in_specs = [pl.BlockSpec((None, tm, cols), lambda b, i: (b, i, 0)),
                pl.BlockSpec((1, HEAD_DIM), lambda b, i: (0, 0)),
                pl.BlockSpec((1, HEAD_DIM), lambda b, i: (0, 0))]
    args = [qkv, qg.reshape(1, HEAD_DIM), kg.reshape(1, HEAD_DIM)]
    if rope:
        in_specs += [pl.BlockSpec((tm, HEAD_DIM), lambda b, i: (i, 0))] * 2
        args += list(rope_tabs)
    hspec = lambda n: pl.BlockSpec((None, n, tm, HEAD_DIM), lambda b, i: (b, 0, i, 0))
    hshape = lambda n: jax.ShapeDtypeStruct((bsz, n, seq, HEAD_DIM), BF16)
    tspec = lambda n: pl.BlockSpec((None, n, HEAD_DIM + VT_PAD, tm), lambda b, i: (b, 0, 0, i))
    tshape = lambda n: jax.ShapeDtypeStruct((bsz, n, HEAD_DIM + VT_PAD, seq), BF16)
    out_specs = [hspec(N_KV_HEADS), tspec(N_KV_HEADS)]
    out_shape = [hshape(N_KV_HEADS), tshape(N_KV_HEADS)]
    if want_q:
        out_specs = [hspec(N_Q_HEADS)] + out_specs
        out_shape = [hshape(N_Q_HEADS)] + out_shape
    return pl.pallas_call(
        functools.partial(_aprep_kernel, rope=rope, want_q=want_q),
        grid=(bsz, seq // tm),
        in_specs=in_specs,
        out_specs=out_specs,
        out_shape=out_shape,
        compiler_params=_cparams(("parallel", "parallel")),
        name="aprep_x" if rope else "aprep_ctx",
    )(*args)


def _rope_tables(n_tokens):
    rows = n_tokens // GRID_W
    axis = HEAD_DIM // 2
    inv_freq = ROPE_THETA ** (-jnp.arange(0, axis, 2, dtype=F32) / axis)
    ang_r = jnp.arange(rows).astype(F32)[:, None] * inv_freq[None]
    ang_c = jnp.arange(GRID_W).astype(F32)[:, None] * inv_freq[None]
    cr, sr = (jnp.repeat(t, GRID_W, axis=0) for t in (jnp.cos(ang_r), jnp.sin(ang_r)))
    cc, sc = (jnp.tile(t, (rows, 1)) for t in (jnp.cos(ang_c), jnp.sin(ang_c)))
    cos = jnp.concatenate([cr, cc, cr, cc], axis=-1)
    sin = jnp.concatenate([-sr, -sc, sr, sc], axis=-1)
    return cos, sin


def _rope_perm():
    quarter = HEAD_DIM // 4
    return jnp.concatenate([jnp.arange(0, quarter), jnp.arange(2 * quarter, 3 * quarter),
                            jnp.arange(quarter, 2 * quarter), jnp.arange(3 * quarter, 4 * quarter)])


def _flash_kernel(q_ref, k_ref, vt_ref, o_ref, s0_ref, sa_ref, sb_ref, m0_ref, ma_ref, mb_ref, *,
                  tq, tk, nk, nt):
    m_rows = Q_PER_KV * tq

    def q_tile(i):
        r0 = pl.multiple_of(i * tq, tq)
        return q_ref[:, pl.ds(r0, tq), :].reshape(m_rows, HEAD_DIM)

    def scores(q, j, s_ref, cmax_ref):
        c0 = pl.multiple_of(j * tk, tk)
        st = lax.dot_general(k_ref[pl.ds(c0, tk), :], q, (((1,), (1,)), ((), ())),
                             preferred_element_type=F32)
        s_ref[...] = st
        cmax_ref[...] = jnp.max(st, axis=0, keepdims=True)

    def update(j, s_ref, cmax_ref, carry):
        m, acc = carry
        m_new = jnp.maximum(m, cmax_ref[...])
        alpha = jnp.exp2(m - m_new)
        p = jnp.exp2(s_ref[...] - m_new)
        vt = vt_ref[:, pl.ds(pl.multiple_of(j * tk, tk), tk)]
        acc = alpha * acc + jnp.dot(vt, p.astype(BF16), preferred_element_type=F32)
        return m_new, acc

    bufs = ((sa_ref, ma_ref), (sb_ref, mb_ref))

    def tile_body(i, _):
        q = q_tile(i)
        q_next = q_tile(jnp.minimum(i + 1, nt - 1))

        def run(j0, count, carry, last):
            for u in range(count):
                j = j0 + u
                if u + 1 < count or not last:
                    scores(q, j + 1, *bufs[(u + 1) % 2])
                else:
                    scores(q_next, 0, s0_ref, m0_ref)
                carry = update(j, *bufs[u % 2], carry)
            return carry

        carry = (jnp.full((1, m_rows), NEG_BIG, F32), jnp.zeros((HEAD_DIM + VT_PAD, m_rows), F32))
        scores(q, 1, sa_ref, ma_ref)
        carry = update(0, s0_ref, m0_ref, carry)
        rest = nk - 1
        niter = (rest - 1) // FLASH_UNROLL
        carry = lax.fori_loop(0, niter, lambda jj, c: run(1 + jj * FLASH_UNROLL, FLASH_UNROLL, c, False),
                              carry)
        _, acc = run(1 + niter * FLASH_UNROLL, rest - niter * FLASH_UNROLL, carry, True)
        o = (acc[:HEAD_DIM] / acc[HEAD_DIM:HEAD_DIM + 1]).T
        r0 = pl.multiple_of(i * tq, tq)
        for r in range(Q_PER_KV):
            o_ref[pl.ds(r0, tq), r * HEAD_DIM:(r + 1) * HEAD_DIM] = o[r * tq:(r + 1) * tq].astype(o_ref.dtype)
        return 0

    scores(q_tile(0), 0, s0_ref, m0_ref)
    lax.fori_loop(0, nt, tile_body, 0)


def _flash_call(q, k, vt, tq, tk):
    bsz, _, seq, _ = q.shape
    lk = k.shape[2]
    nk = lk // tk
    assert nk >= 2, "the score pipeline needs at least two key blocks"
    q5 = q.reshape(bsz, N_KV_HEADS, Q_PER_KV, seq, HEAD_DIM)
    return pl.pallas_call(
        functools.partial(_flash_kernel, tq=tq, tk=tk, nk=nk, nt=seq // tq),
        grid=(bsz, N_KV_HEADS),
        in_specs=[pl.BlockSpec((None, None, Q_PER_KV, seq, HEAD_DIM), lambda b, g: (b, g, 0, 0, 0)),
                  pl.BlockSpec((None, None, lk, HEAD_DIM), lambda b, g: (b, g, 0, 0)),
                  pl.BlockSpec((None, None, HEAD_DIM + VT_PAD, lk), lambda b, g: (b, g, 0, 0))],
        out_specs=pl.BlockSpec((None, seq, Q_PER_KV * HEAD_DIM), lambda b, g: (b, 0, g)),
        out_shape=jax.ShapeDtypeStruct((bsz, seq, N_Q_HEADS * HEAD_DIM), BF16),
        scratch_shapes=([pltpu.VMEM((tk, Q_PER_KV * tq), F32)] * 3
                        + [pltpu.VMEM((1, Q_PER_KV * tq), F32)] * 3),
        compiler_params=_cparams(("parallel", "parallel")),
        name="flash",
    )(q5, k, vt)


def _row_tile(seq, cap):
    t = min(seq, cap)
    while seq % t:
        t //= 2
    return t


def _even_weights(w_in, lru_wa, lru_wi, dn_alog, dn_dtb, dn_g):
    d, cols = w_in.shape
    ngate = 4 * DN_HEADS
    main = cols - ngate
    w_gates = jnp.concatenate([w_in[:, main:], jnp.zeros((d, LANES - ngate), F32)], axis=1)

    def blockdiag(w):
        w = w * (-LOG2E)
        z = jnp.zeros_like(w[:, 0::2])
        top = jnp.concatenate([w[:, 0::2], z], axis=-1)
        bot = jnp.concatenate([z, w[:, 1::2]], axis=-1)
        return jnp.concatenate([top, bot], axis=-2).astype(BF16)

    alog = dn_alog.reshape(-1)
    dtb = dn_dtb.reshape(-1)
    cvec = jnp.zeros((SUBLANES, LANES), F32)
    cvec = cvec.at[0, 2 * DN_HEADS:ngate].set(alog).at[1, 2 * DN_HEADS:ngate].set(dtb)

    ccol = jnp.zeros((2, ngate), F32).at[0, 2 * DN_HEADS:].set(alog).at[1, 2 * DN_HEADS:].set(dtb)

    return dict(w_in=w_in.astype(BF16), w_gates=w_gates.astype(BF16),
                w_gates_t=w_in[:, main:].T.astype(BF16),
                wa=blockdiag(lru_wa), wi=blockdiag(lru_wi), cvec=cvec, ccol=ccol,
                dn_g=jnp.tile(dn_g, DN_HEADS).reshape(1, -1))


def _even_stream(h_in, res, gate, norm_g, sc, sh, ew, lru_cw, lru_cb, lru_ba, lru_bi, lru_lam, dn_cw,
                 w_out, h0, s0, tm_in, tm, dn_rows, fold_batch):
    width = lru_cw.shape[1]
    ngate = 4 * DN_HEADS
    b0, l0 = h_in.shape[:2]
    fold = (lambda a: a.reshape(1, b0 * l0, a.shape[-1])) if fold_batch else (lambda a: a)
    unfold = (lambda a: a.reshape(b0, l0, a.shape[-1])) if fold_batch else (lambda a: a)
    proj, gates, gates_t = _normlin_call(fold(h_in), norm_g, sc, sh, ew["w_in"], ew["w_gates"], ew["w_gates_t"],
                                         tm_in)
    proj, gates = unfold(proj), unfold(gates)
    if fold_batch:
        gates_t = jnp.transpose(gates_t.reshape(gates_t.shape[1], b0, l0), (1, 0, 2))
    lru, h_t = _lru_call(proj, lru_cw, lru_cb, ew["wa"], ew["wi"], lru_ba, lru_bi, lru_lam, h0,
                         _row_tile(proj.shape[1], LRU_ROWS))
    qkv = _dnprep_call(proj, dn_cw, 2 * width // LANES, _row_tile(proj.shape[1], DNPREP_ROWS))
    ccol = jnp.broadcast_to(ew["ccol"][:, :, None], (2, ngate, dn_rows))
    o_f, o_b, s_t = _delta_call(qkv, gates, gates_t, ew["cvec"], ccol, s0, dn_rows)
    z_block = (2 * width + dn_cw.shape[1]) // (DN_HEADS * DN_DK)
    out = _evenout_call(fold(lru), fold(o_f), fold(o_b), fold(proj), z_block, ew["dn_g"], w_out, fold(res), gate, tm)
    return unfold(out), h_t, s_t


def kernel(x, c, ctx, c_ctx, norm_mix_g, norm_mlp_g, w_mod, b_mod, mlp_w1, mlp_w2, even_w_in, lru_conv_w,
           lru_conv_b, lru_w_a, lru_b_a, lru_w_i, lru_b_i, lru_lambda, dn_conv_w, dn_a_log, dn_dt_bias,
           dn_norm_g, even_w_out, attn_w_qkv, attn_q_norm_g, attn_k_norm_g, attn_w_o, final_norm_g):
    bsz, seq, d = x.shape
    lc = ctx.shape[1]
    depth = w_mod.shape[0]
    tm_cf = _row_tile(bsz * lc, 1024)
    fold_c = lambda a: a.reshape(1, bsz * lc, a.shape[-1])
    unfold_c = lambda a: a.reshape(bsz, lc, a.shape[-1])
    tm_mlp = _row_tile(seq, 1024)
    rows_c = _row_tile(lc, 256)

    cc = jnp.zeros((SUBLANES, d), F32).at[:bsz].set(c).at[bsz].set(c_ctx)
    mod = _mod_call(cc, w_mod, b_mod).reshape(depth, SUBLANES, 6, d)

    rope = _rope_tables(seq)
    xs, cs = x, ctx
    for l in range(depth):
        last = l == depth - 1
        j = l // 2
        mx = [mod[l, :bsz, k][:, None, :] for k in range(6)]
        mc = [mod[l, bsz, k][None, None, :] for k in range(6)]
        sh1, sc1, g1, sh2, sc2, g2 = mx
        csh1, csc1, cg1, csh2, csc2, cg2 = mc
        w1 = mlp_w1[l].astype(BF16)
        w2 = mlp_w2[l].astype(BF16)
        if l % 2 == 0:
            ew = _even_weights(even_w_in[j], lru_w_a[j], lru_w_i[j], dn_a_log[j], dn_dt_bias[j], dn_norm_g[j])
            w_out = even_w_out[j].astype(BF16)
            h0 = jnp.zeros((bsz, 2, lru_conv_w.shape[2]), F32)
            s0 = jnp.zeros((bsz, DN_HEADS, 2, DN_DK, DN_DK), F32)
            common = (ew, lru_conv_w[j], lru_conv_b[j], lru_b_a[j], lru_b_i[j], lru_lambda[j], dn_conv_w[j], w_out)
            cs_mix, h_t, s_t = _even_stream(cs, cs, cg1, norm_mix_g[l], csc1, csh1, *common, h0, s0,
                                            tm_cf, tm_cf, rows_c, True)
            xs, _, _ = _even_stream(xs, xs, g1, norm_mix_g[l], sc1, sh1, *common, h_t, s_t,
                                    tm_mlp, tm_mlp, _row_tile(seq, DELTA_ROWS), False)
        else:
            perm = _rope_perm()
            nqk = (N_Q_HEADS + N_KV_HEADS) * HEAD_DIM
            w_qk = attn_w_qkv[j][:, :nqk].reshape(d, N_Q_HEADS + N_KV_HEADS, HEAD_DIM)[:, :, perm]
            w_qkv = jnp.concatenate([w_qk.reshape(d, nqk), attn_w_qkv[j][:, nqk:]], axis=1).astype(BF16)
            q_g = attn_q_norm_g[j][perm]
            k_g = attn_k_norm_g[j][perm]
            w_o = attn_w_o[j].astype(BF16)
            qkv_c = unfold_c(_normlin_call(fold_c(cs), norm_mix_g[l], csc1, csh1, w_qkv, None, None, tm_cf))
            qkv_x = _normlin_call(xs, norm_mix_g[l], sc1, sh1, w_qkv, None, None, tm_mlp)
            if last:
                k_c, vt_c = _aprep_call(qkv_c, q_g, k_g, None, False, rows_c)
            else:
                qt_c, k_c, vt_c = _aprep_call(qkv_c, q_g, k_g, None, True, rows_c)
            qt_x, k_x, vt_x = _aprep_call(qkv_x, q_g, k_g, rope, True, tm_mlp)
            k_all = jnp.concatenate([k_c, k_x], axis=2)
            vt_all = jnp.concatenate([vt_c, vt_x], axis=3)
            o_x = _flash_call(qt_x, k_all, vt_all, _row_tile(seq, FLASH_TQ), _col_tile(lc + seq, FLASH_TK_CAP))
            xs = _linres_call(o_x, w_o, xs, g1, tm_mlp)
            if not last:
                o_c = _flash_call(qt_c, k_c, vt_c, rows_c, _col_tile(lc, min(FLASH_TK_CAP, lc // 2)))
                cs_mix = unfold_c(_linres_call(fold_c(o_c), w_o, fold_c(cs), cg1, tm_cf))
        xs = _mlp_call(xs, norm_mlp_g[l], sc2, sh2, g2, w1, w2, final_norm_g, tm_mlp, last)
        if not last:
            cs = unfold_c(_mlp_call(fold_c(cs_mix), norm_mlp_g[l], csc2, csh2, cg2, w1, w2, final_norm_g,
                                    tm_cf, False))
    return xs
```
